```python
import math
import jax, jax.numpy as jnp
from jax import lax
import numpy as np

D_MODEL = 1024
BATCH = 4
SEQ = 4096
DEPTH = 1

MEM_LEN = 256
HEAD_DIM = 64
GMLP_HEADS = 4
ATTN_HEADS = 8
MEM_HEADS = 4
GMLP_WIDTH = GMLP_HEADS * HEAD_DIM
ATTN_WIDTH = ATTN_HEADS * HEAD_DIM
MEM_WIDTH = MEM_HEADS * HEAD_DIM
MIX_WIDTH = GMLP_WIDTH + ATTN_WIDTH + MEM_WIDTH
IN_WIDTH = 3 * GMLP_WIDTH + 4 * ATTN_WIDTH + 2 * MEM_WIDTH
CHUNK = 128
BLOCK = 128
DILATED_CONFIGS = ((128, 1), (512, 4), (2048, 16))
PAD_MULT = max(d for _, d in DILATED_CONFIGS) * BLOCK
EPS = 1e-6

kernel_name = "hybrid_gmlp_dilated_memory_layer"


def _rms(x, g):
    xf = x.astype(jnp.float32)
    y = xf * lax.rsqrt(jnp.mean(xf * xf, axis=-1, keepdims=True) + EPS)
    return (y * g.astype(jnp.float32)).astype(x.dtype)


def _dilated_branch(q, k, v, dilation, n_win):
    B, Sp, H, hd = q.shape
    L = Sp // dilation
    nb = L // BLOCK

    def to_blocks(t):
        return t.reshape(B, L, dilation, H, hd).transpose(0, 2, 3, 1, 4).reshape(B, dilation, H, nb, BLOCK, hd)

    def with_prev(t):
        prev = jnp.pad(t, ((0, 0), (0, 0), (0, 0), (1, 0), (0, 0), (0, 0)))[:, :, :, :-1]
        return jnp.concatenate([prev, t], axis=4)

    qb = to_blocks(q)
    kc = with_prev(to_blocks(k))
    vc = with_prev(to_blocks(v))
    s = jnp.einsum('bdhnqc,bdhnkc->bdhnqk', qb, kc).astype(jnp.float32) * (1.0 / math.sqrt(hd))
    qi = jnp.arange(BLOCK)[:, None] + BLOCK
    ki = jnp.arange(2 * BLOCK)[None, :]
    rel = qi - ki
    blk = jnp.arange(nb)[:, None, None]
    valid = (rel >= 0) & (rel <= n_win) & ((blk > 0) | (ki >= BLOCK))
    s = jnp.where(valid, s, -jnp.inf)
    lse = jax.nn.logsumexp(s, axis=-1)
    p = jnp.exp(s - lse[..., None])
    o = jnp.einsum('bdhnqk,bdhnkc->bdhnqc', p.astype(v.dtype), vc)
    o = o.reshape(B, dilation, H, L, hd).transpose(0, 3, 1, 2, 4).reshape(B, Sp, H, hd)
    lse = lse.reshape(B, dilation, H, L).transpose(0, 3, 1, 2).reshape(B, Sp, H)
    return o, lse


def _dilated_attention(q, k, v):
    B, S, H, hd = q.shape
    Sp = ((S + PAD_MULT - 1) // PAD_MULT) * PAD_MULT
    pad = ((0, 0), (0, Sp - S), (0, 0), (0, 0))
    qp, kp, vp = jnp.pad(q, pad), jnp.pad(k, pad), jnp.pad(v, pad)
    outs, lses = [], []
    for window, dil in DILATED_CONFIGS:
        o, l = _dilated_branch(qp, kp, vp, dil, window // dil)
        outs.append(o)
        lses.append(l)
    w = jax.nn.softmax(jnp.stack(lses, axis=0), axis=0)
    out = sum(w[i][..., None] * outs[i].astype(jnp.float32) for i in range(len(outs)))
    return out[:, :S].astype(q.dtype)


def _chunked_gmlp(u, v, v_gain, w_s, b_s):
    B, S, GH, hd = v.shape
    nc = S // CHUNK
    vn = _rms(v, v_gain).reshape(B, nc, CHUNK, GH, hd)
    tril = jnp.tril(jnp.ones((CHUNK, CHUNK), dtype=w_s.dtype))
    sp = jnp.einsum('hts,bcshd->bcthd', w_s * tril, vn) + b_s.T[:, :, None]
    return u * sp.reshape(B, S, GH, hd)


def _memory_attention(qm, mem, mem_gain, w_mem_kv, q_gain, k_gain):
    B, S, H, hd = qm.shape
    kv = _rms(mem, mem_gain) @ w_mem_kv
    mk, mv = jnp.split(kv, 2, axis=-1)
    mk = _rms(mk.reshape(B, -1, H, hd), k_gain)
    mv = mv.reshape(B, -1, H, hd)
    qn = _rms(qm, q_gain)
    s = jnp.einsum('bshc,bmhc->bhsm', qn, mk).astype(jnp.float32) * (1.0 / math.sqrt(hd))
    p = jax.nn.softmax(s, axis=-1)
    return jnp.einsum('bhsm,bmhc->bshc', p.astype(mv.dtype), mv)


def setup_inputs(seed: int = 0) -> dict:
    key = jax.random.key(seed)
    ks = jax.random.split(key, 16)
    f32 = jnp.float32
    x = jax.random.normal(ks[0], (BATCH, SEQ, D_MODEL), f32)
    mem = jax.random.normal(ks[1], (BATCH, MEM_LEN, D_MODEL), f32)
    norm_gain = 1.0 + 0.02 * jax.random.normal(ks[2], (DEPTH, D_MODEL), f32)
    w_in = jax.random.normal(ks[3], (DEPTH, D_MODEL, IN_WIDTH), f32) * D_MODEL ** -0.5
    gmlp_v_gain = 1.0 + 0.02 * jax.random.normal(ks[4], (DEPTH, GMLP_HEADS, HEAD_DIM), f32)
    gmlp_w_s = jax.random.normal(ks[5], (DEPTH, GMLP_HEADS, CHUNK, CHUNK), f32) * CHUNK ** -0.5
    gmlp_b = 1.0 + 0.02 * jax.random.normal(ks[6], (DEPTH, GMLP_HEADS, CHUNK), f32)
    attn_q_gain = 1.0 + 0.02 * jax.random.normal(ks[7], (DEPTH, HEAD_DIM), f32)
    attn_k_gain = 1.0 + 0.02 * jax.random.normal(ks[8], (DEPTH, HEAD_DIM), f32)
    mem_norm_gain = 1.0 + 0.02 * jax.random.normal(ks[9], (DEPTH, D_MODEL), f32)
    w_mem_kv = jax.random.normal(ks[10], (DEPTH, D_MODEL, 2 * MEM_WIDTH), f32) * D_MODEL ** -0.5
    mem_q_gain = 1.0 + 0.02 * jax.random.normal(ks[11], (DEPTH, HEAD_DIM), f32)
    mem_k_gain = 1.0 + 0.02 * jax.random.normal(ks[12], (DEPTH, HEAD_DIM), f32)
    w_out = jax.random.normal(ks[13], (DEPTH, MIX_WIDTH, D_MODEL), f32) * MIX_WIDTH ** -0.5
    return {"x": x, "mem": mem, "norm_gain": norm_gain, "w_in": w_in,
            "gmlp_v_gain": gmlp_v_gain, "gmlp_w_s": gmlp_w_s, "gmlp_b": gmlp_b,
            "attn_q_gain": attn_q_gain, "attn_k_gain": attn_k_gain,
            "mem_norm_gain": mem_norm_gain, "w_mem_kv": w_mem_kv,
            "mem_q_gain": mem_q_gain, "mem_k_gain": mem_k_gain, "w_out": w_out}


def reference(x, mem, norm_gain, w_in, gmlp_v_gain, gmlp_w_s, gmlp_b,
              attn_q_gain, attn_k_gain, mem_norm_gain, w_mem_kv,
              mem_q_gain, mem_k_gain, w_out):
    B, S, _ = x.shape
    split_points = np.cumsum([GMLP_WIDTH] * 3 + [ATTN_WIDTH] * 4 + [MEM_WIDTH])
    for l in range(DEPTH):
        h = _rms(x, norm_gain[l])
        proj = h @ w_in[l]
        g_u, g_v, g_gate, a_q, a_k, a_v, a_gate, m_q, m_gate = jnp.split(proj, split_points, axis=-1)

        y_g = _chunked_gmlp(g_u.reshape(B, S, GMLP_HEADS, HEAD_DIM), g_v.reshape(B, S, GMLP_HEADS, HEAD_DIM),
                            gmlp_v_gain[l], gmlp_w_s[l], gmlp_b[l]).reshape(B, S, GMLP_WIDTH)
        y_g = y_g * jax.nn.silu(g_gate)

        q = _rms(a_q.reshape(B, S, ATTN_HEADS, HEAD_DIM), attn_q_gain[l])
        k = _rms(a_k.reshape(B, S, ATTN_HEADS, HEAD_DIM), attn_k_gain[l])
        v = a_v.reshape(B, S, ATTN_HEADS, HEAD_DIM)
        y_a = _dilated_attention(q, k, v).reshape(B, S, ATTN_WIDTH) * jax.nn.silu(a_gate)

        y_m = _memory_attention(m_q.reshape(B, S, MEM_HEADS, HEAD_DIM), mem, mem_norm_gain[l], w_mem_kv[l],
                                mem_q_gain[l], mem_k_gain[l]).reshape(B, S, MEM_WIDTH)
        y_m = y_m * jax.nn.silu(m_gate)

        y = jnp.concatenate([y_g, y_a, y_m], axis=-1) @ w_out[l]
        x = x + y
    return x
```

```python
import functools
import math

import jax
import jax.numpy as jnp
from jax import lax
from jax.experimental import pallas as pl
from jax.experimental.pallas import tpu as pltpu

HEAD_DIM = 64
GMLP_HEADS = 4
ATTN_HEADS = 8
MEM_HEADS = 4
GMLP_WIDTH = GMLP_HEADS * HEAD_DIM
ATTN_WIDTH = ATTN_HEADS * HEAD_DIM
MEM_WIDTH = MEM_HEADS * HEAD_DIM
CHUNK = 128
BLOCK = 128
DILATED_CONFIGS = ((128, 1), (512, 4), (2048, 16))
EPS = 1e-6
MASKED = -1e30

LANES = 128
ROW_TILE = 512
VMEM_LIMIT = 56 * 1024 * 1024

F32 = jnp.float32
BF16 = jnp.bfloat16


def _head_sumsq(acc, ones_bd):
    return jnp.dot((acc * acc).astype(BF16), ones_bd, preferred_element_type=F32)


def _head_rms(acc, ones_bd, gain):
    return acc * lax.rsqrt(_head_sumsq(acc, ones_bd) * (1.0 / HEAD_DIM) + EPS) * gain


def _silu(x):
    return x * jax.nn.sigmoid(x)


def _proj_kernel(x_ref, gain_ref, w_ref, bd_ref, gvg_ref, qg_ref, kg_ref, mqg_ref,
                 ug_ref, gvn_ref, q_ref, k_ref, v_ref, sag_ref, mq_ref, smg_ref):
    x = x_ref[...]
    ms = jnp.mean(x * x, axis=-1, keepdims=True)
    h = (x * lax.rsqrt(ms + EPS) * gain_ref[...]).astype(BF16)

    def seg(lo, width):
        return jnp.dot(h, w_ref[:, lo:lo + width], preferred_element_type=F32)

    bd = bd_ref[...]
    bd_small = bd_ref[:GMLP_WIDTH, :GMLP_WIDTH]
    scale = 1.0 / math.sqrt(HEAD_DIM)
    o = 0
    g_u = seg(o, GMLP_WIDTH); o += GMLP_WIDTH
    g_v = seg(o, GMLP_WIDTH); o += GMLP_WIDTH
    g_gate = seg(o, GMLP_WIDTH); o += GMLP_WIDTH
    ug_ref[...] = (g_u * _silu(g_gate)).astype(BF16)
    gvn_ref[...] = _head_rms(g_v, bd_small, gvg_ref[...]).astype(BF16)
    a_q = seg(o, ATTN_WIDTH); o += ATTN_WIDTH
    q_ref[...] = (_head_rms(a_q, bd, qg_ref[...]) * scale).astype(BF16)
    a_k = seg(o, ATTN_WIDTH); o += ATTN_WIDTH
    k_ref[...] = _head_rms(a_k, bd, kg_ref[...]).astype(BF16)
    v_ref[...] = seg(o, ATTN_WIDTH).astype(BF16); o += ATTN_WIDTH
    sag_ref[...] = _silu(seg(o, ATTN_WIDTH)).astype(BF16); o += ATTN_WIDTH
    m_q = seg(o, MEM_WIDTH); o += MEM_WIDTH
    mq_ref[...] = (_head_rms(m_q, bd_small, mqg_ref[...]) * scale).astype(BF16)
    smg_ref[...] = _silu(seg(o, MEM_WIDTH)).astype(BF16)


def _memkv_kernel(mem_ref, gain_ref, w_ref, bd_ref, kg_ref, mk_ref, mv_ref):
    x = mem_ref[...]
    ms = jnp.mean(x * x, axis=-1, keepdims=True)
    h = (x * lax.rsqrt(ms + EPS) * gain_ref[...]).astype(BF16)
    mk = jnp.dot(h, w_ref[:, :MEM_WIDTH], preferred_element_type=F32)
    mv = jnp.dot(h, w_ref[:, MEM_WIDTH:], preferred_element_type=F32)
    mk_ref[...] = _head_rms(mk, bd_ref[...], kg_ref[...]).astype(BF16)
    mv_ref[...] = mv.astype(BF16)


def _pair_attention(q2, k2, v2, valid):
    rows = q2.shape[0]
    left = lax.broadcasted_iota(jnp.int32, (rows, LANES), 1) < HEAD_DIM
    v_aug = jnp.concatenate([v2, jnp.ones_like(v2)], axis=1)
    outs, lses = [], []
    for keep in (left, jnp.logical_not(left)):
        qm = jnp.where(keep, q2, jnp.zeros_like(q2))
        s = lax.dot_general(qm, k2, (((1,), (1,)), ((), ())), preferred_element_type=F32)
        if valid is not None:
            s = jnp.where(valid, s, MASKED)
        m = jnp.max(s, axis=1, keepdims=True)
        p = jnp.exp(s - m).astype(BF16)
        pv = jnp.dot(p, v_aug, preferred_element_type=F32)
        denom = pv[:, LANES:]
        outs.append(pv[:, :LANES] / denom)
        lses.append(m + jnp.log(denom))
    return jnp.where(left, outs[0], outs[1]), jnp.where(left, lses[0], lses[1])


def _dilated_kernel(q_ref, k_ref, v_ref, o_ref, lse_ref, *, blocks_per_step):
    step = pl.program_id(2)
    row = lax.broadcasted_iota(jnp.int32, (BLOCK, 2 * BLOCK), 0)
    col = lax.broadcasted_iota(jnp.int32, (BLOCK, 2 * BLOCK), 1)

    def body(j, carry):
        blk = step * blocks_per_step + j
        kstart = pl.multiple_of(jnp.maximum(blk - 1, 0) * BLOCK, BLOCK)
        rel = row - col + (blk * BLOCK - kstart)
        valid = (rel >= 0) & (rel <= BLOCK)
        qstart = pl.multiple_of(j * BLOCK, BLOCK)
        for pair in range(ATTN_WIDTH // LANES):
            lanes = slice(pair * LANES, (pair + 1) * LANES)
            o2, lse2 = _pair_attention(q_ref[pl.ds(qstart, BLOCK), lanes],
                                       k_ref[pl.ds(kstart, 2 * BLOCK), lanes],
                                       v_ref[pl.ds(kstart, 2 * BLOCK), lanes], valid)
            o_ref[pl.ds(qstart, BLOCK), lanes] = o2.astype(BF16)
            lse_ref[pl.ds(qstart, BLOCK), lanes] = lse2
        return carry

    lax.fori_loop(0, blocks_per_step, body, 0)


def _mix_kernel(x_ref, ug_ref, gvn_ref, ws_ref, bs_ref,
                o1_ref, l1_ref, o2_ref, l2_ref, o3_ref, l3_ref, sag_ref,
                mq_ref, mk_ref, mv_ref, smg_ref, wout_ref, out_ref):
    rows = x_ref.shape[0]
    tri = (lax.broadcasted_iota(jnp.int32, (CHUNK, CHUNK), 0)
           >= lax.broadcasted_iota(jnp.int32, (CHUNK, CHUNK), 1))
    left = lax.broadcasted_iota(jnp.int32, (CHUNK, LANES), 1) < HEAD_DIM
    w_tri = [jnp.where(tri, ws_ref[h], 0.0).astype(BF16) for h in range(GMLP_HEADS)]
    sp_chunks = []
    for c in range(rows // CHUNK):
        pairs = []
        for pair in range(GMLP_WIDTH // LANES):
            vn2 = gvn_ref[c * CHUNK:(c + 1) * CHUNK, pair * LANES:(pair + 1) * LANES]
            sp_a = jnp.dot(w_tri[2 * pair], vn2, preferred_element_type=F32)
            sp_b = jnp.dot(w_tri[2 * pair + 1], vn2, preferred_element_type=F32)
            pairs.append(jnp.where(left, sp_a, sp_b))
        sp_chunks.append(jnp.concatenate(pairs, axis=1) + bs_ref[...])
    y_g = ug_ref[...].astype(F32) * jnp.concatenate(sp_chunks, axis=0)

    l1, l2, l3 = l1_ref[...], l2_ref[...], l3_ref[...]
    lmax = jnp.maximum(jnp.maximum(l1, l2), l3)
    e1, e2, e3 = jnp.exp(l1 - lmax), jnp.exp(l2 - lmax), jnp.exp(l3 - lmax)
    mixed = (e1 * o1_ref[...].astype(F32) + e2 * o2_ref[...].astype(F32)
             + e3 * o3_ref[...].astype(F32)) / (e1 + e2 + e3)
    y_a = mixed * sag_ref[...].astype(F32)

    mem_pairs = []
    for pair in range(MEM_WIDTH // LANES):
        lanes = slice(pair * LANES, (pair + 1) * LANES)
        o2, _ = _pair_attention(mq_ref[:, lanes], mk_ref[:, lanes], mv_ref[:, lanes], None)
        mem_pairs.append(o2)
    y_m = jnp.concatenate(mem_pairs, axis=1) * smg_ref[...].astype(F32)

    y = (jnp.dot(y_g.astype(BF16), wout_ref[:GMLP_WIDTH, :], preferred_element_type=F32)
         + jnp.dot(y_a.astype(BF16), wout_ref[GMLP_WIDTH:GMLP_WIDTH + ATTN_WIDTH, :],
                   preferred_element_type=F32)
         + jnp.dot(y_m.astype(BF16), wout_ref[GMLP_WIDTH + ATTN_WIDTH:, :],
                   preferred_element_type=F32))
    out_ref[...] = x_ref[...] + y


def _block_diag_ones(width):
    head = jnp.arange(width) // HEAD_DIM
    return (head[:, None] == head[None, :]).astype(BF16)


def _cparams(sem):
    return pltpu.CompilerParams(dimension_semantics=sem, vmem_limit_bytes=VMEM_LIMIT)


def _const_spec(shape):
    return pl.BlockSpec(shape, lambda *idx: (0,) * len(shape))


def _project(x2d, gain, w_in, bd, gvg, qg, kg, mqg):
    rows, d_model = x2d.shape
    widths = (GMLP_WIDTH, GMLP_WIDTH, ATTN_WIDTH, ATTN_WIDTH, ATTN_WIDTH, ATTN_WIDTH, MEM_WIDTH, MEM_WIDTH)
    return pl.pallas_call(
        _proj_kernel,
        grid=(rows // ROW_TILE,),
        in_specs=[pl.BlockSpec((ROW_TILE, d_model), lambda i: (i, 0)),
                  _const_spec(gain.shape), _const_spec(w_in.shape), _const_spec(bd.shape),
                  _const_spec(gvg.shape), _const_spec(qg.shape), _const_spec(kg.shape),
                  _const_spec(mqg.shape)],
        out_specs=[pl.BlockSpec((ROW_TILE, w), lambda i: (i, 0)) for w in widths],
        out_shape=[jax.ShapeDtypeStruct((rows, w), BF16) for w in widths],
        compiler_params=_cparams(("parallel",)),
    )(x2d, gain, w_in, bd, gvg, qg, kg, mqg)


def _memory_kv(mem, gain, w_kv, bd, kg):
    batch, mem_len, d_model = mem.shape
    return pl.pallas_call(
        _memkv_kernel,
        grid=(batch,),
        in_specs=[pl.BlockSpec((None, mem_len, d_model), lambda b: (b, 0, 0)),
                  _const_spec(gain.shape), _const_spec(w_kv.shape), _const_spec(bd.shape),
                  _const_spec(kg.shape)],
        out_specs=[pl.BlockSpec((None, mem_len, MEM_WIDTH), lambda b: (b, 0, 0))] * 2,
        out_shape=[jax.ShapeDtypeStruct((batch, mem_len, MEM_WIDTH), BF16)] * 2,
        compiler_params=_cparams(("parallel",)),
    )(mem, gain, w_kv, bd, kg)


def _dilated_config(q, k, v, batch, seq, dilation):
    length = seq // dilation
    q_rows = min(length, 1024)
    blocks_per_step = q_rows // BLOCK
    view = lambda t: t.reshape(batch, length, dilation * ATTN_WIDTH)
    q_spec = pl.BlockSpec((None, q_rows, ATTN_WIDTH), lambda b, r, n: (b, n, r))
    kv_spec = pl.BlockSpec((None, length, ATTN_WIDTH), lambda b, r, n: (b, 0, r))
    o, lse = pl.pallas_call(
        functools.partial(_dilated_kernel, blocks_per_step=blocks_per_step),
        grid=(batch, dilation, length // q_rows),
        in_specs=[q_spec, kv_spec, kv_spec],
        out_specs=[q_spec, q_spec],
        out_shape=[jax.ShapeDtypeStruct((batch, length, dilation * ATTN_WIDTH), BF16),
                   jax.ShapeDtypeStruct((batch, length, dilation * ATTN_WIDTH), F32)],
        compiler_params=_cparams(("parallel", "parallel", "arbitrary")),
    )(view(q), view(k), view(v))
    return o.reshape(batch * seq, ATTN_WIDTH), lse.reshape(batch * seq, ATTN_WIDTH)


def _mix(x2d, ug, gvn, w_s, b_full, dil, sag, mq, mk, mv, smg, w_out, seq):
    rows, d_model = x2d.shape
    steps_per_batch = seq // ROW_TILE
    row_spec = lambda w: pl.BlockSpec((ROW_TILE, w), lambda i: (i, 0))
    mem_spec = pl.BlockSpec((None,) + mk.shape[1:], lambda i: (i // steps_per_batch, 0, 0))
    dil_specs, dil_args = [], []
    for o, lse in dil:
        dil_specs += [row_spec(ATTN_WIDTH), row_spec(ATTN_WIDTH)]
        dil_args += [o, lse]
    return pl.pallas_call(
        _mix_kernel,
        grid=(rows // ROW_TILE,),
        in_specs=[row_spec(d_model), row_spec(GMLP_WIDTH), row_spec(GMLP_WIDTH),
                  _const_spec(w_s.shape), _const_spec(b_full.shape)]
                 + dil_specs
                 + [row_spec(ATTN_WIDTH), row_spec(MEM_WIDTH), mem_spec, mem_spec,
                    row_spec(MEM_WIDTH), _const_spec(w_out.shape)],
        out_specs=row_spec(d_model),
        out_shape=jax.ShapeDtypeStruct((rows, d_model), F32),
        compiler_params=_cparams(("parallel",)),
    )(x2d, ug, gvn, w_s, b_full, *dil_args, sag, mq, mk, mv, smg, w_out)


def kernel(x, mem, norm_gain, w_in, gmlp_v_gain, gmlp_w_s, gmlp_b, attn_q_gain, attn_k_gain,
           mem_norm_gain, w_mem_kv, mem_q_gain, mem_k_gain, w_out):
    batch, seq, d_model = x.shape
    depth = w_in.shape[0]
    assert seq % ROW_TILE == 0 and seq % (BLOCK * max(d for _, d in DILATED_CONFIGS)) == 0
    assert all(window // d == BLOCK for window, d in DILATED_CONFIGS)
    bd = _block_diag_ones(ATTN_WIDTH)
    bd_mem = _block_diag_ones(MEM_WIDTH)
    x2d = x.reshape(batch * seq, d_model)
    for l in range(depth):
        row = lambda g: g.reshape(1, -1).astype(F32)
        ug, gvn, q, k, v, sag, mq, smg = _project(
            x2d, row(norm_gain[l]), w_in[l].astype(BF16), bd, row(gmlp_v_gain[l]),
            row(jnp.tile(attn_q_gain[l], ATTN_HEADS)), row(jnp.tile(attn_k_gain[l], ATTN_HEADS)),
            row(jnp.tile(mem_q_gain[l], MEM_HEADS)))
        mk, mv = _memory_kv(mem, row(mem_norm_gain[l]), w_mem_kv[l].astype(BF16), bd_mem,
                            row(jnp.tile(mem_k_gain[l], MEM_HEADS)))
        dil = [_dilated_config(q, k, v, batch, seq, d) for _, d in DILATED_CONFIGS]
        b_full = jnp.repeat(gmlp_b[l].T, HEAD_DIM, axis=1)
        x2d = _mix(x2d, ug, gvn, gmlp_w_s[l], b_full, dil, sag, mq, mk, mv, smg,
                   w_out[l].astype(BF16), seq)
    return x2d.reshape(batch, seq, d_model)
```

```python
import functools
import math

import jax
import jax.numpy as jnp
from jax import lax
from jax.experimental import pallas as pl
from jax.experimental.pallas import tpu as pltpu

HEAD_DIM = 64
GMLP_HEADS = 4
ATTN_HEADS = 8
MEM_HEADS = 4
GMLP_WIDTH = GMLP_HEADS * HEAD_DIM
ATTN_WIDTH = ATTN_HEADS * HEAD_DIM
MEM_WIDTH = MEM_HEADS * HEAD_DIM
CHUNK = 128
BLOCK = 128
DILATED_CONFIGS = ((128, 1), (512, 4), (2048, 16))
EPS = 1e-6
MASKED = -1e30

CLASSES = 16
MID = 4
SLABS_PER_MID = CLASSES // MID
LANES = 128
PAIRS = ATTN_WIDTH // LANES
ROW_TILE = 512
VMEM_LIMIT = 56 * 1024 * 1024

F32 = jnp.float32
BF16 = jnp.bfloat16


def _head_sumsq(acc, ones_bd):
    return jnp.dot((acc * acc).astype(BF16), ones_bd, preferred_element_type=F32)


def _head_rms(acc, ones_bd, gain):
    return acc * lax.rsqrt(_head_sumsq(acc, ones_bd) * (1.0 / HEAD_DIM) + EPS) * gain


def _silu(x):
    return x * jax.nn.sigmoid(x)


def _store_both_layouts(val, seq_ref, cls_ref, slab_ref):
    seq_ref[...] = val.astype(BF16)
    per_class = val.shape[0] // CLASSES
    for s in range(val.shape[1] // LANES):
        slab_ref[s] = val[:, s * LANES:(s + 1) * LANES]
    for r in range(CLASSES):
        for s in range(val.shape[1] // LANES):
            cls_ref[r, :, s * LANES:(s + 1) * LANES] = (
                slab_ref[s, pl.ds(r, per_class, stride=CLASSES), :].astype(BF16))


def _proj_kernel(x_ref, gain_ref, w_ref, bd_ref, gvg_ref, qg_ref, kg_ref, mqg_ref,
                 ug_ref, gvn_ref, q_ref, k_ref, v_ref, sag_ref, mq_ref, smg_ref,
                 qc_ref, kc_ref, vc_ref, slab_ref):
    x = x_ref[...]
    ms = jnp.mean(x * x, axis=-1, keepdims=True)
    h = (x * lax.rsqrt(ms + EPS) * gain_ref[...]).astype(BF16)

    def seg(lo, width):
        return jnp.dot(h, w_ref[:, lo:lo + width], preferred_element_type=F32)

    bd = bd_ref[...]
    bd_small = bd_ref[:GMLP_WIDTH, :GMLP_WIDTH]
    scale = 1.0 / math.sqrt(HEAD_DIM)
    o = 0
    g_u = seg(o, GMLP_WIDTH); o += GMLP_WIDTH
    g_v = seg(o, GMLP_WIDTH); o += GMLP_WIDTH
    g_gate = seg(o, GMLP_WIDTH); o += GMLP_WIDTH
    ug_ref[...] = (g_u * _silu(g_gate)).astype(BF16)
    gvn_ref[...] = _head_rms(g_v, bd_small, gvg_ref[...]).astype(BF16)
    a_q = seg(o, ATTN_WIDTH); o += ATTN_WIDTH
    _store_both_layouts(_head_rms(a_q, bd, qg_ref[...]) * scale, q_ref, qc_ref, slab_ref)
    a_k = seg(o, ATTN_WIDTH); o += ATTN_WIDTH
    _store_both_layouts(_head_rms(a_k, bd, kg_ref[...]), k_ref, kc_ref, slab_ref)
    _store_both_layouts(seg(o, ATTN_WIDTH), v_ref, vc_ref, slab_ref); o += ATTN_WIDTH
    sag_ref[...] = _silu(seg(o, ATTN_WIDTH)).astype(BF16); o += ATTN_WIDTH
    m_q = seg(o, MEM_WIDTH); o += MEM_WIDTH
    mq_ref[...] = (_head_rms(m_q, bd_small, mqg_ref[...]) * scale).astype(BF16)
    smg_ref[...] = _silu(seg(o, MEM_WIDTH)).astype(BF16)


def _memkv_kernel(mem_ref, gain_ref, w_ref, bd_ref, kg_ref, mk_ref, mv_ref):
    x = mem_ref[...]
    ms = jnp.mean(x * x, axis=-1, keepdims=True)
    h = (x * lax.rsqrt(ms + EPS) * gain_ref[...]).astype(BF16)
    mk = jnp.dot(h, w_ref[:, :MEM_WIDTH], preferred_element_type=F32)
    mv = jnp.dot(h, w_ref[:, MEM_WIDTH:], preferred_element_type=F32)
    mk_ref[...] = _head_rms(mk, bd_ref[...], kg_ref[...]).astype(BF16)
    mv_ref[...] = mv.astype(BF16)


def _pair_attention(q2, k2, v2, valid):
    rows = q2.shape[0]
    left = lax.broadcasted_iota(jnp.int32, (rows, LANES), 1) < HEAD_DIM
    v_aug = jnp.concatenate([v2, jnp.ones_like(v2)], axis=1)
    outs, lses = [], []
    for keep in (left, jnp.logical_not(left)):
        qm = jnp.where(keep, q2, jnp.zeros_like(q2))
        s = lax.dot_general(qm, k2, (((1,), (1,)), ((), ())), preferred_element_type=F32)
        if valid is not None:
            s = jnp.where(valid, s, MASKED)
        m = jnp.max(s, axis=1, keepdims=True)
        p = jnp.exp(s - m).astype(BF16)
        pv = jnp.dot(p, v_aug, preferred_element_type=F32)
        denom = pv[:, LANES:]
        outs.append(pv[:, :LANES] / denom)
        lses.append(m + jnp.log(denom))
    return jnp.where(left, outs[0], outs[1]), jnp.where(left, lses[0], lses[1])


def _merge_branches(o_a, lse_a, o_b, lse_b):
    m = jnp.maximum(lse_a, lse_b)
    w_a, w_b = jnp.exp(lse_a - m), jnp.exp(lse_b - m)
    tot = w_a + w_b
    return (w_a * o_a + w_b * o_b) / tot, m + jnp.log(tot)


def _band_mask(row_pos, col_pos):
    rel = row_pos - col_pos
    return (rel >= 0) & (rel <= BLOCK)


def _local_kernel(q_ref, k_ref, v_ref, o_ref, lse_ref, *, blocks_per_step):
    step = pl.program_id(1)
    row = lax.broadcasted_iota(jnp.int32, (BLOCK, 2 * BLOCK), 0)
    col = lax.broadcasted_iota(jnp.int32, (BLOCK, 2 * BLOCK), 1)

    def body(j, carry):
        blk = step * blocks_per_step + j
        kstart = pl.multiple_of(jnp.maximum(blk - 1, 0) * BLOCK, BLOCK)
        valid = _band_mask(row + blk * BLOCK, col + kstart)
        qstart = pl.multiple_of(j * BLOCK, BLOCK)
        for pair in range(PAIRS):
            lanes = slice(pair * LANES, (pair + 1) * LANES)
            o2, lse2 = _pair_attention(q_ref[pl.ds(qstart, BLOCK), lanes],
                                       k_ref[pl.ds(kstart, 2 * BLOCK), lanes],
                                       v_ref[pl.ds(kstart, 2 * BLOCK), lanes], valid)
            o_ref[pl.ds(qstart, BLOCK), lanes] = o2.astype(BF16)
            lse_ref[pl.ds(qstart, BLOCK), lanes] = lse2
        return carry

    lax.fori_loop(0, blocks_per_step, body, 0)


def _strided_kernel(q_ref, k_ref, v_ref, o_ref, lse_ref, o16_ref, l16_ref):
    slab_len = q_ref.shape[1]
    row = lax.broadcasted_iota(jnp.int32, (BLOCK, 2 * BLOCK), 0)
    col = lax.broadcasted_iota(jnp.int32, (BLOCK, 2 * BLOCK), 1)

    def coarse(i, carry):
        g = i // (slab_len // BLOCK)
        blk = i % (slab_len // BLOCK)
        kstart = pl.multiple_of(jnp.maximum(blk - 1, 0) * BLOCK, BLOCK)
        valid = _band_mask(row + blk * BLOCK, col + kstart)
        qstart = pl.multiple_of(blk * BLOCK, BLOCK)
        for pair in range(PAIRS):
            lanes = slice(pair * LANES, (pair + 1) * LANES)
            o2, lse2 = _pair_attention(q_ref[g, pl.ds(qstart, BLOCK), lanes],
                                       k_ref[g, pl.ds(kstart, 2 * BLOCK), lanes],
                                       v_ref[g, pl.ds(kstart, 2 * BLOCK), lanes], valid)
            o16_ref[g, pl.ds(qstart, BLOCK), lanes] = o2
            l16_ref[g, pl.ds(qstart, BLOCK), lanes] = lse2
        return carry

    lax.fori_loop(0, SLABS_PER_MID * (slab_len // BLOCK), coarse, 0)

    q_rows = BLOCK // SLABS_PER_MID
    k_rows = 2 * q_rows
    row_j = SLABS_PER_MID * (row % q_rows) + row // q_rows
    col_j = SLABS_PER_MID * (col % k_rows) + col // k_rows

    def gather(ref, start, rows, lanes):
        return jnp.concatenate([ref[g, pl.ds(start, rows), lanes] for g in range(SLABS_PER_MID)], axis=0)

    def mid(i, carry):
        a0 = pl.multiple_of(i * q_rows, q_rows)
        kstart = pl.multiple_of(jnp.maximum(i - 1, 0) * q_rows, q_rows)
        valid = _band_mask(row_j + SLABS_PER_MID * a0, col_j + SLABS_PER_MID * kstart)
        for pair in range(PAIRS):
            lanes = slice(pair * LANES, (pair + 1) * LANES)
            o4, lse4 = _pair_attention(gather(q_ref, a0, q_rows, lanes),
                                       gather(k_ref, kstart, k_rows, lanes),
                                       gather(v_ref, kstart, k_rows, lanes), valid)
            o, lse = _merge_branches(o4, lse4, gather(o16_ref, a0, q_rows, lanes),
                                     gather(l16_ref, a0, q_rows, lanes))
            for g in range(SLABS_PER_MID):
                o_ref[g, pl.ds(a0, q_rows), lanes] = o[g * q_rows:(g + 1) * q_rows].astype(BF16)
                lse_ref[g, pl.ds(a0, q_rows), lanes] = lse[g * q_rows:(g + 1) * q_rows]
        return carry

    lax.fori_loop(0, slab_len // q_rows, mid, 0)


def _to_sequence_order(cls_ref, slab_ref):
    per_class = cls_ref.shape[1]
    n_slabs = cls_ref.shape[2] // LANES
    for r in range(CLASSES):
        for s in range(n_slabs):
            slab_ref[s, pl.ds(r, per_class, stride=CLASSES), :] = (
                cls_ref[r, :, s * LANES:(s + 1) * LANES].astype(F32))
    return jnp.concatenate([slab_ref[s] for s in range(n_slabs)], axis=1)


def _mix_kernel(x_ref, ug_ref, gvn_ref, ws_ref, bs_ref,
                o1_ref, l1_ref, oc_ref, lc_ref, sag_ref,
                mq_ref, mk_ref, mv_ref, smg_ref, wout_ref, out_ref, oslab_ref, lslab_ref):
    rows = x_ref.shape[0]
    tri = (lax.broadcasted_iota(jnp.int32, (CHUNK, CHUNK), 0)
           >= lax.broadcasted_iota(jnp.int32, (CHUNK, CHUNK), 1))
    left = lax.broadcasted_iota(jnp.int32, (CHUNK, LANES), 1) < HEAD_DIM
    w_tri = [jnp.where(tri, ws_ref[h], 0.0).astype(BF16) for h in range(GMLP_HEADS)]
    sp_chunks = []
    for c in range(rows // CHUNK):
        pairs = []
        for pair in range(GMLP_WIDTH // LANES):
            vn2 = gvn_ref[c * CHUNK:(c + 1) * CHUNK, pair * LANES:(pair + 1) * LANES]
            sp_a = jnp.dot(w_tri[2 * pair], vn2, preferred_element_type=F32)
            sp_b = jnp.dot(w_tri[2 * pair + 1], vn2, preferred_element_type=F32)
            pairs.append(jnp.where(left, sp_a, sp_b))
        sp_chunks.append(jnp.concatenate(pairs, axis=1) + bs_ref[...])
    y_g = ug_ref[...].astype(F32) * jnp.concatenate(sp_chunks, axis=0)

    o_c = _to_sequence_order(oc_ref, oslab_ref)
    l_c = _to_sequence_order(lc_ref, lslab_ref)
    mixed, _ = _merge_branches(o1_ref[...].astype(F32), l1_ref[...], o_c, l_c)
    y_a = mixed * sag_ref[...].astype(F32)

    mem_pairs = []
    for pair in range(MEM_WIDTH // LANES):
        lanes = slice(pair * LANES, (pair + 1) * LANES)
        o2, _ = _pair_attention(mq_ref[:, lanes], mk_ref[:, lanes], mv_ref[:, lanes], None)
        mem_pairs.append(o2)
    y_m = jnp.concatenate(mem_pairs, axis=1) * smg_ref[...].astype(F32)

    y = (jnp.dot(y_g.astype(BF16), wout_ref[:GMLP_WIDTH, :], preferred_element_type=F32)
         + jnp.dot(y_a.astype(BF16), wout_ref[GMLP_WIDTH:GMLP_WIDTH + ATTN_WIDTH, :],
                   preferred_element_type=F32)
         + jnp.dot(y_m.astype(BF16), wout_ref[GMLP_WIDTH + ATTN_WIDTH:, :],
                   preferred_element_type=F32))
    out_ref[...] = x_ref[...] + y


def _block_diag_ones(width):
    head = jnp.arange(width) // HEAD_DIM
    return (head[:, None] == head[None, :]).astype(BF16)


def _cparams(sem):
    return pltpu.CompilerParams(dimension_semantics=sem, vmem_limit_bytes=VMEM_LIMIT)


def _const_spec(shape):
    return pl.BlockSpec(shape, lambda *idx: (0,) * len(shape))


def _class_tile_spec(steps_per_batch, width):
    return pl.BlockSpec((None, CLASSES, ROW_TILE // CLASSES, width),
                        lambda i: (i // steps_per_batch, 0, i % steps_per_batch, 0))


def _project(x2d, batch, seq, gain, w_in, bd, gvg, qg, kg, mqg):
    rows, d_model = x2d.shape
    widths = (GMLP_WIDTH, GMLP_WIDTH, ATTN_WIDTH, ATTN_WIDTH, ATTN_WIDTH, ATTN_WIDTH, MEM_WIDTH, MEM_WIDTH)
    cls_shape = jax.ShapeDtypeStruct((batch, CLASSES, seq // CLASSES, ATTN_WIDTH), BF16)
    cls_spec = _class_tile_spec(seq // ROW_TILE, ATTN_WIDTH)
    return pl.pallas_call(
        _proj_kernel,
        grid=(rows // ROW_TILE,),
        in_specs=[pl.BlockSpec((ROW_TILE, d_model), lambda i: (i, 0)),
                  _const_spec(gain.shape), _const_spec(w_in.shape), _const_spec(bd.shape),
                  _const_spec(gvg.shape), _const_spec(qg.shape), _const_spec(kg.shape),
                  _const_spec(mqg.shape)],
        out_specs=[pl.BlockSpec((ROW_TILE, w), lambda i: (i, 0)) for w in widths] + [cls_spec] * 3,
        out_shape=[jax.ShapeDtypeStruct((rows, w), BF16) for w in widths] + [cls_shape] * 3,
        scratch_shapes=[pltpu.VMEM((ATTN_WIDTH // LANES, ROW_TILE, LANES), F32)],
        compiler_params=_cparams(("parallel",)),
    )(x2d, gain, w_in, bd, gvg, qg, kg, mqg)


def _memory_kv(mem, gain, w_kv, bd, kg):
    batch, mem_len, d_model = mem.shape
    return pl.pallas_call(
        _memkv_kernel,
        grid=(batch,),
        in_specs=[pl.BlockSpec((None, mem_len, d_model), lambda b: (b, 0, 0)),
                  _const_spec(gain.shape), _const_spec(w_kv.shape), _const_spec(bd.shape),
                  _const_spec(kg.shape)],
        out_specs=[pl.BlockSpec((None, mem_len, MEM_WIDTH), lambda b: (b, 0, 0))] * 2,
        out_shape=[jax.ShapeDtypeStruct((batch, mem_len, MEM_WIDTH), BF16)] * 2,
        compiler_params=_cparams(("parallel",)),
    )(mem, gain, w_kv, bd, kg)


def _local_attention(q, k, v, batch, seq):
    q_rows = 1024
    view = lambda t: t.reshape(batch, seq, ATTN_WIDTH)
    q_spec = pl.BlockSpec((None, q_rows, ATTN_WIDTH), lambda b, n: (b, n, 0))
    kv_spec = pl.BlockSpec((None, seq, ATTN_WIDTH), lambda b, n: (b, 0, 0))
    o, lse = pl.pallas_call(
        functools.partial(_local_kernel, blocks_per_step=q_rows // BLOCK),
        grid=(batch, seq // q_rows),
        in_specs=[q_spec, kv_spec, kv_spec],
        out_specs=[q_spec, q_spec],
        out_shape=[jax.ShapeDtypeStruct((batch, seq, ATTN_WIDTH), BF16),
                   jax.ShapeDtypeStruct((batch, seq, ATTN_WIDTH), F32)],
        compiler_params=_cparams(("parallel", "arbitrary")),
    )(view(q), view(k), view(v))
    return o.reshape(batch * seq, ATTN_WIDTH), lse.reshape(batch * seq, ATTN_WIDTH)


def _strided_attention(qc, kc, vc):
    batch, _, slab_len, width = qc.shape
    view = lambda t: t.reshape(batch, SLABS_PER_MID, MID, slab_len, width)
    spec = pl.BlockSpec((None, SLABS_PER_MID, None, slab_len, width), lambda b, r4: (b, 0, r4, 0, 0))
    o, lse = pl.pallas_call(
        _strided_kernel,
        grid=(batch, MID),
        in_specs=[spec] * 3,
        out_specs=[spec] * 2,
        out_shape=[jax.ShapeDtypeStruct((batch, SLABS_PER_MID, MID, slab_len, width), BF16),
                   jax.ShapeDtypeStruct((batch, SLABS_PER_MID, MID, slab_len, width), F32)],
        scratch_shapes=[pltpu.VMEM((SLABS_PER_MID, slab_len, width), F32)] * 2,
        compiler_params=_cparams(("parallel", "parallel")),
    )(view(qc), view(kc), view(vc))
    return o.reshape(qc.shape), lse.reshape(qc.shape)


def _mix(x2d, ug, gvn, w_s, b_full, o1, l1, oc, lc, sag, mq, mk, mv, smg, w_out, seq):
    rows, d_model = x2d.shape
    steps_per_batch = seq // ROW_TILE
    row_spec = lambda w: pl.BlockSpec((ROW_TILE, w), lambda i: (i, 0))
    mem_spec = pl.BlockSpec((None,) + mk.shape[1:], lambda i: (i // steps_per_batch, 0, 0))
    cls_spec = _class_tile_spec(steps_per_batch, ATTN_WIDTH)
    return pl.pallas_call(
        _mix_kernel,
        grid=(rows // ROW_TILE,),
        in_specs=[row_spec(d_model), row_spec(GMLP_WIDTH), row_spec(GMLP_WIDTH),
                  _const_spec(w_s.shape), _const_spec(b_full.shape),
                  row_spec(ATTN_WIDTH), row_spec(ATTN_WIDTH), cls_spec, cls_spec,
                  row_spec(ATTN_WIDTH), row_spec(MEM_WIDTH), mem_spec, mem_spec,
                  row_spec(MEM_WIDTH), _const_spec(w_out.shape)],
        out_specs=row_spec(d_model),
        out_shape=jax.ShapeDtypeStruct((rows, d_model), F32),
        scratch_shapes=[pltpu.VMEM((ATTN_WIDTH // LANES, ROW_TILE, LANES), F32)] * 2,
        compiler_params=_cparams(("parallel",)),
    )(x2d, ug, gvn, w_s, b_full, o1, l1, oc, lc, sag, mq, mk, mv, smg, w_out)


def kernel(x, mem, norm_gain, w_in, gmlp_v_gain, gmlp_w_s, gmlp_b, attn_q_gain, attn_k_gain,
           mem_norm_gain, w_mem_kv, mem_q_gain, mem_k_gain, w_out):
    batch, seq, d_model = x.shape
    depth = w_in.shape[0]
    assert DILATED_CONFIGS == ((BLOCK, 1), (BLOCK * MID, MID), (BLOCK * CLASSES, CLASSES))
    assert seq % ROW_TILE == 0 and seq % (BLOCK * CLASSES) == 0
    bd = _block_diag_ones(ATTN_WIDTH)
    bd_mem = _block_diag_ones(MEM_WIDTH)
    x2d = x.reshape(batch * seq, d_model)
    for l in range(depth):
        row = lambda g: g.reshape(1, -1).astype(F32)
        ug, gvn, q, k, v, sag, mq, smg, qc, kc, vc = _project(
            x2d, batch, seq, row(norm_gain[l]), w_in[l].astype(BF16), bd, row(gmlp_v_gain[l]),
            row(jnp.tile(attn_q_gain[l], ATTN_HEADS)), row(jnp.tile(attn_k_gain[l], ATTN_HEADS)),
            row(jnp.tile(mem_q_gain[l], MEM_HEADS)))
        mk, mv = _memory_kv(mem, row(mem_norm_gain[l]), w_mem_kv[l].astype(BF16), bd_mem,
                            row(jnp.tile(mem_k_gain[l], MEM_HEADS)))
        o1, l1 = _local_attention(q, k, v, batch, seq)
        oc, lc = _strided_attention(qc, kc, vc)
        b_full = jnp.repeat(gmlp_b[l].T, HEAD_DIM, axis=1)
        x2d = _mix(x2d, ug, gvn, gmlp_w_s[l], b_full, o1, l1, oc, lc, sag, mq, mk, mv, smg,
                   w_out[l].astype(BF16), seq)
    return x2d.reshape(batch, seq, d_model)
```

```python
import functools
import math

import jax
import jax.numpy as jnp
from jax import lax
from jax.experimental import pallas as pl
from jax.experimental.pallas import tpu as pltpu

HEAD_DIM = 64
GMLP_HEADS = 4
ATTN_HEADS = 8
MEM_HEADS = 4
GMLP_WIDTH = GMLP_HEADS * HEAD_DIM
ATTN_WIDTH = ATTN_HEADS * HEAD_DIM
MEM_WIDTH = MEM_HEADS * HEAD_DIM
CHUNK = 128
BLOCK = 128
DILATED_CONFIGS = ((128, 1), (512, 4), (2048, 16))
EPS = 1e-6
MASKED = -1e30
LN2 = math.log(2.0)
LOGIT_SCALE = math.log2(math.e) / math.sqrt(HEAD_DIM)

CLASSES = 16
MID = 4
SLABS_PER_MID = CLASSES // MID
LANES = 128
PAIRS = ATTN_WIDTH // LANES
ROW_TILE = 512
LOOP_UNROLL = 4
VMEM_LIMIT = 56 * 1024 * 1024

F32 = jnp.float32
BF16 = jnp.bfloat16


def _head_sumsq(acc, ones_bd):
    return jnp.dot((acc * acc).astype(BF16), ones_bd, preferred_element_type=F32)


def _head_rms(acc, ones_bd, gain):
    return acc * lax.rsqrt(_head_sumsq(acc, ones_bd) * (1.0 / HEAD_DIM) + EPS) * gain


def _silu(x):
    return x * jax.nn.sigmoid(x)


def _store_both_layouts(val, seq_ref, cls_ref, slab_ref):
    seq_ref[...] = val.astype(BF16)
    per_class = val.shape[0] // CLASSES
    for s in range(val.shape[1] // LANES):
        slab_ref[s] = val[:, s * LANES:(s + 1) * LANES]
    for r in range(CLASSES):
        for s in range(val.shape[1] // LANES):
            cls_ref[r, :, s * LANES:(s + 1) * LANES] = (
                slab_ref[s, pl.ds(r, per_class, stride=CLASSES), :].astype(BF16))


def _proj_kernel(x_ref, gain_ref, w_ref, bd_ref, gvg_ref, qg_ref, kg_ref, mqg_ref,
                 ug_ref, gvn_ref, q_ref, k_ref, v_ref, sag_ref, mq_ref, smg_ref,
                 qc_ref, kc_ref, vc_ref, slab_ref):
    x = x_ref[...]
    ms = jnp.mean(x * x, axis=-1, keepdims=True)
    h = (x * lax.rsqrt(ms + EPS) * gain_ref[...]).astype(BF16)

    def seg(lo, width):
        return jnp.dot(h, w_ref[:, lo:lo + width], preferred_element_type=F32)

    bd = bd_ref[...]
    bd_small = bd_ref[:GMLP_WIDTH, :GMLP_WIDTH]
    scale = LOGIT_SCALE
    o = 0
    g_u = seg(o, GMLP_WIDTH); o += GMLP_WIDTH
    g_v = seg(o, GMLP_WIDTH); o += GMLP_WIDTH
    g_gate = seg(o, GMLP_WIDTH); o += GMLP_WIDTH
    ug_ref[...] = (g_u * _silu(g_gate)).astype(BF16)
    gvn_ref[...] = _head_rms(g_v, bd_small, gvg_ref[...]).astype(BF16)
    a_q = seg(o, ATTN_WIDTH); o += ATTN_WIDTH
    _store_both_layouts(_head_rms(a_q, bd, qg_ref[...]) * scale, q_ref, qc_ref, slab_ref)
    a_k = seg(o, ATTN_WIDTH); o += ATTN_WIDTH
    _store_both_layouts(_head_rms(a_k, bd, kg_ref[...]), k_ref, kc_ref, slab_ref)
    _store_both_layouts(seg(o, ATTN_WIDTH), v_ref, vc_ref, slab_ref); o += ATTN_WIDTH
    sag_ref[...] = _silu(seg(o, ATTN_WIDTH)).astype(BF16); o += ATTN_WIDTH
    m_q = seg(o, MEM_WIDTH); o += MEM_WIDTH
    mq_ref[...] = (_head_rms(m_q, bd_small, mqg_ref[...]) * scale).astype(BF16)
    smg_ref[...] = _silu(seg(o, MEM_WIDTH)).astype(BF16)


def _memkv_kernel(mem_ref, gain_ref, w_ref, bd_ref, kg_ref, mk_ref, mv_ref):
    x = mem_ref[...]
    ms = jnp.mean(x * x, axis=-1, keepdims=True)
    h = (x * lax.rsqrt(ms + EPS) * gain_ref[...]).astype(BF16)
    mk = jnp.dot(h, w_ref[:, :MEM_WIDTH], preferred_element_type=F32)
    mv = jnp.dot(h, w_ref[:, MEM_WIDTH:], preferred_element_type=F32)
    mk_ref[...] = _head_rms(mk, bd_ref[...], kg_ref[...]).astype(BF16)
    mv_ref[...] = mv.astype(BF16)


def _pair_attention(q2, k2, v2, bias):
    rows = q2.shape[0]
    left = lax.broadcasted_iota(jnp.int32, (rows, LANES), 1) < HEAD_DIM
    zeros = jnp.zeros_like(q2)
    q_stack = jnp.concatenate([jnp.where(left, q2, zeros), jnp.where(left, zeros, q2)], axis=0)
    v_aug = jnp.concatenate([v2, jnp.ones_like(v2)], axis=1)
    s = lax.dot_general(q_stack, k2, (((1,), (1,)), ((), ())), preferred_element_type=F32)
    if bias is not None:
        s = s + bias
    m = jnp.max(s, axis=1, keepdims=True)
    p = jnp.exp2(s - m).astype(BF16)
    pv = jnp.dot(p, v_aug, preferred_element_type=F32)
    pick = lambda t: jnp.where(left, t[:rows], t[rows:])
    denom = pick(pv[:, LANES:])
    o = pick(pv[:, :LANES]) / denom
    lse = pick(jnp.broadcast_to(m, (2 * rows, LANES))) * LN2 + jnp.log(denom)
    return o, lse


def _merge_branches(o_a, lse_a, o_b, lse_b):
    m = jnp.maximum(lse_a, lse_b)
    w_a, w_b = jnp.exp(lse_a - m), jnp.exp(lse_b - m)
    tot = w_a + w_b
    return (w_a * o_a + w_b * o_b) / tot, m + jnp.log(tot)


def _band_bias(rel):
    return jnp.where((rel >= 0) & (rel <= BLOCK), 0.0, MASKED).astype(F32)


def _fill_bias_tables(bias_ref, slabs):
    row = lax.broadcasted_iota(jnp.int32, (2 * BLOCK, 2 * BLOCK), 0) % BLOCK
    col = lax.broadcasted_iota(jnp.int32, (2 * BLOCK, 2 * BLOCK), 1)
    bias_ref[0] = _band_bias(row - col)
    bias_ref[1] = _band_bias(row - col + BLOCK)
    if bias_ref.shape[0] > 2:
        q_rows, k_rows = BLOCK // slabs, 2 * BLOCK // slabs
        rel = (slabs * (row % q_rows) + row // q_rows) - (slabs * (col % k_rows) + col // k_rows)
        bias_ref[2] = _band_bias(rel)
        bias_ref[3] = _band_bias(rel + BLOCK)


def _first_grid_step():
    return (pl.program_id(0) == 0) & (pl.program_id(1) == 0)


def _local_kernel(q_ref, k_ref, v_ref, o_ref, lse_ref, bias_ref, *, blocks_per_step):
    step = pl.program_id(1)

    @pl.when(_first_grid_step())
    def _():
        _fill_bias_tables(bias_ref, 1)

    def body(j, carry):
        blk = step * blocks_per_step + j
        kstart = pl.multiple_of(jnp.maximum(blk - 1, 0) * BLOCK, BLOCK)
        generic = jnp.minimum(blk, 1)
        qstart = pl.multiple_of(j * BLOCK, BLOCK)
        for pair in range(PAIRS):
            lanes = slice(pair * LANES, (pair + 1) * LANES)
            o2, lse2 = _pair_attention(q_ref[pl.ds(qstart, BLOCK), lanes],
                                       k_ref[pl.ds(kstart, 2 * BLOCK), lanes],
                                       v_ref[pl.ds(kstart, 2 * BLOCK), lanes], bias_ref[generic])
            o_ref[pl.ds(qstart, BLOCK), lanes] = o2.astype(BF16)
            lse_ref[pl.ds(qstart, BLOCK), lanes] = lse2
        return carry

    lax.fori_loop(0, blocks_per_step, body, 0, unroll=LOOP_UNROLL)


def _strided_kernel(q_ref, k_ref, v_ref, o_ref, lse_ref, o16_ref, l16_ref, bias_ref):
    slab_len = q_ref.shape[1]

    @pl.when(_first_grid_step())
    def _():
        _fill_bias_tables(bias_ref, SLABS_PER_MID)

    def coarse(i, carry):
        g = i // (slab_len // BLOCK)
        blk = i % (slab_len // BLOCK)
        kstart = pl.multiple_of(jnp.maximum(blk - 1, 0) * BLOCK, BLOCK)
        generic = jnp.minimum(blk, 1)
        qstart = pl.multiple_of(blk * BLOCK, BLOCK)
        for pair in range(PAIRS):
            lanes = slice(pair * LANES, (pair + 1) * LANES)
            o2, lse2 = _pair_attention(q_ref[g, pl.ds(qstart, BLOCK), lanes],
                                       k_ref[g, pl.ds(kstart, 2 * BLOCK), lanes],
                                       v_ref[g, pl.ds(kstart, 2 * BLOCK), lanes], bias_ref[generic])
            o16_ref[g, pl.ds(qstart, BLOCK), lanes] = o2
            l16_ref[g, pl.ds(qstart, BLOCK), lanes] = lse2
        return carry

    lax.fori_loop(0, SLABS_PER_MID * (slab_len // BLOCK), coarse, 0, unroll=LOOP_UNROLL)

    q_rows = BLOCK // SLABS_PER_MID
    k_rows = 2 * q_rows

    def gather(ref, start, rows, lanes):
        return jnp.concatenate([ref[g, pl.ds(start, rows), lanes] for g in range(SLABS_PER_MID)], axis=0)

    def mid(i, carry):
        a0 = pl.multiple_of(i * q_rows, q_rows)
        kstart = pl.multiple_of(jnp.maximum(i - 1, 0) * q_rows, q_rows)
        generic = jnp.minimum(i, 1)
        for pair in range(PAIRS):
            lanes = slice(pair * LANES, (pair + 1) * LANES)
            o4, lse4 = _pair_attention(gather(q_ref, a0, q_rows, lanes),
                                       gather(k_ref, kstart, k_rows, lanes),
                                       gather(v_ref, kstart, k_rows, lanes), bias_ref[2 + generic])
            o, lse = _merge_branches(o4, lse4, gather(o16_ref, a0, q_rows, lanes),
                                     gather(l16_ref, a0, q_rows, lanes))
            for g in range(SLABS_PER_MID):
                o_ref[g, pl.ds(a0, q_rows), lanes] = o[g * q_rows:(g + 1) * q_rows].astype(BF16)
                lse_ref[g, pl.ds(a0, q_rows), lanes] = lse[g * q_rows:(g + 1) * q_rows]
        return carry

    lax.fori_loop(0, slab_len // q_rows, mid, 0, unroll=LOOP_UNROLL)


def _to_sequence_order(cls_ref, slab_ref):
    per_class = cls_ref.shape[1]
    n_slabs = cls_ref.shape[2] // LANES
    for r in range(CLASSES):
        for s in range(n_slabs):
            slab_ref[s, pl.ds(r, per_class, stride=CLASSES), :] = (
                cls_ref[r, :, s * LANES:(s + 1) * LANES].astype(F32))
    return jnp.concatenate([slab_ref[s] for s in range(n_slabs)], axis=1)


def _mix_kernel(x_ref, ug_ref, gvn_ref, ws_ref, bs_ref,
                o1_ref, l1_ref, oc_ref, lc_ref, sag_ref,
                mq_ref, mk_ref, mv_ref, smg_ref, wout_ref, out_ref, oslab_ref, lslab_ref):
    rows = x_ref.shape[0]
    tri = (lax.broadcasted_iota(jnp.int32, (CHUNK, CHUNK), 0)
           >= lax.broadcasted_iota(jnp.int32, (CHUNK, CHUNK), 1))
    left = lax.broadcasted_iota(jnp.int32, (CHUNK, LANES), 1) < HEAD_DIM
    w_tri = [jnp.where(tri, ws_ref[h], 0.0).astype(BF16) for h in range(GMLP_HEADS)]
    sp_chunks = []
    for c in range(rows // CHUNK):
        pairs = []
        for pair in range(GMLP_WIDTH // LANES):
            vn2 = gvn_ref[c * CHUNK:(c + 1) * CHUNK, pair * LANES:(pair + 1) * LANES]
            sp_a = jnp.dot(w_tri[2 * pair], vn2, preferred_element_type=F32)
            sp_b = jnp.dot(w_tri[2 * pair + 1], vn2, preferred_element_type=F32)
            pairs.append(jnp.where(left, sp_a, sp_b))
        sp_chunks.append(jnp.concatenate(pairs, axis=1) + bs_ref[...])
    y_g = ug_ref[...].astype(F32) * jnp.concatenate(sp_chunks, axis=0)

    o_c = _to_sequence_order(oc_ref, oslab_ref)
    l_c = _to_sequence_order(lc_ref, lslab_ref)
    mixed, _ = _merge_branches(o1_ref[...].astype(F32), l1_ref[...], o_c, l_c)
    y_a = mixed * sag_ref[...].astype(F32)

    mem_pairs = []
    for pair in range(MEM_WIDTH // LANES):
        lanes = slice(pair * LANES, (pair + 1) * LANES)
        o2, _ = _pair_attention(mq_ref[:, lanes], mk_ref[:, lanes], mv_ref[:, lanes], None)
        mem_pairs.append(o2)
    y_m = jnp.concatenate(mem_pairs, axis=1) * smg_ref[...].astype(F32)

    y = (jnp.dot(y_g.astype(BF16), wout_ref[:GMLP_WIDTH, :], preferred_element_type=F32)
         + jnp.dot(y_a.astype(BF16), wout_ref[GMLP_WIDTH:GMLP_WIDTH + ATTN_WIDTH, :],
                   preferred_element_type=F32)
         + jnp.dot(y_m.astype(BF16), wout_ref[GMLP_WIDTH + ATTN_WIDTH:, :],
                   preferred_element_type=F32))
    out_ref[...] = x_ref[...] + y


def _block_diag_ones(width):
    head = jnp.arange(width) // HEAD_DIM
    return (head[:, None] == head[None, :]).astype(BF16)


def _cparams(sem):
    return pltpu.CompilerParams(dimension_semantics=sem, vmem_limit_bytes=VMEM_LIMIT)


def _const_spec(shape):
    return pl.BlockSpec(shape, lambda *idx: (0,) * len(shape))


def _class_tile_spec(steps_per_batch, width):
    return pl.BlockSpec((None, CLASSES, ROW_TILE // CLASSES, width),
                        lambda i: (i // steps_per_batch, 0, i % steps_per_batch, 0))


def _project(x2d, batch, seq, gain, w_in, bd, gvg, qg, kg, mqg):
    rows, d_model = x2d.shape
    widths = (GMLP_WIDTH, GMLP_WIDTH, ATTN_WIDTH, ATTN_WIDTH, ATTN_WIDTH, ATTN_WIDTH, MEM_WIDTH, MEM_WIDTH)
    cls_shape = jax.ShapeDtypeStruct((batch, CLASSES, seq // CLASSES, ATTN_WIDTH), BF16)
    cls_spec = _class_tile_spec(seq // ROW_TILE, ATTN_WIDTH)
    return pl.pallas_call(
        _proj_kernel,
        grid=(rows // ROW_TILE,),
        in_specs=[pl.BlockSpec((ROW_TILE, d_model), lambda i: (i, 0)),
                  _const_spec(gain.shape), _const_spec(w_in.shape), _const_spec(bd.shape),
                  _const_spec(gvg.shape), _const_spec(qg.shape), _const_spec(kg.shape),
                  _const_spec(mqg.shape)],
        out_specs=[pl.BlockSpec((ROW_TILE, w), lambda i: (i, 0)) for w in widths] + [cls_spec] * 3,
        out_shape=[jax.ShapeDtypeStruct((rows, w), BF16) for w in widths] + [cls_shape] * 3,
        scratch_shapes=[pltpu.VMEM((ATTN_WIDTH // LANES, ROW_TILE, LANES), F32)],
        compiler_params=_cparams(("parallel",)),
    )(x2d, gain, w_in, bd, gvg, qg, kg, mqg)


def _memory_kv(mem, gain, w_kv, bd, kg):
    batch, mem_len, d_model = mem.shape
    return pl.pallas_call(
        _memkv_kernel,
        grid=(batch,),
        in_specs=[pl.BlockSpec((None, mem_len, d_model), lambda b: (b, 0, 0)),
                  _const_spec(gain.shape), _const_spec(w_kv.shape), _const_spec(bd.shape),
                  _const_spec(kg.shape)],
        out_specs=[pl.BlockSpec((None, mem_len, MEM_WIDTH), lambda b: (b, 0, 0))] * 2,
        out_shape=[jax.ShapeDtypeStruct((batch, mem_len, MEM_WIDTH), BF16)] * 2,
        compiler_params=_cparams(("parallel",)),
    )(mem, gain, w_kv, bd, kg)


def _local_attention(q, k, v, batch, seq):
    q_rows = 1024
    view = lambda t: t.reshape(batch, seq, ATTN_WIDTH)
    q_spec = pl.BlockSpec((None, q_rows, ATTN_WIDTH), lambda b, n: (b, n, 0))
    kv_spec = pl.BlockSpec((None, seq, ATTN_WIDTH), lambda b, n: (b, 0, 0))
    o, lse = pl.pallas_call(
        functools.partial(_local_kernel, blocks_per_step=q_rows // BLOCK),
        grid=(batch, seq // q_rows),
        in_specs=[q_spec, kv_spec, kv_spec],
        out_specs=[q_spec, q_spec],
        out_shape=[jax.ShapeDtypeStruct((batch, seq, ATTN_WIDTH), BF16),
                   jax.ShapeDtypeStruct((batch, seq, ATTN_WIDTH), F32)],
        scratch_shapes=[pltpu.VMEM((2, 2 * BLOCK, 2 * BLOCK), F32)],
        compiler_params=_cparams(("arbitrary", "arbitrary")),
    )(view(q), view(k), view(v))
    return o.reshape(batch * seq, ATTN_WIDTH), lse.reshape(batch * seq, ATTN_WIDTH)


def _strided_attention(qc, kc, vc):
    batch, _, slab_len, width = qc.shape
    view = lambda t: t.reshape(batch, SLABS_PER_MID, MID, slab_len, width)
    spec = pl.BlockSpec((None, SLABS_PER_MID, None, slab_len, width), lambda b, r4: (b, 0, r4, 0, 0))
    o, lse = pl.pallas_call(
        _strided_kernel,
        grid=(batch, MID),
        in_specs=[spec] * 3,
        out_specs=[spec] * 2,
        out_shape=[jax.ShapeDtypeStruct((batch, SLABS_PER_MID, MID, slab_len, width), BF16),
                   jax.ShapeDtypeStruct((batch, SLABS_PER_MID, MID, slab_len, width), F32)],
        scratch_shapes=[pltpu.VMEM((SLABS_PER_MID, slab_len, width), F32)] * 2
                       + [pltpu.VMEM((4, 2 * BLOCK, 2 * BLOCK), F32)],
        compiler_params=_cparams(("arbitrary", "arbitrary")),
    )(view(qc), view(kc), view(vc))
    return o.reshape(qc.shape), lse.reshape(qc.shape)


def _mix(x2d, ug, gvn, w_s, b_full, o1, l1, oc, lc, sag, mq, mk, mv, smg, w_out, seq):
    rows, d_model = x2d.shape
    steps_per_batch = seq // ROW_TILE
    row_spec = lambda w: pl.BlockSpec((ROW_TILE, w), lambda i: (i, 0))
    mem_spec = pl.BlockSpec((None,) + mk.shape[1:], lambda i: (i // steps_per_batch, 0, 0))
    cls_spec = _class_tile_spec(steps_per_batch, ATTN_WIDTH)
    return pl.pallas_call(
        _mix_kernel,
        grid=(rows // ROW_TILE,),
        in_specs=[row_spec(d_model), row_spec(GMLP_WIDTH), row_spec(GMLP_WIDTH),
                  _const_spec(w_s.shape), _const_spec(b_full.shape),
                  row_spec(ATTN_WIDTH), row_spec(ATTN_WIDTH), cls_spec, cls_spec,
                  row_spec(ATTN_WIDTH), row_spec(MEM_WIDTH), mem_spec, mem_spec,
                  row_spec(MEM_WIDTH), _const_spec(w_out.shape)],
        out_specs=row_spec(d_model),
        out_shape=jax.ShapeDtypeStruct((rows, d_model), F32),
        scratch_shapes=[pltpu.VMEM((ATTN_WIDTH // LANES, ROW_TILE, LANES), F32)] * 2,
        compiler_params=_cparams(("parallel",)),
    )(x2d, ug, gvn, w_s, b_full, o1, l1, oc, lc, sag, mq, mk, mv, smg, w_out)


def kernel(x, mem, norm_gain, w_in, gmlp_v_gain, gmlp_w_s, gmlp_b, attn_q_gain, attn_k_gain,
           mem_norm_gain, w_mem_kv, mem_q_gain, mem_k_gain, w_out):
    batch, seq, d_model = x.shape
    depth = w_in.shape[0]
    assert DILATED_CONFIGS == ((BLOCK, 1), (BLOCK * MID, MID), (BLOCK * CLASSES, CLASSES))
    assert seq % ROW_TILE == 0 and seq % (BLOCK * CLASSES) == 0
    bd = _block_diag_ones(ATTN_WIDTH)
    bd_mem = _block_diag_ones(MEM_WIDTH)
    x2d = x.reshape(batch * seq, d_model)
    for l in range(depth):
        row = lambda g: g.reshape(1, -1).astype(F32)
        ug, gvn, q, k, v, sag, mq, smg, qc, kc, vc = _project(
            x2d, batch, seq, row(norm_gain[l]), w_in[l].astype(BF16), bd, row(gmlp_v_gain[l]),
            row(jnp.tile(attn_q_gain[l], ATTN_HEADS)), row(jnp.tile(attn_k_gain[l], ATTN_HEADS)),
            row(jnp.tile(mem_q_gain[l], MEM_HEADS)))
        mk, mv = _memory_kv(mem, row(mem_norm_gain[l]), w_mem_kv[l].astype(BF16), bd_mem,
                            row(jnp.tile(mem_k_gain[l], MEM_HEADS)))
        o1, l1 = _local_attention(q, k, v, batch, seq)
        oc, lc = _strided_attention(qc, kc, vc)
        b_full = jnp.repeat(gmlp_b[l].T, HEAD_DIM, axis=1)
        x2d = _mix(x2d, ug, gvn, gmlp_w_s[l], b_full, o1, l1, oc, lc, sag, mq, mk, mv, smg,
                   w_out[l].astype(BF16), seq)
    return x2d.reshape(batch, seq, d_model)
```

```python
import functools
import math

import jax
import jax.numpy as jnp
from jax import lax
from jax.experimental import pallas as pl
from jax.experimental.pallas import tpu as pltpu

HEAD_DIM = 64
GMLP_HEADS = 4
ATTN_HEADS = 8
MEM_HEADS = 4
GMLP_WIDTH = GMLP_HEADS * HEAD_DIM
ATTN_WIDTH = ATTN_HEADS * HEAD_DIM
MEM_WIDTH = MEM_HEADS * HEAD_DIM
CHUNK = 128
BLOCK = 128
DILATED_CONFIGS = ((128, 1), (512, 4), (2048, 16))
EPS = 1e-6
MASKED = -1e30
LN2 = math.log(2.0)
LOGIT_SCALE = math.log2(math.e) / math.sqrt(HEAD_DIM)

IN_SEGMENTS = (("g_u", GMLP_WIDTH), ("g_v", GMLP_WIDTH), ("g_gate", GMLP_WIDTH),
               ("a_q", ATTN_WIDTH), ("a_k", ATTN_WIDTH), ("a_v", ATTN_WIDTH), ("a_gate", ATTN_WIDTH),
               ("m_q", MEM_WIDTH), ("m_gate", MEM_WIDTH))
MXU_TILE = 256

CLASSES = 16
MID = 4
SLABS_PER_MID = CLASSES // MID
LANES = 128
PAIRS = ATTN_WIDTH // LANES
ROW_TILE = 512
LOOP_UNROLL = 4
VMEM_LIMIT = 56 * 1024 * 1024

F32 = jnp.float32
BF16 = jnp.bfloat16


def _squares(acc):
    return (acc * acc).astype(BF16)


def _head_sumsq(sq, ones_bd):
    tile = ones_bd.shape[0]
    parts = [jnp.dot(sq[:, c:c + tile], ones_bd, preferred_element_type=F32)
             for c in range(0, sq.shape[1], tile)]
    return parts[0] if len(parts) == 1 else jnp.concatenate(parts, axis=1)


def _rms_scale(acc, sumsq, gain):
    return acc * lax.rsqrt(sumsq * (1.0 / HEAD_DIM) + EPS) * gain


def _head_rms(acc, ones_bd, gain):
    return _rms_scale(acc, _head_sumsq(_squares(acc), ones_bd), gain)


def _silu(x):
    return x * jax.nn.sigmoid(x)


def _store_both_layouts(val, seq_ref, cls_ref, slab_ref):
    seq_ref[...] = val.astype(BF16)
    per_class = val.shape[0] // CLASSES
    for s in range(val.shape[1] // LANES):
        slab_ref[s] = val[:, s * LANES:(s + 1) * LANES]
    for r in range(CLASSES):
        for s in range(val.shape[1] // LANES):
            cls_ref[r, :, s * LANES:(s + 1) * LANES] = (
                slab_ref[s, pl.ds(r, per_class, stride=CLASSES), :].astype(BF16))


def _proj_kernel(x_ref, gain_ref, w_ref, bd_ref, gvg_ref, qg_ref, kg_ref, mqg_ref,
                 ug_ref, gvn_ref, q_ref, k_ref, v_ref, sag_ref, mq_ref, smg_ref,
                 qc_ref, kc_ref, vc_ref, proj_even, proj_odd, h_ref, qslab_ref, kslab_ref, vslab_ref):
    i = pl.program_id(0)

    @pl.when(i == 0)
    def _():
        proj_odd[...] = jnp.zeros_like(proj_odd)

    cols, o = {}, 0
    for name, width in IN_SEGMENTS:
        cols[name] = slice(o, o + width)
        o += width

    def step(cur_ref, prev_ref):
        x = x_ref[...]
        ms = jnp.mean(x * x, axis=-1, keepdims=True)
        h_ref[...] = (x * lax.rsqrt(ms + EPS) * gain_ref[...]).astype(BF16)

        def project(name):
            cur_ref[:, cols[name]] = jnp.dot(h_ref[...], w_ref[:, cols[name]], preferred_element_type=F32)

        prev = lambda name: prev_ref[:, cols[name]]
        bd = bd_ref[...]
        sq = {name: _squares(prev(name)) for name in ("g_v", "a_q", "a_k", "m_q")}
        project("g_u")
        project("g_v")
        ug_ref[...] = (prev("g_u") * _silu(prev("g_gate"))).astype(BF16)
        project("g_gate")
        gvn_ref[...] = _rms_scale(prev("g_v"), _head_sumsq(sq["g_v"], bd), gvg_ref[...]).astype(BF16)
        project("a_q")
        _store_both_layouts(_rms_scale(prev("a_q"), _head_sumsq(sq["a_q"], bd), qg_ref[...]) * LOGIT_SCALE,
                            q_ref, qc_ref, qslab_ref)
        project("a_k")
        _store_both_layouts(_rms_scale(prev("a_k"), _head_sumsq(sq["a_k"], bd), kg_ref[...]),
                            k_ref, kc_ref, kslab_ref)
        project("a_v")
        _store_both_layouts(prev("a_v"), v_ref, vc_ref, vslab_ref)
        project("a_gate")
        sag_ref[...] = _silu(prev("a_gate")).astype(BF16)
        project("m_q")
        mq_ref[...] = (_rms_scale(prev("m_q"), _head_sumsq(sq["m_q"], bd), mqg_ref[...])
                       * LOGIT_SCALE).astype(BF16)
        project("m_gate")
        smg_ref[...] = _silu(prev("m_gate")).astype(BF16)

    @pl.when(i % 2 == 0)
    def _():
        step(proj_even, proj_odd)

    @pl.when(i % 2 == 1)
    def _():
        step(proj_odd, proj_even)


def _memkv_kernel(mem_ref, gain_ref, w_ref, bd_ref, kg_ref, mk_ref, mv_ref):
    x = mem_ref[...]
    ms = jnp.mean(x * x, axis=-1, keepdims=True)
    h = (x * lax.rsqrt(ms + EPS) * gain_ref[...]).astype(BF16)
    mk = jnp.dot(h, w_ref[:, :MEM_WIDTH], preferred_element_type=F32)
    mv = jnp.dot(h, w_ref[:, MEM_WIDTH:], preferred_element_type=F32)
    mk_ref[...] = _head_rms(mk, bd_ref[...], kg_ref[...]).astype(BF16)
    mv_ref[...] = mv.astype(BF16)


def _pair_attention(q2, k2, v2, bias):
    rows = q2.shape[0]
    left = lax.broadcasted_iota(jnp.int32, (rows, LANES), 1) < HEAD_DIM
    zeros = jnp.zeros_like(q2)
    q_stack = jnp.concatenate([jnp.where(left, q2, zeros), jnp.where(left, zeros, q2)], axis=0)
    v_aug = jnp.concatenate([v2, jnp.ones_like(v2)], axis=1)
    s = lax.dot_general(q_stack, k2, (((1,), (1,)), ((), ())), preferred_element_type=F32)
    if bias is not None:
        s = s + bias
    m = jnp.max(s, axis=1, keepdims=True)
    p = jnp.exp2(s - m).astype(BF16)
    pv = jnp.dot(p, v_aug, preferred_element_type=F32)
    pick = lambda t: jnp.where(left, t[:rows], t[rows:])
    denom = pick(pv[:, LANES:])
    o = pick(pv[:, :LANES]) / denom
    lse = pick(jnp.broadcast_to(m, (2 * rows, LANES))) * LN2 + jnp.log(denom)
    return o, lse


def _merge_branches(o_a, lse_a, o_b, lse_b):
    m = jnp.maximum(lse_a, lse_b)
    w_a, w_b = jnp.exp(lse_a - m), jnp.exp(lse_b - m)
    tot = w_a + w_b
    return (w_a * o_a + w_b * o_b) / tot, m + jnp.log(tot)


def _band_bias(rel):
    return jnp.where((rel >= 0) & (rel <= BLOCK), 0.0, MASKED).astype(F32)


def _fill_bias_tables(bias_ref, slabs):
    row = lax.broadcasted_iota(jnp.int32, (2 * BLOCK, 2 * BLOCK), 0) % BLOCK
    col = lax.broadcasted_iota(jnp.int32, (2 * BLOCK, 2 * BLOCK), 1)
    bias_ref[0] = _band_bias(row - col)
    bias_ref[1] = _band_bias(row - col + BLOCK)
    if bias_ref.shape[0] > 2:
        q_rows, k_rows = BLOCK // slabs, 2 * BLOCK // slabs
        rel = (slabs * (row % q_rows) + row // q_rows) - (slabs * (col % k_rows) + col // k_rows)
        bias_ref[2] = _band_bias(rel)
        bias_ref[3] = _band_bias(rel + BLOCK)


def _first_grid_step():
    return (pl.program_id(0) == 0) & (pl.program_id(1) == 0)


def _local_kernel(q_ref, k_ref, v_ref, o_ref, lse_ref, bias_ref, *, blocks_per_step):
    step = pl.program_id(1)

    @pl.when(_first_grid_step())
    def _():
        _fill_bias_tables(bias_ref, 1)

    def body(j, carry):
        blk = step * blocks_per_step + j
        kstart = pl.multiple_of(jnp.maximum(blk - 1, 0) * BLOCK, BLOCK)
        generic = jnp.minimum(blk, 1)
        qstart = pl.multiple_of(j * BLOCK, BLOCK)
        for pair in range(PAIRS):
            lanes = slice(pair * LANES, (pair + 1) * LANES)
            o2, lse2 = _pair_attention(q_ref[pl.ds(qstart, BLOCK), lanes],
                                       k_ref[pl.ds(kstart, 2 * BLOCK), lanes],
                                       v_ref[pl.ds(kstart, 2 * BLOCK), lanes], bias_ref[generic])
            o_ref[pl.ds(qstart, BLOCK), lanes] = o2.astype(BF16)
            lse_ref[pl.ds(qstart, BLOCK), lanes] = lse2
        return carry

    lax.fori_loop(0, blocks_per_step, body, 0, unroll=LOOP_UNROLL)


def _strided_kernel(q_ref, k_ref, v_ref, o_ref, lse_ref, o16_ref, l16_ref, bias_ref):
    slab_len = q_ref.shape[1]

    @pl.when(_first_grid_step())
    def _():
        _fill_bias_tables(bias_ref, SLABS_PER_MID)

    def coarse(i, carry):
        g = i // (slab_len // BLOCK)
        blk = i % (slab_len // BLOCK)
        kstart = pl.multiple_of(jnp.maximum(blk - 1, 0) * BLOCK, BLOCK)
        generic = jnp.minimum(blk, 1)
        qstart = pl.multiple_of(blk * BLOCK, BLOCK)
        for pair in range(PAIRS):
            lanes = slice(pair * LANES, (pair + 1) * LANES)
            o2, lse2 = _pair_attention(q_ref[g, pl.ds(qstart, BLOCK), lanes],
                                       k_ref[g, pl.ds(kstart, 2 * BLOCK), lanes],
                                       v_ref[g, pl.ds(kstart, 2 * BLOCK), lanes], bias_ref[generic])
            o16_ref[g, pl.ds(qstart, BLOCK), lanes] = o2
            l16_ref[g, pl.ds(qstart, BLOCK), lanes] = lse2
        return carry

    lax.fori_loop(0, SLABS_PER_MID * (slab_len // BLOCK), coarse, 0, unroll=LOOP_UNROLL)

    q_rows = BLOCK // SLABS_PER_MID
    k_rows = 2 * q_rows

    def gather(ref, start, rows, lanes):
        return jnp.concatenate([ref[g, pl.ds(start, rows), lanes] for g in range(SLABS_PER_MID)], axis=0)

    def mid(i, carry):
        a0 = pl.multiple_of(i * q_rows, q_rows)
        kstart = pl.multiple_of(jnp.maximum(i - 1, 0) * q_rows, q_rows)
        generic = jnp.minimum(i, 1)
        for pair in range(PAIRS):
            lanes = slice(pair * LANES, (pair + 1) * LANES)
            o4, lse4 = _pair_attention(gather(q_ref, a0, q_rows, lanes),
                                       gather(k_ref, kstart, k_rows, lanes),
                                       gather(v_ref, kstart, k_rows, lanes), bias_ref[2 + generic])
            o, lse = _merge_branches(o4, lse4, gather(o16_ref, a0, q_rows, lanes),
                                     gather(l16_ref, a0, q_rows, lanes))
            for g in range(SLABS_PER_MID):
                o_ref[g, pl.ds(a0, q_rows), lanes] = o[g * q_rows:(g + 1) * q_rows].astype(BF16)
                lse_ref[g, pl.ds(a0, q_rows), lanes] = lse[g * q_rows:(g + 1) * q_rows]
        return carry

    lax.fori_loop(0, slab_len // q_rows, mid, 0, unroll=LOOP_UNROLL)


def _to_sequence_order(cls_ref, slab_ref):
    per_class = cls_ref.shape[1]
    n_slabs = cls_ref.shape[2] // LANES
    for r in range(CLASSES):
        for s in range(n_slabs):
            slab_ref[s, pl.ds(r, per_class, stride=CLASSES), :] = (
                cls_ref[r, :, s * LANES:(s + 1) * LANES].astype(F32))
    return jnp.concatenate([slab_ref[s] for s in range(n_slabs)], axis=1)


def _mix_kernel(x_ref, ug_ref, gvn_ref, ws_ref, bs_ref,
                o1_ref, l1_ref, oc_ref, lc_ref, sag_ref,
                mq_ref, mk_ref, mv_ref, smg_ref, wout_ref, out_ref, oslab_ref, lslab_ref):
    rows = x_ref.shape[0]
    tri = (lax.broadcasted_iota(jnp.int32, (CHUNK, CHUNK), 0)
           >= lax.broadcasted_iota(jnp.int32, (CHUNK, CHUNK), 1))
    left = lax.broadcasted_iota(jnp.int32, (CHUNK, LANES), 1) < HEAD_DIM
    w_tri = [jnp.where(tri, ws_ref[h], 0.0).astype(BF16) for h in range(GMLP_HEADS)]
    sp_chunks = []
    for c in range(rows // CHUNK):
        pairs = []
        for pair in range(GMLP_WIDTH // LANES):
            vn2 = gvn_ref[c * CHUNK:(c + 1) * CHUNK, pair * LANES:(pair + 1) * LANES]
            sp_a = jnp.dot(w_tri[2 * pair], vn2, preferred_element_type=F32)
            sp_b = jnp.dot(w_tri[2 * pair + 1], vn2, preferred_element_type=F32)
            pairs.append(jnp.where(left, sp_a, sp_b))
        sp_chunks.append(jnp.concatenate(pairs, axis=1) + bs_ref[...])
    y_g = ug_ref[...].astype(F32) * jnp.concatenate(sp_chunks, axis=0)

    o_c = _to_sequence_order(oc_ref, oslab_ref)
    l_c = _to_sequence_order(lc_ref, lslab_ref)
    mixed, _ = _merge_branches(o1_ref[...].astype(F32), l1_ref[...], o_c, l_c)
    y_a = mixed * sag_ref[...].astype(F32)

    mem_pairs = []
    for pair in range(MEM_WIDTH // LANES):
        lanes = slice(pair * LANES, (pair + 1) * LANES)
        o2, _ = _pair_attention(mq_ref[:, lanes], mk_ref[:, lanes], mv_ref[:, lanes], None)
        mem_pairs.append(o2)
    y_m = jnp.concatenate(mem_pairs, axis=1) * smg_ref[...].astype(F32)

    y = (jnp.dot(y_g.astype(BF16), wout_ref[:GMLP_WIDTH, :], preferred_element_type=F32)
         + jnp.dot(y_a.astype(BF16), wout_ref[GMLP_WIDTH:GMLP_WIDTH + ATTN_WIDTH, :],
                   preferred_element_type=F32)
         + jnp.dot(y_m.astype(BF16), wout_ref[GMLP_WIDTH + ATTN_WIDTH:, :],
                   preferred_element_type=F32))
    out_ref[...] = x_ref[...] + y


def _block_diag_ones(width):
    head = jnp.arange(width) // HEAD_DIM
    return (head[:, None] == head[None, :]).astype(BF16)


def _cparams(sem):
    return pltpu.CompilerParams(dimension_semantics=sem, vmem_limit_bytes=VMEM_LIMIT)


def _const_spec(shape):
    return pl.BlockSpec(shape, lambda *idx: (0,) * len(shape))


def _class_tile_spec(steps_per_batch, width):
    return pl.BlockSpec((None, CLASSES, ROW_TILE // CLASSES, width),
                        lambda i: (i // steps_per_batch, 0, i % steps_per_batch, 0))


def _project(x2d, batch, seq, gain, w_in, bd, gvg, qg, kg, mqg):
    rows, d_model = x2d.shape
    widths = (GMLP_WIDTH, GMLP_WIDTH, ATTN_WIDTH, ATTN_WIDTH, ATTN_WIDTH, ATTN_WIDTH, MEM_WIDTH, MEM_WIDTH)
    cls_shape = jax.ShapeDtypeStruct((batch, CLASSES, seq // CLASSES, ATTN_WIDTH), BF16)
    tiles = rows // ROW_TILE
    steps_per_batch = seq // ROW_TILE
    done = lambda i: jnp.maximum(i - 1, 0)
    cls_spec = pl.BlockSpec((None, CLASSES, ROW_TILE // CLASSES, ATTN_WIDTH),
                            lambda i: (done(i) // steps_per_batch, 0, done(i) % steps_per_batch, 0))
    slab = pltpu.VMEM((ATTN_WIDTH // LANES, ROW_TILE, LANES), F32)
    raw = pltpu.VMEM((ROW_TILE, w_in.shape[1]), F32)
    return pl.pallas_call(
        _proj_kernel,
        grid=(tiles + 1,),
        in_specs=[pl.BlockSpec((ROW_TILE, d_model), lambda i: (jnp.minimum(i, tiles - 1), 0)),
                  _const_spec(gain.shape), _const_spec(w_in.shape), _const_spec(bd.shape),
                  _const_spec(gvg.shape), _const_spec(qg.shape), _const_spec(kg.shape),
                  _const_spec(mqg.shape)],
        out_specs=[pl.BlockSpec((ROW_TILE, w), lambda i: (done(i), 0)) for w in widths] + [cls_spec] * 3,
        out_shape=[jax.ShapeDtypeStruct((rows, w), BF16) for w in widths] + [cls_shape] * 3,
        scratch_shapes=[raw, raw, pltpu.VMEM((ROW_TILE, d_model), BF16), slab, slab, slab],
        compiler_params=_cparams(("arbitrary",)),
    )(x2d, gain, w_in, bd, gvg, qg, kg, mqg)


def _memory_kv(mem, gain, w_kv, bd, kg):
    batch, mem_len, d_model = mem.shape
    return pl.pallas_call(
        _memkv_kernel,
        grid=(batch,),
        in_specs=[pl.BlockSpec((None, mem_len, d_model), lambda b: (b, 0, 0)),
                  _const_spec(gain.shape), _const_spec(w_kv.shape), _const_spec(bd.shape),
                  _const_spec(kg.shape)],
        out_specs=[pl.BlockSpec((None, mem_len, MEM_WIDTH), lambda b: (b, 0, 0))] * 2,
        out_shape=[jax.ShapeDtypeStruct((batch, mem_len, MEM_WIDTH), BF16)] * 2,
        compiler_params=_cparams(("parallel",)),
    )(mem, gain, w_kv, bd, kg)


def _local_attention(q, k, v, batch, seq):
    q_rows = 1024
    view = lambda t: t.reshape(batch, seq, ATTN_WIDTH)
    q_spec = pl.BlockSpec((None, q_rows, ATTN_WIDTH), lambda b, n: (b, n, 0))
    kv_spec = pl.BlockSpec((None, seq, ATTN_WIDTH), lambda b, n: (b, 0, 0))
    o, lse = pl.pallas_call(
        functools.partial(_local_kernel, blocks_per_step=q_rows // BLOCK),
        grid=(batch, seq // q_rows),
        in_specs=[q_spec, kv_spec, kv_spec],
        out_specs=[q_spec, q_spec],
        out_shape=[jax.ShapeDtypeStruct((batch, seq, ATTN_WIDTH), BF16),
                   jax.ShapeDtypeStruct((batch, seq, ATTN_WIDTH), F32)],
        scratch_shapes=[pltpu.VMEM((2, 2 * BLOCK, 2 * BLOCK), F32)],
        compiler_params=_cparams(("arbitrary", "arbitrary")),
    )(view(q), view(k), view(v))
    return o.reshape(batch * seq, ATTN_WIDTH), lse.reshape(batch * seq, ATTN_WIDTH)


def _strided_attention(qc, kc, vc):
    batch, _, slab_len, width = qc.shape
    view = lambda t: t.reshape(batch, SLABS_PER_MID, MID, slab_len, width)
    spec = pl.BlockSpec((None, SLABS_PER_MID, None, slab_len, width), lambda b, r4: (b, 0, r4, 0, 0))
    o, lse = pl.pallas_call(
        _strided_kernel,
        grid=(batch, MID),
        in_specs=[spec] * 3,
        out_specs=[spec] * 2,
        out_shape=[jax.ShapeDtypeStruct((batch, SLABS_PER_MID, MID, slab_len, width), BF16),
                   jax.ShapeDtypeStruct((batch, SLABS_PER_MID, MID, slab_len, width), F32)],
        scratch_shapes=[pltpu.VMEM((SLABS_PER_MID, slab_len, width), F32)] * 2
                       + [pltpu.VMEM((4, 2 * BLOCK, 2 * BLOCK), F32)],
        compiler_params=_cparams(("arbitrary", "arbitrary")),
    )(view(qc), view(kc), view(vc))
    return o.reshape(qc.shape), lse.reshape(qc.shape)


def _mix(x2d, ug, gvn, w_s, b_full, o1, l1, oc, lc, sag, mq, mk, mv, smg, w_out, seq):
    rows, d_model = x2d.shape
    steps_per_batch = seq // ROW_TILE
    row_spec = lambda w: pl.BlockSpec((ROW_TILE, w), lambda i: (i, 0))
    mem_spec = pl.BlockSpec((None,) + mk.shape[1:], lambda i: (i // steps_per_batch, 0, 0))
    cls_spec = _class_tile_spec(steps_per_batch, ATTN_WIDTH)
    return pl.pallas_call(
        _mix_kernel,
        grid=(rows // ROW_TILE,),
        in_specs=[row_spec(d_model), row_spec(GMLP_WIDTH), row_spec(GMLP_WIDTH),
                  _const_spec(w_s.shape), _const_spec(b_full.shape),
                  row_spec(ATTN_WIDTH), row_spec(ATTN_WIDTH), cls_spec, cls_spec,
                  row_spec(ATTN_WIDTH), row_spec(MEM_WIDTH), mem_spec, mem_spec,
                  row_spec(MEM_WIDTH), _const_spec(w_out.shape)],
        out_specs=row_spec(d_model),
        out_shape=jax.ShapeDtypeStruct((rows, d_model), F32),
        scratch_shapes=[pltpu.VMEM((ATTN_WIDTH // LANES, ROW_TILE, LANES), F32)] * 2,
        compiler_params=_cparams(("parallel",)),
    )(x2d, ug, gvn, w_s, b_full, o1, l1, oc, lc, sag, mq, mk, mv, smg, w_out)


def kernel(x, mem, norm_gain, w_in, gmlp_v_gain, gmlp_w_s, gmlp_b, attn_q_gain, attn_k_gain,
           mem_norm_gain, w_mem_kv, mem_q_gain, mem_k_gain, w_out):
    batch, seq, d_model = x.shape
    depth = w_in.shape[0]
    assert DILATED_CONFIGS == ((BLOCK, 1), (BLOCK * MID, MID), (BLOCK * CLASSES, CLASSES))
    assert seq % ROW_TILE == 0 and seq % (BLOCK * CLASSES) == 0
    bd = _block_diag_ones(MXU_TILE)
    x2d = x.reshape(batch * seq, d_model)
    for l in range(depth):
        row = lambda g: g.reshape(1, -1).astype(F32)
        ug, gvn, q, k, v, sag, mq, smg, qc, kc, vc = _project(
            x2d, batch, seq, row(norm_gain[l]), w_in[l].astype(BF16), bd, row(gmlp_v_gain[l]),
            row(jnp.tile(attn_q_gain[l], ATTN_HEADS)), row(jnp.tile(attn_k_gain[l], ATTN_HEADS)),
            row(jnp.tile(mem_q_gain[l], MEM_HEADS)))
        mk, mv = _memory_kv(mem, row(mem_norm_gain[l]), w_mem_kv[l].astype(BF16), bd,
                            row(jnp.tile(mem_k_gain[l], MEM_HEADS)))
        o1, l1 = _local_attention(q, k, v, batch, seq)
        oc, lc = _strided_attention(qc, kc, vc)
        b_full = jnp.repeat(gmlp_b[l].T, HEAD_DIM, axis=1)
        x2d = _mix(x2d, ug, gvn, gmlp_w_s[l], b_full, o1, l1, oc, lc, sag, mq, mk, mv, smg,
                   w_out[l].astype(BF16), seq)
    return x2d.reshape(batch, seq, d_model)
```

```python
import functools
import math

import jax
import jax.numpy as jnp
from jax import lax
from jax.experimental import pallas as pl
from jax.experimental.pallas import tpu as pltpu

HEAD_DIM = 64
GMLP_HEADS = 4
ATTN_HEADS = 8
MEM_HEADS = 4
GMLP_WIDTH = GMLP_HEADS * HEAD_DIM
ATTN_WIDTH = ATTN_HEADS * HEAD_DIM
MEM_WIDTH = MEM_HEADS * HEAD_DIM
CHUNK = 128
BLOCK = 128
DILATED_CONFIGS = ((128, 1), (512, 4), (2048, 16))
EPS = 1e-6
MASKED = -1e30
LN2 = math.log(2.0)
LOGIT_SCALE = math.log2(math.e) / math.sqrt(HEAD_DIM)

IN_SEGMENTS = (("g_u", GMLP_WIDTH), ("g_v", GMLP_WIDTH), ("g_gate", GMLP_WIDTH),
               ("a_q", ATTN_WIDTH), ("a_k", ATTN_WIDTH), ("a_v", ATTN_WIDTH), ("a_gate", ATTN_WIDTH),
               ("m_q", MEM_WIDTH), ("m_gate", MEM_WIDTH))
MXU_TILE = 256

CLASSES = 16
MID = 4
SLABS_PER_MID = CLASSES // MID
LANES = 128
PAIRS = ATTN_WIDTH // LANES
ROW_TILE = 512
LOOP_UNROLL = 4
VMEM_LIMIT = 56 * 1024 * 1024

F32 = jnp.float32
BF16 = jnp.bfloat16


def _squares(acc):
    return (acc * acc).astype(BF16)


def _head_sumsq(sq, ones_bd):
    tile = ones_bd.shape[0]
    parts = [jnp.dot(sq[:, c:c + tile], ones_bd, preferred_element_type=F32)
             for c in range(0, sq.shape[1], tile)]
    return parts[0] if len(parts) == 1 else jnp.concatenate(parts, axis=1)


def _rms_scale(acc, sumsq, gain):
    return acc * lax.rsqrt(sumsq * (1.0 / HEAD_DIM) + EPS) * gain


def _head_rms(acc, ones_bd, gain):
    return _rms_scale(acc, _head_sumsq(_squares(acc), ones_bd), gain)


def _silu(x):
    return x * jax.nn.sigmoid(x)


def _store_both_layouts(val, seq_ref, cls_ref, slab_ref, mid_ref):
    seq_ref[...] = val.astype(BF16)
    rows = val.shape[0]
    per_mid, per_class = rows // MID, rows // CLASSES
    for s in range(val.shape[1] // LANES):
        lanes = slice(s * LANES, (s + 1) * LANES)
        slab_ref[s] = val[:, lanes]
        for r4 in range(MID):
            mid_ref[s, r4 * per_mid:(r4 + 1) * per_mid] = slab_ref[s, pl.ds(r4, per_mid, stride=MID), :]
        for r4 in range(MID):
            for g in range(SLABS_PER_MID):
                cls_ref[r4 + MID * g, :, lanes] = (
                    mid_ref[s, pl.ds(r4 * per_mid + g, per_class, stride=SLABS_PER_MID), :].astype(BF16))


def _proj_kernel(x_ref, gain_ref, w_ref, bd_ref, gvg_ref, qg_ref, kg_ref, mqg_ref,
                 ug_ref, gvn_ref, q_ref, k_ref, v_ref, sag_ref, mq_ref, smg_ref,
                 qc_ref, kc_ref, vc_ref, proj_even, proj_odd, h_ref, qslab_ref, kslab_ref, vslab_ref):
    i = pl.program_id(0)

    @pl.when(i == 0)
    def _():
        proj_odd[...] = jnp.zeros_like(proj_odd)

    cols, o = {}, 0
    for name, width in IN_SEGMENTS:
        cols[name] = slice(o, o + width)
        o += width

    def step(cur_ref, prev_ref):
        x = x_ref[...]
        ms = jnp.mean(x * x, axis=-1, keepdims=True)
        h_ref[...] = (x * lax.rsqrt(ms + EPS) * gain_ref[...]).astype(BF16)

        def project(name):
            cur_ref[:, cols[name]] = jnp.dot(h_ref[...], w_ref[:, cols[name]], preferred_element_type=F32)

        prev = lambda name: prev_ref[:, cols[name]]
        bd = bd_ref[...]
        sq = {name: _squares(prev(name)) for name in ("g_v", "a_q", "a_k", "m_q")}
        project("g_u")
        project("g_v")
        ug_ref[...] = (prev("g_u") * _silu(prev("g_gate"))).astype(BF16)
        project("g_gate")
        gvn_ref[...] = _rms_scale(prev("g_v"), _head_sumsq(sq["g_v"], bd), gvg_ref[...]).astype(BF16)
        project("a_q")
        _store_both_layouts(_rms_scale(prev("a_q"), _head_sumsq(sq["a_q"], bd), qg_ref[...]) * LOGIT_SCALE,
                            q_ref, qc_ref, qslab_ref.at[0], qslab_ref.at[1])
        project("a_k")
        _store_both_layouts(_rms_scale(prev("a_k"), _head_sumsq(sq["a_k"], bd), kg_ref[...]),
                            k_ref, kc_ref, kslab_ref.at[0], kslab_ref.at[1])
        project("a_v")
        _store_both_layouts(prev("a_v"), v_ref, vc_ref, vslab_ref.at[0], vslab_ref.at[1])
        project("a_gate")
        sag_ref[...] = _silu(prev("a_gate")).astype(BF16)
        project("m_q")
        mq_ref[...] = (_rms_scale(prev("m_q"), _head_sumsq(sq["m_q"], bd), mqg_ref[...])
                       * LOGIT_SCALE).astype(BF16)
        project("m_gate")
        smg_ref[...] = _silu(prev("m_gate")).astype(BF16)

    @pl.when(i % 2 == 0)
    def _():
        step(proj_even, proj_odd)

    @pl.when(i % 2 == 1)
    def _():
        step(proj_odd, proj_even)


def _memkv_kernel(mem_ref, gain_ref, w_ref, bd_ref, kg_ref, mk_ref, mv_ref):
    x = mem_ref[...]
    ms = jnp.mean(x * x, axis=-1, keepdims=True)
    h = (x * lax.rsqrt(ms + EPS) * gain_ref[...]).astype(BF16)
    mk = jnp.dot(h, w_ref[:, :MEM_WIDTH], preferred_element_type=F32)
    mv = jnp.dot(h, w_ref[:, MEM_WIDTH:], preferred_element_type=F32)
    mk_ref[...] = _head_rms(mk, bd_ref[...], kg_ref[...]).astype(BF16)
    mv_ref[...] = mv.astype(BF16)


def _pair_attention(q2, k2, v2, bias):
    rows = q2.shape[0]
    left = lax.broadcasted_iota(jnp.int32, (rows, LANES), 1) < HEAD_DIM
    zeros = jnp.zeros_like(q2)
    q_stack = jnp.concatenate([jnp.where(left, q2, zeros), jnp.where(left, zeros, q2)], axis=0)
    v_aug = jnp.concatenate([v2, jnp.ones_like(v2)], axis=1)
    s = lax.dot_general(q_stack, k2, (((1,), (1,)), ((), ())), preferred_element_type=F32)
    if bias is not None:
        s = s + bias
    m = jnp.max(s, axis=1, keepdims=True)
    p = jnp.exp2(s - m).astype(BF16)
    pv = jnp.dot(p, v_aug, preferred_element_type=F32)
    pick = lambda t: jnp.where(left, t[:rows], t[rows:])
    denom = pick(pv[:, LANES:])
    o = pick(pv[:, :LANES]) / denom
    lse = pick(jnp.broadcast_to(m, (2 * rows, LANES))) * LN2 + jnp.log(denom)
    return o, lse


def _merge_branches(o_a, lse_a, o_b, lse_b):
    m = jnp.maximum(lse_a, lse_b)
    w_a, w_b = jnp.exp(lse_a - m), jnp.exp(lse_b - m)
    tot = w_a + w_b
    return (w_a * o_a + w_b * o_b) / tot, m + jnp.log(tot)


def _band_bias(rel):
    return jnp.where((rel >= 0) & (rel <= BLOCK), 0.0, MASKED).astype(F32)


def _fill_bias_tables(bias_ref, slabs):
    row = lax.broadcasted_iota(jnp.int32, (2 * BLOCK, 2 * BLOCK), 0) % BLOCK
    col = lax.broadcasted_iota(jnp.int32, (2 * BLOCK, 2 * BLOCK), 1)
    bias_ref[0] = _band_bias(row - col)
    bias_ref[1] = _band_bias(row - col + BLOCK)
    if bias_ref.shape[0] > 2:
        q_rows, k_rows = BLOCK // slabs, 2 * BLOCK // slabs
        rel = (slabs * (row % q_rows) + row // q_rows) - (slabs * (col % k_rows) + col // k_rows)
        bias_ref[2] = _band_bias(rel)
        bias_ref[3] = _band_bias(rel + BLOCK)


def _first_grid_step():
    return (pl.program_id(0) == 0) & (pl.program_id(1) == 0)


def _local_kernel(q_ref, k_ref, v_ref, o_ref, lse_ref, bias_ref, *, blocks_per_step):
    step = pl.program_id(1)

    @pl.when(_first_grid_step())
    def _():
        _fill_bias_tables(bias_ref, 1)

    def body(j, carry):
        blk = step * blocks_per_step + j
        kstart = pl.multiple_of(jnp.maximum(blk - 1, 0) * BLOCK, BLOCK)
        generic = jnp.minimum(blk, 1)
        qstart = pl.multiple_of(j * BLOCK, BLOCK)
        for pair in range(PAIRS):
            lanes = slice(pair * LANES, (pair + 1) * LANES)
            o2, lse2 = _pair_attention(q_ref[pl.ds(qstart, BLOCK), lanes],
                                       k_ref[pl.ds(kstart, 2 * BLOCK), lanes],
                                       v_ref[pl.ds(kstart, 2 * BLOCK), lanes], bias_ref[generic])
            o_ref[pl.ds(qstart, BLOCK), lanes] = o2.astype(BF16)
            lse_ref[pl.ds(qstart, BLOCK), lanes] = lse2
        return carry

    lax.fori_loop(0, blocks_per_step, body, 0, unroll=LOOP_UNROLL)


def _strided_kernel(q_ref, k_ref, v_ref, o_ref, lse_ref, o16_ref, l16_ref, bias_ref):
    slab_len = q_ref.shape[1]

    @pl.when(_first_grid_step())
    def _():
        _fill_bias_tables(bias_ref, SLABS_PER_MID)

    def coarse(i, carry):
        g = i // (slab_len // BLOCK)
        blk = i % (slab_len // BLOCK)
        kstart = pl.multiple_of(jnp.maximum(blk - 1, 0) * BLOCK, BLOCK)
        generic = jnp.minimum(blk, 1)
        qstart = pl.multiple_of(blk * BLOCK, BLOCK)
        for pair in range(PAIRS):
            lanes = slice(pair * LANES, (pair + 1) * LANES)
            o2, lse2 = _pair_attention(q_ref[g, pl.ds(qstart, BLOCK), lanes],
                                       k_ref[g, pl.ds(kstart, 2 * BLOCK), lanes],
                                       v_ref[g, pl.ds(kstart, 2 * BLOCK), lanes], bias_ref[generic])
            o16_ref[g, pl.ds(qstart, BLOCK), lanes] = o2
            l16_ref[g, pl.ds(qstart, BLOCK), lanes] = lse2
        return carry

    lax.fori_loop(0, SLABS_PER_MID * (slab_len // BLOCK), coarse, 0, unroll=LOOP_UNROLL)

    q_rows = BLOCK // SLABS_PER_MID
    k_rows = 2 * q_rows

    def gather(ref, start, rows, lanes):
        return jnp.concatenate([ref[g, pl.ds(start, rows), lanes] for g in range(SLABS_PER_MID)], axis=0)

    def mid(i, carry):
        a0 = pl.multiple_of(i * q_rows, q_rows)
        kstart = pl.multiple_of(jnp.maximum(i - 1, 0) * q_rows, q_rows)
        generic = jnp.minimum(i, 1)
        for pair in range(PAIRS):
            lanes = slice(pair * LANES, (pair + 1) * LANES)
            o4, lse4 = _pair_attention(gather(q_ref, a0, q_rows, lanes),
                                       gather(k_ref, kstart, k_rows, lanes),
                                       gather(v_ref, kstart, k_rows, lanes), bias_ref[2 + generic])
            o, lse = _merge_branches(o4, lse4, gather(o16_ref, a0, q_rows, lanes),
                                     gather(l16_ref, a0, q_rows, lanes))
            for g in range(SLABS_PER_MID):
                o_ref[g, pl.ds(a0, q_rows), lanes] = o[g * q_rows:(g + 1) * q_rows].astype(BF16)
                lse_ref[g, pl.ds(a0, q_rows), lanes] = lse[g * q_rows:(g + 1) * q_rows]
        return carry

    lax.fori_loop(0, slab_len // q_rows, mid, 0, unroll=LOOP_UNROLL)


def _to_sequence_order(cls_ref, slab_ref, mid_ref):
    per_class = cls_ref.shape[1]
    per_mid = per_class * SLABS_PER_MID
    n_slabs = cls_ref.shape[2] // LANES
    for s in range(n_slabs):
        lanes = slice(s * LANES, (s + 1) * LANES)
        for r4 in range(MID):
            for g in range(SLABS_PER_MID):
                mid_ref[s, pl.ds(r4 * per_mid + g, per_class, stride=SLABS_PER_MID), :] = (
                    cls_ref[r4 + MID * g, :, lanes].astype(F32))
        for r4 in range(MID):
            slab_ref[s, pl.ds(r4, per_mid, stride=MID), :] = mid_ref[s, r4 * per_mid:(r4 + 1) * per_mid]
    return jnp.concatenate([slab_ref[s] for s in range(n_slabs)], axis=1)


def _mix_kernel(x_ref, ug_ref, gvn_ref, ws_ref, bs_ref,
                o1_ref, l1_ref, oc_ref, lc_ref, sag_ref,
                mq_ref, mk_ref, mv_ref, smg_ref, wout_ref, out_ref, oslab_ref, lslab_ref):
    rows = x_ref.shape[0]
    tri = (lax.broadcasted_iota(jnp.int32, (CHUNK, CHUNK), 0)
           >= lax.broadcasted_iota(jnp.int32, (CHUNK, CHUNK), 1))
    left = lax.broadcasted_iota(jnp.int32, (CHUNK, LANES), 1) < HEAD_DIM
    w_tri = [jnp.where(tri, ws_ref[h], 0.0).astype(BF16) for h in range(GMLP_HEADS)]
    sp_chunks = []
    for c in range(rows // CHUNK):
        pairs = []
        for pair in range(GMLP_WIDTH // LANES):
            vn2 = gvn_ref[c * CHUNK:(c + 1) * CHUNK, pair * LANES:(pair + 1) * LANES]
            sp_a = jnp.dot(w_tri[2 * pair], vn2, preferred_element_type=F32)
            sp_b = jnp.dot(w_tri[2 * pair + 1], vn2, preferred_element_type=F32)
            pairs.append(jnp.where(left, sp_a, sp_b))
        sp_chunks.append(jnp.concatenate(pairs, axis=1) + bs_ref[...])
    y_g = ug_ref[...].astype(F32) * jnp.concatenate(sp_chunks, axis=0)

    o_c = _to_sequence_order(oc_ref, oslab_ref.at[0], oslab_ref.at[1])
    l_c = _to_sequence_order(lc_ref, lslab_ref.at[0], lslab_ref.at[1])
    mixed, _ = _merge_branches(o1_ref[...].astype(F32), l1_ref[...], o_c, l_c)
    y_a = mixed * sag_ref[...].astype(F32)

    mem_pairs = []
    for pair in range(MEM_WIDTH // LANES):
        lanes = slice(pair * LANES, (pair + 1) * LANES)
        o2, _ = _pair_attention(mq_ref[:, lanes], mk_ref[:, lanes], mv_ref[:, lanes], None)
        mem_pairs.append(o2)
    y_m = jnp.concatenate(mem_pairs, axis=1) * smg_ref[...].astype(F32)

    y = (jnp.dot(y_g.astype(BF16), wout_ref[:GMLP_WIDTH, :], preferred_element_type=F32)
         + jnp.dot(y_a.astype(BF16), wout_ref[GMLP_WIDTH:GMLP_WIDTH + ATTN_WIDTH, :],
                   preferred_element_type=F32)
         + jnp.dot(y_m.astype(BF16), wout_ref[GMLP_WIDTH + ATTN_WIDTH:, :],
                   preferred_element_type=F32))
    out_ref[...] = x_ref[...] + y


def _block_diag_ones(width):
    head = jnp.arange(width) // HEAD_DIM
    return (head[:, None] == head[None, :]).astype(BF16)


def _cparams(sem):
    return pltpu.CompilerParams(dimension_semantics=sem, vmem_limit_bytes=VMEM_LIMIT)


def _const_spec(shape):
    return pl.BlockSpec(shape, lambda *idx: (0,) * len(shape))


def _regroup_scratch():
    return pltpu.VMEM((2, ATTN_WIDTH // LANES, ROW_TILE, LANES), F32)


def _class_tile_spec(steps_per_batch, width):
    return pl.BlockSpec((None, CLASSES, ROW_TILE // CLASSES, width),
                        lambda i: (i // steps_per_batch, 0, i % steps_per_batch, 0))


def _project(x2d, batch, seq, gain, w_in, bd, gvg, qg, kg, mqg):
    rows, d_model = x2d.shape
    widths = (GMLP_WIDTH, GMLP_WIDTH, ATTN_WIDTH, ATTN_WIDTH, ATTN_WIDTH, ATTN_WIDTH, MEM_WIDTH, MEM_WIDTH)
    cls_shape = jax.ShapeDtypeStruct((batch, CLASSES, seq // CLASSES, ATTN_WIDTH), BF16)
    tiles = rows // ROW_TILE
    steps_per_batch = seq // ROW_TILE
    done = lambda i: jnp.maximum(i - 1, 0)
    cls_spec = pl.BlockSpec((None, CLASSES, ROW_TILE // CLASSES, ATTN_WIDTH),
                            lambda i: (done(i) // steps_per_batch, 0, done(i) % steps_per_batch, 0))
    slab = _regroup_scratch()
    raw = pltpu.VMEM((ROW_TILE, w_in.shape[1]), F32)
    return pl.pallas_call(
        _proj_kernel,
        grid=(tiles + 1,),
        in_specs=[pl.BlockSpec((ROW_TILE, d_model), lambda i: (jnp.minimum(i, tiles - 1), 0)),
                  _const_spec(gain.shape), _const_spec(w_in.shape), _const_spec(bd.shape),
                  _const_spec(gvg.shape), _const_spec(qg.shape), _const_spec(kg.shape),
                  _const_spec(mqg.shape)],
        out_specs=[pl.BlockSpec((ROW_TILE, w), lambda i: (done(i), 0)) for w in widths] + [cls_spec] * 3,
        out_shape=[jax.ShapeDtypeStruct((rows, w), BF16) for w in widths] + [cls_shape] * 3,
        scratch_shapes=[raw, raw, pltpu.VMEM((ROW_TILE, d_model), BF16), slab, slab, slab],
        compiler_params=_cparams(("arbitrary",)),
    )(x2d, gain, w_in, bd, gvg, qg, kg, mqg)


def _memory_kv(mem, gain, w_kv, bd, kg):
    batch, mem_len, d_model = mem.shape
    return pl.pallas_call(
        _memkv_kernel,
        grid=(batch,),
        in_specs=[pl.BlockSpec((None, mem_len, d_model), lambda b: (b, 0, 0)),
                  _const_spec(gain.shape), _const_spec(w_kv.shape), _const_spec(bd.shape),
                  _const_spec(kg.shape)],
        out_specs=[pl.BlockSpec((None, mem_len, MEM_WIDTH), lambda b: (b, 0, 0))] * 2,
        out_shape=[jax.ShapeDtypeStruct((batch, mem_len, MEM_WIDTH), BF16)] * 2,
        compiler_params=_cparams(("parallel",)),
    )(mem, gain, w_kv, bd, kg)


def _local_attention(q, k, v, batch, seq):
    q_rows = 1024
    view = lambda t: t.reshape(batch, seq, ATTN_WIDTH)
    q_spec = pl.BlockSpec((None, q_rows, ATTN_WIDTH), lambda b, n: (b, n, 0))
    kv_spec = pl.BlockSpec((None, seq, ATTN_WIDTH), lambda b, n: (b, 0, 0))
    o, lse = pl.pallas_call(
        functools.partial(_local_kernel, blocks_per_step=q_rows // BLOCK),
        grid=(batch, seq // q_rows),
        in_specs=[q_spec, kv_spec, kv_spec],
        out_specs=[q_spec, q_spec],
        out_shape=[jax.ShapeDtypeStruct((batch, seq, ATTN_WIDTH), BF16),
                   jax.ShapeDtypeStruct((batch, seq, ATTN_WIDTH), F32)],
        scratch_shapes=[pltpu.VMEM((2, 2 * BLOCK, 2 * BLOCK), F32)],
        compiler_params=_cparams(("arbitrary", "arbitrary")),
    )(view(q), view(k), view(v))
    return o.reshape(batch * seq, ATTN_WIDTH), lse.reshape(batch * seq, ATTN_WIDTH)


def _strided_attention(qc, kc, vc):
    batch, _, slab_len, width = qc.shape
    view = lambda t: t.reshape(batch, SLABS_PER_MID, MID, slab_len, width)
    spec = pl.BlockSpec((None, SLABS_PER_MID, None, slab_len, width), lambda b, r4: (b, 0, r4, 0, 0))
    o, lse = pl.pallas_call(
        _strided_kernel,
        grid=(batch, MID),
        in_specs=[spec] * 3,
        out_specs=[spec] * 2,
        out_shape=[jax.ShapeDtypeStruct((batch, SLABS_PER_MID, MID, slab_len, width), BF16),
                   jax.ShapeDtypeStruct((batch, SLABS_PER_MID, MID, slab_len, width), F32)],
        scratch_shapes=[pltpu.VMEM((SLABS_PER_MID, slab_len, width), F32)] * 2
                       + [pltpu.VMEM((4, 2 * BLOCK, 2 * BLOCK), F32)],
        compiler_params=_cparams(("arbitrary", "arbitrary")),
    )(view(qc), view(kc), view(vc))
    return o.reshape(qc.shape), lse.reshape(qc.shape)


def _mix(x2d, ug, gvn, w_s, b_full, o1, l1, oc, lc, sag, mq, mk, mv, smg, w_out, seq):
    rows, d_model = x2d.shape
    steps_per_batch = seq // ROW_TILE
    row_spec = lambda w: pl.BlockSpec((ROW_TILE, w), lambda i: (i, 0))
    mem_spec = pl.BlockSpec((None,) + mk.shape[1:], lambda i: (i // steps_per_batch, 0, 0))
    cls_spec = _class_tile_spec(steps_per_batch, ATTN_WIDTH)
    return pl.pallas_call(
        _mix_kernel,
        grid=(rows // ROW_TILE,),
        in_specs=[row_spec(d_model), row_spec(GMLP_WIDTH), row_spec(GMLP_WIDTH),
                  _const_spec(w_s.shape), _const_spec(b_full.shape),
                  row_spec(ATTN_WIDTH), row_spec(ATTN_WIDTH), cls_spec, cls_spec,
                  row_spec(ATTN_WIDTH), row_spec(MEM_WIDTH), mem_spec, mem_spec,
                  row_spec(MEM_WIDTH), _const_spec(w_out.shape)],
        out_specs=row_spec(d_model),
        out_shape=jax.ShapeDtypeStruct((rows, d_model), F32),
        scratch_shapes=[_regroup_scratch()] * 2,
        compiler_params=_cparams(("parallel",)),
    )(x2d, ug, gvn, w_s, b_full, o1, l1, oc, lc, sag, mq, mk, mv, smg, w_out)


def kernel(x, mem, norm_gain, w_in, gmlp_v_gain, gmlp_w_s, gmlp_b, attn_q_gain, attn_k_gain,
           mem_norm_gain, w_mem_kv, mem_q_gain, mem_k_gain, w_out):
    batch, seq, d_model = x.shape
    depth = w_in.shape[0]
    assert DILATED_CONFIGS == ((BLOCK, 1), (BLOCK * MID, MID), (BLOCK * CLASSES, CLASSES))
    assert seq % ROW_TILE == 0 and seq % (BLOCK * CLASSES) == 0
    bd = _block_diag_ones(MXU_TILE)
    x2d = x.reshape(batch * seq, d_model)
    for l in range(depth):
        row = lambda g: g.reshape(1, -1).astype(F32)
        ug, gvn, q, k, v, sag, mq, smg, qc, kc, vc = _project(
            x2d, batch, seq, row(norm_gain[l]), w_in[l].astype(BF16), bd, row(gmlp_v_gain[l]),
            row(jnp.tile(attn_q_gain[l], ATTN_HEADS)), row(jnp.tile(attn_k_gain[l], ATTN_HEADS)),
            row(jnp.tile(mem_q_gain[l], MEM_HEADS)))
        mk, mv = _memory_kv(mem, row(mem_norm_gain[l]), w_mem_kv[l].astype(BF16), bd,
                            row(jnp.tile(mem_k_gain[l], MEM_HEADS)))
        o1, l1 = _local_attention(q, k, v, batch, seq)
        oc, lc = _strided_attention(qc, kc, vc)
        b_full = jnp.repeat(gmlp_b[l].T, HEAD_DIM, axis=1)
        x2d = _mix(x2d, ug, gvn, gmlp_w_s[l], b_full, o1, l1, oc, lc, sag, mq, mk, mv, smg,
                   w_out[l].astype(BF16), seq)
    return x2d.reshape(batch, seq, d_model)
```

```python
import functools
import math

import jax
import jax.numpy as jnp
from jax import lax
from jax.experimental import pallas as pl
from jax.experimental.pallas import tpu as pltpu

HEAD_DIM = 64
GMLP_HEADS = 4
ATTN_HEADS = 8
MEM_HEADS = 4
GMLP_WIDTH = GMLP_HEADS * HEAD_DIM
ATTN_WIDTH = ATTN_HEADS * HEAD_DIM
MEM_WIDTH = MEM_HEADS * HEAD_DIM
CHUNK = 128
BLOCK = 128
DILATED_CONFIGS = ((128, 1), (512, 4), (2048, 16))
EPS = 1e-6
MASKED = -1e30
LN2 = math.log(2.0)
LOGIT_SCALE = math.log2(math.e) / math.sqrt(HEAD_DIM)

IN_SEGMENTS = (("g_u", GMLP_WIDTH), ("g_v", GMLP_WIDTH), ("g_gate", GMLP_WIDTH),
               ("a_q", ATTN_WIDTH), ("a_k", ATTN_WIDTH), ("a_v", ATTN_WIDTH), ("a_gate", ATTN_WIDTH),
               ("m_q", MEM_WIDTH), ("m_gate", MEM_WIDTH))
MXU_TILE = 256

CLASSES = 16
MID = 4
SLABS_PER_MID = CLASSES // MID
LANES = 128
PAIRS = ATTN_WIDTH // LANES
ROW_TILE = 512
LOOP_UNROLL = 4
VMEM_LIMIT = 56 * 1024 * 1024

F32 = jnp.float32
BF16 = jnp.bfloat16


def _squares(acc):
    return (acc * acc).astype(BF16)


def _head_sumsq(sq, ones_bd):
    tile = ones_bd.shape[0]
    parts = [jnp.dot(sq[:, c:c + tile], ones_bd, preferred_element_type=F32)
             for c in range(0, sq.shape[1], tile)]
    return parts[0] if len(parts) == 1 else jnp.concatenate(parts, axis=1)


def _rms_scale(acc, sumsq, gain):
    return acc * lax.rsqrt(sumsq * (1.0 / HEAD_DIM) + EPS) * gain


def _head_rms(acc, ones_bd, gain):
    return _rms_scale(acc, _head_sumsq(_squares(acc), ones_bd), gain)


def _silu(x):
    return x * jax.nn.sigmoid(x)


def _store_both_layouts(val, seq_ref, cls_ref, slab_ref, mid_ref):
    seq_ref[...] = val.astype(BF16)
    rows = val.shape[0]
    per_mid, per_class = rows // MID, rows // CLASSES
    for s in range(val.shape[1] // LANES):
        lanes = slice(s * LANES, (s + 1) * LANES)
        slab_ref[s] = val[:, lanes]
        for r4 in range(MID):
            mid_ref[s, r4 * per_mid:(r4 + 1) * per_mid] = slab_ref[s, pl.ds(r4, per_mid, stride=MID), :]
        for r4 in range(MID):
            for g in range(SLABS_PER_MID):
                cls_ref[r4 + MID * g, :, lanes] = (
                    mid_ref[s, pl.ds(r4 * per_mid + g, per_class, stride=SLABS_PER_MID), :].astype(BF16))


def _proj_kernel(x_ref, gain_ref, w_ref, bd_ref, gvg_ref, qg_ref, kg_ref, mqg_ref,
                 ug_ref, gvn_ref, q_ref, k_ref, v_ref, sag_ref, mq_ref, smg_ref,
                 qc_ref, kc_ref, vc_ref, proj_even, proj_odd, h_ref, qslab_ref, kslab_ref, vslab_ref):
    i = pl.program_id(0)

    @pl.when(i == 0)
    def _():
        proj_odd[...] = jnp.zeros_like(proj_odd)

    cols, o = {}, 0
    for name, width in IN_SEGMENTS:
        cols[name] = slice(o, o + width)
        o += width

    def step(cur_ref, prev_ref):
        x = x_ref[...]
        ms = jnp.mean(x * x, axis=-1, keepdims=True)
        h_ref[...] = (x * lax.rsqrt(ms + EPS) * gain_ref[...]).astype(BF16)

        def project(name):
            cur_ref[:, cols[name]] = jnp.dot(h_ref[...], w_ref[:, cols[name]], preferred_element_type=F32)

        prev = lambda name: prev_ref[:, cols[name]]
        bd = bd_ref[...]
        sq = {name: _squares(prev(name)) for name in ("g_v", "a_q", "a_k", "m_q")}
        project("g_u")
        project("g_v")
        ug_ref[...] = (prev("g_u") * _silu(prev("g_gate"))).astype(BF16)
        project("g_gate")
        gvn_ref[...] = _rms_scale(prev("g_v"), _head_sumsq(sq["g_v"], bd), gvg_ref[...]).astype(BF16)
        project("a_q")
        _store_both_layouts(_rms_scale(prev("a_q"), _head_sumsq(sq["a_q"], bd), qg_ref[...]) * LOGIT_SCALE,
                            q_ref, qc_ref, qslab_ref.at[0], qslab_ref.at[1])
        project("a_k")
        _store_both_layouts(_rms_scale(prev("a_k"), _head_sumsq(sq["a_k"], bd), kg_ref[...]),
                            k_ref, kc_ref, kslab_ref.at[0], kslab_ref.at[1])
        project("a_v")
        _store_both_layouts(prev("a_v"), v_ref, vc_ref, vslab_ref.at[0], vslab_ref.at[1])
        project("a_gate")
        sag_ref[...] = _silu(prev("a_gate")).astype(BF16)
        project("m_q")
        mq_ref[...] = (_rms_scale(prev("m_q"), _head_sumsq(sq["m_q"], bd), mqg_ref[...])
                       * LOGIT_SCALE).astype(BF16)
        project("m_gate")
        smg_ref[...] = _silu(prev("m_gate")).astype(BF16)

    @pl.when(i % 2 == 0)
    def _():
        step(proj_even, proj_odd)

    @pl.when(i % 2 == 1)
    def _():
        step(proj_odd, proj_even)


def _memkv_kernel(mem_ref, gain_ref, w_ref, bd_ref, kg_ref, mk_ref, mv_ref):
    x = mem_ref[...]
    ms = jnp.mean(x * x, axis=-1, keepdims=True)
    h = (x * lax.rsqrt(ms + EPS) * gain_ref[...]).astype(BF16)
    mk = jnp.dot(h, w_ref[:, :MEM_WIDTH], preferred_element_type=F32)
    mv = jnp.dot(h, w_ref[:, MEM_WIDTH:], preferred_element_type=F32)
    mk_ref[...] = _head_rms(mk, bd_ref[...], kg_ref[...]).astype(BF16)
    mv_ref[...] = mv.astype(BF16)


def _pair_attention(q2, k2, v2, bias):
    rows = q2.shape[0]
    left = lax.broadcasted_iota(jnp.int32, (rows, LANES), 1) < HEAD_DIM
    zeros = jnp.zeros_like(q2)
    q_stack = jnp.concatenate([jnp.where(left, q2, zeros), jnp.where(left, zeros, q2)], axis=0)
    v_aug = jnp.concatenate([v2, jnp.ones_like(v2)], axis=1)
    s = lax.dot_general(q_stack, k2, (((1,), (1,)), ((), ())), preferred_element_type=F32)
    if bias is not None:
        s = s + bias
    m = jnp.max(s, axis=1, keepdims=True)
    p = jnp.exp2(s - m).astype(BF16)
    pv = jnp.dot(p, v_aug, preferred_element_type=F32)
    pick = lambda t: jnp.where(left, t[:rows], t[rows:])
    denom = pick(pv[:, LANES:])
    o = pick(pv[:, :LANES]) / denom
    lse = pick(jnp.broadcast_to(m, (2 * rows, LANES))) * LN2 + jnp.log(denom)
    return o, lse


def _merge_branches(o_a, lse_a, o_b, lse_b):
    m = jnp.maximum(lse_a, lse_b)
    w_a, w_b = jnp.exp(lse_a - m), jnp.exp(lse_b - m)
    tot = w_a + w_b
    return (w_a * o_a + w_b * o_b) / tot, m + jnp.log(tot)


def _band_bias(rel):
    return jnp.where((rel >= 0) & (rel <= BLOCK), 0.0, MASKED).astype(F32)


def _fill_bias_tables(bias_ref, slabs):
    row = lax.broadcasted_iota(jnp.int32, (2 * BLOCK, 2 * BLOCK), 0) % BLOCK
    col = lax.broadcasted_iota(jnp.int32, (2 * BLOCK, 2 * BLOCK), 1)
    bias_ref[0] = _band_bias(row - col)
    bias_ref[1] = _band_bias(row - col + BLOCK)
    if bias_ref.shape[0] > 2:
        q_rows, k_rows = BLOCK // slabs, 2 * BLOCK // slabs
        rel = (slabs * (row % q_rows) + row // q_rows) - (slabs * (col % k_rows) + col // k_rows)
        bias_ref[2] = _band_bias(rel)
        bias_ref[3] = _band_bias(rel + BLOCK)


def _first_grid_step():
    return (pl.program_id(0) == 0) & (pl.program_id(1) == 0)


def _strided_kernel(q_ref, k_ref, v_ref, o_ref, lse_ref, o16_ref, l16_ref, bias_ref):
    slab_len = q_ref.shape[1]

    @pl.when(_first_grid_step())
    def _():
        _fill_bias_tables(bias_ref, SLABS_PER_MID)

    def coarse(i, carry):
        g = i // (slab_len // BLOCK)
        blk = i % (slab_len // BLOCK)
        kstart = pl.multiple_of(jnp.maximum(blk - 1, 0) * BLOCK, BLOCK)
        generic = jnp.minimum(blk, 1)
        qstart = pl.multiple_of(blk * BLOCK, BLOCK)
        for pair in range(PAIRS):
            lanes = slice(pair * LANES, (pair + 1) * LANES)
            o2, lse2 = _pair_attention(q_ref[g, pl.ds(qstart, BLOCK), lanes],
                                       k_ref[g, pl.ds(kstart, 2 * BLOCK), lanes],
                                       v_ref[g, pl.ds(kstart, 2 * BLOCK), lanes], bias_ref[generic])
            o16_ref[g, pl.ds(qstart, BLOCK), lanes] = o2
            l16_ref[g, pl.ds(qstart, BLOCK), lanes] = lse2
        return carry

    lax.fori_loop(0, SLABS_PER_MID * (slab_len // BLOCK), coarse, 0, unroll=LOOP_UNROLL)

    q_rows = BLOCK // SLABS_PER_MID
    k_rows = 2 * q_rows

    def gather(ref, start, rows, lanes):
        return jnp.concatenate([ref[g, pl.ds(start, rows), lanes] for g in range(SLABS_PER_MID)], axis=0)

    def mid(i, carry):
        a0 = pl.multiple_of(i * q_rows, q_rows)
        kstart = pl.multiple_of(jnp.maximum(i - 1, 0) * q_rows, q_rows)
        generic = jnp.minimum(i, 1)
        for pair in range(PAIRS):
            lanes = slice(pair * LANES, (pair + 1) * LANES)
            o4, lse4 = _pair_attention(gather(q_ref, a0, q_rows, lanes),
                                       gather(k_ref, kstart, k_rows, lanes),
                                       gather(v_ref, kstart, k_rows, lanes), bias_ref[2 + generic])
            o, lse = _merge_branches(o4, lse4, gather(o16_ref, a0, q_rows, lanes),
                                     gather(l16_ref, a0, q_rows, lanes))
            for g in range(SLABS_PER_MID):
                o_ref[g, pl.ds(a0, q_rows), lanes] = o[g * q_rows:(g + 1) * q_rows].astype(BF16)
                lse_ref[g, pl.ds(a0, q_rows), lanes] = lse[g * q_rows:(g + 1) * q_rows]
        return carry

    lax.fori_loop(0, slab_len // q_rows, mid, 0, unroll=LOOP_UNROLL)


def _to_sequence_order(cls_ref, slab_ref, mid_ref):
    per_class = cls_ref.shape[1]
    per_mid = per_class * SLABS_PER_MID
    n_slabs = cls_ref.shape[2] // LANES
    for s in range(n_slabs):
        lanes = slice(s * LANES, (s + 1) * LANES)
        for r4 in range(MID):
            for g in range(SLABS_PER_MID):
                mid_ref[s, pl.ds(r4 * per_mid + g, per_class, stride=SLABS_PER_MID), :] = (
                    cls_ref[r4 + MID * g, :, lanes].astype(F32))
        for r4 in range(MID):
            slab_ref[s, pl.ds(r4, per_mid, stride=MID), :] = mid_ref[s, r4 * per_mid:(r4 + 1) * per_mid]


def _mix_kernel(x_ref, ug_ref, gvn_ref, ws_ref, bs_ref,
                q_ref, kprev_ref, k_ref, vprev_ref, v_ref, oc_ref, lc_ref, sag_ref,
                mq_ref, mk_ref, mv_ref, smg_ref, wout_ref, out_ref, oslab_ref, lslab_ref, bias_ref,
                *, steps_per_batch):
    rows = x_ref.shape[0]
    step = pl.program_id(0)

    @pl.when(step == 0)
    def _():
        row = lax.broadcasted_iota(jnp.int32, (2 * BLOCK, 2 * BLOCK), 0) % BLOCK
        col = lax.broadcasted_iota(jnp.int32, (2 * BLOCK, 2 * BLOCK), 1)
        band = _band_bias(row - col + BLOCK)
        bias_ref[1] = band
        bias_ref[0] = jnp.where(col >= BLOCK, band, MASKED)
    tri = (lax.broadcasted_iota(jnp.int32, (CHUNK, CHUNK), 0)
           >= lax.broadcasted_iota(jnp.int32, (CHUNK, CHUNK), 1))
    left = lax.broadcasted_iota(jnp.int32, (CHUNK, LANES), 1) < HEAD_DIM
    w_tri = [jnp.where(tri, ws_ref[h], 0.0).astype(BF16) for h in range(GMLP_HEADS)]
    sp_chunks = []
    for c in range(rows // CHUNK):
        pairs = []
        for pair in range(GMLP_WIDTH // LANES):
            vn2 = gvn_ref[c * CHUNK:(c + 1) * CHUNK, pair * LANES:(pair + 1) * LANES]
            sp_a = jnp.dot(w_tri[2 * pair], vn2, preferred_element_type=F32)
            sp_b = jnp.dot(w_tri[2 * pair + 1], vn2, preferred_element_type=F32)
            pairs.append(jnp.where(left, sp_a, sp_b))
        sp_chunks.append(jnp.concatenate(pairs, axis=1) + bs_ref[...])
    y_g = ug_ref[...].astype(F32) * jnp.concatenate(sp_chunks, axis=0)

    _to_sequence_order(oc_ref, oslab_ref.at[0], oslab_ref.at[1])
    _to_sequence_order(lc_ref, lslab_ref.at[0], lslab_ref.at[1])
    first_bias = jnp.where(step % steps_per_batch == 0, 0, 1)
    block_outs = []
    for blk in range(rows // BLOCK):
        own = slice(blk * BLOCK, (blk + 1) * BLOCK)
        pair_outs = []
        for pair in range(PAIRS):
            lanes = slice(pair * LANES, (pair + 1) * LANES)
            if blk == 0:
                k2 = jnp.concatenate([kprev_ref[:, lanes], k_ref[own, lanes]], axis=0)
                v2 = jnp.concatenate([vprev_ref[:, lanes], v_ref[own, lanes]], axis=0)
                bias = bias_ref[first_bias]
            else:
                window = slice((blk - 1) * BLOCK, (blk + 1) * BLOCK)
                k2, v2, bias = k_ref[window, lanes], v_ref[window, lanes], bias_ref[1]
            o1, l1 = _pair_attention(q_ref[own, lanes], k2, v2, bias)
            mixed, _ = _merge_branches(o1, l1, oslab_ref[0, pair, own, :], lslab_ref[0, pair, own, :])
            pair_outs.append(mixed)
        block_outs.append(jnp.concatenate(pair_outs, axis=1))
    y_a = jnp.concatenate(block_outs, axis=0) * sag_ref[...].astype(F32)

    mem_pairs = []
    for pair in range(MEM_WIDTH // LANES):
        lanes = slice(pair * LANES, (pair + 1) * LANES)
        o2, _ = _pair_attention(mq_ref[:, lanes], mk_ref[:, lanes], mv_ref[:, lanes], None)
        mem_pairs.append(o2)
    y_m = jnp.concatenate(mem_pairs, axis=1) * smg_ref[...].astype(F32)

    y = (jnp.dot(y_g.astype(BF16), wout_ref[:GMLP_WIDTH, :], preferred_element_type=F32)
         + jnp.dot(y_a.astype(BF16), wout_ref[GMLP_WIDTH:GMLP_WIDTH + ATTN_WIDTH, :],
                   preferred_element_type=F32)
         + jnp.dot(y_m.astype(BF16), wout_ref[GMLP_WIDTH + ATTN_WIDTH:, :],
                   preferred_element_type=F32))
    out_ref[...] = x_ref[...] + y


def _block_diag_ones(width):
    head = jnp.arange(width) // HEAD_DIM
    return (head[:, None] == head[None, :]).astype(BF16)


def _cparams(sem):
    return pltpu.CompilerParams(dimension_semantics=sem, vmem_limit_bytes=VMEM_LIMIT)


def _const_spec(shape):
    return pl.BlockSpec(shape, lambda *idx: (0,) * len(shape))


def _regroup_scratch():
    return pltpu.VMEM((2, ATTN_WIDTH // LANES, ROW_TILE, LANES), F32)


def _class_tile_spec(steps_per_batch, width):
    return pl.BlockSpec((None, CLASSES, ROW_TILE // CLASSES, width),
                        lambda i: (i // steps_per_batch, 0, i % steps_per_batch, 0))


def _project(x2d, batch, seq, gain, w_in, bd, gvg, qg, kg, mqg):
    rows, d_model = x2d.shape
    widths = (GMLP_WIDTH, GMLP_WIDTH, ATTN_WIDTH, ATTN_WIDTH, ATTN_WIDTH, ATTN_WIDTH, MEM_WIDTH, MEM_WIDTH)
    cls_shape = jax.ShapeDtypeStruct((batch, CLASSES, seq // CLASSES, ATTN_WIDTH), BF16)
    tiles = rows // ROW_TILE
    steps_per_batch = seq // ROW_TILE
    done = lambda i: jnp.maximum(i - 1, 0)
    cls_spec = pl.BlockSpec((None, CLASSES, ROW_TILE // CLASSES, ATTN_WIDTH),
                            lambda i: (done(i) // steps_per_batch, 0, done(i) % steps_per_batch, 0))
    slab = _regroup_scratch()
    raw = pltpu.VMEM((ROW_TILE, w_in.shape[1]), F32)
    return pl.pallas_call(
        _proj_kernel,
        grid=(tiles + 1,),
        in_specs=[pl.BlockSpec((ROW_TILE, d_model), lambda i: (jnp.minimum(i, tiles - 1), 0)),
                  _const_spec(gain.shape), _const_spec(w_in.shape), _const_spec(bd.shape),
                  _const_spec(gvg.shape), _const_spec(qg.shape), _const_spec(kg.shape),
                  _const_spec(mqg.shape)],
        out_specs=[pl.BlockSpec((ROW_TILE, w), lambda i: (done(i), 0)) for w in widths] + [cls_spec] * 3,
        out_shape=[jax.ShapeDtypeStruct((rows, w), BF16) for w in widths] + [cls_shape] * 3,
        scratch_shapes=[raw, raw, pltpu.VMEM((ROW_TILE, d_model), BF16), slab, slab, slab],
        compiler_params=_cparams(("arbitrary",)),
    )(x2d, gain, w_in, bd, gvg, qg, kg, mqg)


def _memory_kv(mem, gain, w_kv, bd, kg):
    batch, mem_len, d_model = mem.shape
    return pl.pallas_call(
        _memkv_kernel,
        grid=(batch,),
        in_specs=[pl.BlockSpec((None, mem_len, d_model), lambda b: (b, 0, 0)),
                  _const_spec(gain.shape), _const_spec(w_kv.shape), _const_spec(bd.shape),
                  _const_spec(kg.shape)],
        out_specs=[pl.BlockSpec((None, mem_len, MEM_WIDTH), lambda b: (b, 0, 0))] * 2,
        out_shape=[jax.ShapeDtypeStruct((batch, mem_len, MEM_WIDTH), BF16)] * 2,
        compiler_params=_cparams(("parallel",)),
    )(mem, gain, w_kv, bd, kg)


def _strided_attention(qc, kc, vc):
    batch, _, slab_len, width = qc.shape
    view = lambda t: t.reshape(batch, SLABS_PER_MID, MID, slab_len, width)
    spec = pl.BlockSpec((None, SLABS_PER_MID, None, slab_len, width), lambda b, r4: (b, 0, r4, 0, 0))
    o, lse = pl.pallas_call(
        _strided_kernel,
        grid=(batch, MID),
        in_specs=[spec] * 3,
        out_specs=[spec] * 2,
        out_shape=[jax.ShapeDtypeStruct((batch, SLABS_PER_MID, MID, slab_len, width), BF16),
                   jax.ShapeDtypeStruct((batch, SLABS_PER_MID, MID, slab_len, width), F32)],
        scratch_shapes=[pltpu.VMEM((SLABS_PER_MID, slab_len, width), F32)] * 2
                       + [pltpu.VMEM((4, 2 * BLOCK, 2 * BLOCK), F32)],
        compiler_params=_cparams(("arbitrary", "arbitrary")),
    )(view(qc), view(kc), view(vc))
    return o.reshape(qc.shape), lse.reshape(qc.shape)


def _mix(x2d, ug, gvn, w_s, b_full, q, k, v, oc, lc, sag, mq, mk, mv, smg, w_out, seq):
    rows, d_model = x2d.shape
    steps_per_batch = seq // ROW_TILE
    row_spec = lambda w: pl.BlockSpec((ROW_TILE, w), lambda i: (i, 0))
    prev_spec = pl.BlockSpec((BLOCK, ATTN_WIDTH),
                             lambda i: (jnp.maximum(i * (ROW_TILE // BLOCK) - 1, 0), 0))
    mem_spec = pl.BlockSpec((None,) + mk.shape[1:], lambda i: (i // steps_per_batch, 0, 0))
    cls_spec = _class_tile_spec(steps_per_batch, ATTN_WIDTH)
    return pl.pallas_call(
        functools.partial(_mix_kernel, steps_per_batch=steps_per_batch),
        grid=(rows // ROW_TILE,),
        in_specs=[row_spec(d_model), row_spec(GMLP_WIDTH), row_spec(GMLP_WIDTH),
                  _const_spec(w_s.shape), _const_spec(b_full.shape),
                  row_spec(ATTN_WIDTH), prev_spec, row_spec(ATTN_WIDTH), prev_spec, row_spec(ATTN_WIDTH),
                  cls_spec, cls_spec,
                  row_spec(ATTN_WIDTH), row_spec(MEM_WIDTH), mem_spec, mem_spec,
                  row_spec(MEM_WIDTH), _const_spec(w_out.shape)],
        out_specs=row_spec(d_model),
        out_shape=jax.ShapeDtypeStruct((rows, d_model), F32),
        scratch_shapes=[_regroup_scratch()] * 2 + [pltpu.VMEM((2, 2 * BLOCK, 2 * BLOCK), F32)],
        compiler_params=_cparams(("arbitrary",)),
    )(x2d, ug, gvn, w_s, b_full, q, k, k, v, v, oc, lc, sag, mq, mk, mv, smg, w_out)


def kernel(x, mem, norm_gain, w_in, gmlp_v_gain, gmlp_w_s, gmlp_b, attn_q_gain, attn_k_gain,
           mem_norm_gain, w_mem_kv, mem_q_gain, mem_k_gain, w_out):
    batch, seq, d_model = x.shape
    depth = w_in.shape[0]
    assert DILATED_CONFIGS == ((BLOCK, 1), (BLOCK * MID, MID), (BLOCK * CLASSES, CLASSES))
    assert seq % ROW_TILE == 0 and seq % (BLOCK * CLASSES) == 0
    bd = _block_diag_ones(MXU_TILE)
    x2d = x.reshape(batch * seq, d_model)
    for l in range(depth):
        row = lambda g: g.reshape(1, -1).astype(F32)
        ug, gvn, q, k, v, sag, mq, smg, qc, kc, vc = _project(
            x2d, batch, seq, row(norm_gain[l]), w_in[l].astype(BF16), bd, row(gmlp_v_gain[l]),
            row(jnp.tile(attn_q_gain[l], ATTN_HEADS)), row(jnp.tile(attn_k_gain[l], ATTN_HEADS)),
            row(jnp.tile(mem_q_gain[l], MEM_HEADS)))
        mk, mv = _memory_kv(mem, row(mem_norm_gain[l]), w_mem_kv[l].astype(BF16), bd,
                            row(jnp.tile(mem_k_gain[l], MEM_HEADS)))
        oc, lc = _strided_attention(qc, kc, vc)
        b_full = jnp.repeat(gmlp_b[l].T, HEAD_DIM, axis=1)
        x2d = _mix(x2d, ug, gvn, gmlp_w_s[l], b_full, q, k, v, oc, lc, sag, mq, mk, mv, smg,
                   w_out[l].astype(BF16), seq)
    return x2d.reshape(batch, seq, d_model)
```

```python
import functools
import math

import jax
import jax.numpy as jnp
from jax import lax
from jax.experimental import pallas as pl
from jax.experimental.pallas import tpu as pltpu

HEAD_DIM = 64
GMLP_HEADS = 4
ATTN_HEADS = 8
MEM_HEADS = 4
GMLP_WIDTH = GMLP_HEADS * HEAD_DIM
ATTN_WIDTH = ATTN_HEADS * HEAD_DIM
MEM_WIDTH = MEM_HEADS * HEAD_DIM
CHUNK = 128
BLOCK = 128
DILATED_CONFIGS = ((128, 1), (512, 4), (2048, 16))
EPS = 1e-6
MASKED = -1e30
LN2 = math.log(2.0)
LOGIT_SCALE = math.log2(math.e) / math.sqrt(HEAD_DIM)

IN_SEGMENTS = (("g_u", GMLP_WIDTH), ("g_v", GMLP_WIDTH), ("g_gate", GMLP_WIDTH),
               ("a_q", ATTN_WIDTH), ("a_k", ATTN_WIDTH), ("a_v", ATTN_WIDTH), ("a_gate", ATTN_WIDTH),
               ("m_q", MEM_WIDTH), ("m_gate", MEM_WIDTH))
MXU_TILE = 256

CLASSES = 16
MID = 4
SLABS_PER_MID = CLASSES // MID
LANES = 128
PAIRS = ATTN_WIDTH // LANES
ROW_TILE = 512
LOOP_UNROLL = 4
VMEM_LIMIT = 56 * 1024 * 1024

F32 = jnp.float32
BF16 = jnp.bfloat16


def _squares(acc):
    return (acc * acc).astype(BF16)


def _head_sumsq(sq, ones_bd):
    tile = ones_bd.shape[0]
    parts = [jnp.dot(sq[:, c:c + tile], ones_bd, preferred_element_type=F32)
             for c in range(0, sq.shape[1], tile)]
    return parts[0] if len(parts) == 1 else jnp.concatenate(parts, axis=1)


def _rms_scale(acc, sumsq, gain):
    return acc * lax.rsqrt(sumsq * (1.0 / HEAD_DIM) + EPS) * gain


def _head_rms(acc, ones_bd, gain):
    return _rms_scale(acc, _head_sumsq(_squares(acc), ones_bd), gain)


def _silu(x):
    return x * jax.nn.sigmoid(x)


def _store_both_layouts(val, seq_ref, cls_ref, slab_ref, mid_ref):
    seq_ref[...] = val.astype(BF16)
    rows = val.shape[0]
    per_mid, per_class = rows // MID, rows // CLASSES
    for s in range(val.shape[1] // LANES):
        lanes = slice(s * LANES, (s + 1) * LANES)
        slab_ref[s] = val[:, lanes]
        for r4 in range(MID):
            mid_ref[s, r4 * per_mid:(r4 + 1) * per_mid] = slab_ref[s, pl.ds(r4, per_mid, stride=MID), :]
        for r4 in range(MID):
            for g in range(SLABS_PER_MID):
                cls_ref[r4 + MID * g, :, lanes] = (
                    mid_ref[s, pl.ds(r4 * per_mid + g, per_class, stride=SLABS_PER_MID), :].astype(BF16))


def _proj_kernel(x_ref, gain_ref, w_ref, bd_ref, gvg_ref, qg_ref, kg_ref, mqg_ref,
                 ug_ref, gvn_ref, q_ref, k_ref, v_ref, sag_ref, mq_ref, smg_ref,
                 qc_ref, kc_ref, vc_ref, h_ref, qslab_ref, kslab_ref, vslab_ref):
    cols, o = {}, 0
    for name, width in IN_SEGMENTS:
        cols[name] = slice(o, o + width)
        o += width

    x = x_ref[...]
    ms = jnp.mean(x * x, axis=-1, keepdims=True)
    h_ref[...] = (x * lax.rsqrt(ms + EPS) * gain_ref[...]).astype(BF16)

    def project(name):
        return jnp.dot(h_ref[...], w_ref[:, cols[name]], preferred_element_type=F32)

    bd = bd_ref[...]
    g_u = project("g_u")
    g_v = project("g_v")
    sq_gv = _squares(g_v)
    g_gate = project("g_gate")
    ug_ref[...] = (g_u * _silu(g_gate)).astype(BF16)
    a_q = project("a_q")
    sq_q = _squares(a_q)
    gvn_ref[...] = _rms_scale(g_v, _head_sumsq(sq_gv, bd), gvg_ref[...]).astype(BF16)
    a_k = project("a_k")
    sq_k = _squares(a_k)
    _store_both_layouts(_rms_scale(a_q, _head_sumsq(sq_q, bd), qg_ref[...]) * LOGIT_SCALE,
                        q_ref, qc_ref, qslab_ref.at[0], qslab_ref.at[1])
    a_v = project("a_v")
    _store_both_layouts(_rms_scale(a_k, _head_sumsq(sq_k, bd), kg_ref[...]),
                        k_ref, kc_ref, kslab_ref.at[0], kslab_ref.at[1])
    a_gate = project("a_gate")
    _store_both_layouts(a_v, v_ref, vc_ref, vslab_ref.at[0], vslab_ref.at[1])
    m_q = project("m_q")
    sq_mq = _squares(m_q)
    sag_ref[...] = _silu(a_gate).astype(BF16)
    m_gate = project("m_gate")
    mq_ref[...] = (_rms_scale(m_q, _head_sumsq(sq_mq, bd), mqg_ref[...]) * LOGIT_SCALE).astype(BF16)
    smg_ref[...] = _silu(m_gate).astype(BF16)


def _memkv_kernel(mem_ref, gain_ref, w_ref, bd_ref, kg_ref, mk_ref, mv_ref):
    x = mem_ref[...]
    ms = jnp.mean(x * x, axis=-1, keepdims=True)
    h = (x * lax.rsqrt(ms + EPS) * gain_ref[...]).astype(BF16)
    mk = jnp.dot(h, w_ref[:, :MEM_WIDTH], preferred_element_type=F32)
    mv = jnp.dot(h, w_ref[:, MEM_WIDTH:], preferred_element_type=F32)
    mk_ref[...] = _head_rms(mk, bd_ref[...], kg_ref[...]).astype(BF16)
    mv_ref[...] = mv.astype(BF16)


def _pair_attention(q2, k2, v2, bias):
    rows = q2.shape[0]
    left = lax.broadcasted_iota(jnp.int32, (rows, LANES), 1) < HEAD_DIM
    zeros = jnp.zeros_like(q2)
    q_stack = jnp.concatenate([jnp.where(left, q2, zeros), jnp.where(left, zeros, q2)], axis=0)
    v_aug = jnp.concatenate([v2, jnp.ones_like(v2)], axis=1)
    s = lax.dot_general(q_stack, k2, (((1,), (1,)), ((), ())), preferred_element_type=F32)
    if bias is not None:
        s = s + bias
    m = jnp.max(s, axis=1, keepdims=True)
    p = jnp.exp2(s - m).astype(BF16)
    pv = jnp.dot(p, v_aug, preferred_element_type=F32)
    pick = lambda t: jnp.where(left, t[:rows], t[rows:])
    denom = pick(pv[:, LANES:])
    o = pick(pv[:, :LANES]) / denom
    lse = pick(jnp.broadcast_to(m, (2 * rows, LANES))) * LN2 + jnp.log(denom)
    return o, lse


def _merge_branches(o_a, lse_a, o_b, lse_b):
    m = jnp.maximum(lse_a, lse_b)
    w_a, w_b = jnp.exp(lse_a - m), jnp.exp(lse_b - m)
    tot = w_a + w_b
    return (w_a * o_a + w_b * o_b) / tot, m + jnp.log(tot)


def _band_bias(rel):
    return jnp.where((rel >= 0) & (rel <= BLOCK), 0.0, MASKED).astype(F32)


def _fill_bias_tables(bias_ref, slabs):
    row = lax.broadcasted_iota(jnp.int32, (2 * BLOCK, 2 * BLOCK), 0) % BLOCK
    col = lax.broadcasted_iota(jnp.int32, (2 * BLOCK, 2 * BLOCK), 1)
    bias_ref[0] = _band_bias(row - col)
    bias_ref[1] = _band_bias(row - col + BLOCK)
    if bias_ref.shape[0] > 2:
        q_rows, k_rows = BLOCK // slabs, 2 * BLOCK // slabs
        rel = (slabs * (row % q_rows) + row // q_rows) - (slabs * (col % k_rows) + col // k_rows)
        bias_ref[2] = _band_bias(rel)
        bias_ref[3] = _band_bias(rel + BLOCK)


def _first_grid_step():
    return (pl.program_id(0) == 0) & (pl.program_id(1) == 0)


def _strided_kernel(q_ref, k_ref, v_ref, o_ref, lse_ref, o16_ref, l16_ref, bias_ref):
    slab_len = q_ref.shape[1]

    @pl.when(_first_grid_step())
    def _():
        _fill_bias_tables(bias_ref, SLABS_PER_MID)

    def coarse(i, carry):
        g = i // (slab_len // BLOCK)
        blk = i % (slab_len // BLOCK)
        kstart = pl.multiple_of(jnp.maximum(blk - 1, 0) * BLOCK, BLOCK)
        generic = jnp.minimum(blk, 1)
        qstart = pl.multiple_of(blk * BLOCK, BLOCK)
        for pair in range(PAIRS):
            lanes = slice(pair * LANES, (pair + 1) * LANES)
            o2, lse2 = _pair_attention(q_ref[g, pl.ds(qstart, BLOCK), lanes],
                                       k_ref[g, pl.ds(kstart, 2 * BLOCK), lanes],
                                       v_ref[g, pl.ds(kstart, 2 * BLOCK), lanes], bias_ref[generic])
            o16_ref[g, pl.ds(qstart, BLOCK), lanes] = o2
            l16_ref[g, pl.ds(qstart, BLOCK), lanes] = lse2
        return carry

    lax.fori_loop(0, SLABS_PER_MID * (slab_len // BLOCK), coarse, 0, unroll=LOOP_UNROLL)

    q_rows = BLOCK // SLABS_PER_MID
    k_rows = 2 * q_rows

    def gather(ref, start, rows, lanes):
        return jnp.concatenate([ref[g, pl.ds(start, rows), lanes] for g in range(SLABS_PER_MID)], axis=0)

    def mid(i, carry):
        a0 = pl.multiple_of(i * q_rows, q_rows)
        kstart = pl.multiple_of(jnp.maximum(i - 1, 0) * q_rows, q_rows)
        generic = jnp.minimum(i, 1)
        for pair in range(PAIRS):
            lanes = slice(pair * LANES, (pair + 1) * LANES)
            o4, lse4 = _pair_attention(gather(q_ref, a0, q_rows, lanes),
                                       gather(k_ref, kstart, k_rows, lanes),
                                       gather(v_ref, kstart, k_rows, lanes), bias_ref[2 + generic])
            o, lse = _merge_branches(o4, lse4, gather(o16_ref, a0, q_rows, lanes),
                                     gather(l16_ref, a0, q_rows, lanes))
            for g in range(SLABS_PER_MID):
                o_ref[g, pl.ds(a0, q_rows), lanes] = o[g * q_rows:(g + 1) * q_rows].astype(BF16)
                lse_ref[g, pl.ds(a0, q_rows), lanes] = lse[g * q_rows:(g + 1) * q_rows]
        return carry

    lax.fori_loop(0, slab_len // q_rows, mid, 0, unroll=LOOP_UNROLL)


def _to_sequence_order(cls_ref, slab_ref, mid_ref):
    per_class = cls_ref.shape[1]
    per_mid = per_class * SLABS_PER_MID
    n_slabs = cls_ref.shape[2] // LANES
    for s in range(n_slabs):
        lanes = slice(s * LANES, (s + 1) * LANES)
        for r4 in range(MID):
            for g in range(SLABS_PER_MID):
                mid_ref[s, pl.ds(r4 * per_mid + g, per_class, stride=SLABS_PER_MID), :] = (
                    cls_ref[r4 + MID * g, :, lanes].astype(F32))
        for r4 in range(MID):
            slab_ref[s, pl.ds(r4, per_mid, stride=MID), :] = mid_ref[s, r4 * per_mid:(r4 + 1) * per_mid]


def _mix_kernel(x_ref, ug_ref, gvn_ref, ws_ref, bs_ref,
                q_ref, kprev_ref, k_ref, vprev_ref, v_ref, oc_ref, lc_ref, sag_ref,
                mq_ref, mk_ref, mv_ref, smg_ref, wout_ref, out_ref, oslab_ref, lslab_ref, bias_ref,
                *, steps_per_batch):
    rows = x_ref.shape[0]
    step = pl.program_id(0)

    @pl.when(step == 0)
    def _():
        row = lax.broadcasted_iota(jnp.int32, (2 * BLOCK, 2 * BLOCK), 0) % BLOCK
        col = lax.broadcasted_iota(jnp.int32, (2 * BLOCK, 2 * BLOCK), 1)
        band = _band_bias(row - col + BLOCK)
        bias_ref[1] = band
        bias_ref[0] = jnp.where(col >= BLOCK, band, MASKED)
    tri = (lax.broadcasted_iota(jnp.int32, (CHUNK, CHUNK), 0)
           >= lax.broadcasted_iota(jnp.int32, (CHUNK, CHUNK), 1))
    left = lax.broadcasted_iota(jnp.int32, (CHUNK, LANES), 1) < HEAD_DIM
    w_tri = [jnp.where(tri, ws_ref[h], 0.0).astype(BF16) for h in range(GMLP_HEADS)]
    sp_chunks = []
    for c in range(rows // CHUNK):
        pairs = []
        for pair in range(GMLP_WIDTH // LANES):
            vn2 = gvn_ref[c * CHUNK:(c + 1) * CHUNK, pair * LANES:(pair + 1) * LANES]
            sp_a = jnp.dot(w_tri[2 * pair], vn2, preferred_element_type=F32)
            sp_b = jnp.dot(w_tri[2 * pair + 1], vn2, preferred_element_type=F32)
            pairs.append(jnp.where(left, sp_a, sp_b))
        sp_chunks.append(jnp.concatenate(pairs, axis=1) + bs_ref[...])
    y_g = ug_ref[...].astype(F32) * jnp.concatenate(sp_chunks, axis=0)

    _to_sequence_order(oc_ref, oslab_ref.at[0], oslab_ref.at[1])
    _to_sequence_order(lc_ref, lslab_ref.at[0], lslab_ref.at[1])
    first_bias = jnp.where(step % steps_per_batch == 0, 0, 1)
    block_outs = []
    for blk in range(rows // BLOCK):
        own = slice(blk * BLOCK, (blk + 1) * BLOCK)
        pair_outs = []
        for pair in range(PAIRS):
            lanes = slice(pair * LANES, (pair + 1) * LANES)
            if blk == 0:
                k2 = jnp.concatenate([kprev_ref[:, lanes], k_ref[own, lanes]], axis=0)
                v2 = jnp.concatenate([vprev_ref[:, lanes], v_ref[own, lanes]], axis=0)
                bias = bias_ref[first_bias]
            else:
                window = slice((blk - 1) * BLOCK, (blk + 1) * BLOCK)
                k2, v2, bias = k_ref[window, lanes], v_ref[window, lanes], bias_ref[1]
            o1, l1 = _pair_attention(q_ref[own, lanes], k2, v2, bias)
            mixed, _ = _merge_branches(o1, l1, oslab_ref[0, pair, own, :], lslab_ref[0, pair, own, :])
            pair_outs.append(mixed)
        block_outs.append(jnp.concatenate(pair_outs, axis=1))
    y_a = jnp.concatenate(block_outs, axis=0) * sag_ref[...].astype(F32)

    mem_pairs = []
    for pair in range(MEM_WIDTH // LANES):
        lanes = slice(pair * LANES, (pair + 1) * LANES)
        o2, _ = _pair_attention(mq_ref[:, lanes], mk_ref[:, lanes], mv_ref[:, lanes], None)
        mem_pairs.append(o2)
    y_m = jnp.concatenate(mem_pairs, axis=1) * smg_ref[...].astype(F32)

    y = (jnp.dot(y_g.astype(BF16), wout_ref[:GMLP_WIDTH, :], preferred_element_type=F32)
         + jnp.dot(y_a.astype(BF16), wout_ref[GMLP_WIDTH:GMLP_WIDTH + ATTN_WIDTH, :],
                   preferred_element_type=F32)
         + jnp.dot(y_m.astype(BF16), wout_ref[GMLP_WIDTH + ATTN_WIDTH:, :],
                   preferred_element_type=F32))
    out_ref[...] = x_ref[...] + y


def _block_diag_ones(width):
    head = jnp.arange(width) // HEAD_DIM
    return (head[:, None] == head[None, :]).astype(BF16)


def _cparams(sem):
    return pltpu.CompilerParams(dimension_semantics=sem, vmem_limit_bytes=VMEM_LIMIT)


def _const_spec(shape):
    return pl.BlockSpec(shape, lambda *idx: (0,) * len(shape))


def _regroup_scratch():
    return pltpu.VMEM((2, ATTN_WIDTH // LANES, ROW_TILE, LANES), F32)


def _class_tile_spec(steps_per_batch, width):
    return pl.BlockSpec((None, CLASSES, ROW_TILE // CLASSES, width),
                        lambda i: (i // steps_per_batch, 0, i % steps_per_batch, 0))


def _project(x2d, batch, seq, gain, w_in, bd, gvg, qg, kg, mqg):
    rows, d_model = x2d.shape
    widths = (GMLP_WIDTH, GMLP_WIDTH, ATTN_WIDTH, ATTN_WIDTH, ATTN_WIDTH, ATTN_WIDTH, MEM_WIDTH, MEM_WIDTH)
    cls_shape = jax.ShapeDtypeStruct((batch, CLASSES, seq // CLASSES, ATTN_WIDTH), BF16)
    cls_spec = _class_tile_spec(seq // ROW_TILE, ATTN_WIDTH)
    slab = _regroup_scratch()
    return pl.pallas_call(
        _proj_kernel,
        grid=(rows // ROW_TILE,),
        in_specs=[pl.BlockSpec((ROW_TILE, d_model), lambda i: (i, 0)),
                  _const_spec(gain.shape), _const_spec(w_in.shape), _const_spec(bd.shape),
                  _const_spec(gvg.shape), _const_spec(qg.shape), _const_spec(kg.shape),
                  _const_spec(mqg.shape)],
        out_specs=[pl.BlockSpec((ROW_TILE, w), lambda i: (i, 0)) for w in widths] + [cls_spec] * 3,
        out_shape=[jax.ShapeDtypeStruct((rows, w), BF16) for w in widths] + [cls_shape] * 3,
        scratch_shapes=[pltpu.VMEM((ROW_TILE, d_model), BF16), slab, slab, slab],
        compiler_params=_cparams(("parallel",)),
    )(x2d, gain, w_in, bd, gvg, qg, kg, mqg)


def _memory_kv(mem, gain, w_kv, bd, kg):
    batch, mem_len, d_model = mem.shape
    return pl.pallas_call(
        _memkv_kernel,
        grid=(batch,),
        in_specs=[pl.BlockSpec((None, mem_len, d_model), lambda b: (b, 0, 0)),
                  _const_spec(gain.shape), _const_spec(w_kv.shape), _const_spec(bd.shape),
                  _const_spec(kg.shape)],
        out_specs=[pl.BlockSpec((None, mem_len, MEM_WIDTH), lambda b: (b, 0, 0))] * 2,
        out_shape=[jax.ShapeDtypeStruct((batch, mem_len, MEM_WIDTH), BF16)] * 2,
        compiler_params=_cparams(("parallel",)),
    )(mem, gain, w_kv, bd, kg)


def _strided_attention(qc, kc, vc):
    batch, _, slab_len, width = qc.shape
    view = lambda t: t.reshape(batch, SLABS_PER_MID, MID, slab_len, width)
    spec = pl.BlockSpec((None, SLABS_PER_MID, None, slab_len, width), lambda b, r4: (b, 0, r4, 0, 0))
    o, lse = pl.pallas_call(
        _strided_kernel,
        grid=(batch, MID),
        in_specs=[spec] * 3,
        out_specs=[spec] * 2,
        out_shape=[jax.ShapeDtypeStruct((batch, SLABS_PER_MID, MID, slab_len, width), BF16),
                   jax.ShapeDtypeStruct((batch, SLABS_PER_MID, MID, slab_len, width), F32)],
        scratch_shapes=[pltpu.VMEM((SLABS_PER_MID, slab_len, width), F32)] * 2
                       + [pltpu.VMEM((4, 2 * BLOCK, 2 * BLOCK), F32)],
        compiler_params=_cparams(("arbitrary", "arbitrary")),
    )(view(qc), view(kc), view(vc))
    return o.reshape(qc.shape), lse.reshape(qc.shape)


def _mix(x2d, ug, gvn, w_s, b_full, q, k, v, oc, lc, sag, mq, mk, mv, smg, w_out, seq):
    rows, d_model = x2d.shape
    steps_per_batch = seq // ROW_TILE
    row_spec = lambda w: pl.BlockSpec((ROW_TILE, w), lambda i: (i, 0))
    prev_spec = pl.BlockSpec((BLOCK, ATTN_WIDTH),
                             lambda i: (jnp.maximum(i * (ROW_TILE // BLOCK) - 1, 0), 0))
    mem_spec = pl.BlockSpec((None,) + mk.shape[1:], lambda i: (i // steps_per_batch, 0, 0))
    cls_spec = _class_tile_spec(steps_per_batch, ATTN_WIDTH)
    return pl.pallas_call(
        functools.partial(_mix_kernel, steps_per_batch=steps_per_batch),
        grid=(rows // ROW_TILE,),
        in_specs=[row_spec(d_model), row_spec(GMLP_WIDTH), row_spec(GMLP_WIDTH),
                  _const_spec(w_s.shape), _const_spec(b_full.shape),
                  row_spec(ATTN_WIDTH), prev_spec, row_spec(ATTN_WIDTH), prev_spec, row_spec(ATTN_WIDTH),
                  cls_spec, cls_spec,
                  row_spec(ATTN_WIDTH), row_spec(MEM_WIDTH), mem_spec, mem_spec,
                  row_spec(MEM_WIDTH), _const_spec(w_out.shape)],
        out_specs=row_spec(d_model),
        out_shape=jax.ShapeDtypeStruct((rows, d_model), F32),
        scratch_shapes=[_regroup_scratch()] * 2 + [pltpu.VMEM((2, 2 * BLOCK, 2 * BLOCK), F32)],
        compiler_params=_cparams(("arbitrary",)),
    )(x2d, ug, gvn, w_s, b_full, q, k, k, v, v, oc, lc, sag, mq, mk, mv, smg, w_out)


def kernel(x, mem, norm_gain, w_in, gmlp_v_gain, gmlp_w_s, gmlp_b, attn_q_gain, attn_k_gain,
           mem_norm_gain, w_mem_kv, mem_q_gain, mem_k_gain, w_out):
    batch, seq, d_model = x.shape
    depth = w_in.shape[0]
    assert DILATED_CONFIGS == ((BLOCK, 1), (BLOCK * MID, MID), (BLOCK * CLASSES, CLASSES))
    assert seq % ROW_TILE == 0 and seq % (BLOCK * CLASSES) == 0
    bd = _block_diag_ones(MXU_TILE)
    x2d = x.reshape(batch * seq, d_model)
    for l in range(depth):
        row = lambda g: g.reshape(1, -1).astype(F32)
        ug, gvn, q, k, v, sag, mq, smg, qc, kc, vc = _project(
            x2d, batch, seq, row(norm_gain[l]), w_in[l].astype(BF16), bd, row(gmlp_v_gain[l]),
            row(jnp.tile(attn_q_gain[l], ATTN_HEADS)), row(jnp.tile(attn_k_gain[l], ATTN_HEADS)),
            row(jnp.tile(mem_q_gain[l], MEM_HEADS)))
        mk, mv = _memory_kv(mem, row(mem_norm_gain[l]), w_mem_kv[l].astype(BF16), bd,
                            row(jnp.tile(mem_k_gain[l], MEM_HEADS)))
        oc, lc = _strided_attention(qc, kc, vc)
        b_full = jnp.repeat(gmlp_b[l].T, HEAD_DIM, axis=1)
        x2d = _mix(x2d, ug, gvn, gmlp_w_s[l], b_full, q, k, v, oc, lc, sag, mq, mk, mv, smg,
                   w_out[l].astype(BF16), seq)
    return x2d.reshape(batch, seq, d_model)
```

```python
import functools
import math

import jax
import jax.numpy as jnp
from jax import lax
from jax.experimental import pallas as pl
from jax.experimental.pallas import tpu as pltpu

HEAD_DIM = 64
GMLP_HEADS = 4
ATTN_HEADS = 8
MEM_HEADS = 4
GMLP_WIDTH = GMLP_HEADS * HEAD_DIM
ATTN_WIDTH = ATTN_HEADS * HEAD_DIM
MEM_WIDTH = MEM_HEADS * HEAD_DIM
CHUNK = 128
BLOCK = 128
DILATED_CONFIGS = ((128, 1), (512, 4), (2048, 16))
EPS = 1e-6
MASKED = -1e30
LOG2E = math.log2(math.e)
LOGIT_SCALE = LOG2E / math.sqrt(HEAD_DIM)

IN_SEGMENTS = (("g_u", GMLP_WIDTH), ("g_v", GMLP_WIDTH), ("g_gate", GMLP_WIDTH),
               ("a_q", ATTN_WIDTH), ("a_k", ATTN_WIDTH), ("a_v", ATTN_WIDTH), ("a_gate", ATTN_WIDTH),
               ("m_q", MEM_WIDTH), ("m_gate", MEM_WIDTH))
MXU_TILE = 256

CLASSES = 16
MID = 4
SLABS_PER_MID = CLASSES // MID
LANES = 128
PAIRS = ATTN_WIDTH // LANES
ROW_TILE = 512
LOOP_UNROLL = 4
VMEM_LIMIT = 56 * 1024 * 1024

F32 = jnp.float32
BF16 = jnp.bfloat16


def _squares(acc):
    return (acc * acc).astype(BF16)


def _head_sumsq(sq, ones_bd):
    tile = ones_bd.shape[0]
    parts = [jnp.dot(sq[:, c:c + tile], ones_bd, preferred_element_type=F32)
             for c in range(0, sq.shape[1], tile)]
    return parts[0] if len(parts) == 1 else jnp.concatenate(parts, axis=1)


def _rms_scale(acc, sumsq, gain):
    return acc * lax.rsqrt(sumsq * (1.0 / HEAD_DIM) + EPS) * gain


def _head_rms(acc, ones_bd, gain):
    return _rms_scale(acc, _head_sumsq(_squares(acc), ones_bd), gain)


def _silu(x):
    return x * jax.nn.sigmoid(x)


def _store_both_layouts(val, seq_ref, cls_ref, slab_ref, mid_ref):
    seq_ref[...] = val.astype(BF16)
    rows = val.shape[0]
    per_mid, per_class = rows // MID, rows // CLASSES
    for s in range(val.shape[1] // LANES):
        lanes = slice(s * LANES, (s + 1) * LANES)
        slab_ref[s] = val[:, lanes]
        for r4 in range(MID):
            mid_ref[s, r4 * per_mid:(r4 + 1) * per_mid] = slab_ref[s, pl.ds(r4, per_mid, stride=MID), :]
        for r4 in range(MID):
            for g in range(SLABS_PER_MID):
                cls_ref[r4 + MID * g, :, lanes] = (
                    mid_ref[s, pl.ds(r4 * per_mid + g, per_class, stride=SLABS_PER_MID), :].astype(BF16))


def _proj_kernel(x_ref, gain_ref, w_ref, bd_ref, gvg_ref, qg_ref, kg_ref, mqg_ref,
                 ug_ref, gvn_ref, q_ref, k_ref, v_ref, sag_ref, mq_ref, smg_ref,
                 qc_ref, kc_ref, vc_ref, h_ref, qslab_ref, kslab_ref, vslab_ref):
    cols, o = {}, 0
    for name, width in IN_SEGMENTS:
        cols[name] = slice(o, o + width)
        o += width

    x = x_ref[...]
    ms = jnp.mean(x * x, axis=-1, keepdims=True)
    h_ref[...] = (x * lax.rsqrt(ms + EPS) * gain_ref[...]).astype(BF16)

    def project(name):
        return jnp.dot(h_ref[...], w_ref[:, cols[name]], preferred_element_type=F32)

    bd = bd_ref[...]
    g_u = project("g_u")
    g_v = project("g_v")
    sq_gv = _squares(g_v)
    g_gate = project("g_gate")
    ug_ref[...] = (g_u * _silu(g_gate)).astype(BF16)
    a_q = project("a_q")
    sq_q = _squares(a_q)
    gvn_ref[...] = _rms_scale(g_v, _head_sumsq(sq_gv, bd), gvg_ref[...]).astype(BF16)
    a_k = project("a_k")
    sq_k = _squares(a_k)
    _store_both_layouts(_rms_scale(a_q, _head_sumsq(sq_q, bd), qg_ref[...]) * LOGIT_SCALE,
                        q_ref, qc_ref, qslab_ref.at[0], qslab_ref.at[1])
    a_v = project("a_v")
    _store_both_layouts(_rms_scale(a_k, _head_sumsq(sq_k, bd), kg_ref[...]),
                        k_ref, kc_ref, kslab_ref.at[0], kslab_ref.at[1])
    a_gate = project("a_gate")
    _store_both_layouts(a_v, v_ref, vc_ref, vslab_ref.at[0], vslab_ref.at[1])
    m_q = project("m_q")
    sq_mq = _squares(m_q)
    sag_ref[...] = _silu(a_gate).astype(BF16)
    m_gate = project("m_gate")
    mq_ref[...] = (_rms_scale(m_q, _head_sumsq(sq_mq, bd), mqg_ref[...]) * LOGIT_SCALE).astype(BF16)
    smg_ref[...] = _silu(m_gate).astype(BF16)


def _memkv_kernel(mem_ref, gain_ref, w_ref, bd_ref, kg_ref, mk_ref, mv_ref):
    x = mem_ref[...]
    ms = jnp.mean(x * x, axis=-1, keepdims=True)
    h = (x * lax.rsqrt(ms + EPS) * gain_ref[...]).astype(BF16)
    mk = jnp.dot(h, w_ref[:, :MEM_WIDTH], preferred_element_type=F32)
    mv = jnp.dot(h, w_ref[:, MEM_WIDTH:], preferred_element_type=F32)
    mk_ref[...] = _head_rms(mk, bd_ref[...], kg_ref[...]).astype(BF16)
    mv_ref[...] = mv.astype(BF16)


def _pair_attention(q2, k2, v2, bias):
    rows = q2.shape[0]
    left = lax.broadcasted_iota(jnp.int32, (rows, LANES), 1) < HEAD_DIM
    zeros = jnp.zeros_like(q2)
    q_stack = jnp.concatenate([jnp.where(left, q2, zeros), jnp.where(left, zeros, q2)], axis=0)
    v_aug = jnp.concatenate([v2, jnp.ones_like(v2)], axis=1)
    s = lax.dot_general(q_stack, k2, (((1,), (1,)), ((), ())), preferred_element_type=F32)
    if bias is not None:
        s = s + bias
    m = jnp.max(s, axis=1, keepdims=True)
    p = jnp.exp2(s - m).astype(BF16)
    pv = jnp.dot(p, v_aug, preferred_element_type=F32)
    pick = lambda t: jnp.where(left, t[:rows], t[rows:])
    return pick(pv[:, :LANES]), pick(pv[:, LANES:]), pick(jnp.broadcast_to(m, (2 * rows, LANES)))


def _merge_states(a, b):
    top = jnp.maximum(a[2], b[2])
    w_a, w_b = jnp.exp2(a[2] - top), jnp.exp2(b[2] - top)
    return w_a * a[0] + w_b * b[0], w_a * a[1] + w_b * b[1], top


def _normalised(state):
    acc, den, top = state
    return acc / den, top + jnp.log(den) * LOG2E


def _band_bias(rel):
    return jnp.where((rel >= 0) & (rel <= BLOCK), 0.0, MASKED).astype(F32)


def _fill_bias_tables(bias_ref, slabs):
    row = lax.broadcasted_iota(jnp.int32, (2 * BLOCK, 2 * BLOCK), 0) % BLOCK
    col = lax.broadcasted_iota(jnp.int32, (2 * BLOCK, 2 * BLOCK), 1)
    bias_ref[0] = _band_bias(row - col)
    bias_ref[1] = _band_bias(row - col + BLOCK)
    if bias_ref.shape[0] > 2:
        q_rows, k_rows = BLOCK // slabs, 2 * BLOCK // slabs
        rel = (slabs * (row % q_rows) + row // q_rows) - (slabs * (col % k_rows) + col // k_rows)
        bias_ref[2] = _band_bias(rel)
        bias_ref[3] = _band_bias(rel + BLOCK)


def _first_grid_step():
    return (pl.program_id(0) == 0) & (pl.program_id(1) == 0)


def _strided_kernel(q_ref, k_ref, v_ref, o_ref, lse_ref, state16_ref, bias_ref):
    slab_len = q_ref.shape[1]

    @pl.when(_first_grid_step())
    def _():
        _fill_bias_tables(bias_ref, SLABS_PER_MID)

    def coarse(i, carry):
        g = i // (slab_len // BLOCK)
        blk = i % (slab_len // BLOCK)
        kstart = pl.multiple_of(jnp.maximum(blk - 1, 0) * BLOCK, BLOCK)
        generic = jnp.minimum(blk, 1)
        qstart = pl.multiple_of(blk * BLOCK, BLOCK)
        for pair in range(PAIRS):
            lanes = slice(pair * LANES, (pair + 1) * LANES)
            state = _pair_attention(q_ref[g, pl.ds(qstart, BLOCK), lanes],
                                    k_ref[g, pl.ds(kstart, 2 * BLOCK), lanes],
                                    v_ref[g, pl.ds(kstart, 2 * BLOCK), lanes], bias_ref[generic])
            for part in range(3):
                state16_ref[part, g, pl.ds(qstart, BLOCK), lanes] = state[part]
        return carry

    lax.fori_loop(0, SLABS_PER_MID * (slab_len // BLOCK), coarse, 0, unroll=LOOP_UNROLL)

    q_rows = BLOCK // SLABS_PER_MID
    k_rows = 2 * q_rows

    def gather(ref, start, rows, lanes):
        return jnp.concatenate([ref[g, pl.ds(start, rows), lanes] for g in range(SLABS_PER_MID)], axis=0)

    def gather_state16(start, rows, lanes):
        return tuple(gather(state16_ref.at[part], start, rows, lanes) for part in range(3))

    def mid(i, carry):
        a0 = pl.multiple_of(i * q_rows, q_rows)
        kstart = pl.multiple_of(jnp.maximum(i - 1, 0) * q_rows, q_rows)
        generic = jnp.minimum(i, 1)
        for pair in range(PAIRS):
            lanes = slice(pair * LANES, (pair + 1) * LANES)
            state4 = _pair_attention(gather(q_ref, a0, q_rows, lanes),
                                     gather(k_ref, kstart, k_rows, lanes),
                                     gather(v_ref, kstart, k_rows, lanes), bias_ref[2 + generic])
            o, lse = _normalised(_merge_states(state4, gather_state16(a0, q_rows, lanes)))
            for g in range(SLABS_PER_MID):
                o_ref[g, pl.ds(a0, q_rows), lanes] = o[g * q_rows:(g + 1) * q_rows].astype(BF16)
                lse_ref[g, pl.ds(a0, q_rows), lanes] = lse[g * q_rows:(g + 1) * q_rows]
        return carry

    lax.fori_loop(0, slab_len // q_rows, mid, 0, unroll=LOOP_UNROLL)


def _to_sequence_order(cls_ref, slab_ref, mid_ref):
    per_class = cls_ref.shape[1]
    per_mid = per_class * SLABS_PER_MID
    n_slabs = cls_ref.shape[2] // LANES
    for s in range(n_slabs):
        lanes = slice(s * LANES, (s + 1) * LANES)
        for r4 in range(MID):
            for g in range(SLABS_PER_MID):
                mid_ref[s, pl.ds(r4 * per_mid + g, per_class, stride=SLABS_PER_MID), :] = (
                    cls_ref[r4 + MID * g, :, lanes].astype(F32))
        for r4 in range(MID):
            slab_ref[s, pl.ds(r4, per_mid, stride=MID), :] = mid_ref[s, r4 * per_mid:(r4 + 1) * per_mid]


def _mix_kernel(x_ref, ug_ref, gvn_ref, ws_ref, bs_ref,
                q_ref, kprev_ref, k_ref, vprev_ref, v_ref, oc_ref, lc_ref, sag_ref,
                mq_ref, mk_ref, mv_ref, smg_ref, wout_ref, out_ref, oslab_ref, lslab_ref, bias_ref,
                *, steps_per_batch):
    rows = x_ref.shape[0]
    step = pl.program_id(0)

    @pl.when(step == 0)
    def _():
        row = lax.broadcasted_iota(jnp.int32, (2 * BLOCK, 2 * BLOCK), 0) % BLOCK
        col = lax.broadcasted_iota(jnp.int32, (2 * BLOCK, 2 * BLOCK), 1)
        band = _band_bias(row - col + BLOCK)
        bias_ref[1] = band
        bias_ref[0] = jnp.where(col >= BLOCK, band, MASKED)
    tri = (lax.broadcasted_iota(jnp.int32, (CHUNK, CHUNK), 0)
           >= lax.broadcasted_iota(jnp.int32, (CHUNK, CHUNK), 1))
    left = lax.broadcasted_iota(jnp.int32, (CHUNK, LANES), 1) < HEAD_DIM
    w_tri = [jnp.where(tri, ws_ref[h], 0.0).astype(BF16) for h in range(GMLP_HEADS)]
    sp_chunks = []
    for c in range(rows // CHUNK):
        pairs = []
        for pair in range(GMLP_WIDTH // LANES):
            vn2 = gvn_ref[c * CHUNK:(c + 1) * CHUNK, pair * LANES:(pair + 1) * LANES]
            sp_a = jnp.dot(w_tri[2 * pair], vn2, preferred_element_type=F32)
            sp_b = jnp.dot(w_tri[2 * pair + 1], vn2, preferred_element_type=F32)
            pairs.append(jnp.where(left, sp_a, sp_b))
        sp_chunks.append(jnp.concatenate(pairs, axis=1) + bs_ref[...])
    y_g = ug_ref[...].astype(F32) * jnp.concatenate(sp_chunks, axis=0)

    _to_sequence_order(oc_ref, oslab_ref.at[0], oslab_ref.at[1])
    _to_sequence_order(lc_ref, lslab_ref.at[0], lslab_ref.at[1])
    first_bias = jnp.where(step % steps_per_batch == 0, 0, 1)
    block_outs = []
    for blk in range(rows // BLOCK):
        own = slice(blk * BLOCK, (blk + 1) * BLOCK)
        pair_outs = []
        for pair in range(PAIRS):
            lanes = slice(pair * LANES, (pair + 1) * LANES)
            if blk == 0:
                k2 = jnp.concatenate([kprev_ref[:, lanes], k_ref[own, lanes]], axis=0)
                v2 = jnp.concatenate([vprev_ref[:, lanes], v_ref[own, lanes]], axis=0)
                bias = bias_ref[first_bias]
            else:
                window = slice((blk - 1) * BLOCK, (blk + 1) * BLOCK)
                k2, v2, bias = k_ref[window, lanes], v_ref[window, lanes], bias_ref[1]
            acc1, den1, top1 = _pair_attention(q_ref[own, lanes], k2, v2, bias)
            o_c, top_c = oslab_ref[0, pair, own, :], lslab_ref[0, pair, own, :]
            top = jnp.maximum(top1, top_c)
            w_1, w_c = jnp.exp2(top1 - top), jnp.exp2(top_c - top)
            pair_outs.append((w_1 * acc1 + w_c * o_c) / (w_1 * den1 + w_c))
        block_outs.append(jnp.concatenate(pair_outs, axis=1))
    y_a = jnp.concatenate(block_outs, axis=0) * sag_ref[...].astype(F32)

    mem_pairs = []
    for pair in range(MEM_WIDTH // LANES):
        lanes = slice(pair * LANES, (pair + 1) * LANES)
        acc, den, _ = _pair_attention(mq_ref[:, lanes], mk_ref[:, lanes], mv_ref[:, lanes], None)
        mem_pairs.append(acc / den)
    y_m = jnp.concatenate(mem_pairs, axis=1) * smg_ref[...].astype(F32)

    y = (jnp.dot(y_g.astype(BF16), wout_ref[:GMLP_WIDTH, :], preferred_element_type=F32)
         + jnp.dot(y_a.astype(BF16), wout_ref[GMLP_WIDTH:GMLP_WIDTH + ATTN_WIDTH, :],
                   preferred_element_type=F32)
         + jnp.dot(y_m.astype(BF16), wout_ref[GMLP_WIDTH + ATTN_WIDTH:, :],
                   preferred_element_type=F32))
    out_ref[...] = x_ref[...] + y


def _block_diag_ones(width):
    head = jnp.arange(width) // HEAD_DIM
    return (head[:, None] == head[None, :]).astype(BF16)


def _cparams(sem):
    return pltpu.CompilerParams(dimension_semantics=sem, vmem_limit_bytes=VMEM_LIMIT)


def _const_spec(shape):
    return pl.BlockSpec(shape, lambda *idx: (0,) * len(shape))


def _regroup_scratch():
    return pltpu.VMEM((2, ATTN_WIDTH // LANES, ROW_TILE, LANES), F32)


def _class_tile_spec(steps_per_batch, width):
    return pl.BlockSpec((None, CLASSES, ROW_TILE // CLASSES, width),
                        lambda i: (i // steps_per_batch, 0, i % steps_per_batch, 0))


def _project(x2d, batch, seq, gain, w_in, bd, gvg, qg, kg, mqg):
    rows, d_model = x2d.shape
    widths = (GMLP_WIDTH, GMLP_WIDTH, ATTN_WIDTH, ATTN_WIDTH, ATTN_WIDTH, ATTN_WIDTH, MEM_WIDTH, MEM_WIDTH)
    cls_shape = jax.ShapeDtypeStruct((batch, CLASSES, seq // CLASSES, ATTN_WIDTH), BF16)
    cls_spec = _class_tile_spec(seq // ROW_TILE, ATTN_WIDTH)
    slab = _regroup_scratch()
    return pl.pallas_call(
        _proj_kernel,
        grid=(rows // ROW_TILE,),
        in_specs=[pl.BlockSpec((ROW_TILE, d_model), lambda i: (i, 0)),
                  _const_spec(gain.shape), _const_spec(w_in.shape), _const_spec(bd.shape),
                  _const_spec(gvg.shape), _const_spec(qg.shape), _const_spec(kg.shape),
                  _const_spec(mqg.shape)],
        out_specs=[pl.BlockSpec((ROW_TILE, w), lambda i: (i, 0)) for w in widths] + [cls_spec] * 3,
        out_shape=[jax.ShapeDtypeStruct((rows, w), BF16) for w in widths] + [cls_shape] * 3,
        scratch_shapes=[pltpu.VMEM((ROW_TILE, d_model), BF16), slab, slab, slab],
        compiler_params=_cparams(("parallel",)),
    )(x2d, gain, w_in, bd, gvg, qg, kg, mqg)


def _memory_kv(mem, gain, w_kv, bd, kg):
    batch, mem_len, d_model = mem.shape
    return pl.pallas_call(
        _memkv_kernel,
        grid=(batch,),
        in_specs=[pl.BlockSpec((None, mem_len, d_model), lambda b: (b, 0, 0)),
                  _const_spec(gain.shape), _const_spec(w_kv.shape), _const_spec(bd.shape),
                  _const_spec(kg.shape)],
        out_specs=[pl.BlockSpec((None, mem_len, MEM_WIDTH), lambda b: (b, 0, 0))] * 2,
        out_shape=[jax.ShapeDtypeStruct((batch, mem_len, MEM_WIDTH), BF16)] * 2,
        compiler_params=_cparams(("parallel",)),
    )(mem, gain, w_kv, bd, kg)


def _strided_attention(qc, kc, vc):
    batch, _, slab_len, width = qc.shape
    view = lambda t: t.reshape(batch, SLABS_PER_MID, MID, slab_len, width)
    spec = pl.BlockSpec((None, SLABS_PER_MID, None, slab_len, width), lambda b, r4: (b, 0, r4, 0, 0))
    o, lse = pl.pallas_call(
        _strided_kernel,
        grid=(batch, MID),
        in_specs=[spec] * 3,
        out_specs=[spec] * 2,
        out_shape=[jax.ShapeDtypeStruct((batch, SLABS_PER_MID, MID, slab_len, width), BF16),
                   jax.ShapeDtypeStruct((batch, SLABS_PER_MID, MID, slab_len, width), F32)],
        scratch_shapes=[pltpu.VMEM((3, SLABS_PER_MID, slab_len, width), F32),
                        pltpu.VMEM((4, 2 * BLOCK, 2 * BLOCK), F32)],
        compiler_params=_cparams(("arbitrary", "arbitrary")),
    )(view(qc), view(kc), view(vc))
    return o.reshape(qc.shape), lse.reshape(qc.shape)


def _mix(x2d, ug, gvn, w_s, b_full, q, k, v, oc, lc, sag, mq, mk, mv, smg, w_out, seq):
    rows, d_model = x2d.shape
    steps_per_batch = seq // ROW_TILE
    row_spec = lambda w: pl.BlockSpec((ROW_TILE, w), lambda i: (i, 0))
    prev_spec = pl.BlockSpec((BLOCK, ATTN_WIDTH),
                             lambda i: (jnp.maximum(i * (ROW_TILE // BLOCK) - 1, 0), 0))
    mem_spec = pl.BlockSpec((None,) + mk.shape[1:], lambda i: (i // steps_per_batch, 0, 0))
    cls_spec = _class_tile_spec(steps_per_batch, ATTN_WIDTH)
    return pl.pallas_call(
        functools.partial(_mix_kernel, steps_per_batch=steps_per_batch),
        grid=(rows // ROW_TILE,),
        in_specs=[row_spec(d_model), row_spec(GMLP_WIDTH), row_spec(GMLP_WIDTH),
                  _const_spec(w_s.shape), _const_spec(b_full.shape),
                  row_spec(ATTN_WIDTH), prev_spec, row_spec(ATTN_WIDTH), prev_spec, row_spec(ATTN_WIDTH),
                  cls_spec, cls_spec,
                  row_spec(ATTN_WIDTH), row_spec(MEM_WIDTH), mem_spec, mem_spec,
                  row_spec(MEM_WIDTH), _const_spec(w_out.shape)],
        out_specs=row_spec(d_model),
        out_shape=jax.ShapeDtypeStruct((rows, d_model), F32),
        scratch_shapes=[_regroup_scratch()] * 2 + [pltpu.VMEM((2, 2 * BLOCK, 2 * BLOCK), F32)],
        compiler_params=_cparams(("arbitrary",)),
    )(x2d, ug, gvn, w_s, b_full, q, k, k, v, v, oc, lc, sag, mq, mk, mv, smg, w_out)


def kernel(x, mem, norm_gain, w_in, gmlp_v_gain, gmlp_w_s, gmlp_b, attn_q_gain, attn_k_gain,
           mem_norm_gain, w_mem_kv, mem_q_gain, mem_k_gain, w_out):
    batch, seq, d_model = x.shape
    depth = w_in.shape[0]
    assert DILATED_CONFIGS == ((BLOCK, 1), (BLOCK * MID, MID), (BLOCK * CLASSES, CLASSES))
    assert seq % ROW_TILE == 0 and seq % (BLOCK * CLASSES) == 0
    bd = _block_diag_ones(MXU_TILE)
    x2d = x.reshape(batch * seq, d_model)
    for l in range(depth):
        row = lambda g: g.reshape(1, -1).astype(F32)
        ug, gvn, q, k, v, sag, mq, smg, qc, kc, vc = _project(
            x2d, batch, seq, row(norm_gain[l]), w_in[l].astype(BF16), bd, row(gmlp_v_gain[l]),
            row(jnp.tile(attn_q_gain[l], ATTN_HEADS)), row(jnp.tile(attn_k_gain[l], ATTN_HEADS)),
            row(jnp.tile(mem_q_gain[l], MEM_HEADS)))
        mk, mv = _memory_kv(mem, row(mem_norm_gain[l]), w_mem_kv[l].astype(BF16), bd,
                            row(jnp.tile(mem_k_gain[l], MEM_HEADS)))
        oc, lc = _strided_attention(qc, kc, vc)
        b_full = jnp.repeat(gmlp_b[l].T, HEAD_DIM, axis=1)
        x2d = _mix(x2d, ug, gvn, gmlp_w_s[l], b_full, q, k, v, oc, lc, sag, mq, mk, mv, smg,
                   w_out[l].astype(BF16), seq)
    return x2d.reshape(batch, seq, d_model)
```

```python
import functools
import math

import jax
import jax.numpy as jnp
from jax import lax
from jax.experimental import pallas as pl
from jax.experimental.pallas import tpu as pltpu

HEAD_DIM = 64
GMLP_HEADS = 4
ATTN_HEADS = 8
MEM_HEADS = 4
GMLP_WIDTH = GMLP_HEADS * HEAD_DIM
ATTN_WIDTH = ATTN_HEADS * HEAD_DIM
MEM_WIDTH = MEM_HEADS * HEAD_DIM
CHUNK = 128
BLOCK = 128
DILATED_CONFIGS = ((128, 1), (512, 4), (2048, 16))
EPS = 1e-6
MASKED = -1e30
LOG2E = math.log2(math.e)
LOGIT_SCALE = LOG2E / math.sqrt(HEAD_DIM)

IN_SEGMENTS = (("g_u", GMLP_WIDTH), ("g_v", GMLP_WIDTH), ("g_gate", GMLP_WIDTH),
               ("a_q", ATTN_WIDTH), ("a_k", ATTN_WIDTH), ("a_v", ATTN_WIDTH), ("a_gate", ATTN_WIDTH),
               ("m_q", MEM_WIDTH), ("m_gate", MEM_WIDTH))
MXU_TILE = 256

CLASSES = 16
MID = 4
SLABS_PER_MID = CLASSES // MID
LANES = 128
PAIRS = ATTN_WIDTH // LANES
ROW_TILE = 512
LOOP_UNROLL = 8
VMEM_LIMIT = 56 * 1024 * 1024

F32 = jnp.float32
BF16 = jnp.bfloat16


def _squares(acc):
    return (acc * acc).astype(BF16)


def _head_sumsq(sq, ones_bd):
    tile = ones_bd.shape[0]
    parts = [jnp.dot(sq[:, c:c + tile], ones_bd, preferred_element_type=F32)
             for c in range(0, sq.shape[1], tile)]
    return parts[0] if len(parts) == 1 else jnp.concatenate(parts, axis=1)


def _rms_scale(acc, sumsq, gain):
    return acc * lax.rsqrt(sumsq * (1.0 / HEAD_DIM) + EPS) * gain


def _head_rms(acc, ones_bd, gain):
    return _rms_scale(acc, _head_sumsq(_squares(acc), ones_bd), gain)


def _silu(x):
    return x * jax.nn.sigmoid(x)


def _store_both_layouts(val, seq_ref, cls_ref, slab_ref, mid_ref):
    seq_ref[...] = val.astype(BF16)
    rows = val.shape[0]
    per_mid, per_class = rows // MID, rows // CLASSES
    for s in range(val.shape[1] // LANES):
        lanes = slice(s * LANES, (s + 1) * LANES)
        slab_ref[s] = val[:, lanes]
        for r4 in range(MID):
            mid_ref[s, r4 * per_mid:(r4 + 1) * per_mid] = slab_ref[s, pl.ds(r4, per_mid, stride=MID), :]
        for r4 in range(MID):
            for g in range(SLABS_PER_MID):
                cls_ref[r4 + MID * g, :, lanes] = (
                    mid_ref[s, pl.ds(r4 * per_mid + g, per_class, stride=SLABS_PER_MID), :].astype(BF16))


def _proj_kernel(x_ref, gain_ref, w_ref, bd_ref, gvg_ref, qg_ref, kg_ref, mqg_ref,
                 ug_ref, gvn_ref, q_ref, k_ref, v_ref, sag_ref, mq_ref, smg_ref,
                 qc_ref, kc_ref, vc_ref, h_ref, qslab_ref, kslab_ref, vslab_ref):
    cols, o = {}, 0
    for name, width in IN_SEGMENTS:
        cols[name] = slice(o, o + width)
        o += width

    x = x_ref[...]
    ms = jnp.mean(x * x, axis=-1, keepdims=True)
    h_ref[...] = (x * lax.rsqrt(ms + EPS) * gain_ref[...]).astype(BF16)

    def project(name):
        return jnp.dot(h_ref[...], w_ref[:, cols[name]], preferred_element_type=F32)

    bd = bd_ref[...]
    g_u = project("g_u")
    g_v = project("g_v")
    sq_gv = _squares(g_v)
    g_gate = project("g_gate")
    ug_ref[...] = (g_u * _silu(g_gate)).astype(BF16)
    a_q = project("a_q")
    sq_q = _squares(a_q)
    gvn_ref[...] = _rms_scale(g_v, _head_sumsq(sq_gv, bd), gvg_ref[...]).astype(BF16)
    a_k = project("a_k")
    sq_k = _squares(a_k)
    _store_both_layouts(_rms_scale(a_q, _head_sumsq(sq_q, bd), qg_ref[...]) * LOGIT_SCALE,
                        q_ref, qc_ref, qslab_ref.at[0], qslab_ref.at[1])
    a_v = project("a_v")
    _store_both_layouts(_rms_scale(a_k, _head_sumsq(sq_k, bd), kg_ref[...]),
                        k_ref, kc_ref, kslab_ref.at[0], kslab_ref.at[1])
    a_gate = project("a_gate")
    _store_both_layouts(a_v, v_ref, vc_ref, vslab_ref.at[0], vslab_ref.at[1])
    m_q = project("m_q")
    sq_mq = _squares(m_q)
    sag_ref[...] = _silu(a_gate).astype(BF16)
    m_gate = project("m_gate")
    mq_ref[...] = (_rms_scale(m_q, _head_sumsq(sq_mq, bd), mqg_ref[...]) * LOGIT_SCALE).astype(BF16)
    smg_ref[...] = _silu(m_gate).astype(BF16)


def _memkv_kernel(mem_ref, gain_ref, w_ref, bd_ref, kg_ref, mk_ref, mv_ref):
    x = mem_ref[...]
    ms = jnp.mean(x * x, axis=-1, keepdims=True)
    h = (x * lax.rsqrt(ms + EPS) * gain_ref[...]).astype(BF16)
    mk = jnp.dot(h, w_ref[:, :MEM_WIDTH], preferred_element_type=F32)
    mv = jnp.dot(h, w_ref[:, MEM_WIDTH:], preferred_element_type=F32)
    mk_ref[...] = _head_rms(mk, bd_ref[...], kg_ref[...]).astype(BF16)
    mv_ref[...] = mv.astype(BF16)


def _pair_attention(q2, k2, v2, bias):
    rows = q2.shape[0]
    left = lax.broadcasted_iota(jnp.int32, (rows, LANES), 1) < HEAD_DIM
    zeros = jnp.zeros_like(q2)
    q_stack = jnp.concatenate([jnp.where(left, q2, zeros), jnp.where(left, zeros, q2)], axis=0)
    v_aug = jnp.concatenate([v2, jnp.ones_like(v2)], axis=1)
    s = lax.dot_general(q_stack, k2, (((1,), (1,)), ((), ())), preferred_element_type=F32)
    if bias is not None:
        s = s + bias
    m = jnp.max(s, axis=1, keepdims=True)
    p = jnp.exp2(s - m).astype(BF16)
    pv = jnp.dot(p, v_aug, preferred_element_type=F32)
    pick = lambda t: jnp.where(left, t[:rows], t[rows:])
    return pick(pv[:, :LANES]), pick(pv[:, LANES:]), pick(jnp.broadcast_to(m, (2 * rows, LANES)))


def _merge_states(a, b):
    top = jnp.maximum(a[2], b[2])
    w_a, w_b = jnp.exp2(a[2] - top), jnp.exp2(b[2] - top)
    return w_a * a[0] + w_b * b[0], w_a * a[1] + w_b * b[1], top


def _normalised(state):
    acc, den, top = state
    return acc / den, top + jnp.log(den) * LOG2E


def _band_bias(rel):
    return jnp.where((rel >= 0) & (rel <= BLOCK), 0.0, MASKED).astype(F32)


def _fill_bias_tables(bias_ref, slabs):
    row = lax.broadcasted_iota(jnp.int32, (2 * BLOCK, 2 * BLOCK), 0) % BLOCK
    col = lax.broadcasted_iota(jnp.int32, (2 * BLOCK, 2 * BLOCK), 1)
    bias_ref[0] = _band_bias(row - col)
    bias_ref[1] = _band_bias(row - col + BLOCK)
    if bias_ref.shape[0] > 2:
        q_rows, k_rows = BLOCK // slabs, 2 * BLOCK // slabs
        rel = (slabs * (row % q_rows) + row // q_rows) - (slabs * (col % k_rows) + col // k_rows)
        bias_ref[2] = _band_bias(rel)
        bias_ref[3] = _band_bias(rel + BLOCK)


def _first_grid_step():
    return (pl.program_id(0) == 0) & (pl.program_id(1) == 0)


def _strided_kernel(q_ref, k_ref, v_ref, o_ref, lse_ref, state16_ref, bias_ref):
    slab_len = q_ref.shape[1]

    @pl.when(_first_grid_step())
    def _():
        _fill_bias_tables(bias_ref, SLABS_PER_MID)

    def coarse(i, carry):
        g = i // (slab_len // BLOCK)
        blk = i % (slab_len // BLOCK)
        kstart = pl.multiple_of(jnp.maximum(blk - 1, 0) * BLOCK, BLOCK)
        generic = jnp.minimum(blk, 1)
        qstart = pl.multiple_of(blk * BLOCK, BLOCK)
        for pair in range(PAIRS):
            lanes = slice(pair * LANES, (pair + 1) * LANES)
            state = _pair_attention(q_ref[g, pl.ds(qstart, BLOCK), lanes],
                                    k_ref[g, pl.ds(kstart, 2 * BLOCK), lanes],
                                    v_ref[g, pl.ds(kstart, 2 * BLOCK), lanes], bias_ref[generic])
            for part in range(3):
                state16_ref[part, g, pl.ds(qstart, BLOCK), lanes] = state[part]
        return carry

    lax.fori_loop(0, SLABS_PER_MID * (slab_len // BLOCK), coarse, 0, unroll=LOOP_UNROLL)

    q_rows = BLOCK // SLABS_PER_MID
    k_rows = 2 * q_rows

    def gather(ref, start, rows, lanes):
        return jnp.concatenate([ref[g, pl.ds(start, rows), lanes] for g in range(SLABS_PER_MID)], axis=0)

    def gather_state16(start, rows, lanes):
        return tuple(gather(state16_ref.at[part], start, rows, lanes) for part in range(3))

    def mid(i, carry):
        a0 = pl.multiple_of(i * q_rows, q_rows)
        kstart = pl.multiple_of(jnp.maximum(i - 1, 0) * q_rows, q_rows)
        generic = jnp.minimum(i, 1)
        for pair in range(PAIRS):
            lanes = slice(pair * LANES, (pair + 1) * LANES)
            state4 = _pair_attention(gather(q_ref, a0, q_rows, lanes),
                                     gather(k_ref, kstart, k_rows, lanes),
                                     gather(v_ref, kstart, k_rows, lanes), bias_ref[2 + generic])
            o, lse = _normalised(_merge_states(state4, gather_state16(a0, q_rows, lanes)))
            for g in range(SLABS_PER_MID):
                o_ref[g, pl.ds(a0, q_rows), lanes] = o[g * q_rows:(g + 1) * q_rows].astype(BF16)
                lse_ref[g, pl.ds(a0, q_rows), lanes] = lse[g * q_rows:(g + 1) * q_rows]
        return carry

    lax.fori_loop(0, slab_len // q_rows, mid, 0, unroll=LOOP_UNROLL)


def _to_sequence_order(cls_ref, slab_ref, mid_ref):
    per_class = cls_ref.shape[1]
    per_mid = per_class * SLABS_PER_MID
    n_slabs = cls_ref.shape[2] // LANES
    for s in range(n_slabs):
        lanes = slice(s * LANES, (s + 1) * LANES)
        for r4 in range(MID):
            for g in range(SLABS_PER_MID):
                mid_ref[s, pl.ds(r4 * per_mid + g, per_class, stride=SLABS_PER_MID), :] = (
                    cls_ref[r4 + MID * g, :, lanes].astype(F32))
        for r4 in range(MID):
            slab_ref[s, pl.ds(r4, per_mid, stride=MID), :] = mid_ref[s, r4 * per_mid:(r4 + 1) * per_mid]


def _mix_kernel(x_ref, ug_ref, gvn_ref, ws_ref, bs_ref,
                q_ref, kprev_ref, k_ref, vprev_ref, v_ref, oc_ref, lc_ref, sag_ref,
                mq_ref, mk_ref, mv_ref, smg_ref, wout_ref, out_ref, oslab_ref, lslab_ref, bias_ref,
                *, steps_per_batch):
    rows = x_ref.shape[0]
    step = pl.program_id(0)

    @pl.when(step == 0)
    def _():
        row = lax.broadcasted_iota(jnp.int32, (2 * BLOCK, 2 * BLOCK), 0) % BLOCK
        col = lax.broadcasted_iota(jnp.int32, (2 * BLOCK, 2 * BLOCK), 1)
        band = _band_bias(row - col + BLOCK)
        bias_ref[1] = band
        bias_ref[0] = jnp.where(col >= BLOCK, band, MASKED)
    tri = (lax.broadcasted_iota(jnp.int32, (CHUNK, CHUNK), 0)
           >= lax.broadcasted_iota(jnp.int32, (CHUNK, CHUNK), 1))
    left = lax.broadcasted_iota(jnp.int32, (CHUNK, LANES), 1) < HEAD_DIM
    w_tri = [jnp.where(tri, ws_ref[h], 0.0).astype(BF16) for h in range(GMLP_HEADS)]
    sp_chunks = []
    for c in range(rows // CHUNK):
        pairs = []
        for pair in range(GMLP_WIDTH // LANES):
            vn2 = gvn_ref[c * CHUNK:(c + 1) * CHUNK, pair * LANES:(pair + 1) * LANES]
            sp_a = jnp.dot(w_tri[2 * pair], vn2, preferred_element_type=F32)
            sp_b = jnp.dot(w_tri[2 * pair + 1], vn2, preferred_element_type=F32)
            pairs.append(jnp.where(left, sp_a, sp_b))
        sp_chunks.append(jnp.concatenate(pairs, axis=1) + bs_ref[...])
    y_g = ug_ref[...].astype(F32) * jnp.concatenate(sp_chunks, axis=0)

    _to_sequence_order(oc_ref, oslab_ref.at[0], oslab_ref.at[1])
    _to_sequence_order(lc_ref, lslab_ref.at[0], lslab_ref.at[1])
    first_bias = jnp.where(step % steps_per_batch == 0, 0, 1)
    block_outs = []
    for blk in range(rows // BLOCK):
        own = slice(blk * BLOCK, (blk + 1) * BLOCK)
        pair_outs = []
        for pair in range(PAIRS):
            lanes = slice(pair * LANES, (pair + 1) * LANES)
            if blk == 0:
                k2 = jnp.concatenate([kprev_ref[:, lanes], k_ref[own, lanes]], axis=0)
                v2 = jnp.concatenate([vprev_ref[:, lanes], v_ref[own, lanes]], axis=0)
                bias = bias_ref[first_bias]
            else:
                window = slice((blk - 1) * BLOCK, (blk + 1) * BLOCK)
                k2, v2, bias = k_ref[window, lanes], v_ref[window, lanes], bias_ref[1]
            acc1, den1, top1 = _pair_attention(q_ref[own, lanes], k2, v2, bias)
            o_c, top_c = oslab_ref[0, pair, own, :], lslab_ref[0, pair, own, :]
            top = jnp.maximum(top1, top_c)
            w_1, w_c = jnp.exp2(top1 - top), jnp.exp2(top_c - top)
            pair_outs.append((w_1 * acc1 + w_c * o_c) / (w_1 * den1 + w_c))
        block_outs.append(jnp.concatenate(pair_outs, axis=1))
    y_a = jnp.concatenate(block_outs, axis=0) * sag_ref[...].astype(F32)

    mem_pairs = []
    for pair in range(MEM_WIDTH // LANES):
        lanes = slice(pair * LANES, (pair + 1) * LANES)
        acc, den, _ = _pair_attention(mq_ref[:, lanes], mk_ref[:, lanes], mv_ref[:, lanes], None)
        mem_pairs.append(acc / den)
    y_m = jnp.concatenate(mem_pairs, axis=1) * smg_ref[...].astype(F32)

    y = (jnp.dot(y_g.astype(BF16), wout_ref[:GMLP_WIDTH, :], preferred_element_type=F32)
         + jnp.dot(y_a.astype(BF16), wout_ref[GMLP_WIDTH:GMLP_WIDTH + ATTN_WIDTH, :],
                   preferred_element_type=F32)
         + jnp.dot(y_m.astype(BF16), wout_ref[GMLP_WIDTH + ATTN_WIDTH:, :],
                   preferred_element_type=F32))
    out_ref[...] = x_ref[...] + y


def _block_diag_ones(width):
    head = jnp.arange(width) // HEAD_DIM
    return (head[:, None] == head[None, :]).astype(BF16)


def _cparams(sem):
    return pltpu.CompilerParams(dimension_semantics=sem, vmem_limit_bytes=VMEM_LIMIT)


def _const_spec(shape):
    return pl.BlockSpec(shape, lambda *idx: (0,) * len(shape))


def _regroup_scratch():
    return pltpu.VMEM((2, ATTN_WIDTH // LANES, ROW_TILE, LANES), F32)


def _class_tile_spec(steps_per_batch, width):
    return pl.BlockSpec((None, CLASSES, ROW_TILE // CLASSES, width),
                        lambda i: (i // steps_per_batch, 0, i % steps_per_batch, 0))


def _project(x2d, batch, seq, gain, w_in, bd, gvg, qg, kg, mqg):
    rows, d_model = x2d.shape
    widths = (GMLP_WIDTH, GMLP_WIDTH, ATTN_WIDTH, ATTN_WIDTH, ATTN_WIDTH, ATTN_WIDTH, MEM_WIDTH, MEM_WIDTH)
    cls_shape = jax.ShapeDtypeStruct((batch, CLASSES, seq // CLASSES, ATTN_WIDTH), BF16)
    cls_spec = _class_tile_spec(seq // ROW_TILE, ATTN_WIDTH)
    slab = _regroup_scratch()
    return pl.pallas_call(
        _proj_kernel,
        grid=(rows // ROW_TILE,),
        in_specs=[pl.BlockSpec((ROW_TILE, d_model), lambda i: (i, 0)),
                  _const_spec(gain.shape), _const_spec(w_in.shape), _const_spec(bd.shape),
                  _const_spec(gvg.shape), _const_spec(qg.shape), _const_spec(kg.shape),
                  _const_spec(mqg.shape)],
        out_specs=[pl.BlockSpec((ROW_TILE, w), lambda i: (i, 0)) for w in widths] + [cls_spec] * 3,
        out_shape=[jax.ShapeDtypeStruct((rows, w), BF16) for w in widths] + [cls_shape] * 3,
        scratch_shapes=[pltpu.VMEM((ROW_TILE, d_model), BF16), slab, slab, slab],
        compiler_params=_cparams(("parallel",)),
    )(x2d, gain, w_in, bd, gvg, qg, kg, mqg)


def _memory_kv(mem, gain, w_kv, bd, kg):
    batch, mem_len, d_model = mem.shape
    return pl.pallas_call(
        _memkv_kernel,
        grid=(batch,),
        in_specs=[pl.BlockSpec((None, mem_len, d_model), lambda b: (b, 0, 0)),
                  _const_spec(gain.shape), _const_spec(w_kv.shape), _const_spec(bd.shape),
                  _const_spec(kg.shape)],
        out_specs=[pl.BlockSpec((None, mem_len, MEM_WIDTH), lambda b: (b, 0, 0))] * 2,
        out_shape=[jax.ShapeDtypeStruct((batch, mem_len, MEM_WIDTH), BF16)] * 2,
        compiler_params=_cparams(("parallel",)),
    )(mem, gain, w_kv, bd, kg)


def _strided_attention(qc, kc, vc):
    batch, _, slab_len, width = qc.shape
    view = lambda t: t.reshape(batch, SLABS_PER_MID, MID, slab_len, width)
    spec = pl.BlockSpec((None, SLABS_PER_MID, None, slab_len, width), lambda b, r4: (b, 0, r4, 0, 0))
    o, lse = pl.pallas_call(
        _strided_kernel,
        grid=(batch, MID),
        in_specs=[spec] * 3,
        out_specs=[spec] * 2,
        out_shape=[jax.ShapeDtypeStruct((batch, SLABS_PER_MID, MID, slab_len, width), BF16),
                   jax.ShapeDtypeStruct((batch, SLABS_PER_MID, MID, slab_len, width), F32)],
        scratch_shapes=[pltpu.VMEM((3, SLABS_PER_MID, slab_len, width), F32),
                        pltpu.VMEM((4, 2 * BLOCK, 2 * BLOCK), F32)],
        compiler_params=_cparams(("arbitrary", "arbitrary")),
    )(view(qc), view(kc), view(vc))
    return o.reshape(qc.shape), lse.reshape(qc.shape)


def _mix(x2d, ug, gvn, w_s, b_full, q, k, v, oc, lc, sag, mq, mk, mv, smg, w_out, seq):
    rows, d_model = x2d.shape
    steps_per_batch = seq // ROW_TILE
    row_spec = lambda w: pl.BlockSpec((ROW_TILE, w), lambda i: (i, 0))
    prev_spec = pl.BlockSpec((BLOCK, ATTN_WIDTH),
                             lambda i: (jnp.maximum(i * (ROW_TILE // BLOCK) - 1, 0), 0))
    mem_spec = pl.BlockSpec((None,) + mk.shape[1:], lambda i: (i // steps_per_batch, 0, 0))
    cls_spec = _class_tile_spec(steps_per_batch, ATTN_WIDTH)
    return pl.pallas_call(
        functools.partial(_mix_kernel, steps_per_batch=steps_per_batch),
        grid=(rows // ROW_TILE,),
        in_specs=[row_spec(d_model), row_spec(GMLP_WIDTH), row_spec(GMLP_WIDTH),
                  _const_spec(w_s.shape), _const_spec(b_full.shape),
                  row_spec(ATTN_WIDTH), prev_spec, row_spec(ATTN_WIDTH), prev_spec, row_spec(ATTN_WIDTH),
                  cls_spec, cls_spec,
                  row_spec(ATTN_WIDTH), row_spec(MEM_WIDTH), mem_spec, mem_spec,
                  row_spec(MEM_WIDTH), _const_spec(w_out.shape)],
        out_specs=row_spec(d_model),
        out_shape=jax.ShapeDtypeStruct((rows, d_model), F32),
        scratch_shapes=[_regroup_scratch()] * 2 + [pltpu.VMEM((2, 2 * BLOCK, 2 * BLOCK), F32)],
        compiler_params=_cparams(("arbitrary",)),
    )(x2d, ug, gvn, w_s, b_full, q, k, k, v, v, oc, lc, sag, mq, mk, mv, smg, w_out)


def kernel(x, mem, norm_gain, w_in, gmlp_v_gain, gmlp_w_s, gmlp_b, attn_q_gain, attn_k_gain,
           mem_norm_gain, w_mem_kv, mem_q_gain, mem_k_gain, w_out):
    batch, seq, d_model = x.shape
    depth = w_in.shape[0]
    assert DILATED_CONFIGS == ((BLOCK, 1), (BLOCK * MID, MID), (BLOCK * CLASSES, CLASSES))
    assert seq % ROW_TILE == 0 and seq % (BLOCK * CLASSES) == 0
    bd = _block_diag_ones(MXU_TILE)
    x2d = x.reshape(batch * seq, d_model)
    for l in range(depth):
        row = lambda g: g.reshape(1, -1).astype(F32)
        ug, gvn, q, k, v, sag, mq, smg, qc, kc, vc = _project(
            x2d, batch, seq, row(norm_gain[l]), w_in[l].astype(BF16), bd, row(gmlp_v_gain[l]),
            row(jnp.tile(attn_q_gain[l], ATTN_HEADS)), row(jnp.tile(attn_k_gain[l], ATTN_HEADS)),
            row(jnp.tile(mem_q_gain[l], MEM_HEADS)))
        mk, mv = _memory_kv(mem, row(mem_norm_gain[l]), w_mem_kv[l].astype(BF16), bd,
                            row(jnp.tile(mem_k_gain[l], MEM_HEADS)))
        oc, lc = _strided_attention(qc, kc, vc)
        b_full = jnp.repeat(gmlp_b[l].T, HEAD_DIM, axis=1)
        x2d = _mix(x2d, ug, gvn, gmlp_w_s[l], b_full, q, k, v, oc, lc, sag, mq, mk, mv, smg,
                   w_out[l].astype(BF16), seq)
    return x2d.reshape(batch, seq, d_model)
```

```python
import functools
import math

import jax
import jax.numpy as jnp
import numpy as np
from jax import lax
from jax.experimental import pallas as pl
from jax.experimental.pallas import tpu as pltpu

HEAD_DIM = 64
GMLP_HEADS = 4
ATTN_HEADS = 8
MEM_HEADS = 4
GMLP_WIDTH = GMLP_HEADS * HEAD_DIM
ATTN_WIDTH = ATTN_HEADS * HEAD_DIM
MEM_WIDTH = MEM_HEADS * HEAD_DIM
CHUNK = 128
BLOCK = 128
DILATED_CONFIGS = ((128, 1), (512, 4), (2048, 16))
EPS = 1e-6
MASKED = -1e30
LOG2E = math.log2(math.e)
LOGIT_SCALE = LOG2E / math.sqrt(HEAD_DIM)

IN_SEGMENTS = (("g_u", GMLP_WIDTH), ("g_v", GMLP_WIDTH), ("g_gate", GMLP_WIDTH),
               ("a_q", ATTN_WIDTH), ("a_k", ATTN_WIDTH), ("a_v", ATTN_WIDTH), ("a_gate", ATTN_WIDTH),
               ("m_q", MEM_WIDTH), ("m_gate", MEM_WIDTH))
MXU_TILE = 256

CLASSES = 16
MID = 4
SLABS_PER_MID = CLASSES // MID
LANES = 128
PAIRS = ATTN_WIDTH // LANES
ROW_TILE = 512
LOOP_UNROLL = 8
VMEM_LIMIT = 56 * 1024 * 1024

F32 = jnp.float32
BF16 = jnp.bfloat16


def _squares(acc):
    return (acc * acc).astype(BF16)


def _head_sumsq(sq, ones_bd):
    tile = ones_bd.shape[0]
    parts = [jnp.dot(sq[:, c:c + tile], ones_bd, preferred_element_type=F32)
             for c in range(0, sq.shape[1], tile)]
    return parts[0] if len(parts) == 1 else jnp.concatenate(parts, axis=1)


def _rms_scale(acc, sumsq, gain):
    return acc * lax.rsqrt(sumsq * (1.0 / HEAD_DIM) + EPS) * gain


def _head_rms(acc, ones_bd, gain):
    return _rms_scale(acc, _head_sumsq(_squares(acc), ones_bd), gain)


def _silu(x):
    return x * jax.nn.sigmoid(x)


def _store_both_layouts(val, seq_ref, cls_ref, slab_ref, mid_ref):
    seq_ref[...] = val.astype(BF16)
    rows = val.shape[0]
    per_mid, per_class = rows // MID, rows // CLASSES
    for s in range(val.shape[1] // LANES):
        lanes = slice(s * LANES, (s + 1) * LANES)
        slab_ref[s] = val[:, lanes]
        for r4 in range(MID):
            mid_ref[s, r4 * per_mid:(r4 + 1) * per_mid] = slab_ref[s, pl.ds(r4, per_mid, stride=MID), :]
        for r4 in range(MID):
            for g in range(SLABS_PER_MID):
                cls_ref[r4 + MID * g, :, lanes] = (
                    mid_ref[s, pl.ds(r4 * per_mid + g, per_class, stride=SLABS_PER_MID), :].astype(BF16))


def _proj_kernel(x_ref, gains_ref, w32_ref, bd_ref,
                 ug_ref, gvn_ref, q_ref, k_ref, v_ref, sag_ref, mq_ref, smg_ref,
                 qc_ref, kc_ref, vc_ref, w_ref, h_ref, qslab_ref, kslab_ref, vslab_ref):
    d_model = x_ref.shape[1]

    @pl.when(pl.program_id(0) == 0)
    def _():
        w_ref[...] = w32_ref[...].astype(BF16)

    cols, o = {}, 0
    for name, width in IN_SEGMENTS:
        cols[name] = slice(o, o + width)
        o += width
    gains, o = {}, 0
    for name, width in (("x", d_model), ("g_v", GMLP_WIDTH), ("a_q", ATTN_WIDTH), ("a_k", ATTN_WIDTH),
                        ("m_q", MEM_WIDTH)):
        gains[name] = gains_ref[:, o:o + width]
        o += width

    x = x_ref[...]
    ms = jnp.mean(x * x, axis=-1, keepdims=True)
    h_ref[...] = (x * lax.rsqrt(ms + EPS) * gains["x"]).astype(BF16)

    def project(name):
        return jnp.dot(h_ref[...], w_ref[:, cols[name]], preferred_element_type=F32)

    bd = bd_ref[...]
    g_u = project("g_u")
    g_v = project("g_v")
    sq_gv = _squares(g_v)
    g_gate = project("g_gate")
    ug_ref[...] = (g_u * _silu(g_gate)).astype(BF16)
    a_q = project("a_q")
    sq_q = _squares(a_q)
    gvn_ref[...] = _rms_scale(g_v, _head_sumsq(sq_gv, bd), gains["g_v"]).astype(BF16)
    a_k = project("a_k")
    sq_k = _squares(a_k)
    _store_both_layouts(_rms_scale(a_q, _head_sumsq(sq_q, bd), gains["a_q"]) * LOGIT_SCALE,
                        q_ref, qc_ref, qslab_ref.at[0], qslab_ref.at[1])
    a_v = project("a_v")
    _store_both_layouts(_rms_scale(a_k, _head_sumsq(sq_k, bd), gains["a_k"]),
                        k_ref, kc_ref, kslab_ref.at[0], kslab_ref.at[1])
    a_gate = project("a_gate")
    _store_both_layouts(a_v, v_ref, vc_ref, vslab_ref.at[0], vslab_ref.at[1])
    m_q = project("m_q")
    sq_mq = _squares(m_q)
    sag_ref[...] = _silu(a_gate).astype(BF16)
    m_gate = project("m_gate")
    mq_ref[...] = (_rms_scale(m_q, _head_sumsq(sq_mq, bd), gains["m_q"]) * LOGIT_SCALE).astype(BF16)
    smg_ref[...] = _silu(m_gate).astype(BF16)


def _memkv_kernel(mem_ref, gains_ref, w_ref, bd_ref, mk_ref, mv_ref):
    d_model = mem_ref.shape[1]
    x = mem_ref[...]
    ms = jnp.mean(x * x, axis=-1, keepdims=True)
    h = (x * lax.rsqrt(ms + EPS) * gains_ref[:, :d_model]).astype(BF16)
    w = w_ref[...].astype(BF16)
    mk = jnp.dot(h, w[:, :MEM_WIDTH], preferred_element_type=F32)
    mv = jnp.dot(h, w[:, MEM_WIDTH:], preferred_element_type=F32)
    mk_ref[...] = _head_rms(mk, bd_ref[...], gains_ref[:, d_model:]).astype(BF16)
    mv_ref[...] = mv.astype(BF16)


def _pair_attention(q2, k2, v2, bias):
    rows = q2.shape[0]
    left = lax.broadcasted_iota(jnp.int32, (rows, LANES), 1) < HEAD_DIM
    zeros = jnp.zeros_like(q2)
    q_stack = jnp.concatenate([jnp.where(left, q2, zeros), jnp.where(left, zeros, q2)], axis=0)
    v_aug = jnp.concatenate([v2, jnp.ones_like(v2)], axis=1)
    s = lax.dot_general(q_stack, k2, (((1,), (1,)), ((), ())), preferred_element_type=F32)
    if bias is not None:
        s = s + bias
    m = jnp.max(s, axis=1, keepdims=True)
    p = jnp.exp2(s - m).astype(BF16)
    pv = jnp.dot(p, v_aug, preferred_element_type=F32)
    pick = lambda t: jnp.where(left, t[:rows], t[rows:])
    return pick(pv[:, :LANES]), pick(pv[:, LANES:]), pick(jnp.broadcast_to(m, (2 * rows, LANES)))


def _merge_states(a, b):
    top = jnp.maximum(a[2], b[2])
    w_a, w_b = jnp.exp2(a[2] - top), jnp.exp2(b[2] - top)
    return w_a * a[0] + w_b * b[0], w_a * a[1] + w_b * b[1], top


def _normalised(state):
    acc, den, top = state
    return acc / den, top + jnp.log(den) * LOG2E


def _band_bias(rel):
    return jnp.where((rel >= 0) & (rel <= BLOCK), 0.0, MASKED).astype(F32)


def _fill_bias_tables(bias_ref, slabs):
    row = lax.broadcasted_iota(jnp.int32, (2 * BLOCK, 2 * BLOCK), 0) % BLOCK
    col = lax.broadcasted_iota(jnp.int32, (2 * BLOCK, 2 * BLOCK), 1)
    bias_ref[0] = _band_bias(row - col)
    bias_ref[1] = _band_bias(row - col + BLOCK)
    if bias_ref.shape[0] > 2:
        q_rows, k_rows = BLOCK // slabs, 2 * BLOCK // slabs
        rel = (slabs * (row % q_rows) + row // q_rows) - (slabs * (col % k_rows) + col // k_rows)
        bias_ref[2] = _band_bias(rel)
        bias_ref[3] = _band_bias(rel + BLOCK)


def _first_grid_step():
    return (pl.program_id(0) == 0) & (pl.program_id(1) == 0)


def _strided_kernel(q_ref, k_ref, v_ref, o_ref, lse_ref, state16_ref, bias_ref):
    slab_len = q_ref.shape[1]

    @pl.when(_first_grid_step())
    def _():
        _fill_bias_tables(bias_ref, SLABS_PER_MID)

    def coarse(i, carry):
        g = i // (slab_len // BLOCK)
        blk = i % (slab_len // BLOCK)
        kstart = pl.multiple_of(jnp.maximum(blk - 1, 0) * BLOCK, BLOCK)
        generic = jnp.minimum(blk, 1)
        qstart = pl.multiple_of(blk * BLOCK, BLOCK)
        for pair in range(PAIRS):
            lanes = slice(pair * LANES, (pair + 1) * LANES)
            state = _pair_attention(q_ref[g, pl.ds(qstart, BLOCK), lanes],
                                    k_ref[g, pl.ds(kstart, 2 * BLOCK), lanes],
                                    v_ref[g, pl.ds(kstart, 2 * BLOCK), lanes], bias_ref[generic])
            for part in range(3):
                state16_ref[part, g, pl.ds(qstart, BLOCK), lanes] = state[part]
        return carry

    lax.fori_loop(0, SLABS_PER_MID * (slab_len // BLOCK), coarse, 0, unroll=LOOP_UNROLL)

    q_rows = BLOCK // SLABS_PER_MID
    k_rows = 2 * q_rows

    def gather(ref, start, rows, lanes):
        return jnp.concatenate([ref[g, pl.ds(start, rows), lanes] for g in range(SLABS_PER_MID)], axis=0)

    def gather_state16(start, rows, lanes):
        return tuple(gather(state16_ref.at[part], start, rows, lanes) for part in range(3))

    def mid(i, carry):
        a0 = pl.multiple_of(i * q_rows, q_rows)
        kstart = pl.multiple_of(jnp.maximum(i - 1, 0) * q_rows, q_rows)
        generic = jnp.minimum(i, 1)
        for pair in range(PAIRS):
            lanes = slice(pair * LANES, (pair + 1) * LANES)
            state4 = _pair_attention(gather(q_ref, a0, q_rows, lanes),
                                     gather(k_ref, kstart, k_rows, lanes),
                                     gather(v_ref, kstart, k_rows, lanes), bias_ref[2 + generic])
            o, lse = _normalised(_merge_states(state4, gather_state16(a0, q_rows, lanes)))
            for g in range(SLABS_PER_MID):
                o_ref[g, pl.ds(a0, q_rows), lanes] = o[g * q_rows:(g + 1) * q_rows].astype(BF16)
                lse_ref[g, pl.ds(a0, q_rows), lanes] = lse[g * q_rows:(g + 1) * q_rows]
        return carry

    lax.fori_loop(0, slab_len // q_rows, mid, 0, unroll=LOOP_UNROLL)


def _to_sequence_order(cls_ref, slab_ref, mid_ref):
    per_class = cls_ref.shape[1]
    per_mid = per_class * SLABS_PER_MID
    n_slabs = cls_ref.shape[2] // LANES
    for s in range(n_slabs):
        lanes = slice(s * LANES, (s + 1) * LANES)
        for r4 in range(MID):
            for g in range(SLABS_PER_MID):
                mid_ref[s, pl.ds(r4 * per_mid + g, per_class, stride=SLABS_PER_MID), :] = (
                    cls_ref[r4 + MID * g, :, lanes].astype(F32))
        for r4 in range(MID):
            slab_ref[s, pl.ds(r4, per_mid, stride=MID), :] = mid_ref[s, r4 * per_mid:(r4 + 1) * per_mid]


def _mix_kernel(x_ref, ug_ref, gvn_ref, ws_ref, bs_ref,
                q_ref, kprev_ref, k_ref, vprev_ref, v_ref, oc_ref, lc_ref, sag_ref,
                mq_ref, mk_ref, mv_ref, smg_ref, wout32_ref, out_ref, oslab_ref, lslab_ref, bias_ref,
                wout_ref, *, steps_per_batch):
    rows = x_ref.shape[0]
    step = pl.program_id(0)

    @pl.when(step == 0)
    def _():
        wout_ref[...] = wout32_ref[...].astype(BF16)
        row = lax.broadcasted_iota(jnp.int32, (2 * BLOCK, 2 * BLOCK), 0) % BLOCK
        col = lax.broadcasted_iota(jnp.int32, (2 * BLOCK, 2 * BLOCK), 1)
        band = _band_bias(row - col + BLOCK)
        bias_ref[1] = band
        bias_ref[0] = jnp.where(col >= BLOCK, band, MASKED)
    tri = (lax.broadcasted_iota(jnp.int32, (CHUNK, CHUNK), 0)
           >= lax.broadcasted_iota(jnp.int32, (CHUNK, CHUNK), 1))
    left = lax.broadcasted_iota(jnp.int32, (CHUNK, LANES), 1) < HEAD_DIM
    w_tri = [jnp.where(tri, ws_ref[h], 0.0).astype(BF16) for h in range(GMLP_HEADS)]
    sp_chunks = []
    for c in range(rows // CHUNK):
        pairs = []
        for pair in range(GMLP_WIDTH // LANES):
            vn2 = gvn_ref[c * CHUNK:(c + 1) * CHUNK, pair * LANES:(pair + 1) * LANES]
            sp_a = jnp.dot(w_tri[2 * pair], vn2, preferred_element_type=F32)
            sp_b = jnp.dot(w_tri[2 * pair + 1], vn2, preferred_element_type=F32)
            pairs.append(jnp.where(left, sp_a, sp_b))
        sp_chunks.append(jnp.concatenate(pairs, axis=1) + bs_ref[...])
    y_g = ug_ref[...].astype(F32) * jnp.concatenate(sp_chunks, axis=0)

    mem_pairs = []
    for pair in range(MEM_WIDTH // LANES):
        lanes = slice(pair * LANES, (pair + 1) * LANES)
        acc, den, _ = _pair_attention(mq_ref[:, lanes], mk_ref[:, lanes], mv_ref[:, lanes], None)
        mem_pairs.append(acc / den)
    y_m = jnp.concatenate(mem_pairs, axis=1) * smg_ref[...].astype(F32)

    _to_sequence_order(oc_ref, oslab_ref.at[0], oslab_ref.at[1])
    _to_sequence_order(lc_ref, lslab_ref.at[0], lslab_ref.at[1])
    first_bias = jnp.where(step % steps_per_batch == 0, 0, 1)
    block_outs = []
    n_blocks = rows // BLOCK
    for blk in range(n_blocks):
        if blk == n_blocks // 2:
            partial = (x_ref[...]
                       + jnp.dot(y_g.astype(BF16), wout_ref[:GMLP_WIDTH, :], preferred_element_type=F32)
                       + jnp.dot(y_m.astype(BF16), wout_ref[GMLP_WIDTH + ATTN_WIDTH:, :],
                                 preferred_element_type=F32))
        own = slice(blk * BLOCK, (blk + 1) * BLOCK)
        pair_outs = []
        for pair in range(PAIRS):
            lanes = slice(pair * LANES, (pair + 1) * LANES)
            if blk == 0:
                k2 = jnp.concatenate([kprev_ref[:, lanes], k_ref[own, lanes]], axis=0)
                v2 = jnp.concatenate([vprev_ref[:, lanes], v_ref[own, lanes]], axis=0)
                bias = bias_ref[first_bias]
            else:
                window = slice((blk - 1) * BLOCK, (blk + 1) * BLOCK)
                k2, v2, bias = k_ref[window, lanes], v_ref[window, lanes], bias_ref[1]
            acc1, den1, top1 = _pair_attention(q_ref[own, lanes], k2, v2, bias)
            o_c, top_c = oslab_ref[0, pair, own, :], lslab_ref[0, pair, own, :]
            top = jnp.maximum(top1, top_c)
            w_1, w_c = jnp.exp2(top1 - top), jnp.exp2(top_c - top)
            pair_outs.append((w_1 * acc1 + w_c * o_c) / (w_1 * den1 + w_c))
        block_outs.append(jnp.concatenate(pair_outs, axis=1))
    y_a = jnp.concatenate(block_outs, axis=0) * sag_ref[...].astype(F32)
    out_ref[...] = partial + jnp.dot(y_a.astype(BF16), wout_ref[GMLP_WIDTH:GMLP_WIDTH + ATTN_WIDTH, :],
                                     preferred_element_type=F32)


def _block_diag_ones(width):
    head = np.arange(width) // HEAD_DIM
    return jnp.asarray(head[:, None] == head[None, :], dtype=BF16)


def _cparams(sem):
    return pltpu.CompilerParams(dimension_semantics=sem, vmem_limit_bytes=VMEM_LIMIT)


def _const_spec(shape):
    return pl.BlockSpec(shape, lambda *idx: (0,) * len(shape))


def _regroup_scratch():
    return pltpu.VMEM((2, ATTN_WIDTH // LANES, ROW_TILE, LANES), F32)


def _class_tile_spec(steps_per_batch, width):
    return pl.BlockSpec((None, CLASSES, ROW_TILE // CLASSES, width),
                        lambda i: (i // steps_per_batch, 0, i % steps_per_batch, 0))


def _project(x2d, batch, seq, gains, w_in, bd):
    rows, d_model = x2d.shape
    widths = (GMLP_WIDTH, GMLP_WIDTH, ATTN_WIDTH, ATTN_WIDTH, ATTN_WIDTH, ATTN_WIDTH, MEM_WIDTH, MEM_WIDTH)
    cls_shape = jax.ShapeDtypeStruct((batch, CLASSES, seq // CLASSES, ATTN_WIDTH), BF16)
    cls_spec = _class_tile_spec(seq // ROW_TILE, ATTN_WIDTH)
    slab = _regroup_scratch()
    return pl.pallas_call(
        _proj_kernel,
        grid=(rows // ROW_TILE,),
        in_specs=[pl.BlockSpec((ROW_TILE, d_model), lambda i: (i, 0)),
                  _const_spec(gains.shape),
                  pl.BlockSpec(w_in.shape, lambda i: (0, 0), pipeline_mode=pl.Buffered(1)),
                  _const_spec(bd.shape)],
        out_specs=[pl.BlockSpec((ROW_TILE, w), lambda i: (i, 0)) for w in widths] + [cls_spec] * 3,
        out_shape=[jax.ShapeDtypeStruct((rows, w), BF16) for w in widths] + [cls_shape] * 3,
        scratch_shapes=[pltpu.VMEM(w_in.shape, BF16), pltpu.VMEM((ROW_TILE, d_model), BF16),
                        slab, slab, slab],
        compiler_params=_cparams(("arbitrary",)),
    )(x2d, gains, w_in, bd)


def _memory_kv(mem, gains, w_kv, bd):
    batch, mem_len, d_model = mem.shape
    return pl.pallas_call(
        _memkv_kernel,
        grid=(batch,),
        in_specs=[pl.BlockSpec((None, mem_len, d_model), lambda b: (b, 0, 0)),
                  _const_spec(gains.shape), _const_spec(w_kv.shape), _const_spec(bd.shape)],
        out_specs=[pl.BlockSpec((None, mem_len, MEM_WIDTH), lambda b: (b, 0, 0))] * 2,
        out_shape=[jax.ShapeDtypeStruct((batch, mem_len, MEM_WIDTH), BF16)] * 2,
        compiler_params=_cparams(("parallel",)),
    )(mem, gains, w_kv, bd)


def _strided_attention(qc, kc, vc):
    batch, _, slab_len, width = qc.shape
    view = lambda t: t.reshape(batch, SLABS_PER_MID, MID, slab_len, width)
    spec = pl.BlockSpec((None, SLABS_PER_MID, None, slab_len, width), lambda b, r4: (b, 0, r4, 0, 0))
    o, lse = pl.pallas_call(
        _strided_kernel,
        grid=(batch, MID),
        in_specs=[spec] * 3,
        out_specs=[spec] * 2,
        out_shape=[jax.ShapeDtypeStruct((batch, SLABS_PER_MID, MID, slab_len, width), BF16),
                   jax.ShapeDtypeStruct((batch, SLABS_PER_MID, MID, slab_len, width), F32)],
        scratch_shapes=[pltpu.VMEM((3, SLABS_PER_MID, slab_len, width), F32),
                        pltpu.VMEM((4, 2 * BLOCK, 2 * BLOCK), F32)],
        compiler_params=_cparams(("arbitrary", "arbitrary")),
    )(view(qc), view(kc), view(vc))
    return o.reshape(qc.shape), lse.reshape(qc.shape)


def _mix(x2d, ug, gvn, w_s, b_full, q, k, v, oc, lc, sag, mq, mk, mv, smg, w_out, seq):
    rows, d_model = x2d.shape
    steps_per_batch = seq // ROW_TILE
    row_spec = lambda w: pl.BlockSpec((ROW_TILE, w), lambda i: (i, 0))
    prev_spec = pl.BlockSpec((BLOCK, ATTN_WIDTH),
                             lambda i: (jnp.maximum(i * (ROW_TILE // BLOCK) - 1, 0), 0))
    mem_spec = pl.BlockSpec((None,) + mk.shape[1:], lambda i: (i // steps_per_batch, 0, 0))
    cls_spec = _class_tile_spec(steps_per_batch, ATTN_WIDTH)
    return pl.pallas_call(
        functools.partial(_mix_kernel, steps_per_batch=steps_per_batch),
        grid=(rows // ROW_TILE,),
        in_specs=[row_spec(d_model), row_spec(GMLP_WIDTH), row_spec(GMLP_WIDTH),
                  _const_spec(w_s.shape), _const_spec(b_full.shape),
                  row_spec(ATTN_WIDTH), prev_spec, row_spec(ATTN_WIDTH), prev_spec, row_spec(ATTN_WIDTH),
                  cls_spec, cls_spec,
                  row_spec(ATTN_WIDTH), row_spec(MEM_WIDTH), mem_spec, mem_spec,
                  row_spec(MEM_WIDTH),
                  pl.BlockSpec(w_out.shape, lambda i: (0, 0), pipeline_mode=pl.Buffered(1))],
        out_specs=row_spec(d_model),
        out_shape=jax.ShapeDtypeStruct((rows, d_model), F32),
        scratch_shapes=[_regroup_scratch()] * 2 + [pltpu.VMEM((2, 2 * BLOCK, 2 * BLOCK), F32),
                                                   pltpu.VMEM(w_out.shape, BF16)],
        compiler_params=_cparams(("arbitrary",)),
    )(x2d, ug, gvn, w_s, b_full, q, k, k, v, v, oc, lc, sag, mq, mk, mv, smg, w_out)


def kernel(x, mem, norm_gain, w_in, gmlp_v_gain, gmlp_w_s, gmlp_b, attn_q_gain, attn_k_gain,
           mem_norm_gain, w_mem_kv, mem_q_gain, mem_k_gain, w_out):
    batch, seq, d_model = x.shape
    depth = w_in.shape[0]
    assert DILATED_CONFIGS == ((BLOCK, 1), (BLOCK * MID, MID), (BLOCK * CLASSES, CLASSES))
    assert seq % ROW_TILE == 0 and seq % (BLOCK * CLASSES) == 0
    bd = _block_diag_ones(MXU_TILE)
    x2d = x.reshape(batch * seq, d_model)
    for l in range(depth):
        pack = lambda *parts: jnp.concatenate([p.reshape(-1).astype(F32) for p in parts]).reshape(1, -1)
        proj_gains = pack(norm_gain[l], gmlp_v_gain[l], jnp.tile(attn_q_gain[l], ATTN_HEADS),
                          jnp.tile(attn_k_gain[l], ATTN_HEADS), jnp.tile(mem_q_gain[l], MEM_HEADS))
        ug, gvn, q, k, v, sag, mq, smg, qc, kc, vc = _project(x2d, batch, seq, proj_gains, w_in[l], bd)
        mk, mv = _memory_kv(mem, pack(mem_norm_gain[l], jnp.tile(mem_k_gain[l], MEM_HEADS)),
                            w_mem_kv[l], bd)
        oc, lc = _strided_attention(qc, kc, vc)
        b_full = jnp.repeat(gmlp_b[l].T, HEAD_DIM, axis=1)
        x2d = _mix(x2d, ug, gvn, gmlp_w_s[l], b_full, q, k, v, oc, lc, sag, mq, mk, mv, smg,
                   w_out[l], seq)
    return x2d.reshape(batch, seq, d_model)
```

```python
import functools
import math

import jax
import jax.numpy as jnp
import numpy as np
from jax import lax
from jax.experimental import pallas as pl
from jax.experimental.pallas import tpu as pltpu

HEAD_DIM = 64
GMLP_HEADS = 4
ATTN_HEADS = 8
MEM_HEADS = 4
GMLP_WIDTH = GMLP_HEADS * HEAD_DIM
ATTN_WIDTH = ATTN_HEADS * HEAD_DIM
MEM_WIDTH = MEM_HEADS * HEAD_DIM
CHUNK = 128
BLOCK = 128
DILATED_CONFIGS = ((128, 1), (512, 4), (2048, 16))
EPS = 1e-6
MASKED = -1e30
LOG2E = math.log2(math.e)
LOGIT_SCALE = LOG2E / math.sqrt(HEAD_DIM)

IN_SEGMENTS = (("g_u", GMLP_WIDTH), ("g_v", GMLP_WIDTH), ("g_gate", GMLP_WIDTH),
               ("a_q", ATTN_WIDTH), ("a_k", ATTN_WIDTH), ("a_v", ATTN_WIDTH), ("a_gate", ATTN_WIDTH),
               ("m_q", MEM_WIDTH), ("m_gate", MEM_WIDTH))
MXU_TILE = 256

CLASSES = 16
MID = 4
SLABS_PER_MID = CLASSES // MID
LANES = 128
PAIRS = ATTN_WIDTH // LANES
ROW_TILE = 512
LOOP_UNROLL = 8
VMEM_LIMIT = 56 * 1024 * 1024

F32 = jnp.float32
BF16 = jnp.bfloat16


def _squares(acc):
    return (acc * acc).astype(BF16)


def _head_sumsq(sq, ones_bd):
    tile = ones_bd.shape[0]
    parts = [jnp.dot(sq[:, c:c + tile], ones_bd, preferred_element_type=F32)
             for c in range(0, sq.shape[1], tile)]
    return parts[0] if len(parts) == 1 else jnp.concatenate(parts, axis=1)


def _rms_scale(acc, sumsq, gain):
    return acc * lax.rsqrt(sumsq * (1.0 / HEAD_DIM) + EPS) * gain


def _head_rms(acc, ones_bd, gain):
    return _rms_scale(acc, _head_sumsq(_squares(acc), ones_bd), gain)


def _silu(x):
    return x * jax.nn.sigmoid(x)


def _store_both_layouts(val, seq_ref, cls_ref, slab_ref, mid_ref):
    seq_ref[...] = val.astype(BF16)
    rows = val.shape[0]
    per_mid, per_class = rows // MID, rows // CLASSES
    for s in range(val.shape[1] // LANES):
        lanes = slice(s * LANES, (s + 1) * LANES)
        slab_ref[s] = val[:, lanes]
        for r4 in range(MID):
            mid_ref[s, r4 * per_mid:(r4 + 1) * per_mid] = slab_ref[s, pl.ds(r4, per_mid, stride=MID), :]
        for r4 in range(MID):
            for g in range(SLABS_PER_MID):
                cls_ref[r4 + MID * g, :, lanes] = (
                    mid_ref[s, pl.ds(r4 * per_mid + g, per_class, stride=SLABS_PER_MID), :].astype(BF16))


def _proj_kernel(x_ref, gains_ref, w32_ref, bd_ref,
                 ug_ref, gvn_ref, q_ref, k_ref, v_ref, sag_ref, mq_ref, smg_ref,
                 qc_ref, kc_ref, vc_ref, w_ref, h_ref, qslab_ref, kslab_ref, vslab_ref):
    d_model = x_ref.shape[1]

    @pl.when(pl.program_id(0) == 0)
    def _():
        w_ref[...] = w32_ref[...].astype(BF16)

    cols, o = {}, 0
    for name, width in IN_SEGMENTS:
        cols[name] = slice(o, o + width)
        o += width
    gains, o = {}, 0
    for name, width in (("x", d_model), ("g_v", GMLP_WIDTH), ("a_q", ATTN_WIDTH), ("a_k", ATTN_WIDTH),
                        ("m_q", MEM_WIDTH)):
        gains[name] = gains_ref[:, o:o + width]
        o += width

    x = x_ref[...]
    ms = jnp.mean(x * x, axis=-1, keepdims=True)
    h_ref[...] = (x * lax.rsqrt(ms + EPS) * gains["x"]).astype(BF16)

    def project(name):
        return jnp.dot(h_ref[...], w_ref[:, cols[name]], preferred_element_type=F32)

    bd = bd_ref[...]
    g_u = project("g_u")
    g_v = project("g_v")
    sq_gv = _squares(g_v)
    g_gate = project("g_gate")
    ug_ref[...] = (g_u * _silu(g_gate)).astype(BF16)
    a_q = project("a_q")
    sq_q = _squares(a_q)
    gvn_ref[...] = _rms_scale(g_v, _head_sumsq(sq_gv, bd), gains["g_v"]).astype(BF16)
    a_k = project("a_k")
    sq_k = _squares(a_k)
    _store_both_layouts(_rms_scale(a_q, _head_sumsq(sq_q, bd), gains["a_q"]) * LOGIT_SCALE,
                        q_ref, qc_ref, qslab_ref.at[0], qslab_ref.at[1])
    a_v = project("a_v")
    _store_both_layouts(_rms_scale(a_k, _head_sumsq(sq_k, bd), gains["a_k"]),
                        k_ref, kc_ref, kslab_ref.at[0], kslab_ref.at[1])
    a_gate = project("a_gate")
    _store_both_layouts(a_v, v_ref, vc_ref, vslab_ref.at[0], vslab_ref.at[1])
    m_q = project("m_q")
    sq_mq = _squares(m_q)
    sag_ref[...] = _silu(a_gate).astype(BF16)
    m_gate = project("m_gate")
    mq_ref[...] = (_rms_scale(m_q, _head_sumsq(sq_mq, bd), gains["m_q"]) * LOGIT_SCALE).astype(BF16)
    smg_ref[...] = _silu(m_gate).astype(BF16)


def _pair_attention(q2, k2, v2, bias):
    rows = q2.shape[0]
    left = lax.broadcasted_iota(jnp.int32, (rows, LANES), 1) < HEAD_DIM
    zeros = jnp.zeros_like(q2)
    q_stack = jnp.concatenate([jnp.where(left, q2, zeros), jnp.where(left, zeros, q2)], axis=0)
    v_aug = jnp.concatenate([v2, jnp.ones_like(v2)], axis=1)
    s = lax.dot_general(q_stack, k2, (((1,), (1,)), ((), ())), preferred_element_type=F32)
    if bias is not None:
        s = s + bias
    m = jnp.max(s, axis=1, keepdims=True)
    p = jnp.exp2(s - m).astype(BF16)
    pv = jnp.dot(p, v_aug, preferred_element_type=F32)
    pick = lambda t: jnp.where(left, t[:rows], t[rows:])
    return pick(pv[:, :LANES]), pick(pv[:, LANES:]), pick(jnp.broadcast_to(m, (2 * rows, LANES)))


def _merge_states(a, b):
    top = jnp.maximum(a[2], b[2])
    w_a, w_b = jnp.exp2(a[2] - top), jnp.exp2(b[2] - top)
    return w_a * a[0] + w_b * b[0], w_a * a[1] + w_b * b[1], top


def _normalised(state):
    acc, den, top = state
    return acc / den, top + jnp.log(den) * LOG2E


def _band_bias(rel):
    return jnp.where((rel >= 0) & (rel <= BLOCK), 0.0, MASKED).astype(F32)


def _fill_bias_tables(bias_ref, slabs):
    row = lax.broadcasted_iota(jnp.int32, (2 * BLOCK, 2 * BLOCK), 0) % BLOCK
    col = lax.broadcasted_iota(jnp.int32, (2 * BLOCK, 2 * BLOCK), 1)
    bias_ref[0] = _band_bias(row - col)
    bias_ref[1] = _band_bias(row - col + BLOCK)
    if bias_ref.shape[0] > 2:
        q_rows, k_rows = BLOCK // slabs, 2 * BLOCK // slabs
        rel = (slabs * (row % q_rows) + row // q_rows) - (slabs * (col % k_rows) + col // k_rows)
        bias_ref[2] = _band_bias(rel)
        bias_ref[3] = _band_bias(rel + BLOCK)


def _first_grid_step():
    return (pl.program_id(0) == 0) & (pl.program_id(1) == 0)


def _strided_kernel(q_ref, k_ref, v_ref, o_ref, lse_ref, state16_ref, bias_ref):
    slab_len = q_ref.shape[1]

    @pl.when(_first_grid_step())
    def _():
        _fill_bias_tables(bias_ref, SLABS_PER_MID)

    def coarse(i, carry):
        g = i // (slab_len // BLOCK)
        blk = i % (slab_len // BLOCK)
        kstart = pl.multiple_of(jnp.maximum(blk - 1, 0) * BLOCK, BLOCK)
        generic = jnp.minimum(blk, 1)
        qstart = pl.multiple_of(blk * BLOCK, BLOCK)
        for pair in range(PAIRS):
            lanes = slice(pair * LANES, (pair + 1) * LANES)
            state = _pair_attention(q_ref[g, pl.ds(qstart, BLOCK), lanes],
                                    k_ref[g, pl.ds(kstart, 2 * BLOCK), lanes],
                                    v_ref[g, pl.ds(kstart, 2 * BLOCK), lanes], bias_ref[generic])
            for part in range(3):
                state16_ref[part, g, pl.ds(qstart, BLOCK), lanes] = state[part]
        return carry

    lax.fori_loop(0, SLABS_PER_MID * (slab_len // BLOCK), coarse, 0, unroll=LOOP_UNROLL)

    q_rows = BLOCK // SLABS_PER_MID
    k_rows = 2 * q_rows

    def gather(ref, start, rows, lanes):
        return jnp.concatenate([ref[g, pl.ds(start, rows), lanes] for g in range(SLABS_PER_MID)], axis=0)

    def gather_state16(start, rows, lanes):
        return tuple(gather(state16_ref.at[part], start, rows, lanes) for part in range(3))

    def mid(i, carry):
        a0 = pl.multiple_of(i * q_rows, q_rows)
        kstart = pl.multiple_of(jnp.maximum(i - 1, 0) * q_rows, q_rows)
        generic = jnp.minimum(i, 1)
        for pair in range(PAIRS):
            lanes = slice(pair * LANES, (pair + 1) * LANES)
            state4 = _pair_attention(gather(q_ref, a0, q_rows, lanes),
                                     gather(k_ref, kstart, k_rows, lanes),
                                     gather(v_ref, kstart, k_rows, lanes), bias_ref[2 + generic])
            o, lse = _normalised(_merge_states(state4, gather_state16(a0, q_rows, lanes)))
            for g in range(SLABS_PER_MID):
                o_ref[g, pl.ds(a0, q_rows), lanes] = o[g * q_rows:(g + 1) * q_rows].astype(BF16)
                lse_ref[g, pl.ds(a0, q_rows), lanes] = lse[g * q_rows:(g + 1) * q_rows]
        return carry

    lax.fori_loop(0, slab_len // q_rows, mid, 0, unroll=LOOP_UNROLL)


def _to_sequence_order(cls_ref, slab_ref, mid_ref):
    per_class = cls_ref.shape[1]
    per_mid = per_class * SLABS_PER_MID
    n_slabs = cls_ref.shape[2] // LANES
    for s in range(n_slabs):
        lanes = slice(s * LANES, (s + 1) * LANES)
        for r4 in range(MID):
            for g in range(SLABS_PER_MID):
                mid_ref[s, pl.ds(r4 * per_mid + g, per_class, stride=SLABS_PER_MID), :] = (
                    cls_ref[r4 + MID * g, :, lanes].astype(F32))
        for r4 in range(MID):
            slab_ref[s, pl.ds(r4, per_mid, stride=MID), :] = mid_ref[s, r4 * per_mid:(r4 + 1) * per_mid]


def _mix_kernel(x_ref, ug_ref, gvn_ref, ws_ref, bs_ref,
                q_ref, kprev_ref, k_ref, vprev_ref, v_ref, oc_ref, lc_ref, sag_ref,
                mq_ref, mem_ref, mgains_ref, wkv32_ref, bd_ref, smg_ref, wout32_ref, out_ref,
                oslab_ref, lslab_ref, bias_ref, wout_ref, wkv_ref, mk_ref, mv_ref, *, steps_per_batch):
    rows, d_model = x_ref.shape
    step = pl.program_id(0)

    @pl.when(step % steps_per_batch == 0)
    def _():
        @pl.when(step == 0)
        def _():
            wkv_ref[...] = wkv32_ref[...].astype(BF16)
        mem = mem_ref[...]
        ms = jnp.mean(mem * mem, axis=-1, keepdims=True)
        hm = (mem * lax.rsqrt(ms + EPS) * mgains_ref[:, :d_model]).astype(BF16)
        mk = jnp.dot(hm, wkv_ref[:, :MEM_WIDTH], preferred_element_type=F32)
        mk_ref[...] = _head_rms(mk, bd_ref[...], mgains_ref[:, d_model:]).astype(BF16)
        mv_ref[...] = jnp.dot(hm, wkv_ref[:, MEM_WIDTH:], preferred_element_type=F32).astype(BF16)

    @pl.when(step == 0)
    def _():
        wout_ref[...] = wout32_ref[...].astype(BF16)
        row = lax.broadcasted_iota(jnp.int32, (2 * BLOCK, 2 * BLOCK), 0) % BLOCK
        col = lax.broadcasted_iota(jnp.int32, (2 * BLOCK, 2 * BLOCK), 1)
        band = _band_bias(row - col + BLOCK)
        bias_ref[1] = band
        bias_ref[0] = jnp.where(col >= BLOCK, band, MASKED)
    tri = (lax.broadcasted_iota(jnp.int32, (CHUNK, CHUNK), 0)
           >= lax.broadcasted_iota(jnp.int32, (CHUNK, CHUNK), 1))
    left = lax.broadcasted_iota(jnp.int32, (CHUNK, LANES), 1) < HEAD_DIM
    w_tri = [jnp.where(tri, ws_ref[h], 0.0).astype(BF16) for h in range(GMLP_HEADS)]
    sp_chunks = []
    for c in range(rows // CHUNK):
        pairs = []
        for pair in range(GMLP_WIDTH // LANES):
            vn2 = gvn_ref[c * CHUNK:(c + 1) * CHUNK, pair * LANES:(pair + 1) * LANES]
            sp_a = jnp.dot(w_tri[2 * pair], vn2, preferred_element_type=F32)
            sp_b = jnp.dot(w_tri[2 * pair + 1], vn2, preferred_element_type=F32)
            pairs.append(jnp.where(left, sp_a, sp_b))
        sp_chunks.append(jnp.concatenate(pairs, axis=1) + bs_ref[...])
    y_g = ug_ref[...].astype(F32) * jnp.concatenate(sp_chunks, axis=0)

    mem_pairs = []
    for pair in range(MEM_WIDTH // LANES):
        lanes = slice(pair * LANES, (pair + 1) * LANES)
        acc, den, _ = _pair_attention(mq_ref[:, lanes], mk_ref[:, lanes], mv_ref[:, lanes], None)
        mem_pairs.append(acc / den)
    y_m = jnp.concatenate(mem_pairs, axis=1) * smg_ref[...].astype(F32)

    _to_sequence_order(oc_ref, oslab_ref.at[0], oslab_ref.at[1])
    _to_sequence_order(lc_ref, lslab_ref.at[0], lslab_ref.at[1])
    first_bias = jnp.where(step % steps_per_batch == 0, 0, 1)
    block_outs = []
    n_blocks = rows // BLOCK
    for blk in range(n_blocks):
        if blk == n_blocks // 2:
            partial = (x_ref[...]
                       + jnp.dot(y_g.astype(BF16), wout_ref[:GMLP_WIDTH, :], preferred_element_type=F32)
                       + jnp.dot(y_m.astype(BF16), wout_ref[GMLP_WIDTH + ATTN_WIDTH:, :],
                                 preferred_element_type=F32))
        own = slice(blk * BLOCK, (blk + 1) * BLOCK)
        pair_outs = []
        for pair in range(PAIRS):
            lanes = slice(pair * LANES, (pair + 1) * LANES)
            if blk == 0:
                k2 = jnp.concatenate([kprev_ref[:, lanes], k_ref[own, lanes]], axis=0)
                v2 = jnp.concatenate([vprev_ref[:, lanes], v_ref[own, lanes]], axis=0)
                bias = bias_ref[first_bias]
            else:
                window = slice((blk - 1) * BLOCK, (blk + 1) * BLOCK)
                k2, v2, bias = k_ref[window, lanes], v_ref[window, lanes], bias_ref[1]
            acc1, den1, top1 = _pair_attention(q_ref[own, lanes], k2, v2, bias)
            o_c, top_c = oslab_ref[0, pair, own, :], lslab_ref[0, pair, own, :]
            top = jnp.maximum(top1, top_c)
            w_1, w_c = jnp.exp2(top1 - top), jnp.exp2(top_c - top)
            pair_outs.append((w_1 * acc1 + w_c * o_c) / (w_1 * den1 + w_c))
        block_outs.append(jnp.concatenate(pair_outs, axis=1))
    y_a = jnp.concatenate(block_outs, axis=0) * sag_ref[...].astype(F32)
    out_ref[...] = partial + jnp.dot(y_a.astype(BF16), wout_ref[GMLP_WIDTH:GMLP_WIDTH + ATTN_WIDTH, :],
                                     preferred_element_type=F32)


def _block_diag_ones(width):
    head = np.arange(width) // HEAD_DIM
    return jnp.asarray(head[:, None] == head[None, :], dtype=BF16)


def _cparams(sem):
    return pltpu.CompilerParams(dimension_semantics=sem, vmem_limit_bytes=VMEM_LIMIT)


def _const_spec(shape):
    return pl.BlockSpec(shape, lambda *idx: (0,) * len(shape))


def _regroup_scratch():
    return pltpu.VMEM((2, ATTN_WIDTH // LANES, ROW_TILE, LANES), F32)


def _class_tile_spec(steps_per_batch, width):
    return pl.BlockSpec((None, CLASSES, ROW_TILE // CLASSES, width),
                        lambda i: (i // steps_per_batch, 0, i % steps_per_batch, 0))


def _project(x2d, batch, seq, gains, w_in, bd):
    rows, d_model = x2d.shape
    widths = (GMLP_WIDTH, GMLP_WIDTH, ATTN_WIDTH, ATTN_WIDTH, ATTN_WIDTH, ATTN_WIDTH, MEM_WIDTH, MEM_WIDTH)
    cls_shape = jax.ShapeDtypeStruct((batch, CLASSES, seq // CLASSES, ATTN_WIDTH), BF16)
    cls_spec = _class_tile_spec(seq // ROW_TILE, ATTN_WIDTH)
    slab = _regroup_scratch()
    return pl.pallas_call(
        _proj_kernel,
        grid=(rows // ROW_TILE,),
        in_specs=[pl.BlockSpec((ROW_TILE, d_model), lambda i: (i, 0)),
                  _const_spec(gains.shape),
                  pl.BlockSpec(w_in.shape, lambda i: (0, 0), pipeline_mode=pl.Buffered(1)),
                  _const_spec(bd.shape)],
        out_specs=[pl.BlockSpec((ROW_TILE, w), lambda i: (i, 0)) for w in widths] + [cls_spec] * 3,
        out_shape=[jax.ShapeDtypeStruct((rows, w), BF16) for w in widths] + [cls_shape] * 3,
        scratch_shapes=[pltpu.VMEM(w_in.shape, BF16), pltpu.VMEM((ROW_TILE, d_model), BF16),
                        slab, slab, slab],
        compiler_params=_cparams(("arbitrary",)),
    )(x2d, gains, w_in, bd)


def _strided_attention(qc, kc, vc):
    batch, _, slab_len, width = qc.shape
    view = lambda t: t.reshape(batch, SLABS_PER_MID, MID, slab_len, width)
    spec = pl.BlockSpec((None, SLABS_PER_MID, None, slab_len, width), lambda b, r4: (b, 0, r4, 0, 0))
    o, lse = pl.pallas_call(
        _strided_kernel,
        grid=(batch, MID),
        in_specs=[spec] * 3,
        out_specs=[spec] * 2,
        out_shape=[jax.ShapeDtypeStruct((batch, SLABS_PER_MID, MID, slab_len, width), BF16),
                   jax.ShapeDtypeStruct((batch, SLABS_PER_MID, MID, slab_len, width), F32)],
        scratch_shapes=[pltpu.VMEM((3, SLABS_PER_MID, slab_len, width), F32),
                        pltpu.VMEM((4, 2 * BLOCK, 2 * BLOCK), F32)],
        compiler_params=_cparams(("arbitrary", "arbitrary")),
    )(view(qc), view(kc), view(vc))
    return o.reshape(qc.shape), lse.reshape(qc.shape)


def _mix(x2d, ug, gvn, w_s, b_full, q, k, v, oc, lc, sag, mq, mem, mem_gains, w_kv, bd, smg, w_out, seq):
    rows, d_model = x2d.shape
    mem_len = mem.shape[1]
    steps_per_batch = seq // ROW_TILE
    row_spec = lambda w: pl.BlockSpec((ROW_TILE, w), lambda i: (i, 0))
    prev_spec = pl.BlockSpec((BLOCK, ATTN_WIDTH),
                             lambda i: (jnp.maximum(i * (ROW_TILE // BLOCK) - 1, 0), 0))
    mem_spec = pl.BlockSpec((None,) + mem.shape[1:], lambda i: (i // steps_per_batch, 0, 0))
    weight_spec = lambda w: pl.BlockSpec(w.shape, lambda i: (0, 0), pipeline_mode=pl.Buffered(1))
    cls_spec = _class_tile_spec(steps_per_batch, ATTN_WIDTH)
    return pl.pallas_call(
        functools.partial(_mix_kernel, steps_per_batch=steps_per_batch),
        grid=(rows // ROW_TILE,),
        in_specs=[row_spec(d_model), row_spec(GMLP_WIDTH), row_spec(GMLP_WIDTH),
                  _const_spec(w_s.shape), _const_spec(b_full.shape),
                  row_spec(ATTN_WIDTH), prev_spec, row_spec(ATTN_WIDTH), prev_spec, row_spec(ATTN_WIDTH),
                  cls_spec, cls_spec,
                  row_spec(ATTN_WIDTH), row_spec(MEM_WIDTH), mem_spec, _const_spec(mem_gains.shape),
                  weight_spec(w_kv), _const_spec(bd.shape), row_spec(MEM_WIDTH), weight_spec(w_out)],
        out_specs=row_spec(d_model),
        out_shape=jax.ShapeDtypeStruct((rows, d_model), F32),
        scratch_shapes=[_regroup_scratch()] * 2
                       + [pltpu.VMEM((2, 2 * BLOCK, 2 * BLOCK), F32), pltpu.VMEM(w_out.shape, BF16),
                          pltpu.VMEM(w_kv.shape, BF16)] + [pltpu.VMEM((mem_len, MEM_WIDTH), BF16)] * 2,
        compiler_params=_cparams(("arbitrary",)),
    )(x2d, ug, gvn, w_s, b_full, q, k, k, v, v, oc, lc, sag, mq, mem, mem_gains, w_kv, bd, smg, w_out)


def kernel(x, mem, norm_gain, w_in, gmlp_v_gain, gmlp_w_s, gmlp_b, attn_q_gain, attn_k_gain,
           mem_norm_gain, w_mem_kv, mem_q_gain, mem_k_gain, w_out):
    batch, seq, d_model = x.shape
    depth = w_in.shape[0]
    assert DILATED_CONFIGS == ((BLOCK, 1), (BLOCK * MID, MID), (BLOCK * CLASSES, CLASSES))
    assert seq % ROW_TILE == 0 and seq % (BLOCK * CLASSES) == 0
    bd = _block_diag_ones(MXU_TILE)
    x2d = x.reshape(batch * seq, d_model)
    for l in range(depth):
        pack = lambda *parts: jnp.concatenate([p.reshape(-1).astype(F32) for p in parts]).reshape(1, -1)
        proj_gains = pack(norm_gain[l], gmlp_v_gain[l], jnp.tile(attn_q_gain[l], ATTN_HEADS),
                          jnp.tile(attn_k_gain[l], ATTN_HEADS), jnp.tile(mem_q_gain[l], MEM_HEADS))
        ug, gvn, q, k, v, sag, mq, smg, qc, kc, vc = _project(x2d, batch, seq, proj_gains, w_in[l], bd)
        oc, lc = _strided_attention(qc, kc, vc)
        b_full = jnp.repeat(gmlp_b[l].T, HEAD_DIM, axis=1)
        x2d = _mix(x2d, ug, gvn, gmlp_w_s[l], b_full, q, k, v, oc, lc, sag, mq, mem,
                   pack(mem_norm_gain[l], jnp.tile(mem_k_gain[l], MEM_HEADS)), w_mem_kv[l], bd, smg,
                   w_out[l], seq)
    return x2d.reshape(batch, seq, d_model)
```

```python
import functools
import math

import jax
import jax.numpy as jnp
import numpy as np
from jax import lax
from jax.experimental import pallas as pl
from jax.experimental.pallas import tpu as pltpu

HEAD_DIM = 64
GMLP_HEADS = 4
ATTN_HEADS = 8
MEM_HEADS = 4
GMLP_WIDTH = GMLP_HEADS * HEAD_DIM
ATTN_WIDTH = ATTN_HEADS * HEAD_DIM
MEM_WIDTH = MEM_HEADS * HEAD_DIM
CHUNK = 128
BLOCK = 128
DILATED_CONFIGS = ((128, 1), (512, 4), (2048, 16))
EPS = 1e-6
MASKED = -1e30
LOG2E = math.log2(math.e)
LOGIT_SCALE = LOG2E / math.sqrt(HEAD_DIM)

IN_SEGMENTS = (("g_u", GMLP_WIDTH), ("g_v", GMLP_WIDTH), ("g_gate", GMLP_WIDTH),
               ("a_q", ATTN_WIDTH), ("a_k", ATTN_WIDTH), ("a_v", ATTN_WIDTH), ("a_gate", ATTN_WIDTH),
               ("m_q", MEM_WIDTH), ("m_gate", MEM_WIDTH))
MXU_TILE = 256

CLASSES = 16
MID = 4
SLABS_PER_MID = CLASSES // MID
LANES = 128
PAIRS = ATTN_WIDTH // LANES
ROW_TILE = 512
MIX_TILE = 512
LOOP_UNROLL = 8
VMEM_LIMIT = 56 * 1024 * 1024

F32 = jnp.float32
BF16 = jnp.bfloat16


def _squares(acc):
    return (acc * acc).astype(BF16)


def _head_sumsq(sq, ones_bd):
    tile = ones_bd.shape[0]
    parts = [jnp.dot(sq[:, c:c + tile], ones_bd, preferred_element_type=F32)
             for c in range(0, sq.shape[1], tile)]
    return parts[0] if len(parts) == 1 else jnp.concatenate(parts, axis=1)


def _rms_scale(acc, sumsq, gain):
    return acc * lax.rsqrt(sumsq * (1.0 / HEAD_DIM) + EPS) * gain


def _head_rms(acc, ones_bd, gain):
    return _rms_scale(acc, _head_sumsq(_squares(acc), ones_bd), gain)


def _head_rms_lanes(acc, gain):
    left = lax.broadcasted_iota(jnp.int32, (acc.shape[0], LANES), 1) < HEAD_DIM
    groups = []
    for j in range(acc.shape[1] // LANES):
        a = acc[:, j * LANES:(j + 1) * LANES]
        sq = a * a
        ss_a = jnp.sum(jnp.where(left, sq, 0.0), axis=1, keepdims=True)
        ss_b = jnp.sum(jnp.where(left, 0.0, sq), axis=1, keepdims=True)
        inv = jnp.where(left, lax.rsqrt(ss_a * (1.0 / HEAD_DIM) + EPS),
                        lax.rsqrt(ss_b * (1.0 / HEAD_DIM) + EPS))
        groups.append(a * inv)
    return jnp.concatenate(groups, axis=1) * gain


def _silu(x):
    return x * jax.nn.sigmoid(x)


def _store_both_layouts(val, seq_ref, cls_ref, slab_ref, mid_ref):
    seq_ref[...] = val.astype(BF16)
    rows = val.shape[0]
    per_mid, per_class = rows // MID, rows // CLASSES
    for s in range(val.shape[1] // LANES):
        lanes = slice(s * LANES, (s + 1) * LANES)
        slab_ref[s] = val[:, lanes]
        for r4 in range(MID):
            mid_ref[s, r4 * per_mid:(r4 + 1) * per_mid] = slab_ref[s, pl.ds(r4, per_mid, stride=MID), :]
        for r4 in range(MID):
            for g in range(SLABS_PER_MID):
                cls_ref[r4 + MID * g, :, lanes] = (
                    mid_ref[s, pl.ds(r4 * per_mid + g, per_class, stride=SLABS_PER_MID), :].astype(BF16))


def _proj_kernel(x_ref, gains_ref, w32_ref, bd_ref,
                 ug_ref, gvn_ref, q_ref, k_ref, v_ref, sag_ref, mq_ref, smg_ref,
                 qc_ref, kc_ref, vc_ref, w_ref, h_ref, qslab_ref, kslab_ref, vslab_ref):
    d_model = x_ref.shape[1]

    @pl.when(pl.program_id(0) == 0)
    def _():
        w_ref[...] = w32_ref[...].astype(BF16)

    cols, o = {}, 0
    for name, width in IN_SEGMENTS:
        cols[name] = slice(o, o + width)
        o += width
    gains, o = {}, 0
    for name, width in (("x", d_model), ("g_v", GMLP_WIDTH), ("a_q", ATTN_WIDTH), ("a_k", ATTN_WIDTH),
                        ("m_q", MEM_WIDTH)):
        gains[name] = gains_ref[:, o:o + width]
        o += width

    x = x_ref[...]
    ms = jnp.mean(x * x, axis=-1, keepdims=True)
    h_ref[...] = (x * lax.rsqrt(ms + EPS) * gains["x"]).astype(BF16)

    def project(name):
        return jnp.dot(h_ref[...], w_ref[:, cols[name]], preferred_element_type=F32)

    g_u = project("g_u")
    g_v = project("g_v")
    g_gate = project("g_gate")
    ug_ref[...] = (g_u * _silu(g_gate)).astype(BF16)
    a_q = project("a_q")
    gvn_ref[...] = _head_rms_lanes(g_v, gains["g_v"]).astype(BF16)
    a_k = project("a_k")
    _store_both_layouts(_head_rms_lanes(a_q, gains["a_q"]) * LOGIT_SCALE,
                        q_ref, qc_ref, qslab_ref.at[0], qslab_ref.at[1])
    a_v = project("a_v")
    _store_both_layouts(_head_rms_lanes(a_k, gains["a_k"]),
                        k_ref, kc_ref, kslab_ref.at[0], kslab_ref.at[1])
    m_q = project("m_q")
    _store_both_layouts(a_v, v_ref, vc_ref, vslab_ref.at[0], vslab_ref.at[1])
    a_gate = project("a_gate")
    mq_ref[...] = (_head_rms_lanes(m_q, gains["m_q"]) * LOGIT_SCALE).astype(BF16)
    m_gate = project("m_gate")
    sag_ref[...] = _silu(a_gate).astype(BF16)
    smg_ref[...] = _silu(m_gate).astype(BF16)


def _pair_attention(q2, k2, v2, bias):
    rows = q2.shape[0]
    left = lax.broadcasted_iota(jnp.int32, (rows, LANES), 1) < HEAD_DIM
    zeros = jnp.zeros_like(q2)
    q_stack = jnp.concatenate([jnp.where(left, q2, zeros), jnp.where(left, zeros, q2)], axis=0)
    v_aug = jnp.concatenate([v2, jnp.ones_like(v2)], axis=1)
    s = lax.dot_general(q_stack, k2, (((1,), (1,)), ((), ())), preferred_element_type=F32)
    if bias is not None:
        s = s + bias
    m = jnp.max(s, axis=1, keepdims=True)
    p = jnp.exp2(s - m).astype(BF16)
    pv = jnp.dot(p, v_aug, preferred_element_type=F32)
    pick = lambda t: jnp.where(left, t[:rows], t[rows:])
    return pick(pv[:, :LANES]), pick(pv[:, LANES:]), pick(jnp.broadcast_to(m, (2 * rows, LANES)))


def _merge_states(a, b):
    top = jnp.maximum(a[2], b[2])
    w_a, w_b = jnp.exp2(a[2] - top), jnp.exp2(b[2] - top)
    return w_a * a[0] + w_b * b[0], w_a * a[1] + w_b * b[1], top


def _normalised(state):
    acc, den, top = state
    return acc / den, top + jnp.log(den) * LOG2E


def _band_bias(rel):
    return jnp.where((rel >= 0) & (rel <= BLOCK), 0.0, MASKED).astype(F32)


def _fill_bias_tables(bias_ref, slabs):
    row = lax.broadcasted_iota(jnp.int32, (2 * BLOCK, 2 * BLOCK), 0) % BLOCK
    col = lax.broadcasted_iota(jnp.int32, (2 * BLOCK, 2 * BLOCK), 1)
    bias_ref[0] = _band_bias(row - col)
    bias_ref[1] = _band_bias(row - col + BLOCK)
    if bias_ref.shape[0] > 2:
        q_rows, k_rows = BLOCK // slabs, 2 * BLOCK // slabs
        rel = (slabs * (row % q_rows) + row // q_rows) - (slabs * (col % k_rows) + col // k_rows)
        bias_ref[2] = _band_bias(rel)
        bias_ref[3] = _band_bias(rel + BLOCK)


def _first_grid_step():
    return (pl.program_id(0) == 0) & (pl.program_id(1) == 0)


def _strided_kernel(q_ref, k_ref, v_ref, o_ref, lse_ref, state16_ref, bias_ref):
    slab_len = q_ref.shape[1]

    @pl.when(_first_grid_step())
    def _():
        _fill_bias_tables(bias_ref, SLABS_PER_MID)

    def coarse(i, carry):
        g = i // (slab_len // BLOCK)
        blk = i % (slab_len // BLOCK)
        kstart = pl.multiple_of(jnp.maximum(blk - 1, 0) * BLOCK, BLOCK)
        generic = jnp.minimum(blk, 1)
        qstart = pl.multiple_of(blk * BLOCK, BLOCK)
        for pair in range(PAIRS):
            lanes = slice(pair * LANES, (pair + 1) * LANES)
            state = _pair_attention(q_ref[g, pl.ds(qstart, BLOCK), lanes],
                                    k_ref[g, pl.ds(kstart, 2 * BLOCK), lanes],
                                    v_ref[g, pl.ds(kstart, 2 * BLOCK), lanes], bias_ref[generic])
            for part in range(3):
                state16_ref[part, g, pl.ds(qstart, BLOCK), lanes] = state[part]
        return carry

    lax.fori_loop(0, SLABS_PER_MID * (slab_len // BLOCK), coarse, 0, unroll=LOOP_UNROLL)

    q_rows = BLOCK // SLABS_PER_MID
    k_rows = 2 * q_rows

    def gather(ref, start, rows, lanes):
        return jnp.concatenate([ref[g, pl.ds(start, rows), lanes] for g in range(SLABS_PER_MID)], axis=0)

    def gather_state16(start, rows, lanes):
        return tuple(gather(state16_ref.at[part], start, rows, lanes) for part in range(3))

    def mid(i, carry):
        a0 = pl.multiple_of(i * q_rows, q_rows)
        kstart = pl.multiple_of(jnp.maximum(i - 1, 0) * q_rows, q_rows)
        generic = jnp.minimum(i, 1)
        for pair in range(PAIRS):
            lanes = slice(pair * LANES, (pair + 1) * LANES)
            state4 = _pair_attention(gather(q_ref, a0, q_rows, lanes),
                                     gather(k_ref, kstart, k_rows, lanes),
                                     gather(v_ref, kstart, k_rows, lanes), bias_ref[2 + generic])
            o, lse = _normalised(_merge_states(state4, gather_state16(a0, q_rows, lanes)))
            for g in range(SLABS_PER_MID):
                o_ref[g, pl.ds(a0, q_rows), lanes] = o[g * q_rows:(g + 1) * q_rows].astype(BF16)
                lse_ref[g, pl.ds(a0, q_rows), lanes] = lse[g * q_rows:(g + 1) * q_rows]
        return carry

    lax.fori_loop(0, slab_len // q_rows, mid, 0, unroll=LOOP_UNROLL)


def _to_sequence_order(cls_ref, slab_ref, mid_ref):
    per_class = cls_ref.shape[1]
    per_mid = per_class * SLABS_PER_MID
    n_slabs = cls_ref.shape[2] // LANES
    for s in range(n_slabs):
        lanes = slice(s * LANES, (s + 1) * LANES)
        for r4 in range(MID):
            for g in range(SLABS_PER_MID):
                mid_ref[s, pl.ds(r4 * per_mid + g, per_class, stride=SLABS_PER_MID), :] = (
                    cls_ref[r4 + MID * g, :, lanes].astype(F32))
        for r4 in range(MID):
            slab_ref[s, pl.ds(r4, per_mid, stride=MID), :] = mid_ref[s, r4 * per_mid:(r4 + 1) * per_mid]


def _mix_kernel(x_ref, ug_ref, gvn_ref, ws_ref, bs_ref,
                q_ref, kprev_ref, k_ref, vprev_ref, v_ref, oc_ref, lc_ref, sag_ref,
                mq_ref, mem_ref, mgains_ref, wkv32_ref, bd_ref, smg_ref, wout32_ref, out_ref,
                oslab_ref, lslab_ref, bias_ref, wout_ref, wkv_ref, mk_ref, mv_ref, *, steps_per_batch):
    rows, d_model = x_ref.shape
    step = pl.program_id(0)

    @pl.when(step % steps_per_batch == 0)
    def _():
        @pl.when(step == 0)
        def _():
            wkv_ref[...] = wkv32_ref[...].astype(BF16)
        mem = mem_ref[...]
        ms = jnp.mean(mem * mem, axis=-1, keepdims=True)
        hm = (mem * lax.rsqrt(ms + EPS) * mgains_ref[:, :d_model]).astype(BF16)
        mk = jnp.dot(hm, wkv_ref[:, :MEM_WIDTH], preferred_element_type=F32)
        mk_ref[...] = _head_rms(mk, bd_ref[...], mgains_ref[:, d_model:]).astype(BF16)
        mv_ref[...] = jnp.dot(hm, wkv_ref[:, MEM_WIDTH:], preferred_element_type=F32).astype(BF16)

    @pl.when(step == 0)
    def _():
        wout_ref[...] = wout32_ref[...].astype(BF16)
        row = lax.broadcasted_iota(jnp.int32, (2 * BLOCK, 2 * BLOCK), 0) % BLOCK
        col = lax.broadcasted_iota(jnp.int32, (2 * BLOCK, 2 * BLOCK), 1)
        band = _band_bias(row - col + BLOCK)
        bias_ref[1] = band
        bias_ref[0] = jnp.where(col >= BLOCK, band, MASKED)
    tri = (lax.broadcasted_iota(jnp.int32, (CHUNK, CHUNK), 0)
           >= lax.broadcasted_iota(jnp.int32, (CHUNK, CHUNK), 1))
    left = lax.broadcasted_iota(jnp.int32, (CHUNK, LANES), 1) < HEAD_DIM
    w_tri = [jnp.where(tri, ws_ref[h], 0.0).astype(BF16) for h in range(GMLP_HEADS)]
    sp_chunks = []
    for c in range(rows // CHUNK):
        pairs = []
        for pair in range(GMLP_WIDTH // LANES):
            vn2 = gvn_ref[c * CHUNK:(c + 1) * CHUNK, pair * LANES:(pair + 1) * LANES]
            sp_a = jnp.dot(w_tri[2 * pair], vn2, preferred_element_type=F32)
            sp_b = jnp.dot(w_tri[2 * pair + 1], vn2, preferred_element_type=F32)
            pairs.append(jnp.where(left, sp_a, sp_b))
        sp_chunks.append(jnp.concatenate(pairs, axis=1) + bs_ref[...])
    y_g = ug_ref[...].astype(F32) * jnp.concatenate(sp_chunks, axis=0)

    mem_pairs = []
    for pair in range(MEM_WIDTH // LANES):
        lanes = slice(pair * LANES, (pair + 1) * LANES)
        acc, den, _ = _pair_attention(mq_ref[:, lanes], mk_ref[:, lanes], mv_ref[:, lanes], None)
        mem_pairs.append(acc / den)
    y_m = jnp.concatenate(mem_pairs, axis=1) * smg_ref[...].astype(F32)

    _to_sequence_order(oc_ref, oslab_ref.at[0], oslab_ref.at[1])
    _to_sequence_order(lc_ref, lslab_ref.at[0], lslab_ref.at[1])
    first_bias = jnp.where(step % steps_per_batch == 0, 0, 1)
    block_outs = []
    n_blocks = rows // BLOCK
    for blk in range(n_blocks):
        if blk == n_blocks // 2:
            partial = (x_ref[...]
                       + jnp.dot(y_g.astype(BF16), wout_ref[:GMLP_WIDTH, :], preferred_element_type=F32)
                       + jnp.dot(y_m.astype(BF16), wout_ref[GMLP_WIDTH + ATTN_WIDTH:, :],
                                 preferred_element_type=F32))
        own = slice(blk * BLOCK, (blk + 1) * BLOCK)
        pair_outs = []
        for pair in range(PAIRS):
            lanes = slice(pair * LANES, (pair + 1) * LANES)
            if blk == 0:
                k2 = jnp.concatenate([kprev_ref[:, lanes], k_ref[own, lanes]], axis=0)
                v2 = jnp.concatenate([vprev_ref[:, lanes], v_ref[own, lanes]], axis=0)
                bias = bias_ref[first_bias]
            else:
                window = slice((blk - 1) * BLOCK, (blk + 1) * BLOCK)
                k2, v2, bias = k_ref[window, lanes], v_ref[window, lanes], bias_ref[1]
            acc1, den1, top1 = _pair_attention(q_ref[own, lanes], k2, v2, bias)
            o_c, top_c = oslab_ref[0, pair, own, :], lslab_ref[0, pair, own, :]
            top = jnp.maximum(top1, top_c)
            w_1, w_c = jnp.exp2(top1 - top), jnp.exp2(top_c - top)
            pair_outs.append((w_1 * acc1 + w_c * o_c) / (w_1 * den1 + w_c))
        block_outs.append(jnp.concatenate(pair_outs, axis=1))
    y_a = jnp.concatenate(block_outs, axis=0) * sag_ref[...].astype(F32)
    out_ref[...] = partial + jnp.dot(y_a.astype(BF16), wout_ref[GMLP_WIDTH:GMLP_WIDTH + ATTN_WIDTH, :],
                                     preferred_element_type=F32)


def _block_diag_ones(width):
    head = np.arange(width) // HEAD_DIM
    return jnp.asarray(head[:, None] == head[None, :], dtype=BF16)


def _cparams(sem):
    return pltpu.CompilerParams(dimension_semantics=sem, vmem_limit_bytes=VMEM_LIMIT)


def _const_spec(shape):
    return pl.BlockSpec(shape, lambda *idx: (0,) * len(shape))


def _regroup_scratch(tile=ROW_TILE):
    return pltpu.VMEM((2, ATTN_WIDTH // LANES, tile, LANES), F32)


def _class_tile_spec(steps_per_batch, width, tile=ROW_TILE):
    return pl.BlockSpec((None, CLASSES, tile // CLASSES, width),
                        lambda i: (i // steps_per_batch, 0, i % steps_per_batch, 0))


def _project(x2d, batch, seq, gains, w_in, bd):
    rows, d_model = x2d.shape
    widths = (GMLP_WIDTH, GMLP_WIDTH, ATTN_WIDTH, ATTN_WIDTH, ATTN_WIDTH, ATTN_WIDTH, MEM_WIDTH, MEM_WIDTH)
    cls_shape = jax.ShapeDtypeStruct((batch, CLASSES, seq // CLASSES, ATTN_WIDTH), BF16)
    cls_spec = _class_tile_spec(seq // ROW_TILE, ATTN_WIDTH)
    slab = _regroup_scratch()
    return pl.pallas_call(
        _proj_kernel,
        grid=(rows // ROW_TILE,),
        in_specs=[pl.BlockSpec((ROW_TILE, d_model), lambda i: (i, 0)),
                  _const_spec(gains.shape),
                  pl.BlockSpec(w_in.shape, lambda i: (0, 0), pipeline_mode=pl.Buffered(1)),
                  _const_spec(bd.shape)],
        out_specs=[pl.BlockSpec((ROW_TILE, w), lambda i: (i, 0)) for w in widths] + [cls_spec] * 3,
        out_shape=[jax.ShapeDtypeStruct((rows, w), BF16) for w in widths] + [cls_shape] * 3,
        scratch_shapes=[pltpu.VMEM(w_in.shape, BF16), pltpu.VMEM((ROW_TILE, d_model), BF16),
                        slab, slab, slab],
        compiler_params=_cparams(("arbitrary",)),
    )(x2d, gains, w_in, bd)


def _strided_attention(qc, kc, vc):
    batch, _, slab_len, width = qc.shape
    view = lambda t: t.reshape(batch, SLABS_PER_MID, MID, slab_len, width)
    spec = pl.BlockSpec((None, SLABS_PER_MID, None, slab_len, width), lambda b, r4: (b, 0, r4, 0, 0))
    o, lse = pl.pallas_call(
        _strided_kernel,
        grid=(batch, MID),
        in_specs=[spec] * 3,
        out_specs=[spec] * 2,
        out_shape=[jax.ShapeDtypeStruct((batch, SLABS_PER_MID, MID, slab_len, width), BF16),
                   jax.ShapeDtypeStruct((batch, SLABS_PER_MID, MID, slab_len, width), F32)],
        scratch_shapes=[pltpu.VMEM((3, SLABS_PER_MID, slab_len, width), F32),
                        pltpu.VMEM((4, 2 * BLOCK, 2 * BLOCK), F32)],
        compiler_params=_cparams(("arbitrary", "arbitrary")),
    )(view(qc), view(kc), view(vc))
    return o.reshape(qc.shape), lse.reshape(qc.shape)


def _mix(x2d, ug, gvn, w_s, b_full, q, k, v, oc, lc, sag, mq, mem, mem_gains, w_kv, bd, smg, w_out, seq):
    rows, d_model = x2d.shape
    mem_len = mem.shape[1]
    steps_per_batch = seq // MIX_TILE
    row_spec = lambda w: pl.BlockSpec((MIX_TILE, w), lambda i: (i, 0))
    prev_spec = pl.BlockSpec((BLOCK, ATTN_WIDTH),
                             lambda i: (jnp.maximum(i * (MIX_TILE // BLOCK) - 1, 0), 0))
    mem_spec = pl.BlockSpec((None,) + mem.shape[1:], lambda i: (i // steps_per_batch, 0, 0))
    weight_spec = lambda w: pl.BlockSpec(w.shape, lambda i: (0, 0), pipeline_mode=pl.Buffered(1))
    cls_spec = _class_tile_spec(steps_per_batch, ATTN_WIDTH, MIX_TILE)
    return pl.pallas_call(
        functools.partial(_mix_kernel, steps_per_batch=steps_per_batch),
        grid=(rows // MIX_TILE,),
        in_specs=[row_spec(d_model), row_spec(GMLP_WIDTH), row_spec(GMLP_WIDTH),
                  _const_spec(w_s.shape), _const_spec(b_full.shape),
                  row_spec(ATTN_WIDTH), prev_spec, row_spec(ATTN_WIDTH), prev_spec, row_spec(ATTN_WIDTH),
                  cls_spec, cls_spec,
                  row_spec(ATTN_WIDTH), row_spec(MEM_WIDTH), mem_spec, _const_spec(mem_gains.shape),
                  weight_spec(w_kv), _const_spec(bd.shape), row_spec(MEM_WIDTH), weight_spec(w_out)],
        out_specs=row_spec(d_model),
        out_shape=jax.ShapeDtypeStruct((rows, d_model), F32),
        scratch_shapes=[_regroup_scratch(MIX_TILE)] * 2
                       + [pltpu.VMEM((2, 2 * BLOCK, 2 * BLOCK), F32), pltpu.VMEM(w_out.shape, BF16),
                          pltpu.VMEM(w_kv.shape, BF16)] + [pltpu.VMEM((mem_len, MEM_WIDTH), BF16)] * 2,
        compiler_params=_cparams(("arbitrary",)),
    )(x2d, ug, gvn, w_s, b_full, q, k, k, v, v, oc, lc, sag, mq, mem, mem_gains, w_kv, bd, smg, w_out)


def kernel(x, mem, norm_gain, w_in, gmlp_v_gain, gmlp_w_s, gmlp_b, attn_q_gain, attn_k_gain,
           mem_norm_gain, w_mem_kv, mem_q_gain, mem_k_gain, w_out):
    batch, seq, d_model = x.shape
    depth = w_in.shape[0]
    assert DILATED_CONFIGS == ((BLOCK, 1), (BLOCK * MID, MID), (BLOCK * CLASSES, CLASSES))
    assert seq % ROW_TILE == 0 and seq % MIX_TILE == 0 and seq % (BLOCK * CLASSES) == 0
    bd = _block_diag_ones(MXU_TILE)
    x2d = x.reshape(batch * seq, d_model)
    for l in range(depth):
        pack = lambda *parts: jnp.concatenate([p.reshape(-1).astype(F32) for p in parts]).reshape(1, -1)
        proj_gains = pack(norm_gain[l], gmlp_v_gain[l], jnp.tile(attn_q_gain[l], ATTN_HEADS),
                          jnp.tile(attn_k_gain[l], ATTN_HEADS), jnp.tile(mem_q_gain[l], MEM_HEADS))
        ug, gvn, q, k, v, sag, mq, smg, qc, kc, vc = _project(x2d, batch, seq, proj_gains, w_in[l], bd)
        oc, lc = _strided_attention(qc, kc, vc)
        b_full = jnp.repeat(gmlp_b[l].T, HEAD_DIM, axis=1)
        x2d = _mix(x2d, ug, gvn, gmlp_w_s[l], b_full, q, k, v, oc, lc, sag, mq, mem,
                   pack(mem_norm_gain[l], jnp.tile(mem_k_gain[l], MEM_HEADS)), w_mem_kv[l], bd, smg,
                   w_out[l], seq)
    return x2d.reshape(batch, seq, d_model)
```

```python
import functools
import math

import jax
import jax.numpy as jnp
from jax import lax
from jax.experimental import pallas as pl
from jax.experimental.pallas import tpu as pltpu

HEAD_DIM = 64
GMLP_HEADS = 4
ATTN_HEADS = 8
MEM_HEADS = 4
GMLP_WIDTH = GMLP_HEADS * HEAD_DIM
ATTN_WIDTH = ATTN_HEADS * HEAD_DIM
MEM_WIDTH = MEM_HEADS * HEAD_DIM
CHUNK = 128
BLOCK = 128
DILATED_CONFIGS = ((128, 1), (512, 4), (2048, 16))
EPS = 1e-6
MASKED = -1e30
LOG2E = math.log2(math.e)
LOGIT_SCALE = LOG2E / math.sqrt(HEAD_DIM)

IN_SEGMENTS = (("g_u", GMLP_WIDTH), ("g_v", GMLP_WIDTH), ("g_gate", GMLP_WIDTH),
               ("a_q", ATTN_WIDTH), ("a_k", ATTN_WIDTH), ("a_v", ATTN_WIDTH), ("a_gate", ATTN_WIDTH),
               ("m_q", MEM_WIDTH), ("m_gate", MEM_WIDTH))

def _column_slices(segments):
    slices, start = {}, 0
    for name, width in segments:
        slices[name] = slice(start, start + width)
        start += width
    return slices, start


IN_COLS, IN_WIDTH = _column_slices(IN_SEGMENTS)
SEQ_COLS, SEQ_WIDTH = _column_slices((("k", ATTN_WIDTH), ("v", ATTN_WIDTH), ("q", ATTN_WIDTH),
                                      ("a_gate", ATTN_WIDTH), ("g_ug", GMLP_WIDTH), ("g_vn", GMLP_WIDTH),
                                      ("m_q", MEM_WIDTH), ("m_gate", MEM_WIDTH)))
CLS_COLS, CLS_WIDTH = _column_slices((("q", ATTN_WIDTH), ("k", ATTN_WIDTH), ("v", ATTN_WIDTH)))

CLASSES = 16
MID = 4
SLABS_PER_MID = CLASSES // MID
LANES = 128
PAIRS = ATTN_WIDTH // LANES
PROJ_TILE = 1024
ROW_TILE = 512
LOOP_UNROLL = 8
VMEM_LIMIT = 56 * 1024 * 1024

F32 = jnp.float32
BF16 = jnp.bfloat16


def _head_rms_lanes(acc, gain):
    left = lax.broadcasted_iota(jnp.int32, (acc.shape[0], LANES), 1) < HEAD_DIM
    groups = []
    for j in range(acc.shape[1] // LANES):
        a = acc[:, j * LANES:(j + 1) * LANES]
        sq = a * a
        ss_a = jnp.sum(jnp.where(left, sq, 0.0), axis=1, keepdims=True)
        ss_b = jnp.sum(jnp.where(left, 0.0, sq), axis=1, keepdims=True)
        inv = jnp.where(left, lax.rsqrt(ss_a * (1.0 / HEAD_DIM) + EPS),
                        lax.rsqrt(ss_b * (1.0 / HEAD_DIM) + EPS))
        groups.append(a * inv)
    return jnp.concatenate(groups, axis=1) * gain


def _silu(x):
    return x * jax.nn.sigmoid(x)


def _store_both_layouts(val, name, seq_ref, cls_ref, work_ref):
    seq_ref[:, SEQ_COLS[name]] = val.astype(BF16)
    slab_ref, mid_ref = work_ref.at[0], work_ref.at[1]
    rows = val.shape[0]
    per_mid, per_class = rows // MID, rows // CLASSES
    for s in range(val.shape[1] // LANES):
        slab_ref[s] = val[:, s * LANES:(s + 1) * LANES]
        out_lanes = slice(CLS_COLS[name].start + s * LANES, CLS_COLS[name].start + (s + 1) * LANES)
        for r4 in range(MID):
            mid_ref[s, r4 * per_mid:(r4 + 1) * per_mid] = slab_ref[s, pl.ds(r4, per_mid, stride=MID), :]
        for r4 in range(MID):
            for g in range(SLABS_PER_MID):
                cls_ref[r4 + MID * g, :, out_lanes] = (
                    mid_ref[s, pl.ds(r4 * per_mid + g, per_class, stride=SLABS_PER_MID), :].astype(BF16))


def _proj_kernel(x_ref, gains_ref, w32_ref, seq_ref, cls_ref, w_ref, h_ref, work_ref):
    d_model = x_ref.shape[1]

    @pl.when(pl.program_id(0) == 0)
    def _():
        w_ref[...] = w32_ref[...].astype(BF16)

    gains, _ = _column_slices((("x", d_model), ("g_v", GMLP_WIDTH), ("a_q", ATTN_WIDTH),
                               ("a_k", ATTN_WIDTH), ("m_q", MEM_WIDTH)))
    gain = lambda name: gains_ref[:, gains[name]]

    x = x_ref[...]
    ms = jnp.mean(x * x, axis=-1, keepdims=True)
    h_ref[...] = (x * lax.rsqrt(ms + EPS) * gain("x")).astype(BF16)

    def project(name):
        return jnp.dot(h_ref[...], w_ref[:, IN_COLS[name]], preferred_element_type=F32)

    def put(name, val):
        seq_ref[:, SEQ_COLS[name]] = val.astype(BF16)

    g_u = project("g_u")
    g_v = project("g_v")
    g_gate = project("g_gate")
    put("g_ug", g_u * _silu(g_gate))
    a_q = project("a_q")
    put("g_vn", _head_rms_lanes(g_v, gain("g_v")))
    a_k = project("a_k")
    _store_both_layouts(_head_rms_lanes(a_q, gain("a_q")) * LOGIT_SCALE, "q", seq_ref, cls_ref, work_ref)
    a_v = project("a_v")
    _store_both_layouts(_head_rms_lanes(a_k, gain("a_k")), "k", seq_ref, cls_ref, work_ref)
    m_q = project("m_q")
    _store_both_layouts(a_v, "v", seq_ref, cls_ref, work_ref)
    a_gate = project("a_gate")
    put("m_q", _head_rms_lanes(m_q, gain("m_q")) * LOGIT_SCALE)
    m_gate = project("m_gate")
    put("a_gate", _silu(a_gate))
    put("m_gate", _silu(m_gate))


def _pair_attention(q2, k2, v2, bias):
    rows = q2.shape[0]
    left = lax.broadcasted_iota(jnp.int32, (rows, LANES), 1) < HEAD_DIM
    zeros = jnp.zeros_like(q2)
    q_stack = jnp.concatenate([jnp.where(left, q2, zeros), jnp.where(left, zeros, q2)], axis=0)
    v_aug = jnp.concatenate([v2, jnp.ones_like(v2)], axis=1)
    s = lax.dot_general(q_stack, k2, (((1,), (1,)), ((), ())), preferred_element_type=F32)
    if bias is not None:
        s = s + bias
    m = jnp.max(s, axis=1, keepdims=True)
    p = jnp.exp2(s - m).astype(BF16)
    pv = jnp.dot(p, v_aug, preferred_element_type=F32)
    pick = lambda t: jnp.where(left, t[:rows], t[rows:])
    return pick(pv[:, :LANES]), pick(pv[:, LANES:]), pick(jnp.broadcast_to(m, (2 * rows, LANES)))


def _merge_states(a, b):
    top = jnp.maximum(a[2], b[2])
    w_a, w_b = jnp.exp2(a[2] - top), jnp.exp2(b[2] - top)
    return w_a * a[0] + w_b * b[0], w_a * a[1] + w_b * b[1], top


def _normalised(state):
    acc, den, top = state
    return acc / den, top + jnp.log(den) * LOG2E


def _band_bias(rel):
    return jnp.where((rel >= 0) & (rel <= BLOCK), 0.0, MASKED).astype(F32)


def _fill_bias_tables(bias_ref, slabs):
    row = lax.broadcasted_iota(jnp.int32, (2 * BLOCK, 2 * BLOCK), 0) % BLOCK
    col = lax.broadcasted_iota(jnp.int32, (2 * BLOCK, 2 * BLOCK), 1)
    bias_ref[0] = _band_bias(row - col)
    bias_ref[1] = _band_bias(row - col + BLOCK)
    if bias_ref.shape[0] > 2:
        q_rows, k_rows = BLOCK // slabs, 2 * BLOCK // slabs
        rel = (slabs * (row % q_rows) + row // q_rows) - (slabs * (col % k_rows) + col // k_rows)
        bias_ref[2] = _band_bias(rel)
        bias_ref[3] = _band_bias(rel + BLOCK)


def _first_grid_step():
    return (pl.program_id(0) == 0) & (pl.program_id(1) == 0)


def _strided_kernel(qkv_ref, o_ref, lse_ref, state16_ref, bias_ref):
    slab_len = qkv_ref.shape[1]

    def lanes_of(name, pair):
        start = CLS_COLS[name].start + pair * LANES
        return slice(start, start + LANES)

    @pl.when(_first_grid_step())
    def _():
        _fill_bias_tables(bias_ref, SLABS_PER_MID)

    def coarse(i, carry):
        g = i // (slab_len // BLOCK)
        blk = i % (slab_len // BLOCK)
        kstart = pl.multiple_of(jnp.maximum(blk - 1, 0) * BLOCK, BLOCK)
        generic = jnp.minimum(blk, 1)
        qstart = pl.multiple_of(blk * BLOCK, BLOCK)
        for pair in range(PAIRS):
            lanes = slice(pair * LANES, (pair + 1) * LANES)
            state = _pair_attention(qkv_ref[g, pl.ds(qstart, BLOCK), lanes_of("q", pair)],
                                    qkv_ref[g, pl.ds(kstart, 2 * BLOCK), lanes_of("k", pair)],
                                    qkv_ref[g, pl.ds(kstart, 2 * BLOCK), lanes_of("v", pair)],
                                    bias_ref[generic])
            for part in range(3):
                state16_ref[part, g, pl.ds(qstart, BLOCK), lanes] = state[part]
        return carry

    lax.fori_loop(0, SLABS_PER_MID * (slab_len // BLOCK), coarse, 0, unroll=LOOP_UNROLL)

    q_rows = BLOCK // SLABS_PER_MID
    k_rows = 2 * q_rows

    def gather(ref, start, rows, lanes):
        return jnp.concatenate([ref[g, pl.ds(start, rows), lanes] for g in range(SLABS_PER_MID)], axis=0)

    def gather_state16(start, rows, lanes):
        return tuple(gather(state16_ref.at[part], start, rows, lanes) for part in range(3))

    def mid(i, carry):
        a0 = pl.multiple_of(i * q_rows, q_rows)
        kstart = pl.multiple_of(jnp.maximum(i - 1, 0) * q_rows, q_rows)
        generic = jnp.minimum(i, 1)
        for pair in range(PAIRS):
            lanes = slice(pair * LANES, (pair + 1) * LANES)
            state4 = _pair_attention(gather(qkv_ref, a0, q_rows, lanes_of("q", pair)),
                                     gather(qkv_ref, kstart, k_rows, lanes_of("k", pair)),
                                     gather(qkv_ref, kstart, k_rows, lanes_of("v", pair)),
                                     bias_ref[2 + generic])
            o, lse = _normalised(_merge_states(state4, gather_state16(a0, q_rows, lanes)))
            for g in range(SLABS_PER_MID):
                o_ref[g, pl.ds(a0, q_rows), lanes] = o[g * q_rows:(g + 1) * q_rows].astype(BF16)
                lse_ref[g, pl.ds(a0, q_rows), lanes] = lse[g * q_rows:(g + 1) * q_rows]
        return carry

    lax.fori_loop(0, slab_len // q_rows, mid, 0, unroll=LOOP_UNROLL)


def _to_sequence_order(cls_ref, slab_ref, mid_ref):
    per_class = cls_ref.shape[1]
    per_mid = per_class * SLABS_PER_MID
    n_slabs = cls_ref.shape[2] // LANES
    for s in range(n_slabs):
        lanes = slice(s * LANES, (s + 1) * LANES)
        for r4 in range(MID):
            for g in range(SLABS_PER_MID):
                mid_ref[s, pl.ds(r4 * per_mid + g, per_class, stride=SLABS_PER_MID), :] = (
                    cls_ref[r4 + MID * g, :, lanes].astype(F32))
        for r4 in range(MID):
            slab_ref[s, pl.ds(r4, per_mid, stride=MID), :] = mid_ref[s, r4 * per_mid:(r4 + 1) * per_mid]


def _mix_kernel(x_ref, seq_ref, prev_ref, ws_ref, bs_ref, oc_ref, lc_ref,
                mem_ref, mgains_ref, wkv32_ref, wout32_ref, out_ref,
                oslab_ref, lslab_ref, bias_ref, wout_ref, wkv_ref, mk_ref, mv_ref, *, steps_per_batch):
    rows, d_model = x_ref.shape
    step = pl.program_id(0)

    def seq(name, row_slice=slice(None), pair=None):
        cols = SEQ_COLS[name]
        if pair is not None:
            cols = slice(cols.start + pair * LANES, cols.start + (pair + 1) * LANES)
        return seq_ref[row_slice, cols]

    @pl.when(step % steps_per_batch == 0)
    def _():
        @pl.when(step == 0)
        def _():
            wkv_ref[...] = wkv32_ref[...].astype(BF16)
        mem = mem_ref[...]
        ms = jnp.mean(mem * mem, axis=-1, keepdims=True)
        hm = (mem * lax.rsqrt(ms + EPS) * mgains_ref[:, :d_model]).astype(BF16)
        mk = jnp.dot(hm, wkv_ref[:, :MEM_WIDTH], preferred_element_type=F32)
        mk_ref[...] = _head_rms_lanes(mk, mgains_ref[:, d_model:]).astype(BF16)
        mv_ref[...] = jnp.dot(hm, wkv_ref[:, MEM_WIDTH:], preferred_element_type=F32).astype(BF16)

    @pl.when(step == 0)
    def _():
        wout_ref[...] = wout32_ref[...].astype(BF16)
        row = lax.broadcasted_iota(jnp.int32, (2 * BLOCK, 2 * BLOCK), 0) % BLOCK
        col = lax.broadcasted_iota(jnp.int32, (2 * BLOCK, 2 * BLOCK), 1)
        band = _band_bias(row - col + BLOCK)
        bias_ref[1] = band
        bias_ref[0] = jnp.where(col >= BLOCK, band, MASKED)
    tri = (lax.broadcasted_iota(jnp.int32, (CHUNK, CHUNK), 0)
           >= lax.broadcasted_iota(jnp.int32, (CHUNK, CHUNK), 1))
    left = lax.broadcasted_iota(jnp.int32, (CHUNK, LANES), 1) < HEAD_DIM
    w_tri = [jnp.where(tri, ws_ref[h], 0.0).astype(BF16) for h in range(GMLP_HEADS)]
    sp_chunks = []
    for c in range(rows // CHUNK):
        pairs = []
        for pair in range(GMLP_WIDTH // LANES):
            vn2 = seq("g_vn", slice(c * CHUNK, (c + 1) * CHUNK), pair)
            sp_a = jnp.dot(w_tri[2 * pair], vn2, preferred_element_type=F32)
            sp_b = jnp.dot(w_tri[2 * pair + 1], vn2, preferred_element_type=F32)
            pairs.append(jnp.where(left, sp_a, sp_b))
        sp_chunks.append(jnp.concatenate(pairs, axis=1) + bs_ref[...])
    y_g = seq("g_ug").astype(F32) * jnp.concatenate(sp_chunks, axis=0)

    mem_pairs = []
    for pair in range(MEM_WIDTH // LANES):
        lanes = slice(pair * LANES, (pair + 1) * LANES)
        acc, den, _ = _pair_attention(seq("m_q", pair=pair), mk_ref[:, lanes], mv_ref[:, lanes], None)
        mem_pairs.append(acc / den)
    y_m = jnp.concatenate(mem_pairs, axis=1) * seq("m_gate").astype(F32)

    _to_sequence_order(oc_ref, oslab_ref.at[0], oslab_ref.at[1])
    _to_sequence_order(lc_ref, lslab_ref.at[0], lslab_ref.at[1])
    first_bias = jnp.where(step % steps_per_batch == 0, 0, 1)
    block_outs = []
    n_blocks = rows // BLOCK
    for blk in range(n_blocks):
        if blk == n_blocks // 2:
            partial = (x_ref[...]
                       + jnp.dot(y_g.astype(BF16), wout_ref[:GMLP_WIDTH, :], preferred_element_type=F32)
                       + jnp.dot(y_m.astype(BF16), wout_ref[GMLP_WIDTH + ATTN_WIDTH:, :],
                                 preferred_element_type=F32))
        own = slice(blk * BLOCK, (blk + 1) * BLOCK)
        pair_outs = []
        for pair in range(PAIRS):
            lanes = slice(pair * LANES, (pair + 1) * LANES)
            if blk == 0:
                k_lanes = slice(SEQ_COLS["k"].start + pair * LANES, SEQ_COLS["k"].start + (pair + 1) * LANES)
                v_lanes = slice(SEQ_COLS["v"].start + pair * LANES, SEQ_COLS["v"].start + (pair + 1) * LANES)
                k2 = jnp.concatenate([prev_ref[:, k_lanes], seq("k", own, pair)], axis=0)
                v2 = jnp.concatenate([prev_ref[:, v_lanes], seq("v", own, pair)], axis=0)
                bias = bias_ref[first_bias]
            else:
                window = slice((blk - 1) * BLOCK, (blk + 1) * BLOCK)
                k2, v2, bias = seq("k", window, pair), seq("v", window, pair), bias_ref[1]
            acc1, den1, top1 = _pair_attention(seq("q", own, pair), k2, v2, bias)
            o_c, top_c = oslab_ref[0, pair, own, :], lslab_ref[0, pair, own, :]
            top = jnp.maximum(top1, top_c)
            w_1, w_c = jnp.exp2(top1 - top), jnp.exp2(top_c - top)
            pair_outs.append((w_1 * acc1 + w_c * o_c) / (w_1 * den1 + w_c))
        block_outs.append(jnp.concatenate(pair_outs, axis=1))
    y_a = jnp.concatenate(block_outs, axis=0) * seq("a_gate").astype(F32)
    out_ref[...] = partial + jnp.dot(y_a.astype(BF16), wout_ref[GMLP_WIDTH:GMLP_WIDTH + ATTN_WIDTH, :],
                                     preferred_element_type=F32)


def _cparams(sem):
    return pltpu.CompilerParams(dimension_semantics=sem, vmem_limit_bytes=VMEM_LIMIT)


def _const_spec(shape):
    return pl.BlockSpec(shape, lambda *idx: (0,) * len(shape))


def _regroup_scratch(tile):
    return pltpu.VMEM((2, ATTN_WIDTH // LANES, tile, LANES), F32)


def _class_tile_spec(tile, steps_per_batch, width):
    return pl.BlockSpec((None, CLASSES, tile // CLASSES, width),
                        lambda i: (i // steps_per_batch, 0, i % steps_per_batch, 0))


def _project(x2d, batch, seq, gains, w_in):
    rows, d_model = x2d.shape
    return pl.pallas_call(
        _proj_kernel,
        grid=(rows // PROJ_TILE,),
        in_specs=[pl.BlockSpec((PROJ_TILE, d_model), lambda i: (i, 0)),
                  _const_spec(gains.shape),
                  pl.BlockSpec(w_in.shape, lambda i: (0, 0), pipeline_mode=pl.Buffered(1))],
        out_specs=[pl.BlockSpec((PROJ_TILE, SEQ_WIDTH), lambda i: (i, 0)),
                   _class_tile_spec(PROJ_TILE, seq // PROJ_TILE, CLS_WIDTH)],
        out_shape=[jax.ShapeDtypeStruct((rows, SEQ_WIDTH), BF16),
                   jax.ShapeDtypeStruct((batch, CLASSES, seq // CLASSES, CLS_WIDTH), BF16)],
        scratch_shapes=[pltpu.VMEM(w_in.shape, BF16), pltpu.VMEM((PROJ_TILE, d_model), BF16),
                        _regroup_scratch(PROJ_TILE)],
        compiler_params=_cparams(("arbitrary",)),
    )(x2d, gains, w_in)


def _strided_attention(qkv_cls):
    batch, _, slab_len, _ = qkv_cls.shape
    spec = lambda width: pl.BlockSpec((None, SLABS_PER_MID, None, slab_len, width),
                                      lambda b, r4: (b, 0, r4, 0, 0))
    split = lambda width: (batch, SLABS_PER_MID, MID, slab_len, width)
    o, lse = pl.pallas_call(
        _strided_kernel,
        grid=(batch, MID),
        in_specs=[spec(CLS_WIDTH)],
        out_specs=[spec(ATTN_WIDTH)] * 2,
        out_shape=[jax.ShapeDtypeStruct(split(ATTN_WIDTH), BF16), jax.ShapeDtypeStruct(split(ATTN_WIDTH), F32)],
        scratch_shapes=[pltpu.VMEM((3, SLABS_PER_MID, slab_len, ATTN_WIDTH), F32),
                        pltpu.VMEM((4, 2 * BLOCK, 2 * BLOCK), F32)],
        compiler_params=_cparams(("arbitrary", "arbitrary")),
    )(qkv_cls.reshape(split(CLS_WIDTH)))
    merged = (batch, CLASSES, slab_len, ATTN_WIDTH)
    return o.reshape(merged), lse.reshape(merged)


def _mix(x2d, acts, w_s, b_full, oc, lc, mem, mem_gains, w_kv, w_out, seq):
    rows, d_model = x2d.shape
    mem_len = mem.shape[1]
    steps_per_batch = seq // ROW_TILE
    row_spec = lambda w: pl.BlockSpec((ROW_TILE, w), lambda i: (i, 0))
    prev_spec = pl.BlockSpec((BLOCK, SEQ_COLS["v"].stop),
                             lambda i: (jnp.maximum(i * (ROW_TILE // BLOCK) - 1, 0), 0))
    mem_spec = pl.BlockSpec((None,) + mem.shape[1:], lambda i: (i // steps_per_batch, 0, 0))
    weight_spec = lambda w: pl.BlockSpec(w.shape, lambda i: (0, 0), pipeline_mode=pl.Buffered(1))
    cls_spec = _class_tile_spec(ROW_TILE, steps_per_batch, ATTN_WIDTH)
    return pl.pallas_call(
        functools.partial(_mix_kernel, steps_per_batch=steps_per_batch),
        grid=(rows // ROW_TILE,),
        in_specs=[row_spec(d_model), row_spec(SEQ_WIDTH), prev_spec,
                  _const_spec(w_s.shape), _const_spec(b_full.shape), cls_spec, cls_spec,
                  mem_spec, _const_spec(mem_gains.shape), weight_spec(w_kv), weight_spec(w_out)],
        out_specs=row_spec(d_model),
        out_shape=jax.ShapeDtypeStruct((rows, d_model), F32),
        scratch_shapes=[_regroup_scratch(ROW_TILE)] * 2
                       + [pltpu.VMEM((2, 2 * BLOCK, 2 * BLOCK), F32), pltpu.VMEM(w_out.shape, BF16),
                          pltpu.VMEM(w_kv.shape, BF16)] + [pltpu.VMEM((mem_len, MEM_WIDTH), BF16)] * 2,
        compiler_params=_cparams(("arbitrary",)),
    )(x2d, acts, acts, w_s, b_full, oc, lc, mem, mem_gains, w_kv, w_out)


def kernel(x, mem, norm_gain, w_in, gmlp_v_gain, gmlp_w_s, gmlp_b, attn_q_gain, attn_k_gain,
           mem_norm_gain, w_mem_kv, mem_q_gain, mem_k_gain, w_out):
    batch, seq, d_model = x.shape
    depth = w_in.shape[0]
    assert DILATED_CONFIGS == ((BLOCK, 1), (BLOCK * MID, MID), (BLOCK * CLASSES, CLASSES))
    assert seq % PROJ_TILE == 0 and seq % ROW_TILE == 0 and seq % (BLOCK * CLASSES) == 0
    x2d = x.reshape(batch * seq, d_model)
    for l in range(depth):
        pack = lambda *parts: jnp.concatenate([p.reshape(-1).astype(F32) for p in parts]).reshape(1, -1)
        proj_gains = pack(norm_gain[l], gmlp_v_gain[l], jnp.tile(attn_q_gain[l], ATTN_HEADS),
                          jnp.tile(attn_k_gain[l], ATTN_HEADS), jnp.tile(mem_q_gain[l], MEM_HEADS))
        acts, qkv_cls = _project(x2d, batch, seq, proj_gains, w_in[l])
        oc, lc = _strided_attention(qkv_cls)
        b_full = jnp.repeat(gmlp_b[l].T, HEAD_DIM, axis=1)
        x2d = _mix(x2d, acts, gmlp_w_s[l], b_full, oc, lc, mem,
                   pack(mem_norm_gain[l], jnp.tile(mem_k_gain[l], MEM_HEADS)), w_mem_kv[l], w_out[l], seq)
    return x2d.reshape(batch, seq, d_model)
```

```python
import functools
import math

import jax
import jax.numpy as jnp
from jax import lax
from jax.experimental import pallas as pl
from jax.experimental.pallas import tpu as pltpu

HEAD_DIM = 64
GMLP_HEADS = 4
ATTN_HEADS = 8
MEM_HEADS = 4
GMLP_WIDTH = GMLP_HEADS * HEAD_DIM
ATTN_WIDTH = ATTN_HEADS * HEAD_DIM
MEM_WIDTH = MEM_HEADS * HEAD_DIM
CHUNK = 128
BLOCK = 128
DILATED_CONFIGS = ((128, 1), (512, 4), (2048, 16))
EPS = 1e-6
MASKED = -1e30
LOG2E = math.log2(math.e)
LOGIT_SCALE = LOG2E / math.sqrt(HEAD_DIM)

IN_SEGMENTS = (("g_u", GMLP_WIDTH), ("g_v", GMLP_WIDTH), ("g_gate", GMLP_WIDTH),
               ("a_q", ATTN_WIDTH), ("a_k", ATTN_WIDTH), ("a_v", ATTN_WIDTH), ("a_gate", ATTN_WIDTH),
               ("m_q", MEM_WIDTH), ("m_gate", MEM_WIDTH))


def _column_slices(segments):
    slices, start = {}, 0
    for name, width in segments:
        slices[name] = slice(start, start + width)
        start += width
    return slices, start


IN_COLS, _ = _column_slices(IN_SEGMENTS)
SEQ_COLS, SEQ_WIDTH = _column_slices((("k", ATTN_WIDTH), ("v", ATTN_WIDTH), ("q", ATTN_WIDTH),
                                      ("a_gate", ATTN_WIDTH), ("g_ug", GMLP_WIDTH), ("g_vn", GMLP_WIDTH),
                                      ("m_q", MEM_WIDTH), ("m_gate", MEM_WIDTH)))
CLS_COLS, CLS_WIDTH = _column_slices((("q", ATTN_WIDTH), ("k", ATTN_WIDTH), ("v", ATTN_WIDTH)))

CLASSES = 16
MID = 4
SLABS_PER_MID = CLASSES // MID
LANES = 128
PAIRS = ATTN_WIDTH // LANES
PROJ_TILE = 1024
ROW_TILE = 512
LOOP_UNROLL = 8
CLASSES_PER_STEP = 2
VMEM_LIMIT = 56 * 1024 * 1024

F32 = jnp.float32
BF16 = jnp.bfloat16


def _head_rms_lanes(acc, gain):
    left = lax.broadcasted_iota(jnp.int32, (acc.shape[0], LANES), 1) < HEAD_DIM
    groups = []
    for j in range(acc.shape[1] // LANES):
        a = acc[:, j * LANES:(j + 1) * LANES]
        sq = a * a
        ss_a = jnp.sum(jnp.where(left, sq, 0.0), axis=1, keepdims=True)
        ss_b = jnp.sum(jnp.where(left, 0.0, sq), axis=1, keepdims=True)
        inv = jnp.where(left, lax.rsqrt(ss_a * (1.0 / HEAD_DIM) + EPS),
                        lax.rsqrt(ss_b * (1.0 / HEAD_DIM) + EPS))
        groups.append(a * inv)
    return jnp.concatenate(groups, axis=1) * gain


def _silu(x):
    return x * jax.nn.sigmoid(x)


def _store_both_layouts(val, name, seq_ref, cls_ref, work_ref):
    seq_ref[:, SEQ_COLS[name]] = val.astype(BF16)
    slab_ref, mid_ref = work_ref.at[0], work_ref.at[1]
    rows = val.shape[0]
    per_mid, per_class = rows // MID, rows // CLASSES
    for s in range(val.shape[1] // LANES):
        slab_ref[s] = val[:, s * LANES:(s + 1) * LANES]
        out_lanes = slice(CLS_COLS[name].start + s * LANES, CLS_COLS[name].start + (s + 1) * LANES)
        for r4 in range(MID):
            mid_ref[s, r4 * per_mid:(r4 + 1) * per_mid] = slab_ref[s, pl.ds(r4, per_mid, stride=MID), :]
        for r4 in range(MID):
            for g in range(SLABS_PER_MID):
                cls_ref[r4 + MID * g, :, out_lanes] = (
                    mid_ref[s, pl.ds(r4 * per_mid + g, per_class, stride=SLABS_PER_MID), :].astype(BF16))


def _proj_kernel(x_ref, gains_ref, w32_ref, seq_ref, cls_ref, w_ref, h_ref, work_ref):
    d_model = x_ref.shape[1]

    @pl.when(pl.program_id(0) == 0)
    def _():
        w_ref[...] = w32_ref[...].astype(BF16)

    gains, _ = _column_slices((("x", d_model), ("g_v", GMLP_WIDTH), ("a_q", ATTN_WIDTH),
                               ("a_k", ATTN_WIDTH), ("m_q", MEM_WIDTH)))
    gain = lambda name: gains_ref[:, gains[name]]

    x = x_ref[...]
    ms = jnp.mean(x * x, axis=-1, keepdims=True)
    h_ref[...] = (x * lax.rsqrt(ms + EPS) * gain("x")).astype(BF16)

    def project(name):
        return jnp.dot(h_ref[...], w_ref[:, IN_COLS[name]], preferred_element_type=F32)

    def put(name, val):
        seq_ref[:, SEQ_COLS[name]] = val.astype(BF16)

    g_u = project("g_u")
    g_v = project("g_v")
    g_gate = project("g_gate")
    put("g_ug", g_u * _silu(g_gate))
    a_q = project("a_q")
    put("g_vn", _head_rms_lanes(g_v, gain("g_v")))
    a_k = project("a_k")
    _store_both_layouts(_head_rms_lanes(a_q, gain("a_q")) * LOGIT_SCALE, "q", seq_ref, cls_ref, work_ref)
    a_v = project("a_v")
    _store_both_layouts(_head_rms_lanes(a_k, gain("a_k")), "k", seq_ref, cls_ref, work_ref)
    m_q = project("m_q")
    _store_both_layouts(a_v, "v", seq_ref, cls_ref, work_ref)
    a_gate = project("a_gate")
    put("m_q", _head_rms_lanes(m_q, gain("m_q")) * LOGIT_SCALE)
    m_gate = project("m_gate")
    put("a_gate", _silu(a_gate))
    put("m_gate", _silu(m_gate))


def _pair_attention(q2, k2, v2, bias):
    rows = q2.shape[0]
    left = lax.broadcasted_iota(jnp.int32, (rows, LANES), 1) < HEAD_DIM
    zeros = jnp.zeros_like(q2)
    q_stack = jnp.concatenate([jnp.where(left, q2, zeros), jnp.where(left, zeros, q2)], axis=0)
    v_aug = jnp.concatenate([v2, jnp.ones_like(v2)], axis=1)
    s = lax.dot_general(q_stack, k2, (((1,), (1,)), ((), ())), preferred_element_type=F32)
    if bias is not None:
        s = s + bias
    m = jnp.max(s, axis=1, keepdims=True)
    p = jnp.exp2(s - m).astype(BF16)
    pv = jnp.dot(p, v_aug, preferred_element_type=F32)
    pick = lambda t: jnp.where(left, t[:rows], t[rows:])
    return pick(pv[:, :LANES]), pick(pv[:, LANES:]), pick(jnp.broadcast_to(m, (2 * rows, LANES)))


def _merge_states(a, b):
    top = jnp.maximum(a[2], b[2])
    w_a, w_b = jnp.exp2(a[2] - top), jnp.exp2(b[2] - top)
    return w_a * a[0] + w_b * b[0], w_a * a[1] + w_b * b[1], top


def _normalised(state):
    acc, den, top = state
    return acc / den, top + jnp.log(den) * LOG2E


def _band_bias(rel):
    return jnp.where((rel >= 0) & (rel <= BLOCK), 0.0, MASKED).astype(F32)


def _fill_bias_tables(bias_ref, slabs):
    row = lax.broadcasted_iota(jnp.int32, (2 * BLOCK, 2 * BLOCK), 0) % BLOCK
    col = lax.broadcasted_iota(jnp.int32, (2 * BLOCK, 2 * BLOCK), 1)
    bias_ref[0] = _band_bias(row - col)
    bias_ref[1] = _band_bias(row - col + BLOCK)
    if bias_ref.shape[0] > 2:
        q_rows, k_rows = BLOCK // slabs, 2 * BLOCK // slabs
        rel = (slabs * (row % q_rows) + row // q_rows) - (slabs * (col % k_rows) + col // k_rows)
        bias_ref[2] = _band_bias(rel)
        bias_ref[3] = _band_bias(rel + BLOCK)


def _first_grid_step():
    return (pl.program_id(0) == 0) & (pl.program_id(1) == 0)


def _strided_kernel(qkv_ref, o_ref, lse_ref, state16_ref, bias_ref):
    @pl.when(_first_grid_step())
    def _():
        _fill_bias_tables(bias_ref, SLABS_PER_MID)

    for j in range(qkv_ref.shape[1]):
        _strided_class(qkv_ref.at[:, j], o_ref.at[:, j], lse_ref.at[:, j], state16_ref, bias_ref)


def _strided_class(qkv_ref, o_ref, lse_ref, state16_ref, bias_ref):
    slab_len = qkv_ref.shape[1]

    def lanes_of(name, pair):
        start = CLS_COLS[name].start + pair * LANES
        return slice(start, start + LANES)

    def coarse(i, carry):
        g = i // (slab_len // BLOCK)
        blk = i % (slab_len // BLOCK)
        kstart = pl.multiple_of(jnp.maximum(blk - 1, 0) * BLOCK, BLOCK)
        generic = jnp.minimum(blk, 1)
        qstart = pl.multiple_of(blk * BLOCK, BLOCK)
        for pair in range(PAIRS):
            lanes = slice(pair * LANES, (pair + 1) * LANES)
            state = _pair_attention(qkv_ref[g, pl.ds(qstart, BLOCK), lanes_of("q", pair)],
                                    qkv_ref[g, pl.ds(kstart, 2 * BLOCK), lanes_of("k", pair)],
                                    qkv_ref[g, pl.ds(kstart, 2 * BLOCK), lanes_of("v", pair)],
                                    bias_ref[generic])
            for part in range(3):
                state16_ref[part, g, pl.ds(qstart, BLOCK), lanes] = state[part]
        return carry

    lax.fori_loop(0, SLABS_PER_MID * (slab_len // BLOCK), coarse, 0, unroll=LOOP_UNROLL)

    q_rows = BLOCK // SLABS_PER_MID
    k_rows = 2 * q_rows

    def gather(ref, start, rows, lanes):
        return jnp.concatenate([ref[g, pl.ds(start, rows), lanes] for g in range(SLABS_PER_MID)], axis=0)

    def gather_state16(start, rows, lanes):
        return tuple(gather(state16_ref.at[part], start, rows, lanes) for part in range(3))

    def mid(i, carry):
        a0 = pl.multiple_of(i * q_rows, q_rows)
        kstart = pl.multiple_of(jnp.maximum(i - 1, 0) * q_rows, q_rows)
        generic = jnp.minimum(i, 1)
        for pair in range(PAIRS):
            lanes = slice(pair * LANES, (pair + 1) * LANES)
            state4 = _pair_attention(gather(qkv_ref, a0, q_rows, lanes_of("q", pair)),
                                     gather(qkv_ref, kstart, k_rows, lanes_of("k", pair)),
                                     gather(qkv_ref, kstart, k_rows, lanes_of("v", pair)),
                                     bias_ref[2 + generic])
            o, lse = _normalised(_merge_states(state4, gather_state16(a0, q_rows, lanes)))
            for g in range(SLABS_PER_MID):
                o_ref[g, pl.ds(a0, q_rows), lanes] = o[g * q_rows:(g + 1) * q_rows].astype(BF16)
                lse_ref[g, pl.ds(a0, q_rows), lanes] = lse[g * q_rows:(g + 1) * q_rows]
        return carry

    lax.fori_loop(0, slab_len // q_rows, mid, 0, unroll=LOOP_UNROLL)


def _to_sequence_order(cls_ref, slab_ref, mid_ref):
    per_class = cls_ref.shape[1]
    per_mid = per_class * SLABS_PER_MID
    n_slabs = cls_ref.shape[2] // LANES
    for s in range(n_slabs):
        lanes = slice(s * LANES, (s + 1) * LANES)
        for r4 in range(MID):
            for g in range(SLABS_PER_MID):
                mid_ref[s, pl.ds(r4 * per_mid + g, per_class, stride=SLABS_PER_MID), :] = (
                    cls_ref[r4 + MID * g, :, lanes].astype(F32))
        for r4 in range(MID):
            slab_ref[s, pl.ds(r4, per_mid, stride=MID), :] = mid_ref[s, r4 * per_mid:(r4 + 1) * per_mid]


def _mix_kernel(x_ref, seq_ref, prev_ref, ws_ref, bs_ref, oc_ref, lc_ref,
                mem_ref, mgains_ref, wkv32_ref, wout32_ref, out_ref,
                oslab_ref, lslab_ref, bias_ref, wout_ref, wkv_ref, mk_ref, mv_ref, *, steps_per_batch):
    rows, d_model = x_ref.shape
    step = pl.program_id(0)

    def seq(name, row_slice=slice(None), pair=None):
        cols = SEQ_COLS[name]
        if pair is not None:
            cols = slice(cols.start + pair * LANES, cols.start + (pair + 1) * LANES)
        return seq_ref[row_slice, cols]

    @pl.when(step % steps_per_batch == 0)
    def _():
        @pl.when(step == 0)
        def _():
            wkv_ref[...] = wkv32_ref[...].astype(BF16)
        mem = mem_ref[...]
        ms = jnp.mean(mem * mem, axis=-1, keepdims=True)
        hm = (mem * lax.rsqrt(ms + EPS) * mgains_ref[:, :d_model]).astype(BF16)
        mk = jnp.dot(hm, wkv_ref[:, :MEM_WIDTH], preferred_element_type=F32)
        mk_ref[...] = _head_rms_lanes(mk, mgains_ref[:, d_model:]).astype(BF16)
        mv_ref[...] = jnp.dot(hm, wkv_ref[:, MEM_WIDTH:], preferred_element_type=F32).astype(BF16)

    @pl.when(step == 0)
    def _():
        wout_ref[...] = wout32_ref[...].astype(BF16)
        row = lax.broadcasted_iota(jnp.int32, (2 * BLOCK, 2 * BLOCK), 0) % BLOCK
        col = lax.broadcasted_iota(jnp.int32, (2 * BLOCK, 2 * BLOCK), 1)
        band = _band_bias(row - col + BLOCK)
        bias_ref[1] = band
        bias_ref[0] = jnp.where(col >= BLOCK, band, MASKED)
    tri = (lax.broadcasted_iota(jnp.int32, (CHUNK, CHUNK), 0)
           >= lax.broadcasted_iota(jnp.int32, (CHUNK, CHUNK), 1))
    left = lax.broadcasted_iota(jnp.int32, (CHUNK, LANES), 1) < HEAD_DIM
    w_tri = [jnp.where(tri, ws_ref[h], 0.0).astype(BF16) for h in range(GMLP_HEADS)]
    sp_chunks = []
    for c in range(rows // CHUNK):
        pairs = []
        for pair in range(GMLP_WIDTH // LANES):
            vn2 = seq("g_vn", slice(c * CHUNK, (c + 1) * CHUNK), pair)
            sp_a = jnp.dot(w_tri[2 * pair], vn2, preferred_element_type=F32)
            sp_b = jnp.dot(w_tri[2 * pair + 1], vn2, preferred_element_type=F32)
            pairs.append(jnp.where(left, sp_a, sp_b))
        sp_chunks.append(jnp.concatenate(pairs, axis=1) + bs_ref[...])
    y_g = seq("g_ug").astype(F32) * jnp.concatenate(sp_chunks, axis=0)

    mem_pairs = []
    for pair in range(MEM_WIDTH // LANES):
        lanes = slice(pair * LANES, (pair + 1) * LANES)
        acc, den, _ = _pair_attention(seq("m_q", pair=pair), mk_ref[:, lanes], mv_ref[:, lanes], None)
        mem_pairs.append(acc / den)
    y_m = jnp.concatenate(mem_pairs, axis=1) * seq("m_gate").astype(F32)

    _to_sequence_order(oc_ref, oslab_ref.at[0], oslab_ref.at[1])
    _to_sequence_order(lc_ref, lslab_ref.at[0], lslab_ref.at[1])
    first_bias = jnp.where(step % steps_per_batch == 0, 0, 1)
    block_outs = []
    n_blocks = rows // BLOCK
    for blk in range(n_blocks):
        if blk == n_blocks // 2:
            partial = (x_ref[...]
                       + jnp.dot(y_g.astype(BF16), wout_ref[:GMLP_WIDTH, :], preferred_element_type=F32)
                       + jnp.dot(y_m.astype(BF16), wout_ref[GMLP_WIDTH + ATTN_WIDTH:, :],
                                 preferred_element_type=F32))
        own = slice(blk * BLOCK, (blk + 1) * BLOCK)
        pair_outs = []
        for pair in range(PAIRS):
            lanes = slice(pair * LANES, (pair + 1) * LANES)
            if blk == 0:
                k_lanes = slice(SEQ_COLS["k"].start + pair * LANES, SEQ_COLS["k"].start + (pair + 1) * LANES)
                v_lanes = slice(SEQ_COLS["v"].start + pair * LANES, SEQ_COLS["v"].start + (pair + 1) * LANES)
                k2 = jnp.concatenate([prev_ref[:, k_lanes], seq("k", own, pair)], axis=0)
                v2 = jnp.concatenate([prev_ref[:, v_lanes], seq("v", own, pair)], axis=0)
                bias = bias_ref[first_bias]
            else:
                window = slice((blk - 1) * BLOCK, (blk + 1) * BLOCK)
                k2, v2, bias = seq("k", window, pair), seq("v", window, pair), bias_ref[1]
            acc1, den1, top1 = _pair_attention(seq("q", own, pair), k2, v2, bias)
            o_c, top_c = oslab_ref[0, pair, own, :], lslab_ref[0, pair, own, :]
            top = jnp.maximum(top1, top_c)
            w_1, w_c = jnp.exp2(top1 - top), jnp.exp2(top_c - top)
            pair_outs.append((w_1 * acc1 + w_c * o_c) / (w_1 * den1 + w_c))
        block_outs.append(jnp.concatenate(pair_outs, axis=1))
    y_a = jnp.concatenate(block_outs, axis=0) * seq("a_gate").astype(F32)
    out_ref[...] = partial + jnp.dot(y_a.astype(BF16), wout_ref[GMLP_WIDTH:GMLP_WIDTH + ATTN_WIDTH, :],
                                     preferred_element_type=F32)


def _cparams(sem):
    return pltpu.CompilerParams(dimension_semantics=sem, vmem_limit_bytes=VMEM_LIMIT)


def _const_spec(shape):
    return pl.BlockSpec(shape, lambda *idx: (0,) * len(shape))


def _regroup_scratch(tile):
    return pltpu.VMEM((2, ATTN_WIDTH // LANES, tile, LANES), F32)


def _class_tile_spec(tile, steps_per_batch, width):
    return pl.BlockSpec((None, CLASSES, tile // CLASSES, width),
                        lambda i: (i // steps_per_batch, 0, i % steps_per_batch, 0))


def _project(x2d, batch, seq, gains, w_in):
    rows, d_model = x2d.shape
    return pl.pallas_call(
        _proj_kernel,
        grid=(rows // PROJ_TILE,),
        in_specs=[pl.BlockSpec((PROJ_TILE, d_model), lambda i: (i, 0)),
                  _const_spec(gains.shape),
                  pl.BlockSpec(w_in.shape, lambda i: (0, 0), pipeline_mode=pl.Buffered(1))],
        out_specs=[pl.BlockSpec((PROJ_TILE, SEQ_WIDTH), lambda i: (i, 0)),
                   _class_tile_spec(PROJ_TILE, seq // PROJ_TILE, CLS_WIDTH)],
        out_shape=[jax.ShapeDtypeStruct((rows, SEQ_WIDTH), BF16),
                   jax.ShapeDtypeStruct((batch, CLASSES, seq // CLASSES, CLS_WIDTH), BF16)],
        scratch_shapes=[pltpu.VMEM(w_in.shape, BF16), pltpu.VMEM((PROJ_TILE, d_model), BF16),
                        _regroup_scratch(PROJ_TILE)],
        compiler_params=_cparams(("arbitrary",)),
    )(x2d, gains, w_in)


def _strided_attention(qkv_cls):
    batch, _, slab_len, _ = qkv_cls.shape
    spec = lambda width: pl.BlockSpec((None, SLABS_PER_MID, CLASSES_PER_STEP, slab_len, width),
                                      lambda b, r4: (b, 0, r4, 0, 0))
    split = lambda width: (batch, SLABS_PER_MID, MID, slab_len, width)
    o, lse = pl.pallas_call(
        _strided_kernel,
        grid=(batch, MID // CLASSES_PER_STEP),
        in_specs=[spec(CLS_WIDTH)],
        out_specs=[spec(ATTN_WIDTH)] * 2,
        out_shape=[jax.ShapeDtypeStruct(split(ATTN_WIDTH), BF16), jax.ShapeDtypeStruct(split(ATTN_WIDTH), F32)],
        scratch_shapes=[pltpu.VMEM((3, SLABS_PER_MID, slab_len, ATTN_WIDTH), F32),
                        pltpu.VMEM((4, 2 * BLOCK, 2 * BLOCK), F32)],
        compiler_params=_cparams(("arbitrary", "arbitrary")),
    )(qkv_cls.reshape(split(CLS_WIDTH)))
    merged = (batch, CLASSES, slab_len, ATTN_WIDTH)
    return o.reshape(merged), lse.reshape(merged)


def _mix(x2d, acts, w_s, b_full, oc, lc, mem, mem_gains, w_kv, w_out, seq):
    rows, d_model = x2d.shape
    mem_len = mem.shape[1]
    steps_per_batch = seq // ROW_TILE
    row_spec = lambda w: pl.BlockSpec((ROW_TILE, w), lambda i: (i, 0))
    prev_spec = pl.BlockSpec((BLOCK, SEQ_COLS["v"].stop),
                             lambda i: (jnp.maximum(i * (ROW_TILE // BLOCK) - 1, 0), 0))
    mem_spec = pl.BlockSpec((None,) + mem.shape[1:], lambda i: (i // steps_per_batch, 0, 0))
    weight_spec = lambda w: pl.BlockSpec(w.shape, lambda i: (0, 0), pipeline_mode=pl.Buffered(1))
    cls_spec = _class_tile_spec(ROW_TILE, steps_per_batch, ATTN_WIDTH)
    return pl.pallas_call(
        functools.partial(_mix_kernel, steps_per_batch=steps_per_batch),
        grid=(rows // ROW_TILE,),
        in_specs=[row_spec(d_model), row_spec(SEQ_WIDTH), prev_spec,
                  _const_spec(w_s.shape), _const_spec(b_full.shape), cls_spec, cls_spec,
                  mem_spec, _const_spec(mem_gains.shape), weight_spec(w_kv), weight_spec(w_out)],
        out_specs=row_spec(d_model),
        out_shape=jax.ShapeDtypeStruct((rows, d_model), F32),
        scratch_shapes=[_regroup_scratch(ROW_TILE)] * 2
                       + [pltpu.VMEM((2, 2 * BLOCK, 2 * BLOCK), F32), pltpu.VMEM(w_out.shape, BF16),
                          pltpu.VMEM(w_kv.shape, BF16)] + [pltpu.VMEM((mem_len, MEM_WIDTH), BF16)] * 2,
        compiler_params=_cparams(("arbitrary",)),
    )(x2d, acts, acts, w_s, b_full, oc, lc, mem, mem_gains, w_kv, w_out)


def kernel(x, mem, norm_gain, w_in, gmlp_v_gain, gmlp_w_s, gmlp_b, attn_q_gain, attn_k_gain,
           mem_norm_gain, w_mem_kv, mem_q_gain, mem_k_gain, w_out):
    batch, seq, d_model = x.shape
    depth = w_in.shape[0]
    assert DILATED_CONFIGS == ((BLOCK, 1), (BLOCK * MID, MID), (BLOCK * CLASSES, CLASSES))
    assert seq % PROJ_TILE == 0 and seq % ROW_TILE == 0 and seq % (BLOCK * CLASSES) == 0
    x2d = x.reshape(batch * seq, d_model)
    for l in range(depth):
        pack = lambda *parts: jnp.concatenate([p.reshape(-1).astype(F32) for p in parts]).reshape(1, -1)
        proj_gains = pack(norm_gain[l], gmlp_v_gain[l], jnp.tile(attn_q_gain[l], ATTN_HEADS),
                          jnp.tile(attn_k_gain[l], ATTN_HEADS), jnp.tile(mem_q_gain[l], MEM_HEADS))
        acts, qkv_cls = _project(x2d, batch, seq, proj_gains, w_in[l])
        oc, lc = _strided_attention(qkv_cls)
        b_full = jnp.repeat(gmlp_b[l].T, HEAD_DIM, axis=1)
        x2d = _mix(x2d, acts, gmlp_w_s[l], b_full, oc, lc, mem,
                   pack(mem_norm_gain[l], jnp.tile(mem_k_gain[l], MEM_HEADS)), w_mem_kv[l], w_out[l], seq)
    return x2d.reshape(batch, seq, d_model)
```

```python
import functools
import math

import jax
import jax.numpy as jnp
from jax import lax
from jax.experimental import pallas as pl
from jax.experimental.pallas import tpu as pltpu

HEAD_DIM = 64
GMLP_HEADS = 4
ATTN_HEADS = 8
MEM_HEADS = 4
GMLP_WIDTH = GMLP_HEADS * HEAD_DIM
ATTN_WIDTH = ATTN_HEADS * HEAD_DIM
MEM_WIDTH = MEM_HEADS * HEAD_DIM
CHUNK = 128
BLOCK = 128
DILATED_CONFIGS = ((128, 1), (512, 4), (2048, 16))
EPS = 1e-6
MASKED = -1e30
LOG2E = math.log2(math.e)
LOGIT_SCALE = LOG2E / math.sqrt(HEAD_DIM)

IN_SEGMENTS = (("g_u", GMLP_WIDTH), ("g_v", GMLP_WIDTH), ("g_gate", GMLP_WIDTH),
               ("a_q", ATTN_WIDTH), ("a_k", ATTN_WIDTH), ("a_v", ATTN_WIDTH), ("a_gate", ATTN_WIDTH),
               ("m_q", MEM_WIDTH), ("m_gate", MEM_WIDTH))


def _column_slices(segments):
    slices, start = {}, 0
    for name, width in segments:
        slices[name] = slice(start, start + width)
        start += width
    return slices, start


IN_COLS, _ = _column_slices(IN_SEGMENTS)
SEQ_COLS, SEQ_WIDTH = _column_slices((("k", ATTN_WIDTH), ("v", ATTN_WIDTH), ("q", ATTN_WIDTH),
                                      ("a_gate", ATTN_WIDTH), ("g_ug", GMLP_WIDTH), ("g_vn", GMLP_WIDTH),
                                      ("m_q", MEM_WIDTH), ("m_gate", MEM_WIDTH)))
CLS_COLS, CLS_WIDTH = _column_slices((("q", ATTN_WIDTH), ("k", ATTN_WIDTH), ("v", ATTN_WIDTH)))

CLASSES = 16
MID = 4
SLABS_PER_MID = CLASSES // MID
LANES = 128
PAIRS = ATTN_WIDTH // LANES
PROJ_TILE = 1024
ROW_TILE = 512
LOOP_UNROLL = 8
CLASSES_PER_STEP = 1
VMEM_LIMIT = 56 * 1024 * 1024

F32 = jnp.float32
BF16 = jnp.bfloat16


def _head_rms_lanes(acc, gain):
    left = lax.broadcasted_iota(jnp.int32, (acc.shape[0], LANES), 1) < HEAD_DIM
    groups = []
    for j in range(acc.shape[1] // LANES):
        a = acc[:, j * LANES:(j + 1) * LANES]
        sq = a * a
        ss_a = jnp.sum(jnp.where(left, sq, 0.0), axis=1, keepdims=True)
        ss_b = jnp.sum(jnp.where(left, 0.0, sq), axis=1, keepdims=True)
        inv = jnp.where(left, lax.rsqrt(ss_a * (1.0 / HEAD_DIM) + EPS),
                        lax.rsqrt(ss_b * (1.0 / HEAD_DIM) + EPS))
        groups.append(a * inv)
    return jnp.concatenate(groups, axis=1) * gain


def _silu(x):
    half = 0.5 * x
    return half + half * jnp.tanh(half)


def _store_both_layouts(val, name, seq_ref, cls_ref, work_ref):
    seq_ref[:, SEQ_COLS[name]] = val.astype(BF16)
    slab_ref, mid_ref = work_ref.at[0], work_ref.at[1]
    rows = val.shape[0]
    per_mid, per_class = rows // MID, rows // CLASSES
    for s in range(val.shape[1] // LANES):
        slab_ref[s] = val[:, s * LANES:(s + 1) * LANES]
        out_lanes = slice(CLS_COLS[name].start + s * LANES, CLS_COLS[name].start + (s + 1) * LANES)
        for r4 in range(MID):
            mid_ref[s, r4 * per_mid:(r4 + 1) * per_mid] = slab_ref[s, pl.ds(r4, per_mid, stride=MID), :]
        for r4 in range(MID):
            for g in range(SLABS_PER_MID):
                cls_ref[r4 + MID * g, :, out_lanes] = (
                    mid_ref[s, pl.ds(r4 * per_mid + g, per_class, stride=SLABS_PER_MID), :].astype(BF16))


def _proj_kernel(x_ref, gains_ref, w32_ref, seq_ref, cls_ref, w_ref, h_ref, work_ref):
    d_model = x_ref.shape[1]

    @pl.when(pl.program_id(0) == 0)
    def _():
        w_ref[...] = w32_ref[...].astype(BF16)

    gains, _ = _column_slices((("x", d_model), ("g_v", GMLP_WIDTH), ("a_q", ATTN_WIDTH),
                               ("a_k", ATTN_WIDTH), ("m_q", MEM_WIDTH)))
    gain = lambda name: gains_ref[:, gains[name]]

    x = x_ref[...]
    ms = jnp.mean(x * x, axis=-1, keepdims=True)
    h_ref[...] = (x * lax.rsqrt(ms + EPS) * gain("x")).astype(BF16)

    def project(name):
        return jnp.dot(h_ref[...], w_ref[:, IN_COLS[name]], preferred_element_type=F32)

    def put(name, val):
        seq_ref[:, SEQ_COLS[name]] = val.astype(BF16)

    g_u = project("g_u")
    g_v = project("g_v")
    g_gate = project("g_gate")
    put("g_ug", g_u * _silu(g_gate))
    a_q = project("a_q")
    put("g_vn", _head_rms_lanes(g_v, gain("g_v")))
    a_k = project("a_k")
    _store_both_layouts(_head_rms_lanes(a_q, gain("a_q")) * LOGIT_SCALE, "q", seq_ref, cls_ref, work_ref)
    a_v = project("a_v")
    _store_both_layouts(_head_rms_lanes(a_k, gain("a_k")), "k", seq_ref, cls_ref, work_ref)
    m_q = project("m_q")
    _store_both_layouts(a_v, "v", seq_ref, cls_ref, work_ref)
    a_gate = project("a_gate")
    put("m_q", _head_rms_lanes(m_q, gain("m_q")) * LOGIT_SCALE)
    m_gate = project("m_gate")
    put("a_gate", _silu(a_gate))
    put("m_gate", _silu(m_gate))


def _pair_attention(q2, k2, v2, bias):
    rows = q2.shape[0]
    left = lax.broadcasted_iota(jnp.int32, (rows, LANES), 1) < HEAD_DIM
    zeros = jnp.zeros_like(q2)
    q_stack = jnp.concatenate([jnp.where(left, q2, zeros), jnp.where(left, zeros, q2)], axis=0)
    v_aug = jnp.concatenate([v2, jnp.ones_like(v2)], axis=1)
    s = lax.dot_general(q_stack, k2, (((1,), (1,)), ((), ())), preferred_element_type=F32)
    if bias is not None:
        s = s + bias
    m = jnp.max(s, axis=1, keepdims=True)
    p = jnp.exp2(s - m).astype(BF16)
    pv = jnp.dot(p, v_aug, preferred_element_type=F32)
    pick = lambda t: jnp.where(left, t[:rows], t[rows:])
    return pick(pv[:, :LANES]), pick(pv[:, LANES:]), pick(jnp.broadcast_to(m, (2 * rows, LANES)))


def _merge_states(a, b):
    top = jnp.maximum(a[2], b[2])
    w_a, w_b = jnp.exp2(a[2] - top), jnp.exp2(b[2] - top)
    return w_a * a[0] + w_b * b[0], w_a * a[1] + w_b * b[1], top


def _normalised(state):
    acc, den, top = state
    return acc / den, top + jnp.log(den) * LOG2E


def _band_bias(rel):
    return jnp.where((rel >= 0) & (rel <= BLOCK), 0.0, MASKED).astype(F32)


def _fill_bias_tables(bias_ref, slabs):
    row = lax.broadcasted_iota(jnp.int32, (2 * BLOCK, 2 * BLOCK), 0) % BLOCK
    col = lax.broadcasted_iota(jnp.int32, (2 * BLOCK, 2 * BLOCK), 1)
    bias_ref[0] = _band_bias(row - col)
    bias_ref[1] = _band_bias(row - col + BLOCK)
    if bias_ref.shape[0] > 2:
        q_rows, k_rows = BLOCK // slabs, 2 * BLOCK // slabs
        rel = (slabs * (row % q_rows) + row // q_rows) - (slabs * (col % k_rows) + col // k_rows)
        bias_ref[2] = _band_bias(rel)
        bias_ref[3] = _band_bias(rel + BLOCK)


def _first_grid_step():
    return (pl.program_id(0) == 0) & (pl.program_id(1) == 0)


def _strided_kernel(qkv_ref, o_ref, lse_ref, state16_ref, bias_ref):
    @pl.when(_first_grid_step())
    def _():
        _fill_bias_tables(bias_ref, SLABS_PER_MID)

    for j in range(qkv_ref.shape[1]):
        _strided_class(qkv_ref.at[:, j], o_ref.at[:, j], lse_ref.at[:, j], state16_ref, bias_ref)


def _strided_class(qkv_ref, o_ref, lse_ref, state16_ref, bias_ref):
    slab_len = qkv_ref.shape[1]

    def lanes_of(name, pair):
        start = CLS_COLS[name].start + pair * LANES
        return slice(start, start + LANES)

    def coarse(i, carry):
        g = i // (slab_len // BLOCK)
        blk = i % (slab_len // BLOCK)
        kstart = pl.multiple_of(jnp.maximum(blk - 1, 0) * BLOCK, BLOCK)
        generic = jnp.minimum(blk, 1)
        qstart = pl.multiple_of(blk * BLOCK, BLOCK)
        for pair in range(PAIRS):
            lanes = slice(pair * LANES, (pair + 1) * LANES)
            state = _pair_attention(qkv_ref[g, pl.ds(qstart, BLOCK), lanes_of("q", pair)],
                                    qkv_ref[g, pl.ds(kstart, 2 * BLOCK), lanes_of("k", pair)],
                                    qkv_ref[g, pl.ds(kstart, 2 * BLOCK), lanes_of("v", pair)],
                                    bias_ref[generic])
            for part in range(3):
                state16_ref[part, g, pl.ds(qstart, BLOCK), lanes] = state[part]
        return carry

    lax.fori_loop(0, SLABS_PER_MID * (slab_len // BLOCK), coarse, 0, unroll=LOOP_UNROLL)

    q_rows = BLOCK // SLABS_PER_MID
    k_rows = 2 * q_rows

    def gather(ref, start, rows, lanes):
        return jnp.concatenate([ref[g, pl.ds(start, rows), lanes] for g in range(SLABS_PER_MID)], axis=0)

    def gather_state16(start, rows, lanes):
        return tuple(gather(state16_ref.at[part], start, rows, lanes) for part in range(3))

    def mid(i, carry):
        a0 = pl.multiple_of(i * q_rows, q_rows)
        kstart = pl.multiple_of(jnp.maximum(i - 1, 0) * q_rows, q_rows)
        generic = jnp.minimum(i, 1)
        for pair in range(PAIRS):
            lanes = slice(pair * LANES, (pair + 1) * LANES)
            state4 = _pair_attention(gather(qkv_ref, a0, q_rows, lanes_of("q", pair)),
                                     gather(qkv_ref, kstart, k_rows, lanes_of("k", pair)),
                                     gather(qkv_ref, kstart, k_rows, lanes_of("v", pair)),
                                     bias_ref[2 + generic])
            o, lse = _normalised(_merge_states(state4, gather_state16(a0, q_rows, lanes)))
            for g in range(SLABS_PER_MID):
                o_ref[g, pl.ds(a0, q_rows), lanes] = o[g * q_rows:(g + 1) * q_rows].astype(BF16)
                lse_ref[g, pl.ds(a0, q_rows), lanes] = lse[g * q_rows:(g + 1) * q_rows]
        return carry

    lax.fori_loop(0, slab_len // q_rows, mid, 0, unroll=LOOP_UNROLL)


def _to_sequence_order(cls_ref, slab_ref, mid_ref):
    per_class = cls_ref.shape[1]
    per_mid = per_class * SLABS_PER_MID
    n_slabs = cls_ref.shape[2] // LANES
    for s in range(n_slabs):
        lanes = slice(s * LANES, (s + 1) * LANES)
        for r4 in range(MID):
            for g in range(SLABS_PER_MID):
                mid_ref[s, pl.ds(r4 * per_mid + g, per_class, stride=SLABS_PER_MID), :] = (
                    cls_ref[r4 + MID * g, :, lanes].astype(F32))
        for r4 in range(MID):
            slab_ref[s, pl.ds(r4, per_mid, stride=MID), :] = mid_ref[s, r4 * per_mid:(r4 + 1) * per_mid]


def _mix_kernel(x_ref, seq_ref, prev_ref, ws_ref, bs_ref, oc_ref, lc_ref,
                mem_ref, mgains_ref, wkv32_ref, wout32_ref, out_ref,
                oslab_ref, lslab_ref, bias_ref, wout_ref, wkv_ref, mk_ref, mv_ref, *, steps_per_batch):
    rows, d_model = x_ref.shape
    step = pl.program_id(0)

    def seq(name, row_slice=slice(None), pair=None):
        cols = SEQ_COLS[name]
        if pair is not None:
            cols = slice(cols.start + pair * LANES, cols.start + (pair + 1) * LANES)
        return seq_ref[row_slice, cols]

    @pl.when(step % steps_per_batch == 0)
    def _():
        @pl.when(step == 0)
        def _():
            wkv_ref[...] = wkv32_ref[...].astype(BF16)
        mem = mem_ref[...]
        ms = jnp.mean(mem * mem, axis=-1, keepdims=True)
        hm = (mem * lax.rsqrt(ms + EPS) * mgains_ref[:, :d_model]).astype(BF16)
        mk = jnp.dot(hm, wkv_ref[:, :MEM_WIDTH], preferred_element_type=F32)
        mk_ref[...] = _head_rms_lanes(mk, mgains_ref[:, d_model:]).astype(BF16)
        mv_ref[...] = jnp.dot(hm, wkv_ref[:, MEM_WIDTH:], preferred_element_type=F32).astype(BF16)

    @pl.when(step == 0)
    def _():
        wout_ref[...] = wout32_ref[...].astype(BF16)
        row = lax.broadcasted_iota(jnp.int32, (2 * BLOCK, 2 * BLOCK), 0) % BLOCK
        col = lax.broadcasted_iota(jnp.int32, (2 * BLOCK, 2 * BLOCK), 1)
        band = _band_bias(row - col + BLOCK)
        bias_ref[1] = band
        bias_ref[0] = jnp.where(col >= BLOCK, band, MASKED)
    tri = (lax.broadcasted_iota(jnp.int32, (CHUNK, CHUNK), 0)
           >= lax.broadcasted_iota(jnp.int32, (CHUNK, CHUNK), 1))
    left = lax.broadcasted_iota(jnp.int32, (CHUNK, LANES), 1) < HEAD_DIM
    w_tri = [jnp.where(tri, ws_ref[h], 0.0).astype(BF16) for h in range(GMLP_HEADS)]
    sp_chunks = []
    for c in range(rows // CHUNK):
        pairs = []
        for pair in range(GMLP_WIDTH // LANES):
            vn2 = seq("g_vn", slice(c * CHUNK, (c + 1) * CHUNK), pair)
            sp_a = jnp.dot(w_tri[2 * pair], vn2, preferred_element_type=F32)
            sp_b = jnp.dot(w_tri[2 * pair + 1], vn2, preferred_element_type=F32)
            pairs.append(jnp.where(left, sp_a, sp_b))
        sp_chunks.append(jnp.concatenate(pairs, axis=1) + bs_ref[...])
    y_g = seq("g_ug").astype(F32) * jnp.concatenate(sp_chunks, axis=0)

    mem_pairs = []
    for pair in range(MEM_WIDTH // LANES):
        lanes = slice(pair * LANES, (pair + 1) * LANES)
        acc, den, _ = _pair_attention(seq("m_q", pair=pair), mk_ref[:, lanes], mv_ref[:, lanes], None)
        mem_pairs.append(acc / den)
    y_m = jnp.concatenate(mem_pairs, axis=1) * seq("m_gate").astype(F32)

    _to_sequence_order(oc_ref, oslab_ref.at[0], oslab_ref.at[1])
    _to_sequence_order(lc_ref, lslab_ref.at[0], lslab_ref.at[1])
    first_bias = jnp.where(step % steps_per_batch == 0, 0, 1)
    block_outs = []
    n_blocks = rows // BLOCK
    for blk in range(n_blocks):
        if blk == n_blocks // 2:
            partial = (x_ref[...]
                       + jnp.dot(y_g.astype(BF16), wout_ref[:GMLP_WIDTH, :], preferred_element_type=F32)
                       + jnp.dot(y_m.astype(BF16), wout_ref[GMLP_WIDTH + ATTN_WIDTH:, :],
                                 preferred_element_type=F32))
        own = slice(blk * BLOCK, (blk + 1) * BLOCK)
        pair_outs = []
        for pair in range(PAIRS):
            lanes = slice(pair * LANES, (pair + 1) * LANES)
            if blk == 0:
                k_lanes = slice(SEQ_COLS["k"].start + pair * LANES, SEQ_COLS["k"].start + (pair + 1) * LANES)
                v_lanes = slice(SEQ_COLS["v"].start + pair * LANES, SEQ_COLS["v"].start + (pair + 1) * LANES)
                k2 = jnp.concatenate([prev_ref[:, k_lanes], seq("k", own, pair)], axis=0)
                v2 = jnp.concatenate([prev_ref[:, v_lanes], seq("v", own, pair)], axis=0)
                bias = bias_ref[first_bias]
            else:
                window = slice((blk - 1) * BLOCK, (blk + 1) * BLOCK)
                k2, v2, bias = seq("k", window, pair), seq("v", window, pair), bias_ref[1]
            acc1, den1, top1 = _pair_attention(seq("q", own, pair), k2, v2, bias)
            o_c, top_c = oslab_ref[0, pair, own, :], lslab_ref[0, pair, own, :]
            top = jnp.maximum(top1, top_c)
            w_1, w_c = jnp.exp2(top1 - top), jnp.exp2(top_c - top)
            pair_outs.append((w_1 * acc1 + w_c * o_c) / (w_1 * den1 + w_c))
        block_outs.append(jnp.concatenate(pair_outs, axis=1))
    y_a = jnp.concatenate(block_outs, axis=0) * seq("a_gate").astype(F32)
    out_ref[...] = partial + jnp.dot(y_a.astype(BF16), wout_ref[GMLP_WIDTH:GMLP_WIDTH + ATTN_WIDTH, :],
                                     preferred_element_type=F32)


def _cparams(sem):
    return pltpu.CompilerParams(dimension_semantics=sem, vmem_limit_bytes=VMEM_LIMIT)


def _const_spec(shape):
    return pl.BlockSpec(shape, lambda *idx: (0,) * len(shape))


def _regroup_scratch(tile):
    return pltpu.VMEM((2, ATTN_WIDTH // LANES, tile, LANES), F32)


def _class_tile_spec(tile, steps_per_batch, width):
    return pl.BlockSpec((None, CLASSES, tile // CLASSES, width),
                        lambda i: (i // steps_per_batch, 0, i % steps_per_batch, 0))


def _project(x2d, batch, seq, gains, w_in):
    rows, d_model = x2d.shape
    return pl.pallas_call(
        _proj_kernel,
        grid=(rows // PROJ_TILE,),
        in_specs=[pl.BlockSpec((PROJ_TILE, d_model), lambda i: (i, 0)),
                  _const_spec(gains.shape),
                  pl.BlockSpec(w_in.shape, lambda i: (0, 0), pipeline_mode=pl.Buffered(1))],
        out_specs=[pl.BlockSpec((PROJ_TILE, SEQ_WIDTH), lambda i: (i, 0)),
                   _class_tile_spec(PROJ_TILE, seq // PROJ_TILE, CLS_WIDTH)],
        out_shape=[jax.ShapeDtypeStruct((rows, SEQ_WIDTH), BF16),
                   jax.ShapeDtypeStruct((batch, CLASSES, seq // CLASSES, CLS_WIDTH), BF16)],
        scratch_shapes=[pltpu.VMEM(w_in.shape, BF16), pltpu.VMEM((PROJ_TILE, d_model), BF16),
                        _regroup_scratch(PROJ_TILE)],
        compiler_params=_cparams(("arbitrary",)),
    )(x2d, gains, w_in)


def _strided_attention(qkv_cls):
    batch, _, slab_len, _ = qkv_cls.shape
    spec = lambda width: pl.BlockSpec((None, SLABS_PER_MID, CLASSES_PER_STEP, slab_len, width),
                                      lambda b, r4: (b, 0, r4, 0, 0))
    split = lambda width: (batch, SLABS_PER_MID, MID, slab_len, width)
    o, lse = pl.pallas_call(
        _strided_kernel,
        grid=(batch, MID // CLASSES_PER_STEP),
        in_specs=[spec(CLS_WIDTH)],
        out_specs=[spec(ATTN_WIDTH)] * 2,
        out_shape=[jax.ShapeDtypeStruct(split(ATTN_WIDTH), BF16), jax.ShapeDtypeStruct(split(ATTN_WIDTH), F32)],
        scratch_shapes=[pltpu.VMEM((3, SLABS_PER_MID, slab_len, ATTN_WIDTH), F32),
                        pltpu.VMEM((4, 2 * BLOCK, 2 * BLOCK), F32)],
        compiler_params=_cparams(("arbitrary", "arbitrary")),
    )(qkv_cls.reshape(split(CLS_WIDTH)))
    merged = (batch, CLASSES, slab_len, ATTN_WIDTH)
    return o.reshape(merged), lse.reshape(merged)


def _mix(x2d, acts, w_s, b_full, oc, lc, mem, mem_gains, w_kv, w_out, seq):
    rows, d_model = x2d.shape
    mem_len = mem.shape[1]
    steps_per_batch = seq // ROW_TILE
    row_spec = lambda w: pl.BlockSpec((ROW_TILE, w), lambda i: (i, 0))
    prev_spec = pl.BlockSpec((BLOCK, SEQ_COLS["v"].stop),
                             lambda i: (jnp.maximum(i * (ROW_TILE // BLOCK) - 1, 0), 0))
    mem_spec = pl.BlockSpec((None,) + mem.shape[1:], lambda i: (i // steps_per_batch, 0, 0))
    weight_spec = lambda w: pl.BlockSpec(w.shape, lambda i: (0, 0), pipeline_mode=pl.Buffered(1))
    cls_spec = _class_tile_spec(ROW_TILE, steps_per_batch, ATTN_WIDTH)
    return pl.pallas_call(
        functools.partial(_mix_kernel, steps_per_batch=steps_per_batch),
        grid=(rows // ROW_TILE,),
        in_specs=[row_spec(d_model), row_spec(SEQ_WIDTH), prev_spec,
                  _const_spec(w_s.shape), _const_spec(b_full.shape), cls_spec, cls_spec,
                  mem_spec, _const_spec(mem_gains.shape), weight_spec(w_kv), weight_spec(w_out)],
        out_specs=row_spec(d_model),
        out_shape=jax.ShapeDtypeStruct((rows, d_model), F32),
        scratch_shapes=[_regroup_scratch(ROW_TILE)] * 2
                       + [pltpu.VMEM((2, 2 * BLOCK, 2 * BLOCK), F32), pltpu.VMEM(w_out.shape, BF16),
                          pltpu.VMEM(w_kv.shape, BF16)] + [pltpu.VMEM((mem_len, MEM_WIDTH), BF16)] * 2,
        compiler_params=_cparams(("arbitrary",)),
    )(x2d, acts, acts, w_s, b_full, oc, lc, mem, mem_gains, w_kv, w_out)


def kernel(x, mem, norm_gain, w_in, gmlp_v_gain, gmlp_w_s, gmlp_b, attn_q_gain, attn_k_gain,
           mem_norm_gain, w_mem_kv, mem_q_gain, mem_k_gain, w_out):
    batch, seq, d_model = x.shape
    depth = w_in.shape[0]
    assert DILATED_CONFIGS == ((BLOCK, 1), (BLOCK * MID, MID), (BLOCK * CLASSES, CLASSES))
    assert seq % PROJ_TILE == 0 and seq % ROW_TILE == 0 and seq % (BLOCK * CLASSES) == 0
    x2d = x.reshape(batch * seq, d_model)
    for l in range(depth):
        pack = lambda *parts: jnp.concatenate([p.reshape(-1).astype(F32) for p in parts]).reshape(1, -1)
        proj_gains = pack(norm_gain[l], gmlp_v_gain[l], jnp.tile(attn_q_gain[l], ATTN_HEADS),
                          jnp.tile(attn_k_gain[l], ATTN_HEADS), jnp.tile(mem_q_gain[l], MEM_HEADS))
        acts, qkv_cls = _project(x2d, batch, seq, proj_gains, w_in[l])
        oc, lc = _strided_attention(qkv_cls)
        b_full = jnp.repeat(gmlp_b[l].T, HEAD_DIM, axis=1)
        x2d = _mix(x2d, acts, gmlp_w_s[l], b_full, oc, lc, mem,
                   pack(mem_norm_gain[l], jnp.tile(mem_k_gain[l], MEM_HEADS)), w_mem_kv[l], w_out[l], seq)
    return x2d.reshape(batch, seq, d_model)
```

```python
import functools
import math

import jax
import jax.numpy as jnp
from jax import lax
from jax.experimental import pallas as pl
from jax.experimental.pallas import tpu as pltpu

HEAD_DIM = 64
GMLP_HEADS = 4
ATTN_HEADS = 8
MEM_HEADS = 4
GMLP_WIDTH = GMLP_HEADS * HEAD_DIM
ATTN_WIDTH = ATTN_HEADS * HEAD_DIM
MEM_WIDTH = MEM_HEADS * HEAD_DIM
CHUNK = 128
BLOCK = 128
DILATED_CONFIGS = ((128, 1), (512, 4), (2048, 16))
EPS = 1e-6
MASKED = -1e30
LOG2E = math.log2(math.e)
LOGIT_SCALE = LOG2E / math.sqrt(HEAD_DIM)

IN_SEGMENTS = (("g_u", GMLP_WIDTH), ("g_v", GMLP_WIDTH), ("g_gate", GMLP_WIDTH),
               ("a_q", ATTN_WIDTH), ("a_k", ATTN_WIDTH), ("a_v", ATTN_WIDTH), ("a_gate", ATTN_WIDTH),
               ("m_q", MEM_WIDTH), ("m_gate", MEM_WIDTH))


def _column_slices(segments):
    slices, start = {}, 0
    for name, width in segments:
        slices[name] = slice(start, start + width)
        start += width
    return slices, start


IN_COLS, _ = _column_slices(IN_SEGMENTS)
SEQ_COLS, SEQ_WIDTH = _column_slices((("k", ATTN_WIDTH), ("v", ATTN_WIDTH), ("q", ATTN_WIDTH),
                                      ("a_gate", ATTN_WIDTH), ("g_ug", GMLP_WIDTH), ("g_vn", GMLP_WIDTH),
                                      ("m_q", MEM_WIDTH), ("m_gate", MEM_WIDTH)))
CLS_COLS, CLS_WIDTH = _column_slices((("q", ATTN_WIDTH), ("k", ATTN_WIDTH), ("v", ATTN_WIDTH)))

CLASSES = 16
MID = 4
SLABS_PER_MID = CLASSES // MID
LANES = 128
PAIRS = ATTN_WIDTH // LANES
PROJ_TILE = 1024
ROW_TILE = 512
LOOP_UNROLL = 8
CLASSES_PER_STEP = 1
VMEM_LIMIT = 56 * 1024 * 1024

F32 = jnp.float32
BF16 = jnp.bfloat16


def _head_rms_lanes(acc, gain):
    left = lax.broadcasted_iota(jnp.int32, (acc.shape[0], LANES), 1) < HEAD_DIM
    groups = []
    for j in range(acc.shape[1] // LANES):
        a = acc[:, j * LANES:(j + 1) * LANES]
        sq = a * a
        ss_a = jnp.sum(jnp.where(left, sq, 0.0), axis=1, keepdims=True)
        ss_b = jnp.sum(jnp.where(left, 0.0, sq), axis=1, keepdims=True)
        groups.append(a * lax.rsqrt(jnp.where(left, ss_a, ss_b) * (1.0 / HEAD_DIM) + EPS))
    return jnp.concatenate(groups, axis=1) * gain


def _silu(x):
    half = 0.5 * x
    return half + half * jnp.tanh(half)


def _store_both_layouts(val, name, seq_ref, cls_ref, work_ref):
    seq_ref[:, SEQ_COLS[name]] = val.astype(BF16)
    slab_ref, mid_ref = work_ref.at[0], work_ref.at[1]
    rows = val.shape[0]
    per_mid, per_class = rows // MID, rows // CLASSES
    for s in range(val.shape[1] // LANES):
        slab_ref[s] = val[:, s * LANES:(s + 1) * LANES]
        out_lanes = slice(CLS_COLS[name].start + s * LANES, CLS_COLS[name].start + (s + 1) * LANES)
        for r4 in range(MID):
            mid_ref[s, r4 * per_mid:(r4 + 1) * per_mid] = slab_ref[s, pl.ds(r4, per_mid, stride=MID), :]
        for r4 in range(MID):
            for g in range(SLABS_PER_MID):
                cls_ref[r4 + MID * g, :, out_lanes] = (
                    mid_ref[s, pl.ds(r4 * per_mid + g, per_class, stride=SLABS_PER_MID), :].astype(BF16))


def _proj_kernel(x_ref, gains_ref, w32_ref, seq_ref, cls_ref, w_ref, h_ref, work_ref):
    d_model = x_ref.shape[1]

    @pl.when(pl.program_id(0) == 0)
    def _():
        w_ref[...] = w32_ref[...].astype(BF16)

    gains, _ = _column_slices((("x", d_model), ("g_v", GMLP_WIDTH), ("a_q", ATTN_WIDTH),
                               ("a_k", ATTN_WIDTH), ("m_q", MEM_WIDTH)))
    gain = lambda name: gains_ref[:, gains[name]]

    x = x_ref[...]
    ms = jnp.mean(x * x, axis=-1, keepdims=True)
    h_ref[...] = (x * lax.rsqrt(ms + EPS) * gain("x")).astype(BF16)

    def project(name):
        return jnp.dot(h_ref[...], w_ref[:, IN_COLS[name]], preferred_element_type=F32)

    def put(name, val):
        seq_ref[:, SEQ_COLS[name]] = val.astype(BF16)

    g_u = project("g_u")
    g_v = project("g_v")
    g_gate = project("g_gate")
    put("g_ug", g_u * _silu(g_gate))
    a_q = project("a_q")
    put("g_vn", _head_rms_lanes(g_v, gain("g_v")))
    a_k = project("a_k")
    _store_both_layouts(_head_rms_lanes(a_q, gain("a_q")) * LOGIT_SCALE, "q", seq_ref, cls_ref, work_ref)
    a_v = project("a_v")
    _store_both_layouts(_head_rms_lanes(a_k, gain("a_k")), "k", seq_ref, cls_ref, work_ref)
    m_q = project("m_q")
    _store_both_layouts(a_v, "v", seq_ref, cls_ref, work_ref)
    a_gate = project("a_gate")
    put("m_q", _head_rms_lanes(m_q, gain("m_q")) * LOGIT_SCALE)
    m_gate = project("m_gate")
    put("a_gate", _silu(a_gate))
    put("m_gate", _silu(m_gate))


def _pair_attention(q2, k2, v2, bias):
    rows = q2.shape[0]
    left = lax.broadcasted_iota(jnp.int32, (rows, LANES), 1) < HEAD_DIM
    zeros = jnp.zeros_like(q2)
    q_stack = jnp.concatenate([jnp.where(left, q2, zeros), jnp.where(left, zeros, q2)], axis=0)
    v_aug = jnp.concatenate([v2, jnp.ones_like(v2)], axis=1)
    s = lax.dot_general(q_stack, k2, (((1,), (1,)), ((), ())), preferred_element_type=F32)
    if bias is not None:
        s = s + bias
    m = jnp.max(s, axis=1, keepdims=True)
    p = jnp.exp2(s - m).astype(BF16)
    pv = jnp.dot(p, v_aug, preferred_element_type=F32)
    pick = lambda t: jnp.where(left, t[:rows], t[rows:])
    return pick(pv[:, :LANES]), pick(pv[:, LANES:]), pick(jnp.broadcast_to(m, (2 * rows, LANES)))


def _merge_states(a, b):
    top = jnp.maximum(a[2], b[2])
    w_a, w_b = jnp.exp2(a[2] - top), jnp.exp2(b[2] - top)
    return w_a * a[0] + w_b * b[0], w_a * a[1] + w_b * b[1], top


def _normalised(state):
    acc, den, top = state
    return acc / den, top + jnp.log(den) * LOG2E


def _band_bias(rel):
    return jnp.where((rel >= 0) & (rel <= BLOCK), 0.0, MASKED).astype(F32)


def _fill_bias_tables(bias_ref, slabs):
    row = lax.broadcasted_iota(jnp.int32, (2 * BLOCK, 2 * BLOCK), 0) % BLOCK
    col = lax.broadcasted_iota(jnp.int32, (2 * BLOCK, 2 * BLOCK), 1)
    bias_ref[0] = _band_bias(row - col)
    bias_ref[1] = _band_bias(row - col + BLOCK)
    if bias_ref.shape[0] > 2:
        q_rows, k_rows = BLOCK // slabs, 2 * BLOCK // slabs
        rel = (slabs * (row % q_rows) + row // q_rows) - (slabs * (col % k_rows) + col // k_rows)
        bias_ref[2] = _band_bias(rel)
        bias_ref[3] = _band_bias(rel + BLOCK)


def _first_grid_step():
    return (pl.program_id(0) == 0) & (pl.program_id(1) == 0)


def _strided_kernel(qkv_ref, o_ref, lse_ref, state16_ref, bias_ref):
    @pl.when(_first_grid_step())
    def _():
        _fill_bias_tables(bias_ref, SLABS_PER_MID)

    for j in range(qkv_ref.shape[1]):
        _strided_class(qkv_ref.at[:, j], o_ref.at[:, j], lse_ref.at[:, j], state16_ref, bias_ref)


def _strided_class(qkv_ref, o_ref, lse_ref, state16_ref, bias_ref):
    slab_len = qkv_ref.shape[1]

    def lanes_of(name, pair):
        start = CLS_COLS[name].start + pair * LANES
        return slice(start, start + LANES)

    def coarse(i, carry):
        g = i // (slab_len // BLOCK)
        blk = i % (slab_len // BLOCK)
        kstart = pl.multiple_of(jnp.maximum(blk - 1, 0) * BLOCK, BLOCK)
        generic = jnp.minimum(blk, 1)
        qstart = pl.multiple_of(blk * BLOCK, BLOCK)
        for pair in range(PAIRS):
            lanes = slice(pair * LANES, (pair + 1) * LANES)
            state = _pair_attention(qkv_ref[g, pl.ds(qstart, BLOCK), lanes_of("q", pair)],
                                    qkv_ref[g, pl.ds(kstart, 2 * BLOCK), lanes_of("k", pair)],
                                    qkv_ref[g, pl.ds(kstart, 2 * BLOCK), lanes_of("v", pair)],
                                    bias_ref[generic])
            for part in range(3):
                state16_ref[part, g, pl.ds(qstart, BLOCK), lanes] = state[part]
        return carry

    lax.fori_loop(0, SLABS_PER_MID * (slab_len // BLOCK), coarse, 0, unroll=LOOP_UNROLL)

    q_rows = BLOCK // SLABS_PER_MID
    k_rows = 2 * q_rows

    def gather(ref, start, rows, lanes):
        return jnp.concatenate([ref[g, pl.ds(start, rows), lanes] for g in range(SLABS_PER_MID)], axis=0)

    def gather_state16(start, rows, lanes):
        return tuple(gather(state16_ref.at[part], start, rows, lanes) for part in range(3))

    def mid(i, carry):
        a0 = pl.multiple_of(i * q_rows, q_rows)
        kstart = pl.multiple_of(jnp.maximum(i - 1, 0) * q_rows, q_rows)
        generic = jnp.minimum(i, 1)
        for pair in range(PAIRS):
            lanes = slice(pair * LANES, (pair + 1) * LANES)
            state4 = _pair_attention(gather(qkv_ref, a0, q_rows, lanes_of("q", pair)),
                                     gather(qkv_ref, kstart, k_rows, lanes_of("k", pair)),
                                     gather(qkv_ref, kstart, k_rows, lanes_of("v", pair)),
                                     bias_ref[2 + generic])
            o, lse = _normalised(_merge_states(state4, gather_state16(a0, q_rows, lanes)))
            for g in range(SLABS_PER_MID):
                o_ref[g, pl.ds(a0, q_rows), lanes] = o[g * q_rows:(g + 1) * q_rows].astype(BF16)
                lse_ref[g, pl.ds(a0, q_rows), lanes] = lse[g * q_rows:(g + 1) * q_rows]
        return carry

    lax.fori_loop(0, slab_len // q_rows, mid, 0, unroll=LOOP_UNROLL)


def _to_sequence_order(cls_ref, slab_ref, mid_ref):
    per_class = cls_ref.shape[1]
    per_mid = per_class * SLABS_PER_MID
    n_slabs = cls_ref.shape[2] // LANES
    for s in range(n_slabs):
        lanes = slice(s * LANES, (s + 1) * LANES)
        for r4 in range(MID):
            for g in range(SLABS_PER_MID):
                mid_ref[s, pl.ds(r4 * per_mid + g, per_class, stride=SLABS_PER_MID), :] = (
                    cls_ref[r4 + MID * g, :, lanes].astype(F32))
        for r4 in range(MID):
            slab_ref[s, pl.ds(r4, per_mid, stride=MID), :] = mid_ref[s, r4 * per_mid:(r4 + 1) * per_mid]


def _mix_kernel(x_ref, seq_ref, prev_ref, ws_ref, bs_ref, oc_ref, lc_ref,
                mem_ref, mgains_ref, wkv32_ref, wout32_ref, out_ref,
                oslab_ref, lslab_ref, bias_ref, wout_ref, wkv_ref, mk_ref, mv_ref, *, steps_per_batch):
    rows, d_model = x_ref.shape
    step = pl.program_id(0)

    def seq(name, row_slice=slice(None), pair=None):
        cols = SEQ_COLS[name]
        if pair is not None:
            cols = slice(cols.start + pair * LANES, cols.start + (pair + 1) * LANES)
        return seq_ref[row_slice, cols]

    @pl.when(step % steps_per_batch == 0)
    def _():
        @pl.when(step == 0)
        def _():
            wkv_ref[...] = wkv32_ref[...].astype(BF16)
        mem = mem_ref[...]
        ms = jnp.mean(mem * mem, axis=-1, keepdims=True)
        hm = (mem * lax.rsqrt(ms + EPS) * mgains_ref[:, :d_model]).astype(BF16)
        mk = jnp.dot(hm, wkv_ref[:, :MEM_WIDTH], preferred_element_type=F32)
        mk_ref[...] = _head_rms_lanes(mk, mgains_ref[:, d_model:]).astype(BF16)
        mv_ref[...] = jnp.dot(hm, wkv_ref[:, MEM_WIDTH:], preferred_element_type=F32).astype(BF16)

    @pl.when(step == 0)
    def _():
        wout_ref[...] = wout32_ref[...].astype(BF16)
        row = lax.broadcasted_iota(jnp.int32, (2 * BLOCK, 2 * BLOCK), 0) % BLOCK
        col = lax.broadcasted_iota(jnp.int32, (2 * BLOCK, 2 * BLOCK), 1)
        band = _band_bias(row - col + BLOCK)
        bias_ref[1] = band
        bias_ref[0] = jnp.where(col >= BLOCK, band, MASKED)
    tri = (lax.broadcasted_iota(jnp.int32, (CHUNK, CHUNK), 0)
           >= lax.broadcasted_iota(jnp.int32, (CHUNK, CHUNK), 1))
    left = lax.broadcasted_iota(jnp.int32, (CHUNK, LANES), 1) < HEAD_DIM
    w_tri = [jnp.where(tri, ws_ref[h], 0.0).astype(BF16) for h in range(GMLP_HEADS)]
    sp_chunks = []
    for c in range(rows // CHUNK):
        pairs = []
        for pair in range(GMLP_WIDTH // LANES):
            vn2 = seq("g_vn", slice(c * CHUNK, (c + 1) * CHUNK), pair)
            sp_a = jnp.dot(w_tri[2 * pair], vn2, preferred_element_type=F32)
            sp_b = jnp.dot(w_tri[2 * pair + 1], vn2, preferred_element_type=F32)
            pairs.append(jnp.where(left, sp_a, sp_b))
        sp_chunks.append(jnp.concatenate(pairs, axis=1) + bs_ref[...])
    y_g = seq("g_ug").astype(F32) * jnp.concatenate(sp_chunks, axis=0)

    mem_pairs = []
    for pair in range(MEM_WIDTH // LANES):
        lanes = slice(pair * LANES, (pair + 1) * LANES)
        acc, den, _ = _pair_attention(seq("m_q", pair=pair), mk_ref[:, lanes], mv_ref[:, lanes], None)
        mem_pairs.append(acc / den)
    y_m = jnp.concatenate(mem_pairs, axis=1) * seq("m_gate").astype(F32)

    _to_sequence_order(oc_ref, oslab_ref.at[0], oslab_ref.at[1])
    _to_sequence_order(lc_ref, lslab_ref.at[0], lslab_ref.at[1])
    first_bias = jnp.where(step % steps_per_batch == 0, 0, 1)
    block_outs = []
    n_blocks = rows // BLOCK
    for blk in range(n_blocks):
        if blk == n_blocks // 2:
            partial = (x_ref[...]
                       + jnp.dot(y_g.astype(BF16), wout_ref[:GMLP_WIDTH, :], preferred_element_type=F32)
                       + jnp.dot(y_m.astype(BF16), wout_ref[GMLP_WIDTH + ATTN_WIDTH:, :],
                                 preferred_element_type=F32))
        own = slice(blk * BLOCK, (blk + 1) * BLOCK)
        pair_outs = []
        for pair in range(PAIRS):
            lanes = slice(pair * LANES, (pair + 1) * LANES)
            if blk == 0:
                k_lanes = slice(SEQ_COLS["k"].start + pair * LANES, SEQ_COLS["k"].start + (pair + 1) * LANES)
                v_lanes = slice(SEQ_COLS["v"].start + pair * LANES, SEQ_COLS["v"].start + (pair + 1) * LANES)
                k2 = jnp.concatenate([prev_ref[:, k_lanes], seq("k", own, pair)], axis=0)
                v2 = jnp.concatenate([prev_ref[:, v_lanes], seq("v", own, pair)], axis=0)
                bias = bias_ref[first_bias]
            else:
                window = slice((blk - 1) * BLOCK, (blk + 1) * BLOCK)
                k2, v2, bias = seq("k", window, pair), seq("v", window, pair), bias_ref[1]
            acc1, den1, top1 = _pair_attention(seq("q", own, pair), k2, v2, bias)
            o_c, top_c = oslab_ref[0, pair, own, :], lslab_ref[0, pair, own, :]
            top = jnp.maximum(top1, top_c)
            w_1, w_c = jnp.exp2(top1 - top), jnp.exp2(top_c - top)
            pair_outs.append((w_1 * acc1 + w_c * o_c) / (w_1 * den1 + w_c))
        block_outs.append(jnp.concatenate(pair_outs, axis=1))
    y_a = jnp.concatenate(block_outs, axis=0) * seq("a_gate").astype(F32)
    out_ref[...] = partial + jnp.dot(y_a.astype(BF16), wout_ref[GMLP_WIDTH:GMLP_WIDTH + ATTN_WIDTH, :],
                                     preferred_element_type=F32)


def _cparams(sem):
    return pltpu.CompilerParams(dimension_semantics=sem, vmem_limit_bytes=VMEM_LIMIT)


def _const_spec(shape):
    return pl.BlockSpec(shape, lambda *idx: (0,) * len(shape))


def _regroup_scratch(tile):
    return pltpu.VMEM((2, ATTN_WIDTH // LANES, tile, LANES), F32)


def _class_tile_spec(tile, steps_per_batch, width):
    return pl.BlockSpec((None, CLASSES, tile // CLASSES, width),
                        lambda i: (i // steps_per_batch, 0, i % steps_per_batch, 0))


def _project(x2d, batch, seq, gains, w_in):
    rows, d_model = x2d.shape
    return pl.pallas_call(
        _proj_kernel,
        grid=(rows // PROJ_TILE,),
        in_specs=[pl.BlockSpec((PROJ_TILE, d_model), lambda i: (i, 0)),
                  _const_spec(gains.shape),
                  pl.BlockSpec(w_in.shape, lambda i: (0, 0), pipeline_mode=pl.Buffered(1))],
        out_specs=[pl.BlockSpec((PROJ_TILE, SEQ_WIDTH), lambda i: (i, 0)),
                   _class_tile_spec(PROJ_TILE, seq // PROJ_TILE, CLS_WIDTH)],
        out_shape=[jax.ShapeDtypeStruct((rows, SEQ_WIDTH), BF16),
                   jax.ShapeDtypeStruct((batch, CLASSES, seq // CLASSES, CLS_WIDTH), BF16)],
        scratch_shapes=[pltpu.VMEM(w_in.shape, BF16), pltpu.VMEM((PROJ_TILE, d_model), BF16),
                        _regroup_scratch(PROJ_TILE)],
        compiler_params=_cparams(("arbitrary",)),
    )(x2d, gains, w_in)


def _strided_attention(qkv_cls):
    batch, _, slab_len, _ = qkv_cls.shape
    spec = lambda width: pl.BlockSpec((None, SLABS_PER_MID, CLASSES_PER_STEP, slab_len, width),
                                      lambda b, r4: (b, 0, r4, 0, 0))
    split = lambda width: (batch, SLABS_PER_MID, MID, slab_len, width)
    o, lse = pl.pallas_call(
        _strided_kernel,
        grid=(batch, MID // CLASSES_PER_STEP),
        in_specs=[spec(CLS_WIDTH)],
        out_specs=[spec(ATTN_WIDTH)] * 2,
        out_shape=[jax.ShapeDtypeStruct(split(ATTN_WIDTH), BF16), jax.ShapeDtypeStruct(split(ATTN_WIDTH), F32)],
        scratch_shapes=[pltpu.VMEM((3, SLABS_PER_MID, slab_len, ATTN_WIDTH), F32),
                        pltpu.VMEM((4, 2 * BLOCK, 2 * BLOCK), F32)],
        compiler_params=_cparams(("arbitrary", "arbitrary")),
    )(qkv_cls.reshape(split(CLS_WIDTH)))
    merged = (batch, CLASSES, slab_len, ATTN_WIDTH)
    return o.reshape(merged), lse.reshape(merged)


def _mix(x2d, acts, w_s, b_full, oc, lc, mem, mem_gains, w_kv, w_out, seq):
    rows, d_model = x2d.shape
    mem_len = mem.shape[1]
    steps_per_batch = seq // ROW_TILE
    row_spec = lambda w: pl.BlockSpec((ROW_TILE, w), lambda i: (i, 0))
    prev_spec = pl.BlockSpec((BLOCK, SEQ_COLS["v"].stop),
                             lambda i: (jnp.maximum(i * (ROW_TILE // BLOCK) - 1, 0), 0))
    mem_spec = pl.BlockSpec((None,) + mem.shape[1:], lambda i: (i // steps_per_batch, 0, 0))
    weight_spec = lambda w: pl.BlockSpec(w.shape, lambda i: (0, 0), pipeline_mode=pl.Buffered(1))
    cls_spec = _class_tile_spec(ROW_TILE, steps_per_batch, ATTN_WIDTH)
    return pl.pallas_call(
        functools.partial(_mix_kernel, steps_per_batch=steps_per_batch),
        grid=(rows // ROW_TILE,),
        in_specs=[row_spec(d_model), row_spec(SEQ_WIDTH), prev_spec,
                  _const_spec(w_s.shape), _const_spec(b_full.shape), cls_spec, cls_spec,
                  mem_spec, _const_spec(mem_gains.shape), weight_spec(w_kv), weight_spec(w_out)],
        out_specs=row_spec(d_model),
        out_shape=jax.ShapeDtypeStruct((rows, d_model), F32),
        scratch_shapes=[_regroup_scratch(ROW_TILE)] * 2
                       + [pltpu.VMEM((2, 2 * BLOCK, 2 * BLOCK), F32), pltpu.VMEM(w_out.shape, BF16),
                          pltpu.VMEM(w_kv.shape, BF16)] + [pltpu.VMEM((mem_len, MEM_WIDTH), BF16)] * 2,
        compiler_params=_cparams(("arbitrary",)),
    )(x2d, acts, acts, w_s, b_full, oc, lc, mem, mem_gains, w_kv, w_out)


def kernel(x, mem, norm_gain, w_in, gmlp_v_gain, gmlp_w_s, gmlp_b, attn_q_gain, attn_k_gain,
           mem_norm_gain, w_mem_kv, mem_q_gain, mem_k_gain, w_out):
    batch, seq, d_model = x.shape
    depth = w_in.shape[0]
    assert DILATED_CONFIGS == ((BLOCK, 1), (BLOCK * MID, MID), (BLOCK * CLASSES, CLASSES))
    assert seq % PROJ_TILE == 0 and seq % ROW_TILE == 0 and seq % (BLOCK * CLASSES) == 0
    x2d = x.reshape(batch * seq, d_model)
    for l in range(depth):
        pack = lambda *parts: jnp.concatenate([p.reshape(-1).astype(F32) for p in parts]).reshape(1, -1)
        proj_gains = pack(norm_gain[l], gmlp_v_gain[l], jnp.tile(attn_q_gain[l], ATTN_HEADS),
                          jnp.tile(attn_k_gain[l], ATTN_HEADS), jnp.tile(mem_q_gain[l], MEM_HEADS))
        acts, qkv_cls = _project(x2d, batch, seq, proj_gains, w_in[l])
        oc, lc = _strided_attention(qkv_cls)
        b_full = jnp.repeat(gmlp_b[l].T, HEAD_DIM, axis=1)
        x2d = _mix(x2d, acts, gmlp_w_s[l], b_full, oc, lc, mem,
                   pack(mem_norm_gain[l], jnp.tile(mem_k_gain[l], MEM_HEADS)), w_mem_kv[l], w_out[l], seq)
    return x2d.reshape(batch, seq, d_model)
```

```python
import functools
import math

import jax
import jax.numpy as jnp
from jax import lax
from jax.experimental import pallas as pl
from jax.experimental.pallas import tpu as pltpu

HEAD_DIM = 64
GMLP_HEADS = 4
ATTN_HEADS = 8
MEM_HEADS = 4
GMLP_WIDTH = GMLP_HEADS * HEAD_DIM
ATTN_WIDTH = ATTN_HEADS * HEAD_DIM
MEM_WIDTH = MEM_HEADS * HEAD_DIM
CHUNK = 128
BLOCK = 128
DILATED_CONFIGS = ((128, 1), (512, 4), (2048, 16))
EPS = 1e-6
MASKED = -1e30
LOG2E = math.log2(math.e)
LOGIT_SCALE = LOG2E / math.sqrt(HEAD_DIM)

IN_SEGMENTS = (("g_u", GMLP_WIDTH), ("g_v", GMLP_WIDTH), ("g_gate", GMLP_WIDTH),
               ("a_q", ATTN_WIDTH), ("a_k", ATTN_WIDTH), ("a_v", ATTN_WIDTH), ("a_gate", ATTN_WIDTH),
               ("m_q", MEM_WIDTH), ("m_gate", MEM_WIDTH))


def _column_slices(segments):
    slices, start = {}, 0
    for name, width in segments:
        slices[name] = slice(start, start + width)
        start += width
    return slices, start


IN_COLS, _ = _column_slices(IN_SEGMENTS)
SEQ_COLS, SEQ_WIDTH = _column_slices((("k", ATTN_WIDTH), ("v", ATTN_WIDTH), ("q", ATTN_WIDTH),
                                      ("a_gate", ATTN_WIDTH), ("g_ug", GMLP_WIDTH), ("g_vn", GMLP_WIDTH),
                                      ("m_q", MEM_WIDTH), ("m_gate", MEM_WIDTH)))
CLS_COLS, CLS_WIDTH = _column_slices((("q", ATTN_WIDTH), ("k", ATTN_WIDTH), ("v", ATTN_WIDTH)))

CLASSES = 16
MID = 4
SLABS_PER_MID = CLASSES // MID
LANES = 128
PAIRS = ATTN_WIDTH // LANES
PROJ_TILE = 1024
ROW_TILE = 1024
LOOP_UNROLL = 8
CLASSES_PER_STEP = 1
VMEM_LIMIT = 62 * 1024 * 1024

F32 = jnp.float32
BF16 = jnp.bfloat16


def _head_rms_lanes(acc, gain):
    left = lax.broadcasted_iota(jnp.int32, (acc.shape[0], LANES), 1) < HEAD_DIM
    groups = []
    for j in range(acc.shape[1] // LANES):
        a = acc[:, j * LANES:(j + 1) * LANES]
        sq = a * a
        ss_a = jnp.sum(jnp.where(left, sq, 0.0), axis=1, keepdims=True)
        ss_b = jnp.sum(jnp.where(left, 0.0, sq), axis=1, keepdims=True)
        inv = jnp.where(left, lax.rsqrt(ss_a * (1.0 / HEAD_DIM) + EPS),
                        lax.rsqrt(ss_b * (1.0 / HEAD_DIM) + EPS))
        groups.append(a * inv)
    return jnp.concatenate(groups, axis=1) * gain


def _silu(x):
    half = 0.5 * x
    return half + half * jnp.tanh(half)


def _store_both_layouts(val, name, seq_ref, cls_ref, work_ref):
    seq_ref[:, SEQ_COLS[name]] = val.astype(BF16)
    slab_ref, mid_ref = work_ref.at[0], work_ref.at[1]
    rows = val.shape[0]
    per_mid, per_class = rows // MID, rows // CLASSES
    for s in range(val.shape[1] // LANES):
        slab_ref[s] = val[:, s * LANES:(s + 1) * LANES]
        out_lanes = slice(CLS_COLS[name].start + s * LANES, CLS_COLS[name].start + (s + 1) * LANES)
        for r4 in range(MID):
            mid_ref[s, r4 * per_mid:(r4 + 1) * per_mid] = slab_ref[s, pl.ds(r4, per_mid, stride=MID), :]
        for r4 in range(MID):
            for g in range(SLABS_PER_MID):
                cls_ref[r4 + MID * g, :, out_lanes] = (
                    mid_ref[s, pl.ds(r4 * per_mid + g, per_class, stride=SLABS_PER_MID), :].astype(BF16))


def _proj_kernel(x_ref, gains_ref, w32_ref, seq_ref, cls_ref, w_ref, h_ref, work_ref):
    d_model = x_ref.shape[1]

    @pl.when(pl.program_id(0) == 0)
    def _():
        w_ref[...] = w32_ref[...].astype(BF16)

    gains, _ = _column_slices((("x", d_model), ("g_v", GMLP_WIDTH), ("a_q", ATTN_WIDTH),
                               ("a_k", ATTN_WIDTH), ("m_q", MEM_WIDTH)))
    gain = lambda name: gains_ref[:, gains[name]]

    x = x_ref[...]
    ms = jnp.mean(x * x, axis=-1, keepdims=True)
    h_ref[...] = (x * lax.rsqrt(ms + EPS) * gain("x")).astype(BF16)

    def project(name):
        return jnp.dot(h_ref[...], w_ref[:, IN_COLS[name]], preferred_element_type=F32)

    def put(name, val):
        seq_ref[:, SEQ_COLS[name]] = val.astype(BF16)

    g_u = project("g_u")
    g_v = project("g_v")
    g_gate = project("g_gate")
    put("g_ug", g_u * _silu(g_gate))
    a_q = project("a_q")
    put("g_vn", _head_rms_lanes(g_v, gain("g_v")))
    a_k = project("a_k")
    _store_both_layouts(_head_rms_lanes(a_q, gain("a_q")) * LOGIT_SCALE, "q", seq_ref, cls_ref, work_ref)
    a_v = project("a_v")
    _store_both_layouts(_head_rms_lanes(a_k, gain("a_k")), "k", seq_ref, cls_ref, work_ref)
    m_q = project("m_q")
    _store_both_layouts(a_v, "v", seq_ref, cls_ref, work_ref)
    a_gate = project("a_gate")
    put("m_q", _head_rms_lanes(m_q, gain("m_q")) * LOGIT_SCALE)
    m_gate = project("m_gate")
    put("a_gate", _silu(a_gate))
    put("m_gate", _silu(m_gate))


def _pair_attention(q2, k2, v2, bias):
    rows = q2.shape[0]
    left = lax.broadcasted_iota(jnp.int32, (rows, LANES), 1) < HEAD_DIM
    zeros = jnp.zeros_like(q2)
    q_stack = jnp.concatenate([jnp.where(left, q2, zeros), jnp.where(left, zeros, q2)], axis=0)
    v_aug = jnp.concatenate([v2, jnp.ones_like(v2)], axis=1)
    s = lax.dot_general(q_stack, k2, (((1,), (1,)), ((), ())), preferred_element_type=F32)
    if bias is not None:
        s = s + bias
    m = jnp.max(s, axis=1, keepdims=True)
    p = jnp.exp2(s - m).astype(BF16)
    pv = jnp.dot(p, v_aug, preferred_element_type=F32)
    pick = lambda t: jnp.where(left, t[:rows], t[rows:])
    return pick(pv[:, :LANES]), pick(pv[:, LANES:]), pick(jnp.broadcast_to(m, (2 * rows, LANES)))


def _merge_states(a, b):
    top = jnp.maximum(a[2], b[2])
    w_a, w_b = jnp.exp2(a[2] - top), jnp.exp2(b[2] - top)
    return w_a * a[0] + w_b * b[0], w_a * a[1] + w_b * b[1], top


def _normalised(state):
    acc, den, top = state
    return acc / den, top + jnp.log(den) * LOG2E


def _band_bias(rel):
    return jnp.where((rel >= 0) & (rel <= BLOCK), 0.0, MASKED).astype(F32)


def _fill_bias_tables(bias_ref, slabs):
    row = lax.broadcasted_iota(jnp.int32, (2 * BLOCK, 2 * BLOCK), 0) % BLOCK
    col = lax.broadcasted_iota(jnp.int32, (2 * BLOCK, 2 * BLOCK), 1)
    bias_ref[0] = _band_bias(row - col)
    bias_ref[1] = _band_bias(row - col + BLOCK)
    if bias_ref.shape[0] > 2:
        q_rows, k_rows = BLOCK // slabs, 2 * BLOCK // slabs
        rel = (slabs * (row % q_rows) + row // q_rows) - (slabs * (col % k_rows) + col // k_rows)
        bias_ref[2] = _band_bias(rel)
        bias_ref[3] = _band_bias(rel + BLOCK)


def _first_grid_step():
    return (pl.program_id(0) == 0) & (pl.program_id(1) == 0)


def _strided_kernel(qkv_ref, o_ref, lse_ref, state16_ref, bias_ref):
    @pl.when(_first_grid_step())
    def _():
        _fill_bias_tables(bias_ref, SLABS_PER_MID)

    for j in range(qkv_ref.shape[1]):
        _strided_class(qkv_ref.at[:, j], o_ref.at[:, j], lse_ref.at[:, j], state16_ref, bias_ref)


def _strided_class(qkv_ref, o_ref, lse_ref, state16_ref, bias_ref):
    slab_len = qkv_ref.shape[1]

    def lanes_of(name, pair):
        start = CLS_COLS[name].start + pair * LANES
        return slice(start, start + LANES)

    def coarse(i, carry):
        g = i // (slab_len // BLOCK)
        blk = i % (slab_len // BLOCK)
        kstart = pl.multiple_of(jnp.maximum(blk - 1, 0) * BLOCK, BLOCK)
        generic = jnp.minimum(blk, 1)
        qstart = pl.multiple_of(blk * BLOCK, BLOCK)
        for pair in range(PAIRS):
            lanes = slice(pair * LANES, (pair + 1) * LANES)
            state = _pair_attention(qkv_ref[g, pl.ds(qstart, BLOCK), lanes_of("q", pair)],
                                    qkv_ref[g, pl.ds(kstart, 2 * BLOCK), lanes_of("k", pair)],
                                    qkv_ref[g, pl.ds(kstart, 2 * BLOCK), lanes_of("v", pair)],
                                    bias_ref[generic])
            for part in range(3):
                state16_ref[part, g, pl.ds(qstart, BLOCK), lanes] = state[part]
        return carry

    lax.fori_loop(0, SLABS_PER_MID * (slab_len // BLOCK), coarse, 0, unroll=LOOP_UNROLL)

    q_rows = BLOCK // SLABS_PER_MID
    k_rows = 2 * q_rows

    def gather(ref, start, rows, lanes):
        return jnp.concatenate([ref[g, pl.ds(start, rows), lanes] for g in range(SLABS_PER_MID)], axis=0)

    def gather_state16(start, rows, lanes):
        return tuple(gather(state16_ref.at[part], start, rows, lanes) for part in range(3))

    def mid(i, carry):
        a0 = pl.multiple_of(i * q_rows, q_rows)
        kstart = pl.multiple_of(jnp.maximum(i - 1, 0) * q_rows, q_rows)
        generic = jnp.minimum(i, 1)
        for pair in range(PAIRS):
            lanes = slice(pair * LANES, (pair + 1) * LANES)
            state4 = _pair_attention(gather(qkv_ref, a0, q_rows, lanes_of("q", pair)),
                                     gather(qkv_ref, kstart, k_rows, lanes_of("k", pair)),
                                     gather(qkv_ref, kstart, k_rows, lanes_of("v", pair)),
                                     bias_ref[2 + generic])
            o, lse = _normalised(_merge_states(state4, gather_state16(a0, q_rows, lanes)))
            for g in range(SLABS_PER_MID):
                o_ref[g, pl.ds(a0, q_rows), lanes] = o[g * q_rows:(g + 1) * q_rows].astype(BF16)
                lse_ref[g, pl.ds(a0, q_rows), lanes] = lse[g * q_rows:(g + 1) * q_rows]
        return carry

    lax.fori_loop(0, slab_len // q_rows, mid, 0, unroll=LOOP_UNROLL)


def _to_sequence_order(cls_ref, slab_ref, mid_ref):
    per_class = cls_ref.shape[1]
    per_mid = per_class * SLABS_PER_MID
    n_slabs = cls_ref.shape[2] // LANES
    for s in range(n_slabs):
        lanes = slice(s * LANES, (s + 1) * LANES)
        for r4 in range(MID):
            for g in range(SLABS_PER_MID):
                mid_ref[s, pl.ds(r4 * per_mid + g, per_class, stride=SLABS_PER_MID), :] = (
                    cls_ref[r4 + MID * g, :, lanes].astype(F32))
        for r4 in range(MID):
            slab_ref[s, pl.ds(r4, per_mid, stride=MID), :] = mid_ref[s, r4 * per_mid:(r4 + 1) * per_mid]


def _mix_kernel(x_ref, seq_ref, prev_ref, ws_ref, bs_ref, oc_ref, lc_ref,
                mem_ref, mgains_ref, wkv32_ref, wout32_ref, out_ref,
                oslab_ref, lslab_ref, bias_ref, wout_ref, wkv_ref, mk_ref, mv_ref, *, steps_per_batch):
    rows, d_model = x_ref.shape
    step = pl.program_id(0)

    def seq(name, row_slice=slice(None), pair=None):
        cols = SEQ_COLS[name]
        if pair is not None:
            cols = slice(cols.start + pair * LANES, cols.start + (pair + 1) * LANES)
        return seq_ref[row_slice, cols]

    @pl.when(step % steps_per_batch == 0)
    def _():
        @pl.when(step == 0)
        def _():
            wkv_ref[...] = wkv32_ref[...].astype(BF16)
        mem = mem_ref[...]
        ms = jnp.mean(mem * mem, axis=-1, keepdims=True)
        hm = (mem * lax.rsqrt(ms + EPS) * mgains_ref[:, :d_model]).astype(BF16)
        mk = jnp.dot(hm, wkv_ref[:, :MEM_WIDTH], preferred_element_type=F32)
        mk_ref[...] = _head_rms_lanes(mk, mgains_ref[:, d_model:]).astype(BF16)
        mv_ref[...] = jnp.dot(hm, wkv_ref[:, MEM_WIDTH:], preferred_element_type=F32).astype(BF16)

    @pl.when(step == 0)
    def _():
        wout_ref[...] = wout32_ref[...].astype(BF16)
        row = lax.broadcasted_iota(jnp.int32, (2 * BLOCK, 2 * BLOCK), 0) % BLOCK
        col = lax.broadcasted_iota(jnp.int32, (2 * BLOCK, 2 * BLOCK), 1)
        band = _band_bias(row - col + BLOCK)
        bias_ref[1] = band
        bias_ref[0] = jnp.where(col >= BLOCK, band, MASKED)
    tri = (lax.broadcasted_iota(jnp.int32, (CHUNK, CHUNK), 0)
           >= lax.broadcasted_iota(jnp.int32, (CHUNK, CHUNK), 1))
    left = lax.broadcasted_iota(jnp.int32, (CHUNK, LANES), 1) < HEAD_DIM
    w_tri = [jnp.where(tri, ws_ref[h], 0.0).astype(BF16) for h in range(GMLP_HEADS)]
    sp_chunks = []
    for c in range(rows // CHUNK):
        pairs = []
        for pair in range(GMLP_WIDTH // LANES):
            vn2 = seq("g_vn", slice(c * CHUNK, (c + 1) * CHUNK), pair)
            sp_a = jnp.dot(w_tri[2 * pair], vn2, preferred_element_type=F32)
            sp_b = jnp.dot(w_tri[2 * pair + 1], vn2, preferred_element_type=F32)
            pairs.append(jnp.where(left, sp_a, sp_b))
        sp_chunks.append(jnp.concatenate(pairs, axis=1) + bs_ref[...])
    y_g = seq("g_ug").astype(F32) * jnp.concatenate(sp_chunks, axis=0)

    mem_pairs = []
    for pair in range(MEM_WIDTH // LANES):
        lanes = slice(pair * LANES, (pair + 1) * LANES)
        acc, den, _ = _pair_attention(seq("m_q", pair=pair), mk_ref[:, lanes], mv_ref[:, lanes], None)
        mem_pairs.append(acc / den)
    y_m = jnp.concatenate(mem_pairs, axis=1) * seq("m_gate").astype(F32)

    _to_sequence_order(oc_ref, oslab_ref.at[0], oslab_ref.at[1])
    _to_sequence_order(lc_ref, lslab_ref.at[0], oslab_ref.at[1])
    first_bias = jnp.where(step % steps_per_batch == 0, 0, 1)
    block_outs = []
    n_blocks = rows // BLOCK
    for blk in range(n_blocks):
        if blk == n_blocks // 2:
            partial = (x_ref[...]
                       + jnp.dot(y_g.astype(BF16), wout_ref[:GMLP_WIDTH, :], preferred_element_type=F32)
                       + jnp.dot(y_m.astype(BF16), wout_ref[GMLP_WIDTH + ATTN_WIDTH:, :],
                                 preferred_element_type=F32))
        own = slice(blk * BLOCK, (blk + 1) * BLOCK)
        pair_outs = []
        for pair in range(PAIRS):
            lanes = slice(pair * LANES, (pair + 1) * LANES)
            if blk == 0:
                k_lanes = slice(SEQ_COLS["k"].start + pair * LANES, SEQ_COLS["k"].start + (pair + 1) * LANES)
                v_lanes = slice(SEQ_COLS["v"].start + pair * LANES, SEQ_COLS["v"].start + (pair + 1) * LANES)
                k2 = jnp.concatenate([prev_ref[:, k_lanes], seq("k", own, pair)], axis=0)
                v2 = jnp.concatenate([prev_ref[:, v_lanes], seq("v", own, pair)], axis=0)
                bias = bias_ref[first_bias]
            else:
                window = slice((blk - 1) * BLOCK, (blk + 1) * BLOCK)
                k2, v2, bias = seq("k", window, pair), seq("v", window, pair), bias_ref[1]
            acc1, den1, top1 = _pair_attention(seq("q", own, pair), k2, v2, bias)
            o_c, top_c = oslab_ref[0, pair, own, :], lslab_ref[0, pair, own, :]
            top = jnp.maximum(top1, top_c)
            w_1, w_c = jnp.exp2(top1 - top), jnp.exp2(top_c - top)
            pair_outs.append((w_1 * acc1 + w_c * o_c) / (w_1 * den1 + w_c))
        block_outs.append(jnp.concatenate(pair_outs, axis=1))
    y_a = jnp.concatenate(block_outs, axis=0) * seq("a_gate").astype(F32)
    out_ref[...] = partial + jnp.dot(y_a.astype(BF16), wout_ref[GMLP_WIDTH:GMLP_WIDTH + ATTN_WIDTH, :],
                                     preferred_element_type=F32)


def _cparams(sem):
    return pltpu.CompilerParams(dimension_semantics=sem, vmem_limit_bytes=VMEM_LIMIT)


def _const_spec(shape):
    return pl.BlockSpec(shape, lambda *idx: (0,) * len(shape))


def _regroup_scratch(tile):
    return pltpu.VMEM((2, ATTN_WIDTH // LANES, tile, LANES), F32)


def _class_tile_spec(tile, steps_per_batch, width):
    return pl.BlockSpec((None, CLASSES, tile // CLASSES, width),
                        lambda i: (i // steps_per_batch, 0, i % steps_per_batch, 0))


def _project(x2d, batch, seq, gains, w_in):
    rows, d_model = x2d.shape
    return pl.pallas_call(
        _proj_kernel,
        grid=(rows // PROJ_TILE,),
        in_specs=[pl.BlockSpec((PROJ_TILE, d_model), lambda i: (i, 0)),
                  _const_spec(gains.shape),
                  pl.BlockSpec(w_in.shape, lambda i: (0, 0), pipeline_mode=pl.Buffered(1))],
        out_specs=[pl.BlockSpec((PROJ_TILE, SEQ_WIDTH), lambda i: (i, 0)),
                   _class_tile_spec(PROJ_TILE, seq // PROJ_TILE, CLS_WIDTH)],
        out_shape=[jax.ShapeDtypeStruct((rows, SEQ_WIDTH), BF16),
                   jax.ShapeDtypeStruct((batch, CLASSES, seq // CLASSES, CLS_WIDTH), BF16)],
        scratch_shapes=[pltpu.VMEM(w_in.shape, BF16), pltpu.VMEM((PROJ_TILE, d_model), BF16),
                        _regroup_scratch(PROJ_TILE)],
        compiler_params=_cparams(("arbitrary",)),
    )(x2d, gains, w_in)


def _strided_attention(qkv_cls):
    batch, _, slab_len, _ = qkv_cls.shape
    spec = lambda width: pl.BlockSpec((None, SLABS_PER_MID, CLASSES_PER_STEP, slab_len, width),
                                      lambda b, r4: (b, 0, r4, 0, 0))
    split = lambda width: (batch, SLABS_PER_MID, MID, slab_len, width)
    o, lse = pl.pallas_call(
        _strided_kernel,
        grid=(batch, MID // CLASSES_PER_STEP),
        in_specs=[spec(CLS_WIDTH)],
        out_specs=[spec(ATTN_WIDTH)] * 2,
        out_shape=[jax.ShapeDtypeStruct(split(ATTN_WIDTH), BF16), jax.ShapeDtypeStruct(split(ATTN_WIDTH), F32)],
        scratch_shapes=[pltpu.VMEM((3, SLABS_PER_MID, slab_len, ATTN_WIDTH), F32),
                        pltpu.VMEM((4, 2 * BLOCK, 2 * BLOCK), F32)],
        compiler_params=_cparams(("arbitrary", "arbitrary")),
    )(qkv_cls.reshape(split(CLS_WIDTH)))
    merged = (batch, CLASSES, slab_len, ATTN_WIDTH)
    return o.reshape(merged), lse.reshape(merged)


def _mix(x2d, acts, w_s, b_full, oc, lc, mem, mem_gains, w_kv, w_out, seq):
    rows, d_model = x2d.shape
    mem_len = mem.shape[1]
    steps_per_batch = seq // ROW_TILE
    row_spec = lambda w: pl.BlockSpec((ROW_TILE, w), lambda i: (i, 0))
    prev_spec = pl.BlockSpec((BLOCK, SEQ_COLS["v"].stop),
                             lambda i: (jnp.maximum(i * (ROW_TILE // BLOCK) - 1, 0), 0))
    mem_spec = pl.BlockSpec((None,) + mem.shape[1:], lambda i: (i // steps_per_batch, 0, 0))
    weight_spec = lambda w: pl.BlockSpec(w.shape, lambda i: (0, 0), pipeline_mode=pl.Buffered(1))
    cls_spec = _class_tile_spec(ROW_TILE, steps_per_batch, ATTN_WIDTH)
    return pl.pallas_call(
        functools.partial(_mix_kernel, steps_per_batch=steps_per_batch),
        grid=(rows // ROW_TILE,),
        in_specs=[row_spec(d_model), row_spec(SEQ_WIDTH), prev_spec,
                  _const_spec(w_s.shape), _const_spec(b_full.shape), cls_spec, cls_spec,
                  mem_spec, _const_spec(mem_gains.shape), weight_spec(w_kv), weight_spec(w_out)],
        out_specs=row_spec(d_model),
        out_shape=jax.ShapeDtypeStruct((rows, d_model), F32),
        scratch_shapes=[_regroup_scratch(ROW_TILE), pltpu.VMEM((1, ATTN_WIDTH // LANES, ROW_TILE, LANES), F32)]
                       + [pltpu.VMEM((2, 2 * BLOCK, 2 * BLOCK), F32), pltpu.VMEM(w_out.shape, BF16),
                          pltpu.VMEM(w_kv.shape, BF16)] + [pltpu.VMEM((mem_len, MEM_WIDTH), BF16)] * 2,
        compiler_params=_cparams(("arbitrary",)),
    )(x2d, acts, acts, w_s, b_full, oc, lc, mem, mem_gains, w_kv, w_out)


def kernel(x, mem, norm_gain, w_in, gmlp_v_gain, gmlp_w_s, gmlp_b, attn_q_gain, attn_k_gain,
           mem_norm_gain, w_mem_kv, mem_q_gain, mem_k_gain, w_out):
    batch, seq, d_model = x.shape
    depth = w_in.shape[0]
    assert DILATED_CONFIGS == ((BLOCK, 1), (BLOCK * MID, MID), (BLOCK * CLASSES, CLASSES))
    assert seq % PROJ_TILE == 0 and seq % ROW_TILE == 0 and seq % (BLOCK * CLASSES) == 0
    x2d = x.reshape(batch * seq, d_model)
    for l in range(depth):
        pack = lambda *parts: jnp.concatenate([p.reshape(-1).astype(F32) for p in parts]).reshape(1, -1)
        proj_gains = pack(norm_gain[l], gmlp_v_gain[l], jnp.tile(attn_q_gain[l], ATTN_HEADS),
                          jnp.tile(attn_k_gain[l], ATTN_HEADS), jnp.tile(mem_q_gain[l], MEM_HEADS))
        acts, qkv_cls = _project(x2d, batch, seq, proj_gains, w_in[l])
        oc, lc = _strided_attention(qkv_cls)
        b_full = jnp.repeat(gmlp_b[l].T, HEAD_DIM, axis=1)
        x2d = _mix(x2d, acts, gmlp_w_s[l], b_full, oc, lc, mem,
                   pack(mem_norm_gain[l], jnp.tile(mem_k_gain[l], MEM_HEADS)), w_mem_kv[l], w_out[l], seq)
    return x2d.reshape(batch, seq, d_model)
```

```python
import functools
import math

import jax
import jax.numpy as jnp
from jax import lax
from jax.experimental import pallas as pl
from jax.experimental.pallas import tpu as pltpu

HEAD_DIM = 64
GMLP_HEADS = 4
ATTN_HEADS = 8
MEM_HEADS = 4
GMLP_WIDTH = GMLP_HEADS * HEAD_DIM
ATTN_WIDTH = ATTN_HEADS * HEAD_DIM
MEM_WIDTH = MEM_HEADS * HEAD_DIM
CHUNK = 128
BLOCK = 128
DILATED_CONFIGS = ((128, 1), (512, 4), (2048, 16))
EPS = 1e-6
MASKED = -1e30
LOG2E = math.log2(math.e)
LOGIT_SCALE = LOG2E / math.sqrt(HEAD_DIM)

IN_SEGMENTS = (("g_u", GMLP_WIDTH), ("g_v", GMLP_WIDTH), ("g_gate", GMLP_WIDTH),
               ("a_q", ATTN_WIDTH), ("a_k", ATTN_WIDTH), ("a_v", ATTN_WIDTH), ("a_gate", ATTN_WIDTH),
               ("m_q", MEM_WIDTH), ("m_gate", MEM_WIDTH))


def _column_slices(segments):
    slices, start = {}, 0
    for name, width in segments:
        slices[name] = slice(start, start + width)
        start += width
    return slices, start


IN_COLS, _ = _column_slices(IN_SEGMENTS)
SEQ_COLS, SEQ_WIDTH = _column_slices((("k", ATTN_WIDTH), ("v", ATTN_WIDTH), ("q", ATTN_WIDTH),
                                      ("a_gate", ATTN_WIDTH), ("g_ug", GMLP_WIDTH), ("g_vn", GMLP_WIDTH),
                                      ("m_q", MEM_WIDTH), ("m_gate", MEM_WIDTH)))
CLS_COLS, CLS_WIDTH = _column_slices((("q", ATTN_WIDTH), ("k", ATTN_WIDTH), ("v", ATTN_WIDTH)))

HEAD_GAINS = ("a_q", "a_k", "m_q", "m_k")

CLASSES = 16
MID = 4
SLABS_PER_MID = CLASSES // MID
LANES = 128
PAIRS = ATTN_WIDTH // LANES
PROJ_TILE = 1024
ROW_TILE = 512
LOOP_UNROLL = 8
CLASSES_PER_STEP = 1
VMEM_LIMIT = 56 * 1024 * 1024

F32 = jnp.float32
BF16 = jnp.bfloat16


def _head_rms_lanes(acc, gain):
    left = lax.broadcasted_iota(jnp.int32, (acc.shape[0], LANES), 1) < HEAD_DIM
    groups = []
    for j in range(acc.shape[1] // LANES):
        a = acc[:, j * LANES:(j + 1) * LANES]
        sq = a * a
        ss_a = jnp.sum(jnp.where(left, sq, 0.0), axis=1, keepdims=True)
        ss_b = jnp.sum(jnp.where(left, 0.0, sq), axis=1, keepdims=True)
        inv = jnp.where(left, lax.rsqrt(ss_a * (1.0 / HEAD_DIM) + EPS),
                        lax.rsqrt(ss_b * (1.0 / HEAD_DIM) + EPS))
        group_gain = gain if gain.shape[1] == LANES else gain[:, j * LANES:(j + 1) * LANES]
        groups.append(a * inv * group_gain)
    return jnp.concatenate(groups, axis=1)


def _silu(x):
    half = 0.5 * x
    return half + half * jnp.tanh(half)


def _store_both_layouts(val, name, seq_ref, cls_ref, work_ref):
    seq_ref[:, SEQ_COLS[name]] = val.astype(BF16)
    slab_ref, mid_ref = work_ref.at[0], work_ref.at[1]
    rows = val.shape[0]
    per_mid, per_class = rows // MID, rows // CLASSES
    for s in range(val.shape[1] // LANES):
        slab_ref[s] = val[:, s * LANES:(s + 1) * LANES]
        out_lanes = slice(CLS_COLS[name].start + s * LANES, CLS_COLS[name].start + (s + 1) * LANES)
        for r4 in range(MID):
            mid_ref[s, r4 * per_mid:(r4 + 1) * per_mid] = slab_ref[s, pl.ds(r4, per_mid, stride=MID), :]
        for r4 in range(MID):
            for g in range(SLABS_PER_MID):
                cls_ref[r4 + MID * g, :, out_lanes] = (
                    mid_ref[s, pl.ds(r4 * per_mid + g, per_class, stride=SLABS_PER_MID), :].astype(BF16))


def _proj_kernel(x_ref, xgain_ref, gvgain_ref, hgains_ref, w32_ref, seq_ref, cls_ref, w_ref, h_ref, work_ref):
    @pl.when(pl.program_id(0) == 0)
    def _():
        w_ref[...] = w32_ref[...].astype(BF16)

    def gain(name):
        if name == "g_v":
            return gvgain_ref[...]
        return hgains_ref[HEAD_GAINS.index(name):HEAD_GAINS.index(name) + 1, :]

    x = x_ref[...]
    ms = jnp.mean(x * x, axis=-1, keepdims=True)
    h_ref[...] = (x * lax.rsqrt(ms + EPS) * xgain_ref[...]).astype(BF16)

    def project(name):
        return jnp.dot(h_ref[...], w_ref[:, IN_COLS[name]], preferred_element_type=F32)

    def put(name, val):
        seq_ref[:, SEQ_COLS[name]] = val.astype(BF16)

    g_u = project("g_u")
    g_v = project("g_v")
    g_gate = project("g_gate")
    put("g_ug", g_u * _silu(g_gate))
    a_q = project("a_q")
    put("g_vn", _head_rms_lanes(g_v, gain("g_v")))
    a_k = project("a_k")
    _store_both_layouts(_head_rms_lanes(a_q, gain("a_q")) * LOGIT_SCALE, "q", seq_ref, cls_ref, work_ref)
    a_v = project("a_v")
    _store_both_layouts(_head_rms_lanes(a_k, gain("a_k")), "k", seq_ref, cls_ref, work_ref)
    m_q = project("m_q")
    _store_both_layouts(a_v, "v", seq_ref, cls_ref, work_ref)
    a_gate = project("a_gate")
    put("m_q", _head_rms_lanes(m_q, gain("m_q")) * LOGIT_SCALE)
    m_gate = project("m_gate")
    put("a_gate", _silu(a_gate))
    put("m_gate", _silu(m_gate))


def _pair_attention(q2, k2, v2, bias):
    rows = q2.shape[0]
    left = lax.broadcasted_iota(jnp.int32, (rows, LANES), 1) < HEAD_DIM
    zeros = jnp.zeros_like(q2)
    q_stack = jnp.concatenate([jnp.where(left, q2, zeros), jnp.where(left, zeros, q2)], axis=0)
    v_aug = jnp.concatenate([v2, jnp.ones_like(v2)], axis=1)
    s = lax.dot_general(q_stack, k2, (((1,), (1,)), ((), ())), preferred_element_type=F32)
    if bias is not None:
        s = s + bias
    m = jnp.max(s, axis=1, keepdims=True)
    p = jnp.exp2(s - m).astype(BF16)
    pv = jnp.dot(p, v_aug, preferred_element_type=F32)
    pick = lambda t: jnp.where(left, t[:rows], t[rows:])
    return pick(pv[:, :LANES]), pick(pv[:, LANES:]), pick(jnp.broadcast_to(m, (2 * rows, LANES)))


def _merge_states(a, b):
    top = jnp.maximum(a[2], b[2])
    w_a, w_b = jnp.exp2(a[2] - top), jnp.exp2(b[2] - top)
    return w_a * a[0] + w_b * b[0], w_a * a[1] + w_b * b[1], top


def _normalised(state):
    acc, den, top = state
    return acc / den, top + jnp.log(den) * LOG2E


def _band_bias(rel):
    return jnp.where((rel >= 0) & (rel <= BLOCK), 0.0, MASKED).astype(F32)


def _fill_bias_tables(bias_ref, slabs):
    row = lax.broadcasted_iota(jnp.int32, (2 * BLOCK, 2 * BLOCK), 0) % BLOCK
    col = lax.broadcasted_iota(jnp.int32, (2 * BLOCK, 2 * BLOCK), 1)
    bias_ref[0] = _band_bias(row - col)
    bias_ref[1] = _band_bias(row - col + BLOCK)
    if bias_ref.shape[0] > 2:
        q_rows, k_rows = BLOCK // slabs, 2 * BLOCK // slabs
        rel = (slabs * (row % q_rows) + row // q_rows) - (slabs * (col % k_rows) + col // k_rows)
        bias_ref[2] = _band_bias(rel)
        bias_ref[3] = _band_bias(rel + BLOCK)


def _first_grid_step():
    return (pl.program_id(0) == 0) & (pl.program_id(1) == 0)


def _strided_kernel(qkv_ref, o_ref, lse_ref, state16_ref, bias_ref):
    @pl.when(_first_grid_step())
    def _():
        _fill_bias_tables(bias_ref, SLABS_PER_MID)

    for j in range(qkv_ref.shape[1]):
        _strided_class(qkv_ref.at[:, j], o_ref.at[:, j], lse_ref.at[:, j], state16_ref, bias_ref)


def _strided_class(qkv_ref, o_ref, lse_ref, state16_ref, bias_ref):
    slab_len = qkv_ref.shape[1]

    def lanes_of(name, pair):
        start = CLS_COLS[name].start + pair * LANES
        return slice(start, start + LANES)

    def coarse(i, carry):
        g = i // (slab_len // BLOCK)
        blk = i % (slab_len // BLOCK)
        kstart = pl.multiple_of(jnp.maximum(blk - 1, 0) * BLOCK, BLOCK)
        generic = jnp.minimum(blk, 1)
        qstart = pl.multiple_of(blk * BLOCK, BLOCK)
        for pair in range(PAIRS):
            lanes = slice(pair * LANES, (pair + 1) * LANES)
            state = _pair_attention(qkv_ref[g, pl.ds(qstart, BLOCK), lanes_of("q", pair)],
                                    qkv_ref[g, pl.ds(kstart, 2 * BLOCK), lanes_of("k", pair)],
                                    qkv_ref[g, pl.ds(kstart, 2 * BLOCK), lanes_of("v", pair)],
                                    bias_ref[generic])
            for part in range(3):
                state16_ref[part, g, pl.ds(qstart, BLOCK), lanes] = state[part]
        return carry

    lax.fori_loop(0, SLABS_PER_MID * (slab_len // BLOCK), coarse, 0, unroll=LOOP_UNROLL)

    q_rows = BLOCK // SLABS_PER_MID
    k_rows = 2 * q_rows

    def gather(ref, start, rows, lanes):
        return jnp.concatenate([ref[g, pl.ds(start, rows), lanes] for g in range(SLABS_PER_MID)], axis=0)

    def gather_state16(start, rows, lanes):
        return tuple(gather(state16_ref.at[part], start, rows, lanes) for part in range(3))

    def mid(i, carry):
        a0 = pl.multiple_of(i * q_rows, q_rows)
        kstart = pl.multiple_of(jnp.maximum(i - 1, 0) * q_rows, q_rows)
        generic = jnp.minimum(i, 1)
        for pair in range(PAIRS):
            lanes = slice(pair * LANES, (pair + 1) * LANES)
            state4 = _pair_attention(gather(qkv_ref, a0, q_rows, lanes_of("q", pair)),
                                     gather(qkv_ref, kstart, k_rows, lanes_of("k", pair)),
                                     gather(qkv_ref, kstart, k_rows, lanes_of("v", pair)),
                                     bias_ref[2 + generic])
            o, lse = _normalised(_merge_states(state4, gather_state16(a0, q_rows, lanes)))
            for g in range(SLABS_PER_MID):
                o_ref[g, pl.ds(a0, q_rows), lanes] = o[g * q_rows:(g + 1) * q_rows].astype(BF16)
                lse_ref[g, pl.ds(a0, q_rows), lanes] = lse[g * q_rows:(g + 1) * q_rows]
        return carry

    lax.fori_loop(0, slab_len // q_rows, mid, 0, unroll=LOOP_UNROLL)


def _to_sequence_order(cls_ref, slab_ref, mid_ref):
    per_class = cls_ref.shape[1]
    per_mid = per_class * SLABS_PER_MID
    n_slabs = cls_ref.shape[2] // LANES
    for s in range(n_slabs):
        lanes = slice(s * LANES, (s + 1) * LANES)
        for r4 in range(MID):
            for g in range(SLABS_PER_MID):
                mid_ref[s, pl.ds(r4 * per_mid + g, per_class, stride=SLABS_PER_MID), :] = (
                    cls_ref[r4 + MID * g, :, lanes].astype(F32))
        for r4 in range(MID):
            slab_ref[s, pl.ds(r4, per_mid, stride=MID), :] = mid_ref[s, r4 * per_mid:(r4 + 1) * per_mid]


def _mix_kernel(x_ref, seq_ref, prev_ref, ws_ref, bs_ref, oc_ref, lc_ref,
                mem_ref, mgain_ref, hgains_ref, wkv32_ref, wout32_ref, out_ref,
                oslab_ref, lslab_ref, bias_ref, wout_ref, wkv_ref, mk_ref, mv_ref, *, steps_per_batch):
    rows = x_ref.shape[0]
    step = pl.program_id(0)

    def seq(name, row_slice=slice(None), pair=None):
        cols = SEQ_COLS[name]
        if pair is not None:
            cols = slice(cols.start + pair * LANES, cols.start + (pair + 1) * LANES)
        return seq_ref[row_slice, cols]

    @pl.when(step % steps_per_batch == 0)
    def _():
        @pl.when(step == 0)
        def _():
            wkv_ref[...] = wkv32_ref[...].astype(BF16)
        mem = mem_ref[...]
        ms = jnp.mean(mem * mem, axis=-1, keepdims=True)
        hm = (mem * lax.rsqrt(ms + EPS) * mgain_ref[...]).astype(BF16)
        mk = jnp.dot(hm, wkv_ref[:, :MEM_WIDTH], preferred_element_type=F32)
        mem_k = HEAD_GAINS.index("m_k")
        mk_ref[...] = _head_rms_lanes(mk, hgains_ref[mem_k:mem_k + 1, :]).astype(BF16)
        mv_ref[...] = jnp.dot(hm, wkv_ref[:, MEM_WIDTH:], preferred_element_type=F32).astype(BF16)

    @pl.when(step == 0)
    def _():
        wout_ref[...] = wout32_ref[...].astype(BF16)
        row = lax.broadcasted_iota(jnp.int32, (2 * BLOCK, 2 * BLOCK), 0) % BLOCK
        col = lax.broadcasted_iota(jnp.int32, (2 * BLOCK, 2 * BLOCK), 1)
        band = _band_bias(row - col + BLOCK)
        bias_ref[1] = band
        bias_ref[0] = jnp.where(col >= BLOCK, band, MASKED)
    tri = (lax.broadcasted_iota(jnp.int32, (CHUNK, CHUNK), 0)
           >= lax.broadcasted_iota(jnp.int32, (CHUNK, CHUNK), 1))
    left = lax.broadcasted_iota(jnp.int32, (CHUNK, LANES), 1) < HEAD_DIM
    w_tri = [jnp.where(tri, ws_ref[h], 0.0).astype(BF16) for h in range(GMLP_HEADS)]
    sp_chunks = []
    for c in range(rows // CHUNK):
        pairs = []
        for pair in range(GMLP_WIDTH // LANES):
            vn2 = seq("g_vn", slice(c * CHUNK, (c + 1) * CHUNK), pair)
            sp_a = jnp.dot(w_tri[2 * pair], vn2, preferred_element_type=F32)
            sp_b = jnp.dot(w_tri[2 * pair + 1], vn2, preferred_element_type=F32)
            pairs.append(jnp.where(left, sp_a, sp_b))
        sp_chunks.append(jnp.concatenate(pairs, axis=1) + bs_ref[...])
    y_g = seq("g_ug").astype(F32) * jnp.concatenate(sp_chunks, axis=0)

    mem_pairs = []
    for pair in range(MEM_WIDTH // LANES):
        lanes = slice(pair * LANES, (pair + 1) * LANES)
        acc, den, _ = _pair_attention(seq("m_q", pair=pair), mk_ref[:, lanes], mv_ref[:, lanes], None)
        mem_pairs.append(acc / den)
    y_m = jnp.concatenate(mem_pairs, axis=1) * seq("m_gate").astype(F32)

    _to_sequence_order(oc_ref, oslab_ref.at[0], oslab_ref.at[1])
    _to_sequence_order(lc_ref, lslab_ref.at[0], lslab_ref.at[1])
    first_bias = jnp.where(step % steps_per_batch == 0, 0, 1)
    block_outs = []
    n_blocks = rows // BLOCK
    for blk in range(n_blocks):
        if blk == n_blocks // 2:
            partial = (x_ref[...]
                       + jnp.dot(y_g.astype(BF16), wout_ref[:GMLP_WIDTH, :], preferred_element_type=F32)
                       + jnp.dot(y_m.astype(BF16), wout_ref[GMLP_WIDTH + ATTN_WIDTH:, :],
                                 preferred_element_type=F32))
        own = slice(blk * BLOCK, (blk + 1) * BLOCK)
        pair_outs = []
        for pair in range(PAIRS):
            lanes = slice(pair * LANES, (pair + 1) * LANES)
            if blk == 0:
                k_lanes = slice(SEQ_COLS["k"].start + pair * LANES, SEQ_COLS["k"].start + (pair + 1) * LANES)
                v_lanes = slice(SEQ_COLS["v"].start + pair * LANES, SEQ_COLS["v"].start + (pair + 1) * LANES)
                k2 = jnp.concatenate([prev_ref[:, k_lanes], seq("k", own, pair)], axis=0)
                v2 = jnp.concatenate([prev_ref[:, v_lanes], seq("v", own, pair)], axis=0)
                bias = bias_ref[first_bias]
            else:
                window = slice((blk - 1) * BLOCK, (blk + 1) * BLOCK)
                k2, v2, bias = seq("k", window, pair), seq("v", window, pair), bias_ref[1]
            acc1, den1, top1 = _pair_attention(seq("q", own, pair), k2, v2, bias)
            o_c, top_c = oslab_ref[0, pair, own, :], lslab_ref[0, pair, own, :]
            top = jnp.maximum(top1, top_c)
            w_1, w_c = jnp.exp2(top1 - top), jnp.exp2(top_c - top)
            pair_outs.append((w_1 * acc1 + w_c * o_c) / (w_1 * den1 + w_c))
        block_outs.append(jnp.concatenate(pair_outs, axis=1))
    y_a = jnp.concatenate(block_outs, axis=0) * seq("a_gate").astype(F32)
    out_ref[...] = partial + jnp.dot(y_a.astype(BF16), wout_ref[GMLP_WIDTH:GMLP_WIDTH + ATTN_WIDTH, :],
                                     preferred_element_type=F32)


def _cparams(sem):
    return pltpu.CompilerParams(dimension_semantics=sem, vmem_limit_bytes=VMEM_LIMIT)


def _const_spec(shape):
    return pl.BlockSpec(shape, lambda *idx: (0,) * len(shape))


def _regroup_scratch(tile):
    return pltpu.VMEM((2, ATTN_WIDTH // LANES, tile, LANES), F32)


def _class_tile_spec(tile, steps_per_batch, width):
    return pl.BlockSpec((None, CLASSES, tile // CLASSES, width),
                        lambda i: (i // steps_per_batch, 0, i % steps_per_batch, 0))


def _project(x2d, batch, seq, x_gain, gv_gain, head_gains, w_in):
    rows, d_model = x2d.shape
    return pl.pallas_call(
        _proj_kernel,
        grid=(rows // PROJ_TILE,),
        in_specs=[pl.BlockSpec((PROJ_TILE, d_model), lambda i: (i, 0)),
                  _const_spec(x_gain.shape), _const_spec(gv_gain.shape), _const_spec(head_gains.shape),
                  pl.BlockSpec(w_in.shape, lambda i: (0, 0), pipeline_mode=pl.Buffered(1))],
        out_specs=[pl.BlockSpec((PROJ_TILE, SEQ_WIDTH), lambda i: (i, 0)),
                   _class_tile_spec(PROJ_TILE, seq // PROJ_TILE, CLS_WIDTH)],
        out_shape=[jax.ShapeDtypeStruct((rows, SEQ_WIDTH), BF16),
                   jax.ShapeDtypeStruct((batch, CLASSES, seq // CLASSES, CLS_WIDTH), BF16)],
        scratch_shapes=[pltpu.VMEM(w_in.shape, BF16), pltpu.VMEM((PROJ_TILE, d_model), BF16),
                        _regroup_scratch(PROJ_TILE)],
        compiler_params=_cparams(("arbitrary",)),
    )(x2d, x_gain, gv_gain, head_gains, w_in)


def _strided_attention(qkv_cls):
    batch, _, slab_len, _ = qkv_cls.shape
    spec = lambda width: pl.BlockSpec((None, SLABS_PER_MID, CLASSES_PER_STEP, slab_len, width),
                                      lambda b, r4: (b, 0, r4, 0, 0))
    split = lambda width: (batch, SLABS_PER_MID, MID, slab_len, width)
    o, lse = pl.pallas_call(
        _strided_kernel,
        grid=(batch, MID // CLASSES_PER_STEP),
        in_specs=[spec(CLS_WIDTH)],
        out_specs=[spec(ATTN_WIDTH)] * 2,
        out_shape=[jax.ShapeDtypeStruct(split(ATTN_WIDTH), BF16), jax.ShapeDtypeStruct(split(ATTN_WIDTH), F32)],
        scratch_shapes=[pltpu.VMEM((3, SLABS_PER_MID, slab_len, ATTN_WIDTH), F32),
                        pltpu.VMEM((4, 2 * BLOCK, 2 * BLOCK), F32)],
        compiler_params=_cparams(("arbitrary", "arbitrary")),
    )(qkv_cls.reshape(split(CLS_WIDTH)))
    merged = (batch, CLASSES, slab_len, ATTN_WIDTH)
    return o.reshape(merged), lse.reshape(merged)


def _mix(x2d, acts, w_s, b_full, oc, lc, mem, mem_gain, head_gains, w_kv, w_out, seq):
    rows, d_model = x2d.shape
    mem_len = mem.shape[1]
    steps_per_batch = seq // ROW_TILE
    row_spec = lambda w: pl.BlockSpec((ROW_TILE, w), lambda i: (i, 0))
    prev_spec = pl.BlockSpec((BLOCK, SEQ_COLS["v"].stop),
                             lambda i: (jnp.maximum(i * (ROW_TILE // BLOCK) - 1, 0), 0))
    mem_spec = pl.BlockSpec((None,) + mem.shape[1:], lambda i: (i // steps_per_batch, 0, 0))
    weight_spec = lambda w: pl.BlockSpec(w.shape, lambda i: (0, 0), pipeline_mode=pl.Buffered(1))
    cls_spec = _class_tile_spec(ROW_TILE, steps_per_batch, ATTN_WIDTH)
    return pl.pallas_call(
        functools.partial(_mix_kernel, steps_per_batch=steps_per_batch),
        grid=(rows // ROW_TILE,),
        in_specs=[row_spec(d_model), row_spec(SEQ_WIDTH), prev_spec,
                  _const_spec(w_s.shape), _const_spec(b_full.shape), cls_spec, cls_spec,
                  mem_spec, _const_spec(mem_gain.shape), _const_spec(head_gains.shape),
                  weight_spec(w_kv), weight_spec(w_out)],
        out_specs=row_spec(d_model),
        out_shape=jax.ShapeDtypeStruct((rows, d_model), F32),
        scratch_shapes=[_regroup_scratch(ROW_TILE)] * 2
                       + [pltpu.VMEM((2, 2 * BLOCK, 2 * BLOCK), F32), pltpu.VMEM(w_out.shape, BF16),
                          pltpu.VMEM(w_kv.shape, BF16)] + [pltpu.VMEM((mem_len, MEM_WIDTH), BF16)] * 2,
        compiler_params=_cparams(("arbitrary",)),
    )(x2d, acts, acts, w_s, b_full, oc, lc, mem, mem_gain, head_gains, w_kv, w_out)


def kernel(x, mem, norm_gain, w_in, gmlp_v_gain, gmlp_w_s, gmlp_b, attn_q_gain, attn_k_gain,
           mem_norm_gain, w_mem_kv, mem_q_gain, mem_k_gain, w_out):
    batch, seq, d_model = x.shape
    depth = w_in.shape[0]
    assert DILATED_CONFIGS == ((BLOCK, 1), (BLOCK * MID, MID), (BLOCK * CLASSES, CLASSES))
    assert seq % PROJ_TILE == 0 and seq % ROW_TILE == 0 and seq % (BLOCK * CLASSES) == 0
    x2d = x.reshape(batch * seq, d_model)
    for l in range(depth):
        row = lambda g: g.reshape(1, -1).astype(F32)
        shared = dict(a_q=attn_q_gain[l], a_k=attn_k_gain[l], m_q=mem_q_gain[l], m_k=mem_k_gain[l])
        head_gains = jnp.tile(jnp.stack([shared[name] for name in HEAD_GAINS]).astype(F32), (1, 2))
        acts, qkv_cls = _project(x2d, batch, seq, row(norm_gain[l]), row(gmlp_v_gain[l]), head_gains,
                                 w_in[l])
        oc, lc = _strided_attention(qkv_cls)
        b_full = jnp.repeat(gmlp_b[l].T, HEAD_DIM, axis=1)
        x2d = _mix(x2d, acts, gmlp_w_s[l], b_full, oc, lc, mem, row(mem_norm_gain[l]), head_gains,
                   w_mem_kv[l], w_out[l], seq)
    return x2d.reshape(batch, seq, d_model)
```

```python
import functools
import math

import jax
import jax.numpy as jnp
from jax import lax
from jax.experimental import pallas as pl
from jax.experimental.pallas import tpu as pltpu

HEAD_DIM = 64
GMLP_HEADS = 4
ATTN_HEADS = 8
MEM_HEADS = 4
GMLP_WIDTH = GMLP_HEADS * HEAD_DIM
ATTN_WIDTH = ATTN_HEADS * HEAD_DIM
MEM_WIDTH = MEM_HEADS * HEAD_DIM
CHUNK = 128
BLOCK = 128
DILATED_CONFIGS = ((128, 1), (512, 4), (2048, 16))
EPS = 1e-6
MASKED = -1e30
LOG2E = math.log2(math.e)
LOGIT_SCALE = LOG2E / math.sqrt(HEAD_DIM)

IN_SEGMENTS = (("g_u", GMLP_WIDTH), ("g_v", GMLP_WIDTH), ("g_gate", GMLP_WIDTH),
               ("a_q", ATTN_WIDTH), ("a_k", ATTN_WIDTH), ("a_v", ATTN_WIDTH), ("a_gate", ATTN_WIDTH),
               ("m_q", MEM_WIDTH), ("m_gate", MEM_WIDTH))


def _column_slices(segments):
    slices, start = {}, 0
    for name, width in segments:
        slices[name] = slice(start, start + width)
        start += width
    return slices, start


IN_COLS, _ = _column_slices(IN_SEGMENTS)
SEQ_COLS, SEQ_WIDTH = _column_slices((("k", ATTN_WIDTH), ("v", ATTN_WIDTH), ("q", ATTN_WIDTH),
                                      ("a_gate", ATTN_WIDTH), ("g_ug", GMLP_WIDTH), ("g_vn", GMLP_WIDTH),
                                      ("m_q", MEM_WIDTH), ("m_gate", MEM_WIDTH)))
CLS_COLS, CLS_WIDTH = _column_slices((("q", ATTN_WIDTH), ("k", ATTN_WIDTH), ("v", ATTN_WIDTH)))

HEAD_GAINS = ("a_q", "a_k", "m_q", "m_k")

CLASSES = 16
MID = 4
SLABS_PER_MID = CLASSES // MID
LANES = 128
PAIRS = ATTN_WIDTH // LANES
PROJ_TILE = 1024
ROW_TILE = 512
CLASSES_PER_STEP = 1
VMEM_LIMIT = 56 * 1024 * 1024

F32 = jnp.float32
BF16 = jnp.bfloat16


def _head_rms_lanes(acc, gain):
    left = lax.broadcasted_iota(jnp.int32, (acc.shape[0], LANES), 1) < HEAD_DIM
    groups = []
    for j in range(acc.shape[1] // LANES):
        a = acc[:, j * LANES:(j + 1) * LANES]
        sq = a * a
        ss_a = jnp.sum(jnp.where(left, sq, 0.0), axis=1, keepdims=True)
        ss_b = jnp.sum(jnp.where(left, 0.0, sq), axis=1, keepdims=True)
        inv = jnp.where(left, lax.rsqrt(ss_a * (1.0 / HEAD_DIM) + EPS),
                        lax.rsqrt(ss_b * (1.0 / HEAD_DIM) + EPS))
        group_gain = gain if gain.shape[1] == LANES else gain[:, j * LANES:(j + 1) * LANES]
        groups.append(a * inv * group_gain)
    return jnp.concatenate(groups, axis=1)


def _silu(x):
    half = 0.5 * x
    return half + half * jnp.tanh(half)


def _store_both_layouts(val, name, seq_ref, cls_ref, work_ref):
    seq_ref[:, SEQ_COLS[name]] = val.astype(BF16)
    slab_ref, mid_ref = work_ref.at[0], work_ref.at[1]
    rows = val.shape[0]
    per_mid, per_class = rows // MID, rows // CLASSES
    for s in range(val.shape[1] // LANES):
        slab_ref[s] = val[:, s * LANES:(s + 1) * LANES]
        out_lanes = slice(CLS_COLS[name].start + s * LANES, CLS_COLS[name].start + (s + 1) * LANES)
        for r4 in range(MID):
            mid_ref[s, r4 * per_mid:(r4 + 1) * per_mid] = slab_ref[s, pl.ds(r4, per_mid, stride=MID), :]
        for r4 in range(MID):
            for g in range(SLABS_PER_MID):
                cls_ref[r4 + MID * g, :, out_lanes] = (
                    mid_ref[s, pl.ds(r4 * per_mid + g, per_class, stride=SLABS_PER_MID), :].astype(BF16))


def _proj_kernel(x_ref, xgain_ref, gvgain_ref, hgains_ref, w32_ref, seq_ref, cls_ref, w_ref, h_ref, work_ref):
    @pl.when(pl.program_id(0) == 0)
    def _():
        w_ref[...] = w32_ref[...].astype(BF16)

    def gain(name):
        if name == "g_v":
            return gvgain_ref[...]
        return hgains_ref[HEAD_GAINS.index(name):HEAD_GAINS.index(name) + 1, :]

    x = x_ref[...]
    ms = jnp.mean(x * x, axis=-1, keepdims=True)
    h_ref[...] = (x * lax.rsqrt(ms + EPS) * xgain_ref[...]).astype(BF16)

    def project(name):
        return jnp.dot(h_ref[...], w_ref[:, IN_COLS[name]], preferred_element_type=F32)

    def put(name, val):
        seq_ref[:, SEQ_COLS[name]] = val.astype(BF16)

    a_q = project("a_q")
    a_k = project("a_k")
    _store_both_layouts(_head_rms_lanes(a_q, gain("a_q")) * LOGIT_SCALE, "q", seq_ref, cls_ref, work_ref)
    a_v = project("a_v")
    _store_both_layouts(_head_rms_lanes(a_k, gain("a_k")), "k", seq_ref, cls_ref, work_ref)
    a_gate = project("a_gate")
    _store_both_layouts(a_v, "v", seq_ref, cls_ref, work_ref)
    m_q = project("m_q")
    put("a_gate", _silu(a_gate))
    g_u = project("g_u")
    put("m_q", _head_rms_lanes(m_q, gain("m_q")) * LOGIT_SCALE)
    g_v = project("g_v")
    g_gate = project("g_gate")
    put("g_vn", _head_rms_lanes(g_v, gain("g_v")))
    m_gate = project("m_gate")
    put("g_ug", g_u * _silu(g_gate))
    put("m_gate", _silu(m_gate))


def _pair_attention(q2, k2, v2, bias):
    rows = q2.shape[0]
    left = lax.broadcasted_iota(jnp.int32, (rows, LANES), 1) < HEAD_DIM
    zeros = jnp.zeros_like(q2)
    q_stack = jnp.concatenate([jnp.where(left, q2, zeros), jnp.where(left, zeros, q2)], axis=0)
    v_aug = jnp.concatenate([v2, jnp.ones_like(v2)], axis=1)
    s = lax.dot_general(q_stack, k2, (((1,), (1,)), ((), ())), preferred_element_type=F32)
    if bias is not None:
        s = s + bias
    m = jnp.max(s, axis=1, keepdims=True)
    p = jnp.exp2(s - m).astype(BF16)
    pv = jnp.dot(p, v_aug, preferred_element_type=F32)
    pick = lambda t: jnp.where(left, t[:rows], t[rows:])
    return pick(pv[:, :LANES]), pick(pv[:, LANES:]), pick(jnp.broadcast_to(m, (2 * rows, LANES)))


def _merge_states(a, b):
    top = jnp.maximum(a[2], b[2])
    w_a, w_b = jnp.exp2(a[2] - top), jnp.exp2(b[2] - top)
    return w_a * a[0] + w_b * b[0], w_a * a[1] + w_b * b[1], top


def _normalised(state):
    acc, den, top = state
    return acc / den, top + jnp.log(den) * LOG2E


def _band_bias(rel):
    return jnp.where((rel >= 0) & (rel <= BLOCK), 0.0, MASKED).astype(F32)


def _fill_bias_tables(bias_ref, slabs):
    row = lax.broadcasted_iota(jnp.int32, (2 * BLOCK, 2 * BLOCK), 0) % BLOCK
    col = lax.broadcasted_iota(jnp.int32, (2 * BLOCK, 2 * BLOCK), 1)
    bias_ref[0] = _band_bias(row - col)
    bias_ref[1] = _band_bias(row - col + BLOCK)
    if bias_ref.shape[0] > 2:
        q_rows, k_rows = BLOCK // slabs, 2 * BLOCK // slabs
        rel = (slabs * (row % q_rows) + row // q_rows) - (slabs * (col % k_rows) + col // k_rows)
        bias_ref[2] = _band_bias(rel)
        bias_ref[3] = _band_bias(rel + BLOCK)


def _first_grid_step():
    return (pl.program_id(0) == 0) & (pl.program_id(1) == 0)


def _strided_kernel(qkv_ref, o_ref, lse_ref, state16_ref, bias_ref):
    @pl.when(_first_grid_step())
    def _():
        _fill_bias_tables(bias_ref, SLABS_PER_MID)

    for j in range(qkv_ref.shape[1]):
        _strided_class(qkv_ref.at[:, j], o_ref.at[:, j], lse_ref.at[:, j], state16_ref, bias_ref)


def _strided_class(qkv_ref, o_ref, lse_ref, state16_ref, bias_ref):
    slab_len = qkv_ref.shape[1]

    def lanes_of(name, pair):
        start = CLS_COLS[name].start + pair * LANES
        return slice(start, start + LANES)

    def coarse(g, blk):
        kstart = max(blk - 1, 0) * BLOCK
        keys = slice(kstart, kstart + 2 * BLOCK)
        own = slice(blk * BLOCK, (blk + 1) * BLOCK)
        for pair in range(PAIRS):
            lanes = slice(pair * LANES, (pair + 1) * LANES)
            state = _pair_attention(qkv_ref[g, own, lanes_of("q", pair)], qkv_ref[g, keys, lanes_of("k", pair)],
                                    qkv_ref[g, keys, lanes_of("v", pair)], bias_ref[min(blk, 1)])
            for part in range(3):
                state16_ref[part, g, own, lanes] = state[part]

    q_rows = BLOCK // SLABS_PER_MID
    k_rows = 2 * q_rows

    def gather(ref, rows, lanes):
        return jnp.concatenate([ref[g, rows, lanes] for g in range(SLABS_PER_MID)], axis=0)

    def mid(i):
        own = slice(i * q_rows, (i + 1) * q_rows)
        kstart = max(i - 1, 0) * q_rows
        keys = slice(kstart, kstart + k_rows)
        for pair in range(PAIRS):
            lanes = slice(pair * LANES, (pair + 1) * LANES)
            state4 = _pair_attention(gather(qkv_ref, own, lanes_of("q", pair)),
                                     gather(qkv_ref, keys, lanes_of("k", pair)),
                                     gather(qkv_ref, keys, lanes_of("v", pair)), bias_ref[2 + min(i, 1)])
            state16 = tuple(gather(state16_ref.at[part], own, lanes) for part in range(3))
            o, lse = _normalised(_merge_states(state4, state16))
            for g in range(SLABS_PER_MID):
                o_ref[g, own, lanes] = o[g * q_rows:(g + 1) * q_rows].astype(BF16)
                lse_ref[g, own, lanes] = lse[g * q_rows:(g + 1) * q_rows]

    bands = slab_len // BLOCK
    per_band = BLOCK // q_rows
    for band in range(bands):
        for g in range(SLABS_PER_MID):
            coarse(g, band)
            if band > 0:
                for i in range(g * per_band // SLABS_PER_MID, (g + 1) * per_band // SLABS_PER_MID):
                    mid((band - 1) * per_band + i)
    for i in range(per_band):
        mid((bands - 1) * per_band + i)


def _to_sequence_order(cls_ref, slab_ref, mid_ref):
    per_class = cls_ref.shape[1]
    per_mid = per_class * SLABS_PER_MID
    n_slabs = cls_ref.shape[2] // LANES
    for s in range(n_slabs):
        lanes = slice(s * LANES, (s + 1) * LANES)
        for r4 in range(MID):
            for g in range(SLABS_PER_MID):
                mid_ref[s, pl.ds(r4 * per_mid + g, per_class, stride=SLABS_PER_MID), :] = (
                    cls_ref[r4 + MID * g, :, lanes].astype(F32))
        for r4 in range(MID):
            slab_ref[s, pl.ds(r4, per_mid, stride=MID), :] = mid_ref[s, r4 * per_mid:(r4 + 1) * per_mid]


def _mix_kernel(x_ref, seq_ref, prev_ref, ws_ref, bs_ref, oc_ref, lc_ref,
                mem_ref, mgain_ref, hgains_ref, wkv32_ref, wout32_ref, out_ref,
                oslab_ref, lslab_ref, bias_ref, wout_ref, wkv_ref, mk_ref, mv_ref, *, steps_per_batch):
    rows = x_ref.shape[0]
    step = pl.program_id(0)

    def seq(name, row_slice=slice(None), pair=None):
        cols = SEQ_COLS[name]
        if pair is not None:
            cols = slice(cols.start + pair * LANES, cols.start + (pair + 1) * LANES)
        return seq_ref[row_slice, cols]

    @pl.when(step % steps_per_batch == 0)
    def _():
        @pl.when(step == 0)
        def _():
            wkv_ref[...] = wkv32_ref[...].astype(BF16)
        mem = mem_ref[...]
        ms = jnp.mean(mem * mem, axis=-1, keepdims=True)
        hm = (mem * lax.rsqrt(ms + EPS) * mgain_ref[...]).astype(BF16)
        mk = jnp.dot(hm, wkv_ref[:, :MEM_WIDTH], preferred_element_type=F32)
        mem_k = HEAD_GAINS.index("m_k")
        mk_ref[...] = _head_rms_lanes(mk, hgains_ref[mem_k:mem_k + 1, :]).astype(BF16)
        mv_ref[...] = jnp.dot(hm, wkv_ref[:, MEM_WIDTH:], preferred_element_type=F32).astype(BF16)

    @pl.when(step == 0)
    def _():
        wout_ref[...] = wout32_ref[...].astype(BF16)
        row = lax.broadcasted_iota(jnp.int32, (2 * BLOCK, 2 * BLOCK), 0) % BLOCK
        col = lax.broadcasted_iota(jnp.int32, (2 * BLOCK, 2 * BLOCK), 1)
        band = _band_bias(row - col + BLOCK)
        bias_ref[1] = band
        bias_ref[0] = jnp.where(col >= BLOCK, band, MASKED)
    tri = (lax.broadcasted_iota(jnp.int32, (CHUNK, CHUNK), 0)
           >= lax.broadcasted_iota(jnp.int32, (CHUNK, CHUNK), 1))
    left = lax.broadcasted_iota(jnp.int32, (CHUNK, LANES), 1) < HEAD_DIM
    w_tri = [jnp.where(tri, ws_ref[h], 0.0).astype(BF16) for h in range(GMLP_HEADS)]
    sp_chunks = []
    for c in range(rows // CHUNK):
        pairs = []
        for pair in range(GMLP_WIDTH // LANES):
            vn2 = seq("g_vn", slice(c * CHUNK, (c + 1) * CHUNK), pair)
            sp_a = jnp.dot(w_tri[2 * pair], vn2, preferred_element_type=F32)
            sp_b = jnp.dot(w_tri[2 * pair + 1], vn2, preferred_element_type=F32)
            pairs.append(jnp.where(left, sp_a, sp_b))
        sp_chunks.append(jnp.concatenate(pairs, axis=1) + bs_ref[...])
    y_g = seq("g_ug").astype(F32) * jnp.concatenate(sp_chunks, axis=0)

    mem_pairs = []
    for pair in range(MEM_WIDTH // LANES):
        lanes = slice(pair * LANES, (pair + 1) * LANES)
        acc, den, _ = _pair_attention(seq("m_q", pair=pair), mk_ref[:, lanes], mv_ref[:, lanes], None)
        mem_pairs.append(acc / den)
    y_m = jnp.concatenate(mem_pairs, axis=1) * seq("m_gate").astype(F32)

    _to_sequence_order(oc_ref, oslab_ref.at[0], oslab_ref.at[1])
    _to_sequence_order(lc_ref, lslab_ref.at[0], lslab_ref.at[1])
    first_bias = jnp.where(step % steps_per_batch == 0, 0, 1)
    block_outs = []
    n_blocks = rows // BLOCK
    for blk in range(n_blocks):
        if blk == n_blocks - 1:
            partial = (x_ref[...]
                       + jnp.dot(y_g.astype(BF16), wout_ref[:GMLP_WIDTH, :], preferred_element_type=F32)
                       + jnp.dot(y_m.astype(BF16), wout_ref[GMLP_WIDTH + ATTN_WIDTH:, :],
                                 preferred_element_type=F32))
        own = slice(blk * BLOCK, (blk + 1) * BLOCK)
        pair_outs = []
        for pair in range(PAIRS):
            lanes = slice(pair * LANES, (pair + 1) * LANES)
            if blk == 0:
                k_lanes = slice(SEQ_COLS["k"].start + pair * LANES, SEQ_COLS["k"].start + (pair + 1) * LANES)
                v_lanes = slice(SEQ_COLS["v"].start + pair * LANES, SEQ_COLS["v"].start + (pair + 1) * LANES)
                k2 = jnp.concatenate([prev_ref[:, k_lanes], seq("k", own, pair)], axis=0)
                v2 = jnp.concatenate([prev_ref[:, v_lanes], seq("v", own, pair)], axis=0)
                bias = bias_ref[first_bias]
            else:
                window = slice((blk - 1) * BLOCK, (blk + 1) * BLOCK)
                k2, v2, bias = seq("k", window, pair), seq("v", window, pair), bias_ref[1]
            acc1, den1, top1 = _pair_attention(seq("q", own, pair), k2, v2, bias)
            o_c, top_c = oslab_ref[0, pair, own, :], lslab_ref[0, pair, own, :]
            top = jnp.maximum(top1, top_c)
            w_1, w_c = jnp.exp2(top1 - top), jnp.exp2(top_c - top)
            pair_outs.append((w_1 * acc1 + w_c * o_c) / (w_1 * den1 + w_c))
        block_outs.append(jnp.concatenate(pair_outs, axis=1))
    y_a = jnp.concatenate(block_outs, axis=0) * seq("a_gate").astype(F32)
    out_ref[...] = partial + jnp.dot(y_a.astype(BF16), wout_ref[GMLP_WIDTH:GMLP_WIDTH + ATTN_WIDTH, :],
                                     preferred_element_type=F32)


def _cparams(sem):
    return pltpu.CompilerParams(dimension_semantics=sem, vmem_limit_bytes=VMEM_LIMIT)


def _const_spec(shape):
    return pl.BlockSpec(shape, lambda *idx: (0,) * len(shape))


def _regroup_scratch(tile):
    return pltpu.VMEM((2, ATTN_WIDTH // LANES, tile, LANES), F32)


def _class_tile_spec(tile, steps_per_batch, width):
    return pl.BlockSpec((None, CLASSES, tile // CLASSES, width),
                        lambda i: (i // steps_per_batch, 0, i % steps_per_batch, 0))


def _project(x2d, batch, seq, x_gain, gv_gain, head_gains, w_in):
    rows, d_model = x2d.shape
    return pl.pallas_call(
        _proj_kernel,
        grid=(rows // PROJ_TILE,),
        in_specs=[pl.BlockSpec((PROJ_TILE, d_model), lambda i: (i, 0)),
                  _const_spec(x_gain.shape), _const_spec(gv_gain.shape), _const_spec(head_gains.shape),
                  pl.BlockSpec(w_in.shape, lambda i: (0, 0), pipeline_mode=pl.Buffered(1))],
        out_specs=[pl.BlockSpec((PROJ_TILE, SEQ_WIDTH), lambda i: (i, 0)),
                   _class_tile_spec(PROJ_TILE, seq // PROJ_TILE, CLS_WIDTH)],
        out_shape=[jax.ShapeDtypeStruct((rows, SEQ_WIDTH), BF16),
                   jax.ShapeDtypeStruct((batch, CLASSES, seq // CLASSES, CLS_WIDTH), BF16)],
        scratch_shapes=[pltpu.VMEM(w_in.shape, BF16), pltpu.VMEM((PROJ_TILE, d_model), BF16),
                        _regroup_scratch(PROJ_TILE)],
        compiler_params=_cparams(("arbitrary",)),
    )(x2d, x_gain, gv_gain, head_gains, w_in)


def _strided_attention(qkv_cls):
    batch, _, slab_len, _ = qkv_cls.shape
    spec = lambda width: pl.BlockSpec((None, SLABS_PER_MID, CLASSES_PER_STEP, slab_len, width),
                                      lambda b, r4: (b, 0, r4, 0, 0))
    split = lambda width: (batch, SLABS_PER_MID, MID, slab_len, width)
    o, lse = pl.pallas_call(
        _strided_kernel,
        grid=(batch, MID // CLASSES_PER_STEP),
        in_specs=[spec(CLS_WIDTH)],
        out_specs=[spec(ATTN_WIDTH)] * 2,
        out_shape=[jax.ShapeDtypeStruct(split(ATTN_WIDTH), BF16), jax.ShapeDtypeStruct(split(ATTN_WIDTH), F32)],
        scratch_shapes=[pltpu.VMEM((3, SLABS_PER_MID, slab_len, ATTN_WIDTH), F32),
                        pltpu.VMEM((4, 2 * BLOCK, 2 * BLOCK), F32)],
        compiler_params=_cparams(("arbitrary", "arbitrary")),
    )(qkv_cls.reshape(split(CLS_WIDTH)))
    merged = (batch, CLASSES, slab_len, ATTN_WIDTH)
    return o.reshape(merged), lse.reshape(merged)


def _mix(x2d, acts, w_s, b_full, oc, lc, mem, mem_gain, head_gains, w_kv, w_out, seq):
    rows, d_model = x2d.shape
    mem_len = mem.shape[1]
    steps_per_batch = seq // ROW_TILE
    row_spec = lambda w: pl.BlockSpec((ROW_TILE, w), lambda i: (i, 0))
    prev_spec = pl.BlockSpec((BLOCK, SEQ_COLS["v"].stop),
                             lambda i: (jnp.maximum(i * (ROW_TILE // BLOCK) - 1, 0), 0))
    mem_spec = pl.BlockSpec((None,) + mem.shape[1:], lambda i: (i // steps_per_batch, 0, 0))
    weight_spec = lambda w: pl.BlockSpec(w.shape, lambda i: (0, 0), pipeline_mode=pl.Buffered(1))
    cls_spec = _class_tile_spec(ROW_TILE, steps_per_batch, ATTN_WIDTH)
    return pl.pallas_call(
        functools.partial(_mix_kernel, steps_per_batch=steps_per_batch),
        grid=(rows // ROW_TILE,),
        in_specs=[row_spec(d_model), row_spec(SEQ_WIDTH), prev_spec,
                  _const_spec(w_s.shape), _const_spec(b_full.shape), cls_spec, cls_spec,
                  mem_spec, _const_spec(mem_gain.shape), _const_spec(head_gains.shape),
                  weight_spec(w_kv), weight_spec(w_out)],
        out_specs=row_spec(d_model),
        out_shape=jax.ShapeDtypeStruct((rows, d_model), F32),
        scratch_shapes=[_regroup_scratch(ROW_TILE)] * 2
                       + [pltpu.VMEM((2, 2 * BLOCK, 2 * BLOCK), F32), pltpu.VMEM(w_out.shape, BF16),
                          pltpu.VMEM(w_kv.shape, BF16)] + [pltpu.VMEM((mem_len, MEM_WIDTH), BF16)] * 2,
        compiler_params=_cparams(("arbitrary",)),
    )(x2d, acts, acts, w_s, b_full, oc, lc, mem, mem_gain, head_gains, w_kv, w_out)


def kernel(x, mem, norm_gain, w_in, gmlp_v_gain, gmlp_w_s, gmlp_b, attn_q_gain, attn_k_gain,
           mem_norm_gain, w_mem_kv, mem_q_gain, mem_k_gain, w_out):
    batch, seq, d_model = x.shape
    depth = w_in.shape[0]
    assert DILATED_CONFIGS == ((BLOCK, 1), (BLOCK * MID, MID), (BLOCK * CLASSES, CLASSES))
    assert seq % PROJ_TILE == 0 and seq % ROW_TILE == 0 and seq % (BLOCK * CLASSES) == 0
    x2d = x.reshape(batch * seq, d_model)
    for l in range(depth):
        row = lambda g: g.reshape(1, -1).astype(F32)
        shared = dict(a_q=attn_q_gain[l], a_k=attn_k_gain[l], m_q=mem_q_gain[l], m_k=mem_k_gain[l])
        head_gains = jnp.tile(jnp.stack([shared[name] for name in HEAD_GAINS]).astype(F32), (1, 2))
        acts, qkv_cls = _project(x2d, batch, seq, row(norm_gain[l]), row(gmlp_v_gain[l]), head_gains,
                                 w_in[l])
        oc, lc = _strided_attention(qkv_cls)
        b_full = jnp.repeat(gmlp_b[l].T, HEAD_DIM, axis=1)
        x2d = _mix(x2d, acts, gmlp_w_s[l], b_full, oc, lc, mem, row(mem_norm_gain[l]), head_gains,
                   w_mem_kv[l], w_out[l], seq)
    return x2d.reshape(batch, seq, d_model)
```

```python
import functools
import math

import jax
import jax.numpy as jnp
from jax import lax
from jax.experimental import pallas as pl
from jax.experimental.pallas import tpu as pltpu

HEAD_DIM = 64
GMLP_HEADS = 4
ATTN_HEADS = 8
MEM_HEADS = 4
GMLP_WIDTH = GMLP_HEADS * HEAD_DIM
ATTN_WIDTH = ATTN_HEADS * HEAD_DIM
MEM_WIDTH = MEM_HEADS * HEAD_DIM
CHUNK = 128
BLOCK = 128
DILATED_CONFIGS = ((128, 1), (512, 4), (2048, 16))
EPS = 1e-6
MASKED = -1e30
LOG2E = math.log2(math.e)
LOGIT_SCALE = LOG2E / math.sqrt(HEAD_DIM)

IN_SEGMENTS = (("g_u", GMLP_WIDTH), ("g_v", GMLP_WIDTH), ("g_gate", GMLP_WIDTH),
               ("a_q", ATTN_WIDTH), ("a_k", ATTN_WIDTH), ("a_v", ATTN_WIDTH), ("a_gate", ATTN_WIDTH),
               ("m_q", MEM_WIDTH), ("m_gate", MEM_WIDTH))


def _column_slices(segments):
    slices, start = {}, 0
    for name, width in segments:
        slices[name] = slice(start, start + width)
        start += width
    return slices, start


IN_COLS, _ = _column_slices(IN_SEGMENTS)
SEQ_COLS, SEQ_WIDTH = _column_slices((("k", ATTN_WIDTH), ("v", ATTN_WIDTH), ("q", ATTN_WIDTH),
                                      ("a_gate", ATTN_WIDTH), ("g_ug", GMLP_WIDTH), ("g_vn", GMLP_WIDTH),
                                      ("m_q", MEM_WIDTH), ("m_gate", MEM_WIDTH)))
CLS_COLS, CLS_WIDTH = _column_slices((("q", ATTN_WIDTH), ("k", ATTN_WIDTH), ("v", ATTN_WIDTH)))

HEAD_GAINS = ("a_q", "a_k", "m_q", "m_k")

CLASSES = 16
MID = 4
SLABS_PER_MID = CLASSES // MID
LANES = 128
PAIRS = ATTN_WIDTH // LANES
PROJ_TILE = 1024
ROW_TILE = 512
CLASSES_PER_STEP = 1
MEM_AT = 1
VMEM_LIMIT = 56 * 1024 * 1024

F32 = jnp.float32
BF16 = jnp.bfloat16


def _head_rms_lanes(acc, gain):
    left = lax.broadcasted_iota(jnp.int32, (acc.shape[0], LANES), 1) < HEAD_DIM
    groups = []
    for j in range(acc.shape[1] // LANES):
        a = acc[:, j * LANES:(j + 1) * LANES]
        sq = a * a
        ss_a = jnp.sum(jnp.where(left, sq, 0.0), axis=1, keepdims=True)
        ss_b = jnp.sum(jnp.where(left, 0.0, sq), axis=1, keepdims=True)
        inv = jnp.where(left, lax.rsqrt(ss_a * (1.0 / HEAD_DIM) + EPS),
                        lax.rsqrt(ss_b * (1.0 / HEAD_DIM) + EPS))
        group_gain = gain if gain.shape[1] == LANES else gain[:, j * LANES:(j + 1) * LANES]
        groups.append(a * inv * group_gain)
    return jnp.concatenate(groups, axis=1)


def _silu(x):
    half = 0.5 * x
    return half + half * jnp.tanh(half)


def _store_both_layouts(val, name, seq_ref, cls_ref, work_ref):
    seq_ref[:, SEQ_COLS[name]] = val.astype(BF16)
    slab_ref, mid_ref = work_ref.at[0], work_ref.at[1]
    rows = val.shape[0]
    per_mid, per_class = rows // MID, rows // CLASSES
    for s in range(val.shape[1] // LANES):
        slab_ref[s] = val[:, s * LANES:(s + 1) * LANES]
        out_lanes = slice(CLS_COLS[name].start + s * LANES, CLS_COLS[name].start + (s + 1) * LANES)
        for r4 in range(MID):
            mid_ref[s, r4 * per_mid:(r4 + 1) * per_mid] = slab_ref[s, pl.ds(r4, per_mid, stride=MID), :]
        for r4 in range(MID):
            for g in range(SLABS_PER_MID):
                cls_ref[r4 + MID * g, :, out_lanes] = (
                    mid_ref[s, pl.ds(r4 * per_mid + g, per_class, stride=SLABS_PER_MID), :].astype(BF16))


def _proj_kernel(x_ref, xgain_ref, gvgain_ref, hgains_ref, w32_ref, seq_ref, cls_ref, w_ref, h_ref, work_ref):
    @pl.when(pl.program_id(0) == 0)
    def _():
        w_ref[...] = w32_ref[...].astype(BF16)

    def gain(name):
        if name == "g_v":
            return gvgain_ref[...]
        return hgains_ref[HEAD_GAINS.index(name):HEAD_GAINS.index(name) + 1, :]

    x = x_ref[...]
    ms = jnp.mean(x * x, axis=-1, keepdims=True)
    h_ref[...] = (x * lax.rsqrt(ms + EPS) * xgain_ref[...]).astype(BF16)

    def project(name):
        return jnp.dot(h_ref[...], w_ref[:, IN_COLS[name]], preferred_element_type=F32)

    def put(name, val):
        seq_ref[:, SEQ_COLS[name]] = val.astype(BF16)

    a_q = project("a_q")
    a_k = project("a_k")
    _store_both_layouts(_head_rms_lanes(a_q, gain("a_q")) * LOGIT_SCALE, "q", seq_ref, cls_ref, work_ref)
    a_v = project("a_v")
    _store_both_layouts(_head_rms_lanes(a_k, gain("a_k")), "k", seq_ref, cls_ref, work_ref)
    a_gate = project("a_gate")
    _store_both_layouts(a_v, "v", seq_ref, cls_ref, work_ref)
    m_q = project("m_q")
    put("a_gate", _silu(a_gate))
    g_u = project("g_u")
    put("m_q", _head_rms_lanes(m_q, gain("m_q")) * LOGIT_SCALE)
    g_v = project("g_v")
    g_gate = project("g_gate")
    put("g_vn", _head_rms_lanes(g_v, gain("g_v")))
    m_gate = project("m_gate")
    put("g_ug", g_u * _silu(g_gate))
    put("m_gate", _silu(m_gate))


def _pair_attention(q2, k2, v2, bias):
    rows = q2.shape[0]
    left = lax.broadcasted_iota(jnp.int32, (rows, LANES), 1) < HEAD_DIM
    zeros = jnp.zeros_like(q2)
    q_stack = jnp.concatenate([jnp.where(left, q2, zeros), jnp.where(left, zeros, q2)], axis=0)
    v_aug = jnp.concatenate([v2, jnp.ones_like(v2)], axis=1)
    s = lax.dot_general(q_stack, k2, (((1,), (1,)), ((), ())), preferred_element_type=F32)
    if bias is not None:
        s = s + bias
    m = jnp.max(s, axis=1, keepdims=True)
    p = jnp.exp2(s - m).astype(BF16)
    pv = jnp.dot(p, v_aug, preferred_element_type=F32)
    pick = lambda t: jnp.where(left, t[:rows], t[rows:])
    return pick(pv[:, :LANES]), pick(pv[:, LANES:]), pick(jnp.broadcast_to(m, (2 * rows, LANES)))


def _merge_states(a, b):
    top = jnp.maximum(a[2], b[2])
    w_a, w_b = jnp.exp2(a[2] - top), jnp.exp2(b[2] - top)
    return w_a * a[0] + w_b * b[0], w_a * a[1] + w_b * b[1], top


def _normalised(state):
    acc, den, top = state
    return acc / den, top + jnp.log(den) * LOG2E


def _band_bias(rel):
    return jnp.where((rel >= 0) & (rel <= BLOCK), 0.0, MASKED).astype(F32)


def _fill_bias_tables(bias_ref, slabs):
    row = lax.broadcasted_iota(jnp.int32, (2 * BLOCK, 2 * BLOCK), 0) % BLOCK
    col = lax.broadcasted_iota(jnp.int32, (2 * BLOCK, 2 * BLOCK), 1)
    bias_ref[0] = _band_bias(row - col)
    bias_ref[1] = _band_bias(row - col + BLOCK)
    if bias_ref.shape[0] > 2:
        q_rows, k_rows = BLOCK // slabs, 2 * BLOCK // slabs
        rel = (slabs * (row % q_rows) + row // q_rows) - (slabs * (col % k_rows) + col // k_rows)
        bias_ref[2] = _band_bias(rel)
        bias_ref[3] = _band_bias(rel + BLOCK)


def _first_grid_step():
    return (pl.program_id(0) == 0) & (pl.program_id(1) == 0)


def _strided_kernel(qkv_ref, o_ref, lse_ref, state16_ref, bias_ref):
    @pl.when(_first_grid_step())
    def _():
        _fill_bias_tables(bias_ref, SLABS_PER_MID)

    for j in range(qkv_ref.shape[1]):
        _strided_class(qkv_ref.at[:, j], o_ref.at[:, j], lse_ref.at[:, j], state16_ref, bias_ref)


def _strided_class(qkv_ref, o_ref, lse_ref, state16_ref, bias_ref):
    slab_len = qkv_ref.shape[1]

    def lanes_of(name, pair):
        start = CLS_COLS[name].start + pair * LANES
        return slice(start, start + LANES)

    def coarse(g, blk):
        kstart = max(blk - 1, 0) * BLOCK
        keys = slice(kstart, kstart + 2 * BLOCK)
        own = slice(blk * BLOCK, (blk + 1) * BLOCK)
        for pair in range(PAIRS):
            lanes = slice(pair * LANES, (pair + 1) * LANES)
            state = _pair_attention(qkv_ref[g, own, lanes_of("q", pair)], qkv_ref[g, keys, lanes_of("k", pair)],
                                    qkv_ref[g, keys, lanes_of("v", pair)], bias_ref[min(blk, 1)])
            for part in range(3):
                state16_ref[part, g, own, lanes] = state[part]

    q_rows = BLOCK // SLABS_PER_MID
    k_rows = 2 * q_rows

    def gather(ref, rows, lanes):
        return jnp.concatenate([ref[g, rows, lanes] for g in range(SLABS_PER_MID)], axis=0)

    def mid(i):
        own = slice(i * q_rows, (i + 1) * q_rows)
        kstart = max(i - 1, 0) * q_rows
        keys = slice(kstart, kstart + k_rows)
        for pair in range(PAIRS):
            lanes = slice(pair * LANES, (pair + 1) * LANES)
            state4 = _pair_attention(gather(qkv_ref, own, lanes_of("q", pair)),
                                     gather(qkv_ref, keys, lanes_of("k", pair)),
                                     gather(qkv_ref, keys, lanes_of("v", pair)), bias_ref[2 + min(i, 1)])
            state16 = tuple(gather(state16_ref.at[part], own, lanes) for part in range(3))
            o, lse = _normalised(_merge_states(state4, state16))
            for g in range(SLABS_PER_MID):
                o_ref[g, own, lanes] = o[g * q_rows:(g + 1) * q_rows].astype(BF16)
                lse_ref[g, own, lanes] = lse[g * q_rows:(g + 1) * q_rows]

    bands = slab_len // BLOCK
    per_band = BLOCK // q_rows
    for band in range(bands):
        for g in range(SLABS_PER_MID):
            coarse(g, band)
            if band > 0:
                for i in range(g * per_band // SLABS_PER_MID, (g + 1) * per_band // SLABS_PER_MID):
                    mid((band - 1) * per_band + i)
    for i in range(per_band):
        mid((bands - 1) * per_band + i)


def _to_sequence_order(cls_ref, slab_ref, mid_ref):
    per_class = cls_ref.shape[1]
    per_mid = per_class * SLABS_PER_MID
    n_slabs = cls_ref.shape[2] // LANES
    for s in range(n_slabs):
        lanes = slice(s * LANES, (s + 1) * LANES)
        for r4 in range(MID):
            for g in range(SLABS_PER_MID):
                mid_ref[s, pl.ds(r4 * per_mid + g, per_class, stride=SLABS_PER_MID), :] = (
                    cls_ref[r4 + MID * g, :, lanes].astype(F32))
        for r4 in range(MID):
            slab_ref[s, pl.ds(r4, per_mid, stride=MID), :] = mid_ref[s, r4 * per_mid:(r4 + 1) * per_mid]


def _mix_kernel(x_ref, seq_ref, prev_ref, ws_ref, bs_ref, oc_ref, lc_ref,
                mem_ref, mgain_ref, hgains_ref, wkv32_ref, wout32_ref, out_ref,
                oslab_ref, lslab_ref, bias_ref, wout_ref, wkv_ref, mk_ref, mv_ref, *, steps_per_batch):
    rows = x_ref.shape[0]
    step = pl.program_id(0)

    def seq(name, row_slice=slice(None), pair=None):
        cols = SEQ_COLS[name]
        if pair is not None:
            cols = slice(cols.start + pair * LANES, cols.start + (pair + 1) * LANES)
        return seq_ref[row_slice, cols]

    @pl.when(step % steps_per_batch == 0)
    def _():
        @pl.when(step == 0)
        def _():
            wkv_ref[...] = wkv32_ref[...].astype(BF16)
        mem = mem_ref[...]
        ms = jnp.mean(mem * mem, axis=-1, keepdims=True)
        hm = (mem * lax.rsqrt(ms + EPS) * mgain_ref[...]).astype(BF16)
        mk = jnp.dot(hm, wkv_ref[:, :MEM_WIDTH], preferred_element_type=F32)
        mem_k = HEAD_GAINS.index("m_k")
        mk_ref[...] = _head_rms_lanes(mk, hgains_ref[mem_k:mem_k + 1, :]).astype(BF16)
        mv_ref[...] = jnp.dot(hm, wkv_ref[:, MEM_WIDTH:], preferred_element_type=F32).astype(BF16)

    @pl.when(step == 0)
    def _():
        wout_ref[...] = wout32_ref[...].astype(BF16)
        row = lax.broadcasted_iota(jnp.int32, (2 * BLOCK, 2 * BLOCK), 0) % BLOCK
        col = lax.broadcasted_iota(jnp.int32, (2 * BLOCK, 2 * BLOCK), 1)
        band = _band_bias(row - col + BLOCK)
        bias_ref[1] = band
        bias_ref[0] = jnp.where(col >= BLOCK, band, MASKED)
    tri = (lax.broadcasted_iota(jnp.int32, (CHUNK, CHUNK), 0)
           >= lax.broadcasted_iota(jnp.int32, (CHUNK, CHUNK), 1))
    left = lax.broadcasted_iota(jnp.int32, (CHUNK, LANES), 1) < HEAD_DIM
    w_tri = [jnp.where(tri, ws_ref[h], 0.0).astype(BF16) for h in range(GMLP_HEADS)]
    sp_chunks = []
    for c in range(rows // CHUNK):
        pairs = []
        for pair in range(GMLP_WIDTH // LANES):
            vn2 = seq("g_vn", slice(c * CHUNK, (c + 1) * CHUNK), pair)
            sp_a = jnp.dot(w_tri[2 * pair], vn2, preferred_element_type=F32)
            sp_b = jnp.dot(w_tri[2 * pair + 1], vn2, preferred_element_type=F32)
            pairs.append(jnp.where(left, sp_a, sp_b))
        sp_chunks.append(jnp.concatenate(pairs, axis=1) + bs_ref[...])
    y_g = seq("g_ug").astype(F32) * jnp.concatenate(sp_chunks, axis=0)

    def memory_pair(pair):
        lanes = slice(pair * LANES, (pair + 1) * LANES)
        acc, den, _ = _pair_attention(seq("m_q", pair=pair), mk_ref[:, lanes], mv_ref[:, lanes], None)
        return acc / den

    mem_pairs = []

    _to_sequence_order(oc_ref, oslab_ref.at[0], oslab_ref.at[1])
    _to_sequence_order(lc_ref, lslab_ref.at[0], lslab_ref.at[1])
    first_bias = jnp.where(step % steps_per_batch == 0, 0, 1)
    block_outs = []
    n_blocks = rows // BLOCK
    assert MEM_AT + MEM_WIDTH // LANES <= n_blocks
    for blk in range(n_blocks):
        if MEM_AT <= blk < MEM_AT + MEM_WIDTH // LANES:
            mem_pairs.append(memory_pair(blk - MEM_AT))
        if blk == n_blocks - 1:
            y_m = jnp.concatenate(mem_pairs, axis=1) * seq("m_gate").astype(F32)
            partial = (x_ref[...]
                       + jnp.dot(y_g.astype(BF16), wout_ref[:GMLP_WIDTH, :], preferred_element_type=F32)
                       + jnp.dot(y_m.astype(BF16), wout_ref[GMLP_WIDTH + ATTN_WIDTH:, :],
                                 preferred_element_type=F32))
        own = slice(blk * BLOCK, (blk + 1) * BLOCK)
        pair_outs = []
        for pair in range(PAIRS):
            lanes = slice(pair * LANES, (pair + 1) * LANES)
            if blk == 0:
                k_lanes = slice(SEQ_COLS["k"].start + pair * LANES, SEQ_COLS["k"].start + (pair + 1) * LANES)
                v_lanes = slice(SEQ_COLS["v"].start + pair * LANES, SEQ_COLS["v"].start + (pair + 1) * LANES)
                k2 = jnp.concatenate([prev_ref[:, k_lanes], seq("k", own, pair)], axis=0)
                v2 = jnp.concatenate([prev_ref[:, v_lanes], seq("v", own, pair)], axis=0)
                bias = bias_ref[first_bias]
            else:
                window = slice((blk - 1) * BLOCK, (blk + 1) * BLOCK)
                k2, v2, bias = seq("k", window, pair), seq("v", window, pair), bias_ref[1]
            acc1, den1, top1 = _pair_attention(seq("q", own, pair), k2, v2, bias)
            o_c, top_c = oslab_ref[0, pair, own, :], lslab_ref[0, pair, own, :]
            top = jnp.maximum(top1, top_c)
            w_1, w_c = jnp.exp2(top1 - top), jnp.exp2(top_c - top)
            pair_outs.append((w_1 * acc1 + w_c * o_c) / (w_1 * den1 + w_c))
        block_outs.append(jnp.concatenate(pair_outs, axis=1))
    y_a = jnp.concatenate(block_outs, axis=0) * seq("a_gate").astype(F32)
    out_ref[...] = partial + jnp.dot(y_a.astype(BF16), wout_ref[GMLP_WIDTH:GMLP_WIDTH + ATTN_WIDTH, :],
                                     preferred_element_type=F32)


def _cparams(sem):
    return pltpu.CompilerParams(dimension_semantics=sem, vmem_limit_bytes=VMEM_LIMIT)


def _const_spec(shape):
    return pl.BlockSpec(shape, lambda *idx: (0,) * len(shape))


def _regroup_scratch(tile):
    return pltpu.VMEM((2, ATTN_WIDTH // LANES, tile, LANES), F32)


def _class_tile_spec(tile, steps_per_batch, width):
    return pl.BlockSpec((None, CLASSES, tile // CLASSES, width),
                        lambda i: (i // steps_per_batch, 0, i % steps_per_batch, 0))


def _project(x2d, batch, seq, x_gain, gv_gain, head_gains, w_in):
    rows, d_model = x2d.shape
    return pl.pallas_call(
        _proj_kernel,
        grid=(rows // PROJ_TILE,),
        in_specs=[pl.BlockSpec((PROJ_TILE, d_model), lambda i: (i, 0)),
                  _const_spec(x_gain.shape), _const_spec(gv_gain.shape), _const_spec(head_gains.shape),
                  pl.BlockSpec(w_in.shape, lambda i: (0, 0), pipeline_mode=pl.Buffered(1))],
        out_specs=[pl.BlockSpec((PROJ_TILE, SEQ_WIDTH), lambda i: (i, 0)),
                   _class_tile_spec(PROJ_TILE, seq // PROJ_TILE, CLS_WIDTH)],
        out_shape=[jax.ShapeDtypeStruct((rows, SEQ_WIDTH), BF16),
                   jax.ShapeDtypeStruct((batch, CLASSES, seq // CLASSES, CLS_WIDTH), BF16)],
        scratch_shapes=[pltpu.VMEM(w_in.shape, BF16), pltpu.VMEM((PROJ_TILE, d_model), BF16),
                        _regroup_scratch(PROJ_TILE)],
        compiler_params=_cparams(("arbitrary",)),
    )(x2d, x_gain, gv_gain, head_gains, w_in)


def _strided_attention(qkv_cls):
    batch, _, slab_len, _ = qkv_cls.shape
    spec = lambda width: pl.BlockSpec((None, SLABS_PER_MID, CLASSES_PER_STEP, slab_len, width),
                                      lambda b, r4: (b, 0, r4, 0, 0))
    split = lambda width: (batch, SLABS_PER_MID, MID, slab_len, width)
    o, lse = pl.pallas_call(
        _strided_kernel,
        grid=(batch, MID // CLASSES_PER_STEP),
        in_specs=[spec(CLS_WIDTH)],
        out_specs=[spec(ATTN_WIDTH)] * 2,
        out_shape=[jax.ShapeDtypeStruct(split(ATTN_WIDTH), BF16), jax.ShapeDtypeStruct(split(ATTN_WIDTH), F32)],
        scratch_shapes=[pltpu.VMEM((3, SLABS_PER_MID, slab_len, ATTN_WIDTH), F32),
                        pltpu.VMEM((4, 2 * BLOCK, 2 * BLOCK), F32)],
        compiler_params=_cparams(("arbitrary", "arbitrary")),
    )(qkv_cls.reshape(split(CLS_WIDTH)))
    merged = (batch, CLASSES, slab_len, ATTN_WIDTH)
    return o.reshape(merged), lse.reshape(merged)


def _mix(x2d, acts, w_s, b_full, oc, lc, mem, mem_gain, head_gains, w_kv, w_out, seq):
    rows, d_model = x2d.shape
    mem_len = mem.shape[1]
    steps_per_batch = seq // ROW_TILE
    row_spec = lambda w: pl.BlockSpec((ROW_TILE, w), lambda i: (i, 0))
    prev_spec = pl.BlockSpec((BLOCK, SEQ_COLS["v"].stop),
                             lambda i: (jnp.maximum(i * (ROW_TILE // BLOCK) - 1, 0), 0))
    mem_spec = pl.BlockSpec((None,) + mem.shape[1:], lambda i: (i // steps_per_batch, 0, 0))
    weight_spec = lambda w: pl.BlockSpec(w.shape, lambda i: (0, 0), pipeline_mode=pl.Buffered(1))
    cls_spec = _class_tile_spec(ROW_TILE, steps_per_batch, ATTN_WIDTH)
    return pl.pallas_call(
        functools.partial(_mix_kernel, steps_per_batch=steps_per_batch),
        grid=(rows // ROW_TILE,),
        in_specs=[row_spec(d_model), row_spec(SEQ_WIDTH), prev_spec,
                  _const_spec(w_s.shape), _const_spec(b_full.shape), cls_spec, cls_spec,
                  mem_spec, _const_spec(mem_gain.shape), _const_spec(head_gains.shape),
                  weight_spec(w_kv), weight_spec(w_out)],
        out_specs=row_spec(d_model),
        out_shape=jax.ShapeDtypeStruct((rows, d_model), F32),
        scratch_shapes=[_regroup_scratch(ROW_TILE)] * 2
                       + [pltpu.VMEM((2, 2 * BLOCK, 2 * BLOCK), F32), pltpu.VMEM(w_out.shape, BF16),
                          pltpu.VMEM(w_kv.shape, BF16)] + [pltpu.VMEM((mem_len, MEM_WIDTH), BF16)] * 2,
        compiler_params=_cparams(("arbitrary",)),
    )(x2d, acts, acts, w_s, b_full, oc, lc, mem, mem_gain, head_gains, w_kv, w_out)


def kernel(x, mem, norm_gain, w_in, gmlp_v_gain, gmlp_w_s, gmlp_b, attn_q_gain, attn_k_gain,
           mem_norm_gain, w_mem_kv, mem_q_gain, mem_k_gain, w_out):
    batch, seq, d_model = x.shape
    depth = w_in.shape[0]
    assert DILATED_CONFIGS == ((BLOCK, 1), (BLOCK * MID, MID), (BLOCK * CLASSES, CLASSES))
    assert seq % PROJ_TILE == 0 and seq % ROW_TILE == 0 and seq % (BLOCK * CLASSES) == 0
    x2d = x.reshape(batch * seq, d_model)
    for l in range(depth):
        row = lambda g: g.reshape(1, -1).astype(F32)
        shared = dict(a_q=attn_q_gain[l], a_k=attn_k_gain[l], m_q=mem_q_gain[l], m_k=mem_k_gain[l])
        head_gains = jnp.tile(jnp.stack([shared[name] for name in HEAD_GAINS]).astype(F32), (1, 2))
        acts, qkv_cls = _project(x2d, batch, seq, row(norm_gain[l]), row(gmlp_v_gain[l]), head_gains,
                                 w_in[l])
        oc, lc = _strided_attention(qkv_cls)
        b_full = jnp.repeat(gmlp_b[l].T, HEAD_DIM, axis=1)
        x2d = _mix(x2d, acts, gmlp_w_s[l], b_full, oc, lc, mem, row(mem_norm_gain[l]), head_gains,
                   w_mem_kv[l], w_out[l], seq)
    return x2d.reshape(batch, seq, d_model)
```

```python
import functools
import math

import jax
import jax.numpy as jnp
from jax import lax
from jax.experimental import pallas as pl
from jax.experimental.pallas import tpu as pltpu

HEAD_DIM = 64
GMLP_HEADS = 4
ATTN_HEADS = 8
MEM_HEADS = 4
GMLP_WIDTH = GMLP_HEADS * HEAD_DIM
ATTN_WIDTH = ATTN_HEADS * HEAD_DIM
MEM_WIDTH = MEM_HEADS * HEAD_DIM
CHUNK = 128
BLOCK = 128
DILATED_CONFIGS = ((128, 1), (512, 4), (2048, 16))
EPS = 1e-6
MASKED = -1e30
LOG2E = math.log2(math.e)
LOGIT_SCALE = LOG2E / math.sqrt(HEAD_DIM)

IN_SEGMENTS = (("g_u", GMLP_WIDTH), ("g_v", GMLP_WIDTH), ("g_gate", GMLP_WIDTH),
               ("a_q", ATTN_WIDTH), ("a_k", ATTN_WIDTH), ("a_v", ATTN_WIDTH), ("a_gate", ATTN_WIDTH),
               ("m_q", MEM_WIDTH), ("m_gate", MEM_WIDTH))


def _column_slices(segments):
    slices, start = {}, 0
    for name, width in segments:
        slices[name] = slice(start, start + width)
        start += width
    return slices, start


IN_COLS, _ = _column_slices(IN_SEGMENTS)
SEQ_COLS, SEQ_WIDTH = _column_slices((("k", ATTN_WIDTH), ("v", ATTN_WIDTH), ("q", ATTN_WIDTH),
                                      ("a_gate", ATTN_WIDTH), ("g_ug", GMLP_WIDTH), ("g_vn", GMLP_WIDTH),
                                      ("m_q", MEM_WIDTH), ("m_gate", MEM_WIDTH)))
CLS_COLS, CLS_WIDTH = _column_slices((("q", ATTN_WIDTH), ("k", ATTN_WIDTH), ("v", ATTN_WIDTH)))

HEAD_GAINS = ("a_q", "a_k", "m_q", "m_k")

CLASSES = 16
MID = 4
SLABS_PER_MID = CLASSES // MID
LANES = 128
PAIRS = ATTN_WIDTH // LANES
PROJ_TILE = 1024
ROW_TILE = 512
CLASSES_PER_STEP = 1
MEM_AT = 1
VMEM_LIMIT = 56 * 1024 * 1024

F32 = jnp.float32
BF16 = jnp.bfloat16


def _head_rms_lanes(acc, gain, scale=1.0):
    left = lax.broadcasted_iota(jnp.int32, (acc.shape[0], LANES), 1) < HEAD_DIM
    gain = gain * (math.sqrt(HEAD_DIM) * scale)
    groups = []
    for j in range(acc.shape[1] // LANES):
        a = acc[:, j * LANES:(j + 1) * LANES]
        sq = a * a
        ss_a = jnp.sum(jnp.where(left, sq, 0.0), axis=1, keepdims=True)
        ss_b = jnp.sum(jnp.where(left, 0.0, sq), axis=1, keepdims=True)
        inv = jnp.where(left, lax.rsqrt(ss_a + HEAD_DIM * EPS), lax.rsqrt(ss_b + HEAD_DIM * EPS))
        group_gain = gain if gain.shape[1] == LANES else gain[:, j * LANES:(j + 1) * LANES]
        groups.append(a * inv * group_gain)
    return jnp.concatenate(groups, axis=1)


def _silu(x):
    half = 0.5 * x
    return half + half * jnp.tanh(half)


def _store_both_layouts(val, name, seq_ref, cls_ref, work_ref):
    seq_ref[:, SEQ_COLS[name]] = val.astype(BF16)
    slab_ref, mid_ref = work_ref.at[0], work_ref.at[1]
    rows = val.shape[0]
    per_mid, per_class = rows // MID, rows // CLASSES
    for s in range(val.shape[1] // LANES):
        slab_ref[s] = val[:, s * LANES:(s + 1) * LANES]
        out_lanes = slice(CLS_COLS[name].start + s * LANES, CLS_COLS[name].start + (s + 1) * LANES)
        for r4 in range(MID):
            mid_ref[s, r4 * per_mid:(r4 + 1) * per_mid] = slab_ref[s, pl.ds(r4, per_mid, stride=MID), :]
        for r4 in range(MID):
            for g in range(SLABS_PER_MID):
                cls_ref[r4 + MID * g, :, out_lanes] = (
                    mid_ref[s, pl.ds(r4 * per_mid + g, per_class, stride=SLABS_PER_MID), :].astype(BF16))


def _proj_kernel(x_ref, xgain_ref, gvgain_ref, hgains_ref, w32_ref, seq_ref, cls_ref, w_ref, h_ref, work_ref):
    @pl.when(pl.program_id(0) == 0)
    def _():
        w_ref[...] = w32_ref[...].astype(BF16)

    def gain(name):
        if name == "g_v":
            return gvgain_ref[...]
        return hgains_ref[HEAD_GAINS.index(name):HEAD_GAINS.index(name) + 1, :]

    x = x_ref[...]
    ms = jnp.mean(x * x, axis=-1, keepdims=True)
    h_ref[...] = (x * lax.rsqrt(ms + EPS) * xgain_ref[...]).astype(BF16)

    def project(name):
        return jnp.dot(h_ref[...], w_ref[:, IN_COLS[name]], preferred_element_type=F32)

    def put(name, val):
        seq_ref[:, SEQ_COLS[name]] = val.astype(BF16)

    a_q = project("a_q")
    a_k = project("a_k")
    _store_both_layouts(_head_rms_lanes(a_q, gain("a_q"), LOGIT_SCALE), "q", seq_ref, cls_ref, work_ref)
    a_v = project("a_v")
    _store_both_layouts(_head_rms_lanes(a_k, gain("a_k")), "k", seq_ref, cls_ref, work_ref)
    a_gate = project("a_gate")
    _store_both_layouts(a_v, "v", seq_ref, cls_ref, work_ref)
    m_q = project("m_q")
    put("a_gate", _silu(a_gate))
    g_u = project("g_u")
    put("m_q", _head_rms_lanes(m_q, gain("m_q"), LOGIT_SCALE))
    g_v = project("g_v")
    g_gate = project("g_gate")
    put("g_vn", _head_rms_lanes(g_v, gain("g_v")))
    m_gate = project("m_gate")
    put("g_ug", g_u * _silu(g_gate))
    put("m_gate", _silu(m_gate))


def _pair_attention(q2, k2, v2, bias):
    rows = q2.shape[0]
    left = lax.broadcasted_iota(jnp.int32, (rows, LANES), 1) < HEAD_DIM
    zeros = jnp.zeros_like(q2)
    q_stack = jnp.concatenate([jnp.where(left, q2, zeros), jnp.where(left, zeros, q2)], axis=0)
    v_aug = jnp.concatenate([v2, jnp.ones_like(v2)], axis=1)
    s = lax.dot_general(q_stack, k2, (((1,), (1,)), ((), ())), preferred_element_type=F32)
    if bias is not None:
        s = s + bias
    m = jnp.max(s, axis=1, keepdims=True)
    p = jnp.exp2(s - m).astype(BF16)
    pv = jnp.dot(p, v_aug, preferred_element_type=F32)
    pick = lambda t: jnp.where(left, t[:rows], t[rows:])
    return pick(pv[:, :LANES]), pick(pv[:, LANES:]), pick(jnp.broadcast_to(m, (2 * rows, LANES)))


def _merge_states(a, b):
    top = jnp.maximum(a[2], b[2])
    w_a, w_b = jnp.exp2(a[2] - top), jnp.exp2(b[2] - top)
    return w_a * a[0] + w_b * b[0], w_a * a[1] + w_b * b[1], top


def _normalised(state):
    acc, den, top = state
    return acc / den, top + jnp.log(den) * LOG2E


def _band_bias(rel):
    return jnp.where((rel >= 0) & (rel <= BLOCK), 0.0, MASKED).astype(F32)


def _fill_bias_tables(bias_ref, slabs):
    row = lax.broadcasted_iota(jnp.int32, (2 * BLOCK, 2 * BLOCK), 0) % BLOCK
    col = lax.broadcasted_iota(jnp.int32, (2 * BLOCK, 2 * BLOCK), 1)
    bias_ref[0] = _band_bias(row - col)
    bias_ref[1] = _band_bias(row - col + BLOCK)
    if bias_ref.shape[0] > 2:
        q_rows, k_rows = BLOCK // slabs, 2 * BLOCK // slabs
        rel = (slabs * (row % q_rows) + row // q_rows) - (slabs * (col % k_rows) + col // k_rows)
        bias_ref[2] = _band_bias(rel)
        bias_ref[3] = _band_bias(rel + BLOCK)


def _first_grid_step():
    return (pl.program_id(0) == 0) & (pl.program_id(1) == 0)


def _strided_kernel(qkv_ref, o_ref, lse_ref, state16_ref, bias_ref):
    @pl.when(_first_grid_step())
    def _():
        _fill_bias_tables(bias_ref, SLABS_PER_MID)

    for j in range(qkv_ref.shape[1]):
        _strided_class(qkv_ref.at[:, j], o_ref.at[:, j], lse_ref.at[:, j], state16_ref, bias_ref)


def _strided_class(qkv_ref, o_ref, lse_ref, state16_ref, bias_ref):
    slab_len = qkv_ref.shape[1]

    def lanes_of(name, pair):
        start = CLS_COLS[name].start + pair * LANES
        return slice(start, start + LANES)

    def coarse(g, blk):
        kstart = max(blk - 1, 0) * BLOCK
        keys = slice(kstart, kstart + 2 * BLOCK)
        own = slice(blk * BLOCK, (blk + 1) * BLOCK)
        for pair in range(PAIRS):
            lanes = slice(pair * LANES, (pair + 1) * LANES)
            state = _pair_attention(qkv_ref[g, own, lanes_of("q", pair)], qkv_ref[g, keys, lanes_of("k", pair)],
                                    qkv_ref[g, keys, lanes_of("v", pair)], bias_ref[min(blk, 1)])
            for part in range(3):
                state16_ref[part, g, own, lanes] = state[part]

    q_rows = BLOCK // SLABS_PER_MID
    k_rows = 2 * q_rows

    def gather(ref, rows, lanes):
        return jnp.concatenate([ref[g, rows, lanes] for g in range(SLABS_PER_MID)], axis=0)

    def mid(i):
        own = slice(i * q_rows, (i + 1) * q_rows)
        kstart = max(i - 1, 0) * q_rows
        keys = slice(kstart, kstart + k_rows)
        for pair in range(PAIRS):
            lanes = slice(pair * LANES, (pair + 1) * LANES)
            state4 = _pair_attention(gather(qkv_ref, own, lanes_of("q", pair)),
                                     gather(qkv_ref, keys, lanes_of("k", pair)),
                                     gather(qkv_ref, keys, lanes_of("v", pair)), bias_ref[2 + min(i, 1)])
            state16 = tuple(gather(state16_ref.at[part], own, lanes) for part in range(3))
            o, lse = _normalised(_merge_states(state4, state16))
            for g in range(SLABS_PER_MID):
                o_ref[g, own, lanes] = o[g * q_rows:(g + 1) * q_rows].astype(BF16)
                lse_ref[g, own, lanes] = lse[g * q_rows:(g + 1) * q_rows]

    bands = slab_len // BLOCK
    per_band = BLOCK // q_rows
    for band in range(bands):
        for g in range(SLABS_PER_MID):
            coarse(g, band)
            if band > 0:
                for i in range(g * per_band // SLABS_PER_MID, (g + 1) * per_band // SLABS_PER_MID):
                    mid((band - 1) * per_band + i)
    for i in range(per_band):
        mid((bands - 1) * per_band + i)


def _to_sequence_order(cls_ref, slab_ref, mid_ref):
    per_class = cls_ref.shape[1]
    per_mid = per_class * SLABS_PER_MID
    n_slabs = cls_ref.shape[2] // LANES
    for s in range(n_slabs):
        lanes = slice(s * LANES, (s + 1) * LANES)
        for r4 in range(MID):
            for g in range(SLABS_PER_MID):
                mid_ref[s, pl.ds(r4 * per_mid + g, per_class, stride=SLABS_PER_MID), :] = (
                    cls_ref[r4 + MID * g, :, lanes].astype(F32))
        for r4 in range(MID):
            slab_ref[s, pl.ds(r4, per_mid, stride=MID), :] = mid_ref[s, r4 * per_mid:(r4 + 1) * per_mid]


def _mix_kernel(x_ref, seq_ref, prev_ref, ws_ref, bs_ref, oc_ref, lc_ref,
                mem_ref, mgain_ref, hgains_ref, wkv32_ref, wout32_ref, out_ref,
                oslab_ref, lslab_ref, bias_ref, wout_ref, wkv_ref, mk_ref, mv_ref, *, steps_per_batch):
    rows = x_ref.shape[0]
    step = pl.program_id(0)

    def seq(name, row_slice=slice(None), pair=None):
        cols = SEQ_COLS[name]
        if pair is not None:
            cols = slice(cols.start + pair * LANES, cols.start + (pair + 1) * LANES)
        return seq_ref[row_slice, cols]

    @pl.when(step % steps_per_batch == 0)
    def _():
        @pl.when(step == 0)
        def _():
            wkv_ref[...] = wkv32_ref[...].astype(BF16)
        mem = mem_ref[...]
        ms = jnp.mean(mem * mem, axis=-1, keepdims=True)
        hm = (mem * lax.rsqrt(ms + EPS) * mgain_ref[...]).astype(BF16)
        mk = jnp.dot(hm, wkv_ref[:, :MEM_WIDTH], preferred_element_type=F32)
        mem_k = HEAD_GAINS.index("m_k")
        mk_ref[...] = _head_rms_lanes(mk, hgains_ref[mem_k:mem_k + 1, :]).astype(BF16)
        mv_ref[...] = jnp.dot(hm, wkv_ref[:, MEM_WIDTH:], preferred_element_type=F32).astype(BF16)

    @pl.when(step == 0)
    def _():
        wout_ref[...] = wout32_ref[...].astype(BF16)
        row = lax.broadcasted_iota(jnp.int32, (2 * BLOCK, 2 * BLOCK), 0) % BLOCK
        col = lax.broadcasted_iota(jnp.int32, (2 * BLOCK, 2 * BLOCK), 1)
        band = _band_bias(row - col + BLOCK)
        bias_ref[1] = band
        bias_ref[0] = jnp.where(col >= BLOCK, band, MASKED)
    tri = (lax.broadcasted_iota(jnp.int32, (CHUNK, CHUNK), 0)
           >= lax.broadcasted_iota(jnp.int32, (CHUNK, CHUNK), 1))
    left = lax.broadcasted_iota(jnp.int32, (CHUNK, LANES), 1) < HEAD_DIM
    w_tri = [jnp.where(tri, ws_ref[h], 0.0).astype(BF16) for h in range(GMLP_HEADS)]
    sp_chunks = []
    for c in range(rows // CHUNK):
        pairs = []
        for pair in range(GMLP_WIDTH // LANES):
            vn2 = seq("g_vn", slice(c * CHUNK, (c + 1) * CHUNK), pair)
            sp_a = jnp.dot(w_tri[2 * pair], vn2, preferred_element_type=F32)
            sp_b = jnp.dot(w_tri[2 * pair + 1], vn2, preferred_element_type=F32)
            pairs.append(jnp.where(left, sp_a, sp_b))
        sp_chunks.append(jnp.concatenate(pairs, axis=1) + bs_ref[...])
    y_g = seq("g_ug").astype(F32) * jnp.concatenate(sp_chunks, axis=0)

    def memory_pair(pair):
        lanes = slice(pair * LANES, (pair + 1) * LANES)
        acc, den, _ = _pair_attention(seq("m_q", pair=pair), mk_ref[:, lanes], mv_ref[:, lanes], None)
        return acc / den

    mem_pairs = []

    _to_sequence_order(oc_ref, oslab_ref.at[0], oslab_ref.at[1])
    _to_sequence_order(lc_ref, lslab_ref.at[0], lslab_ref.at[1])
    first_bias = jnp.where(step % steps_per_batch == 0, 0, 1)
    block_outs = []
    n_blocks = rows // BLOCK
    assert MEM_AT + MEM_WIDTH // LANES <= n_blocks

    def project_attention(done):
        blocks = block_outs[done.start // BLOCK:done.stop // BLOCK]
        y_a = jnp.concatenate(blocks, axis=0) * seq("a_gate", done).astype(F32)
        out_ref[done, :] = partial[done, :] + jnp.dot(
            y_a.astype(BF16), wout_ref[GMLP_WIDTH:GMLP_WIDTH + ATTN_WIDTH, :], preferred_element_type=F32)

    for blk in range(n_blocks):
        if MEM_AT <= blk < MEM_AT + MEM_WIDTH // LANES:
            mem_pairs.append(memory_pair(blk - MEM_AT))
        if blk == n_blocks - 1:
            y_m = jnp.concatenate(mem_pairs, axis=1) * seq("m_gate").astype(F32)
            partial = (x_ref[...]
                       + jnp.dot(y_g.astype(BF16), wout_ref[:GMLP_WIDTH, :], preferred_element_type=F32)
                       + jnp.dot(y_m.astype(BF16), wout_ref[GMLP_WIDTH + ATTN_WIDTH:, :],
                                 preferred_element_type=F32))
            project_attention(slice(0, rows // 2))
        own = slice(blk * BLOCK, (blk + 1) * BLOCK)
        pair_outs = []
        for pair in range(PAIRS):
            lanes = slice(pair * LANES, (pair + 1) * LANES)
            if blk == 0:
                k_lanes = slice(SEQ_COLS["k"].start + pair * LANES, SEQ_COLS["k"].start + (pair + 1) * LANES)
                v_lanes = slice(SEQ_COLS["v"].start + pair * LANES, SEQ_COLS["v"].start + (pair + 1) * LANES)
                k2 = jnp.concatenate([prev_ref[:, k_lanes], seq("k", own, pair)], axis=0)
                v2 = jnp.concatenate([prev_ref[:, v_lanes], seq("v", own, pair)], axis=0)
                bias = bias_ref[first_bias]
            else:
                window = slice((blk - 1) * BLOCK, (blk + 1) * BLOCK)
                k2, v2, bias = seq("k", window, pair), seq("v", window, pair), bias_ref[1]
            acc1, den1, top1 = _pair_attention(seq("q", own, pair), k2, v2, bias)
            o_c, top_c = oslab_ref[0, pair, own, :], lslab_ref[0, pair, own, :]
            top = jnp.maximum(top1, top_c)
            w_1, w_c = jnp.exp2(top1 - top), jnp.exp2(top_c - top)
            pair_outs.append((w_1 * acc1 + w_c * o_c) / (w_1 * den1 + w_c))
        block_outs.append(jnp.concatenate(pair_outs, axis=1))
    project_attention(slice(rows // 2, rows))


def _cparams(sem):
    return pltpu.CompilerParams(dimension_semantics=sem, vmem_limit_bytes=VMEM_LIMIT)


def _const_spec(shape):
    return pl.BlockSpec(shape, lambda *idx: (0,) * len(shape))


def _regroup_scratch(tile):
    return pltpu.VMEM((2, ATTN_WIDTH // LANES, tile, LANES), F32)


def _class_tile_spec(tile, steps_per_batch, width):
    return pl.BlockSpec((None, CLASSES, tile // CLASSES, width),
                        lambda i: (i // steps_per_batch, 0, i % steps_per_batch, 0))


def _project(x2d, batch, seq, x_gain, gv_gain, head_gains, w_in):
    rows, d_model = x2d.shape
    return pl.pallas_call(
        _proj_kernel,
        grid=(rows // PROJ_TILE,),
        in_specs=[pl.BlockSpec((PROJ_TILE, d_model), lambda i: (i, 0)),
                  _const_spec(x_gain.shape), _const_spec(gv_gain.shape), _const_spec(head_gains.shape),
                  pl.BlockSpec(w_in.shape, lambda i: (0, 0), pipeline_mode=pl.Buffered(1))],
        out_specs=[pl.BlockSpec((PROJ_TILE, SEQ_WIDTH), lambda i: (i, 0)),
                   _class_tile_spec(PROJ_TILE, seq // PROJ_TILE, CLS_WIDTH)],
        out_shape=[jax.ShapeDtypeStruct((rows, SEQ_WIDTH), BF16),
                   jax.ShapeDtypeStruct((batch, CLASSES, seq // CLASSES, CLS_WIDTH), BF16)],
        scratch_shapes=[pltpu.VMEM(w_in.shape, BF16), pltpu.VMEM((PROJ_TILE, d_model), BF16),
                        _regroup_scratch(PROJ_TILE)],
        compiler_params=_cparams(("arbitrary",)),
    )(x2d, x_gain, gv_gain, head_gains, w_in)


def _strided_attention(qkv_cls):
    batch, _, slab_len, _ = qkv_cls.shape
    spec = lambda width: pl.BlockSpec((None, SLABS_PER_MID, CLASSES_PER_STEP, slab_len, width),
                                      lambda b, r4: (b, 0, r4, 0, 0))
    split = lambda width: (batch, SLABS_PER_MID, MID, slab_len, width)
    o, lse = pl.pallas_call(
        _strided_kernel,
        grid=(batch, MID // CLASSES_PER_STEP),
        in_specs=[spec(CLS_WIDTH)],
        out_specs=[spec(ATTN_WIDTH)] * 2,
        out_shape=[jax.ShapeDtypeStruct(split(ATTN_WIDTH), BF16), jax.ShapeDtypeStruct(split(ATTN_WIDTH), F32)],
        scratch_shapes=[pltpu.VMEM((3, SLABS_PER_MID, slab_len, ATTN_WIDTH), F32),
                        pltpu.VMEM((4, 2 * BLOCK, 2 * BLOCK), F32)],
        compiler_params=_cparams(("arbitrary", "arbitrary")),
    )(qkv_cls.reshape(split(CLS_WIDTH)))
    merged = (batch, CLASSES, slab_len, ATTN_WIDTH)
    return o.reshape(merged), lse.reshape(merged)


def _mix(x2d, acts, w_s, b_full, oc, lc, mem, mem_gain, head_gains, w_kv, w_out, seq):
    rows, d_model = x2d.shape
    mem_len = mem.shape[1]
    steps_per_batch = seq // ROW_TILE
    row_spec = lambda w: pl.BlockSpec((ROW_TILE, w), lambda i: (i, 0))
    prev_spec = pl.BlockSpec((BLOCK, SEQ_COLS["v"].stop),
                             lambda i: (jnp.maximum(i * (ROW_TILE // BLOCK) - 1, 0), 0))
    mem_spec = pl.BlockSpec((None,) + mem.shape[1:], lambda i: (i // steps_per_batch, 0, 0))
    weight_spec = lambda w: pl.BlockSpec(w.shape, lambda i: (0, 0), pipeline_mode=pl.Buffered(1))
    cls_spec = _class_tile_spec(ROW_TILE, steps_per_batch, ATTN_WIDTH)
    return pl.pallas_call(
        functools.partial(_mix_kernel, steps_per_batch=steps_per_batch),
        grid=(rows // ROW_TILE,),
        in_specs=[row_spec(d_model), row_spec(SEQ_WIDTH), prev_spec,
                  _const_spec(w_s.shape), _const_spec(b_full.shape), cls_spec, cls_spec,
                  mem_spec, _const_spec(mem_gain.shape), _const_spec(head_gains.shape),
                  weight_spec(w_kv), weight_spec(w_out)],
        out_specs=row_spec(d_model),
        out_shape=jax.ShapeDtypeStruct((rows, d_model), F32),
        scratch_shapes=[_regroup_scratch(ROW_TILE)] * 2
                       + [pltpu.VMEM((2, 2 * BLOCK, 2 * BLOCK), F32), pltpu.VMEM(w_out.shape, BF16),
                          pltpu.VMEM(w_kv.shape, BF16)] + [pltpu.VMEM((mem_len, MEM_WIDTH), BF16)] * 2,
        compiler_params=_cparams(("arbitrary",)),
    )(x2d, acts, acts, w_s, b_full, oc, lc, mem, mem_gain, head_gains, w_kv, w_out)


def kernel(x, mem, norm_gain, w_in, gmlp_v_gain, gmlp_w_s, gmlp_b, attn_q_gain, attn_k_gain,
           mem_norm_gain, w_mem_kv, mem_q_gain, mem_k_gain, w_out):
    batch, seq, d_model = x.shape
    depth = w_in.shape[0]
    assert DILATED_CONFIGS == ((BLOCK, 1), (BLOCK * MID, MID), (BLOCK * CLASSES, CLASSES))
    assert seq % PROJ_TILE == 0 and seq % ROW_TILE == 0 and seq % (BLOCK * CLASSES) == 0
    x2d = x.reshape(batch * seq, d_model)
    for l in range(depth):
        row = lambda g: g.reshape(1, -1).astype(F32)
        shared = dict(a_q=attn_q_gain[l], a_k=attn_k_gain[l], m_q=mem_q_gain[l], m_k=mem_k_gain[l])
        head_gains = jnp.tile(jnp.stack([shared[name] for name in HEAD_GAINS]).astype(F32), (1, 2))
        acts, qkv_cls = _project(x2d, batch, seq, row(norm_gain[l]), row(gmlp_v_gain[l]), head_gains,
                                 w_in[l])
        oc, lc = _strided_attention(qkv_cls)
        b_full = jnp.repeat(gmlp_b[l].T, HEAD_DIM, axis=1)
        x2d = _mix(x2d, acts, gmlp_w_s[l], b_full, oc, lc, mem, row(mem_norm_gain[l]), head_gains,
                   w_mem_kv[l], w_out[l], seq)
    return x2d.reshape(batch, seq, d_model)
```

```python
import functools
import math

import jax
import jax.numpy as jnp
from jax import lax
from jax.experimental import pallas as pl
from jax.experimental.pallas import tpu as pltpu

HEAD_DIM = 64
GMLP_HEADS = 4
ATTN_HEADS = 8
MEM_HEADS = 4
GMLP_WIDTH = GMLP_HEADS * HEAD_DIM
ATTN_WIDTH = ATTN_HEADS * HEAD_DIM
MEM_WIDTH = MEM_HEADS * HEAD_DIM
CHUNK = 128
BLOCK = 128
DILATED_CONFIGS = ((128, 1), (512, 4), (2048, 16))
EPS = 1e-6
MASKED = -1e30
LOG2E = math.log2(math.e)
LOGIT_SCALE = LOG2E / math.sqrt(HEAD_DIM)

IN_SEGMENTS = (("g_u", GMLP_WIDTH), ("g_v", GMLP_WIDTH), ("g_gate", GMLP_WIDTH),
               ("a_q", ATTN_WIDTH), ("a_k", ATTN_WIDTH), ("a_v", ATTN_WIDTH), ("a_gate", ATTN_WIDTH),
               ("m_q", MEM_WIDTH), ("m_gate", MEM_WIDTH))


def _column_slices(segments):
    slices, start = {}, 0
    for name, width in segments:
        slices[name] = slice(start, start + width)
        start += width
    return slices, start


IN_COLS, _ = _column_slices(IN_SEGMENTS)
SEQ_COLS, SEQ_WIDTH = _column_slices((("k", ATTN_WIDTH), ("v", ATTN_WIDTH), ("q", ATTN_WIDTH),
                                      ("a_gate", ATTN_WIDTH), ("g_ug", GMLP_WIDTH), ("g_vn", GMLP_WIDTH),
                                      ("m_q", MEM_WIDTH), ("m_gate", MEM_WIDTH)))
CLS_COLS, CLS_WIDTH = _column_slices((("q", ATTN_WIDTH), ("k", ATTN_WIDTH), ("v", ATTN_WIDTH)))

HEAD_GAINS = ("a_q", "a_k", "m_q", "m_k")

CLASSES = 16
MID = 4
SLABS_PER_MID = CLASSES // MID
LANES = 128
PAIRS = ATTN_WIDTH // LANES
PROJ_TILE = 1024
NORM_CHUNKS = 4
ROW_TILE = 512
CLASSES_PER_STEP = 1
MEM_AT = 1
VMEM_LIMIT = 58 * 1024 * 1024

F32 = jnp.float32
BF16 = jnp.bfloat16


def _head_rms_lanes(acc, gain):
    left = lax.broadcasted_iota(jnp.int32, (acc.shape[0], LANES), 1) < HEAD_DIM
    groups = []
    for j in range(acc.shape[1] // LANES):
        a = acc[:, j * LANES:(j + 1) * LANES]
        sq = a * a
        ss_a = jnp.sum(jnp.where(left, sq, 0.0), axis=1, keepdims=True)
        ss_b = jnp.sum(jnp.where(left, 0.0, sq), axis=1, keepdims=True)
        inv = jnp.where(left, lax.rsqrt(ss_a * (1.0 / HEAD_DIM) + EPS),
                        lax.rsqrt(ss_b * (1.0 / HEAD_DIM) + EPS))
        group_gain = gain if gain.shape[1] == LANES else gain[:, j * LANES:(j + 1) * LANES]
        groups.append(a * inv * group_gain)
    return jnp.concatenate(groups, axis=1)


def _silu(x):
    half = 0.5 * x
    return half + half * jnp.tanh(half)


def _store_both_layouts(val, name, seq_ref, cls_ref, work_ref):
    seq_ref[:, SEQ_COLS[name]] = val.astype(BF16)
    slab_ref, mid_ref = work_ref.at[0], work_ref.at[1]
    rows = val.shape[0]
    per_mid, per_class = rows // MID, rows // CLASSES
    for s in range(val.shape[1] // LANES):
        slab_ref[s] = val[:, s * LANES:(s + 1) * LANES]
        out_lanes = slice(CLS_COLS[name].start + s * LANES, CLS_COLS[name].start + (s + 1) * LANES)
        for r4 in range(MID):
            mid_ref[s, r4 * per_mid:(r4 + 1) * per_mid] = slab_ref[s, pl.ds(r4, per_mid, stride=MID), :]
        for r4 in range(MID):
            for g in range(SLABS_PER_MID):
                cls_ref[r4 + MID * g, :, out_lanes] = (
                    mid_ref[s, pl.ds(r4 * per_mid + g, per_class, stride=SLABS_PER_MID), :].astype(BF16))


def _proj_kernel(x_ref, xgain_ref, gvgain_ref, hgains_ref, w32_ref, seq_ref, cls_ref, w_ref, ha_ref, hb_ref,
                 work_ref):
    step = pl.program_id(0)
    chunk = x_ref.shape[0] // NORM_CHUNKS

    def normalised(x):
        ms = jnp.mean(x * x, axis=-1, keepdims=True)
        return (x * lax.rsqrt(ms + EPS) * xgain_ref[...]).astype(BF16)

    @pl.when(step == 0)
    def _():
        w_ref[...] = w32_ref[...].astype(BF16)
        ha_ref[...] = normalised(x_ref[...])

    def project_tile(cur_ref, next_ref):
        def gain(name):
            if name == "g_v":
                return gvgain_ref[...]
            return hgains_ref[HEAD_GAINS.index(name):HEAD_GAINS.index(name) + 1, :]

        def project(name):
            return jnp.dot(cur_ref[...], w_ref[:, IN_COLS[name]], preferred_element_type=F32)

        def put(name, val):
            seq_ref[:, SEQ_COLS[name]] = val.astype(BF16)

        def normalise_next(c, after):
            rows = slice(c * chunk, (c + 1) * chunk)
            tie = lax.bitcast_convert_type(after[rows, :LANES], jnp.uint32)
            tie = (tie >> 16) >> 16
            words = lax.bitcast_convert_type(x_ref[rows, :], jnp.uint32)
            words = words | jnp.concatenate([tie] * (words.shape[1] // LANES), axis=1)
            next_ref[rows, :] = normalised(lax.bitcast_convert_type(words, F32))

        a_q = project("a_q")
        a_k = project("a_k")
        _store_both_layouts(_head_rms_lanes(a_q, gain("a_q")) * LOGIT_SCALE, "q", seq_ref, cls_ref, work_ref)
        a_v = project("a_v")
        _store_both_layouts(_head_rms_lanes(a_k, gain("a_k")), "k", seq_ref, cls_ref, work_ref)
        a_gate = project("a_gate")
        _store_both_layouts(a_v, "v", seq_ref, cls_ref, work_ref)
        m_q = project("m_q")
        put("a_gate", _silu(a_gate))
        g_u = project("g_u")
        put("m_q", _head_rms_lanes(m_q, gain("m_q")) * LOGIT_SCALE)
        normalise_next(0, a_gate)
        g_v = project("g_v")
        normalise_next(1, m_q)
        g_gate = project("g_gate")
        normalise_next(2, g_u)
        put("g_vn", _head_rms_lanes(g_v, gain("g_v")))
        m_gate = project("m_gate")
        normalise_next(3, g_v)
        put("g_ug", g_u * _silu(g_gate))
        put("m_gate", _silu(m_gate))

    pl.when(step % 2 == 1)(lambda: project_tile(ha_ref, hb_ref))
    pl.when((step > 0) & (step % 2 == 0))(lambda: project_tile(hb_ref, ha_ref))


def _pair_attention(q2, k2, v2, bias):
    rows = q2.shape[0]
    left = lax.broadcasted_iota(jnp.int32, (rows, LANES), 1) < HEAD_DIM
    zeros = jnp.zeros_like(q2)
    q_stack = jnp.concatenate([jnp.where(left, q2, zeros), jnp.where(left, zeros, q2)], axis=0)
    v_aug = jnp.concatenate([v2, jnp.ones_like(v2)], axis=1)
    s = lax.dot_general(q_stack, k2, (((1,), (1,)), ((), ())), preferred_element_type=F32)
    if bias is not None:
        s = s + bias
    m = jnp.max(s, axis=1, keepdims=True)
    p = jnp.exp2(s - m).astype(BF16)
    pv = jnp.dot(p, v_aug, preferred_element_type=F32)
    pick = lambda t: jnp.where(left, t[:rows], t[rows:])
    return pick(pv[:, :LANES]), pick(pv[:, LANES:]), pick(jnp.broadcast_to(m, (2 * rows, LANES)))


def _merge_states(a, b):
    top = jnp.maximum(a[2], b[2])
    w_a, w_b = jnp.exp2(a[2] - top), jnp.exp2(b[2] - top)
    return w_a * a[0] + w_b * b[0], w_a * a[1] + w_b * b[1], top


def _normalised(state):
    acc, den, top = state
    return acc / den, top + jnp.log(den) * LOG2E


def _band_bias(rel):
    return jnp.where((rel >= 0) & (rel <= BLOCK), 0.0, MASKED).astype(F32)


def _fill_bias_tables(bias_ref, slabs):
    row = lax.broadcasted_iota(jnp.int32, (2 * BLOCK, 2 * BLOCK), 0) % BLOCK
    col = lax.broadcasted_iota(jnp.int32, (2 * BLOCK, 2 * BLOCK), 1)
    bias_ref[0] = _band_bias(row - col)
    bias_ref[1] = _band_bias(row - col + BLOCK)
    if bias_ref.shape[0] > 2:
        q_rows, k_rows = BLOCK // slabs, 2 * BLOCK // slabs
        rel = (slabs * (row % q_rows) + row // q_rows) - (slabs * (col % k_rows) + col // k_rows)
        bias_ref[2] = _band_bias(rel)
        bias_ref[3] = _band_bias(rel + BLOCK)


def _first_grid_step():
    return (pl.program_id(0) == 0) & (pl.program_id(1) == 0)


def _strided_kernel(qkv_ref, o_ref, lse_ref, state16_ref, bias_ref):
    @pl.when(_first_grid_step())
    def _():
        _fill_bias_tables(bias_ref, SLABS_PER_MID)

    for j in range(qkv_ref.shape[1]):
        _strided_class(qkv_ref.at[:, j], o_ref.at[:, j], lse_ref.at[:, j], state16_ref, bias_ref)


def _strided_class(qkv_ref, o_ref, lse_ref, state16_ref, bias_ref):
    slab_len = qkv_ref.shape[1]

    def lanes_of(name, pair):
        start = CLS_COLS[name].start + pair * LANES
        return slice(start, start + LANES)

    def coarse(g, blk):
        kstart = max(blk - 1, 0) * BLOCK
        keys = slice(kstart, kstart + 2 * BLOCK)
        own = slice(blk * BLOCK, (blk + 1) * BLOCK)
        for pair in range(PAIRS):
            lanes = slice(pair * LANES, (pair + 1) * LANES)
            state = _pair_attention(qkv_ref[g, own, lanes_of("q", pair)], qkv_ref[g, keys, lanes_of("k", pair)],
                                    qkv_ref[g, keys, lanes_of("v", pair)], bias_ref[min(blk, 1)])
            for part in range(3):
                state16_ref[part, g, own, lanes] = state[part]

    q_rows = BLOCK // SLABS_PER_MID
    k_rows = 2 * q_rows

    def gather(ref, rows, lanes):
        return jnp.concatenate([ref[g, rows, lanes] for g in range(SLABS_PER_MID)], axis=0)

    def mid(i):
        own = slice(i * q_rows, (i + 1) * q_rows)
        kstart = max(i - 1, 0) * q_rows
        keys = slice(kstart, kstart + k_rows)
        for pair in range(PAIRS):
            lanes = slice(pair * LANES, (pair + 1) * LANES)
            state4 = _pair_attention(gather(qkv_ref, own, lanes_of("q", pair)),
                                     gather(qkv_ref, keys, lanes_of("k", pair)),
                                     gather(qkv_ref, keys, lanes_of("v", pair)), bias_ref[2 + min(i, 1)])
            state16 = tuple(gather(state16_ref.at[part], own, lanes) for part in range(3))
            o, lse = _normalised(_merge_states(state4, state16))
            for g in range(SLABS_PER_MID):
                o_ref[g, own, lanes] = o[g * q_rows:(g + 1) * q_rows].astype(BF16)
                lse_ref[g, own, lanes] = lse[g * q_rows:(g + 1) * q_rows]

    bands = slab_len // BLOCK
    per_band = BLOCK // q_rows
    for band in range(bands):
        for g in range(SLABS_PER_MID):
            coarse(g, band)
            if band > 0:
                for i in range(g * per_band // SLABS_PER_MID, (g + 1) * per_band // SLABS_PER_MID):
                    mid((band - 1) * per_band + i)
    for i in range(per_band):
        mid((bands - 1) * per_band + i)


def _to_sequence_order(cls_ref, slab_ref, mid_ref):
    per_class = cls_ref.shape[1]
    per_mid = per_class * SLABS_PER_MID
    n_slabs = cls_ref.shape[2] // LANES
    for s in range(n_slabs):
        lanes = slice(s * LANES, (s + 1) * LANES)
        for r4 in range(MID):
            for g in range(SLABS_PER_MID):
                mid_ref[s, pl.ds(r4 * per_mid + g, per_class, stride=SLABS_PER_MID), :] = (
                    cls_ref[r4 + MID * g, :, lanes].astype(F32))
        for r4 in range(MID):
            slab_ref[s, pl.ds(r4, per_mid, stride=MID), :] = mid_ref[s, r4 * per_mid:(r4 + 1) * per_mid]


def _mix_kernel(x_ref, seq_ref, prev_ref, ws_ref, bs_ref, oc_ref, lc_ref,
                mem_ref, mgain_ref, hgains_ref, wkv32_ref, wout32_ref, out_ref,
                oslab_ref, lslab_ref, bias_ref, wout_ref, wkv_ref, mk_ref, mv_ref, *, steps_per_batch):
    rows = x_ref.shape[0]
    step = pl.program_id(0)

    def seq(name, row_slice=slice(None), pair=None):
        cols = SEQ_COLS[name]
        if pair is not None:
            cols = slice(cols.start + pair * LANES, cols.start + (pair + 1) * LANES)
        return seq_ref[row_slice, cols]

    @pl.when(step % steps_per_batch == 0)
    def _():
        @pl.when(step == 0)
        def _():
            wkv_ref[...] = wkv32_ref[...].astype(BF16)
        mem = mem_ref[...]
        ms = jnp.mean(mem * mem, axis=-1, keepdims=True)
        hm = (mem * lax.rsqrt(ms + EPS) * mgain_ref[...]).astype(BF16)
        mk = jnp.dot(hm, wkv_ref[:, :MEM_WIDTH], preferred_element_type=F32)
        mem_k = HEAD_GAINS.index("m_k")
        mk_ref[...] = _head_rms_lanes(mk, hgains_ref[mem_k:mem_k + 1, :]).astype(BF16)
        mv_ref[...] = jnp.dot(hm, wkv_ref[:, MEM_WIDTH:], preferred_element_type=F32).astype(BF16)

    @pl.when(step == 0)
    def _():
        wout_ref[...] = wout32_ref[...].astype(BF16)
        row = lax.broadcasted_iota(jnp.int32, (2 * BLOCK, 2 * BLOCK), 0) % BLOCK
        col = lax.broadcasted_iota(jnp.int32, (2 * BLOCK, 2 * BLOCK), 1)
        band = _band_bias(row - col + BLOCK)
        bias_ref[1] = band
        bias_ref[0] = jnp.where(col >= BLOCK, band, MASKED)
    tri = (lax.broadcasted_iota(jnp.int32, (CHUNK, CHUNK), 0)
           >= lax.broadcasted_iota(jnp.int32, (CHUNK, CHUNK), 1))
    left = lax.broadcasted_iota(jnp.int32, (CHUNK, LANES), 1) < HEAD_DIM
    w_tri = [jnp.where(tri, ws_ref[h], 0.0).astype(BF16) for h in range(GMLP_HEADS)]
    sp_chunks = []
    for c in range(rows // CHUNK):
        pairs = []
        for pair in range(GMLP_WIDTH // LANES):
            vn2 = seq("g_vn", slice(c * CHUNK, (c + 1) * CHUNK), pair)
            sp_a = jnp.dot(w_tri[2 * pair], vn2, preferred_element_type=F32)
            sp_b = jnp.dot(w_tri[2 * pair + 1], vn2, preferred_element_type=F32)
            pairs.append(jnp.where(left, sp_a, sp_b))
        sp_chunks.append(jnp.concatenate(pairs, axis=1) + bs_ref[...])
    y_g = seq("g_ug").astype(F32) * jnp.concatenate(sp_chunks, axis=0)

    def memory_pair(pair):
        lanes = slice(pair * LANES, (pair + 1) * LANES)
        acc, den, _ = _pair_attention(seq("m_q", pair=pair), mk_ref[:, lanes], mv_ref[:, lanes], None)
        return acc / den

    mem_pairs = []

    _to_sequence_order(oc_ref, oslab_ref.at[0], oslab_ref.at[1])
    _to_sequence_order(lc_ref, lslab_ref.at[0], lslab_ref.at[1])
    first_bias = jnp.where(step % steps_per_batch == 0, 0, 1)
    block_outs = []
    n_blocks = rows // BLOCK
    assert MEM_AT + MEM_WIDTH // LANES <= n_blocks
    for blk in range(n_blocks):
        if MEM_AT <= blk < MEM_AT + MEM_WIDTH // LANES:
            mem_pairs.append(memory_pair(blk - MEM_AT))
        if blk == n_blocks - 1:
            y_m = jnp.concatenate(mem_pairs, axis=1) * seq("m_gate").astype(F32)
            partial = (x_ref[...]
                       + jnp.dot(y_g.astype(BF16), wout_ref[:GMLP_WIDTH, :], preferred_element_type=F32)
                       + jnp.dot(y_m.astype(BF16), wout_ref[GMLP_WIDTH + ATTN_WIDTH:, :],
                                 preferred_element_type=F32))
        own = slice(blk * BLOCK, (blk + 1) * BLOCK)
        pair_outs = []
        for pair in range(PAIRS):
            lanes = slice(pair * LANES, (pair + 1) * LANES)
            if blk == 0:
                k_lanes = slice(SEQ_COLS["k"].start + pair * LANES, SEQ_COLS["k"].start + (pair + 1) * LANES)
                v_lanes = slice(SEQ_COLS["v"].start + pair * LANES, SEQ_COLS["v"].start + (pair + 1) * LANES)
                k2 = jnp.concatenate([prev_ref[:, k_lanes], seq("k", own, pair)], axis=0)
                v2 = jnp.concatenate([prev_ref[:, v_lanes], seq("v", own, pair)], axis=0)
                bias = bias_ref[first_bias]
            else:
                window = slice((blk - 1) * BLOCK, (blk + 1) * BLOCK)
                k2, v2, bias = seq("k", window, pair), seq("v", window, pair), bias_ref[1]
            acc1, den1, top1 = _pair_attention(seq("q", own, pair), k2, v2, bias)
            o_c, top_c = oslab_ref[0, pair, own, :], lslab_ref[0, pair, own, :]
            top = jnp.maximum(top1, top_c)
            w_1, w_c = jnp.exp2(top1 - top), jnp.exp2(top_c - top)
            pair_outs.append((w_1 * acc1 + w_c * o_c) / (w_1 * den1 + w_c))
        block_outs.append(jnp.concatenate(pair_outs, axis=1))
    y_a = jnp.concatenate(block_outs, axis=0) * seq("a_gate").astype(F32)
    out_ref[...] = partial + jnp.dot(y_a.astype(BF16), wout_ref[GMLP_WIDTH:GMLP_WIDTH + ATTN_WIDTH, :],
                                     preferred_element_type=F32)


def _cparams(sem):
    return pltpu.CompilerParams(dimension_semantics=sem, vmem_limit_bytes=VMEM_LIMIT)


def _const_spec(shape):
    return pl.BlockSpec(shape, lambda *idx: (0,) * len(shape))


def _regroup_scratch(tile):
    return pltpu.VMEM((2, ATTN_WIDTH // LANES, tile, LANES), F32)


def _class_tile_spec(tile, steps_per_batch, width):
    return pl.BlockSpec((None, CLASSES, tile // CLASSES, width),
                        lambda i: (i // steps_per_batch, 0, i % steps_per_batch, 0))


def _project(x2d, batch, seq, x_gain, gv_gain, head_gains, w_in):
    rows, d_model = x2d.shape
    tiles, per_batch = rows // PROJ_TILE, seq // PROJ_TILE
    done = lambda i: jnp.maximum(i - 1, 0)
    return pl.pallas_call(
        _proj_kernel,
        grid=(tiles + 1,),
        in_specs=[pl.BlockSpec((PROJ_TILE, d_model), lambda i: (jnp.minimum(i, tiles - 1), 0)),
                  _const_spec(x_gain.shape), _const_spec(gv_gain.shape), _const_spec(head_gains.shape),
                  pl.BlockSpec(w_in.shape, lambda i: (0, 0), pipeline_mode=pl.Buffered(1))],
        out_specs=[pl.BlockSpec((PROJ_TILE, SEQ_WIDTH), lambda i: (done(i), 0)),
                   pl.BlockSpec((None, CLASSES, PROJ_TILE // CLASSES, CLS_WIDTH),
                                lambda i: (done(i) // per_batch, 0, done(i) % per_batch, 0))],
        out_shape=[jax.ShapeDtypeStruct((rows, SEQ_WIDTH), BF16),
                   jax.ShapeDtypeStruct((batch, CLASSES, seq // CLASSES, CLS_WIDTH), BF16)],
        scratch_shapes=[pltpu.VMEM(w_in.shape, BF16), pltpu.VMEM((PROJ_TILE, d_model), BF16),
                        pltpu.VMEM((PROJ_TILE, d_model), BF16), _regroup_scratch(PROJ_TILE)],
        compiler_params=_cparams(("arbitrary",)),
    )(x2d, x_gain, gv_gain, head_gains, w_in)


def _strided_attention(qkv_cls):
    batch, _, slab_len, _ = qkv_cls.shape
    spec = lambda width: pl.BlockSpec((None, SLABS_PER_MID, CLASSES_PER_STEP, slab_len, width),
                                      lambda b, r4: (b, 0, r4, 0, 0))
    split = lambda width: (batch, SLABS_PER_MID, MID, slab_len, width)
    o, lse = pl.pallas_call(
        _strided_kernel,
        grid=(batch, MID // CLASSES_PER_STEP),
        in_specs=[spec(CLS_WIDTH)],
        out_specs=[spec(ATTN_WIDTH)] * 2,
        out_shape=[jax.ShapeDtypeStruct(split(ATTN_WIDTH), BF16), jax.ShapeDtypeStruct(split(ATTN_WIDTH), F32)],
        scratch_shapes=[pltpu.VMEM((3, SLABS_PER_MID, slab_len, ATTN_WIDTH), F32),
                        pltpu.VMEM((4, 2 * BLOCK, 2 * BLOCK), F32)],
        compiler_params=_cparams(("arbitrary", "arbitrary")),
    )(qkv_cls.reshape(split(CLS_WIDTH)))
    merged = (batch, CLASSES, slab_len, ATTN_WIDTH)
    return o.reshape(merged), lse.reshape(merged)


def _mix(x2d, acts, w_s, b_full, oc, lc, mem, mem_gain, head_gains, w_kv, w_out, seq):
    rows, d_model = x2d.shape
    mem_len = mem.shape[1]
    steps_per_batch = seq // ROW_TILE
    row_spec = lambda w: pl.BlockSpec((ROW_TILE, w), lambda i: (i, 0))
    prev_spec = pl.BlockSpec((BLOCK, SEQ_COLS["v"].stop),
                             lambda i: (jnp.maximum(i * (ROW_TILE // BLOCK) - 1, 0), 0))
    mem_spec = pl.BlockSpec((None,) + mem.shape[1:], lambda i: (i // steps_per_batch, 0, 0))
    weight_spec = lambda w: pl.BlockSpec(w.shape, lambda i: (0, 0), pipeline_mode=pl.Buffered(1))
    cls_spec = _class_tile_spec(ROW_TILE, steps_per_batch, ATTN_WIDTH)
    return pl.pallas_call(
        functools.partial(_mix_kernel, steps_per_batch=steps_per_batch),
        grid=(rows // ROW_TILE,),
        in_specs=[row_spec(d_model), row_spec(SEQ_WIDTH), prev_spec,
                  _const_spec(w_s.shape), _const_spec(b_full.shape), cls_spec, cls_spec,
                  mem_spec, _const_spec(mem_gain.shape), _const_spec(head_gains.shape),
                  weight_spec(w_kv), weight_spec(w_out)],
        out_specs=row_spec(d_model),
        out_shape=jax.ShapeDtypeStruct((rows, d_model), F32),
        scratch_shapes=[_regroup_scratch(ROW_TILE)] * 2
                       + [pltpu.VMEM((2, 2 * BLOCK, 2 * BLOCK), F32), pltpu.VMEM(w_out.shape, BF16),
                          pltpu.VMEM(w_kv.shape, BF16)] + [pltpu.VMEM((mem_len, MEM_WIDTH), BF16)] * 2,
        compiler_params=_cparams(("arbitrary",)),
    )(x2d, acts, acts, w_s, b_full, oc, lc, mem, mem_gain, head_gains, w_kv, w_out)


def kernel(x, mem, norm_gain, w_in, gmlp_v_gain, gmlp_w_s, gmlp_b, attn_q_gain, attn_k_gain,
           mem_norm_gain, w_mem_kv, mem_q_gain, mem_k_gain, w_out):
    batch, seq, d_model = x.shape
    depth = w_in.shape[0]
    assert DILATED_CONFIGS == ((BLOCK, 1), (BLOCK * MID, MID), (BLOCK * CLASSES, CLASSES))
    assert seq % PROJ_TILE == 0 and seq % ROW_TILE == 0 and seq % (BLOCK * CLASSES) == 0
    x2d = x.reshape(batch * seq, d_model)
    for l in range(depth):
        row = lambda g: g.reshape(1, -1).astype(F32)
        shared = dict(a_q=attn_q_gain[l], a_k=attn_k_gain[l], m_q=mem_q_gain[l], m_k=mem_k_gain[l])
        head_gains = jnp.tile(jnp.stack([shared[name] for name in HEAD_GAINS]).astype(F32), (1, 2))
        acts, qkv_cls = _project(x2d, batch, seq, row(norm_gain[l]), row(gmlp_v_gain[l]), head_gains,
                                 w_in[l])
        oc, lc = _strided_attention(qkv_cls)
        b_full = jnp.repeat(gmlp_b[l].T, HEAD_DIM, axis=1)
        x2d = _mix(x2d, acts, gmlp_w_s[l], b_full, oc, lc, mem, row(mem_norm_gain[l]), head_gains,
                   w_mem_kv[l], w_out[l], seq)
    return x2d.reshape(batch, seq, d_model)
```

```python
import functools
import math

import jax
import jax.numpy as jnp
from jax import lax
from jax.experimental import pallas as pl
from jax.experimental.pallas import tpu as pltpu

HEAD_DIM = 64
GMLP_HEADS = 4
ATTN_HEADS = 8
MEM_HEADS = 4
GMLP_WIDTH = GMLP_HEADS * HEAD_DIM
ATTN_WIDTH = ATTN_HEADS * HEAD_DIM
MEM_WIDTH = MEM_HEADS * HEAD_DIM
CHUNK = 128
BLOCK = 128
DILATED_CONFIGS = ((128, 1), (512, 4), (2048, 16))
EPS = 1e-6
MASKED = -1e30
LOG2E = math.log2(math.e)
LOGIT_SCALE = LOG2E / math.sqrt(HEAD_DIM)

IN_SEGMENTS = (("g_u", GMLP_WIDTH), ("g_v", GMLP_WIDTH), ("g_gate", GMLP_WIDTH),
               ("a_q", ATTN_WIDTH), ("a_k", ATTN_WIDTH), ("a_v", ATTN_WIDTH), ("a_gate", ATTN_WIDTH),
               ("m_q", MEM_WIDTH), ("m_gate", MEM_WIDTH))


def _column_slices(segments):
    slices, start = {}, 0
    for name, width in segments:
        slices[name] = slice(start, start + width)
        start += width
    return slices, start


IN_COLS, _ = _column_slices(IN_SEGMENTS)
SEQ_COLS, SEQ_WIDTH = _column_slices((("k", ATTN_WIDTH), ("v", ATTN_WIDTH), ("q", ATTN_WIDTH),
                                      ("a_gate", ATTN_WIDTH), ("g_ug", GMLP_WIDTH), ("g_vn", GMLP_WIDTH),
                                      ("m_q", MEM_WIDTH), ("m_gate", MEM_WIDTH)))
CLS_COLS, CLS_WIDTH = _column_slices((("q", ATTN_WIDTH), ("k", ATTN_WIDTH), ("v", ATTN_WIDTH)))

HEAD_GAINS = ("a_q", "a_k", "m_q", "m_k")

CLASSES = 16
MID = 4
SLABS_PER_MID = CLASSES // MID
LANES = 128
PAIRS = ATTN_WIDTH // LANES
PROJ_TILE = 1024
ROW_TILE = 512
CLASSES_PER_STEP = 1
MEM_AT = 1
VMEM_LIMIT = 56 * 1024 * 1024

F32 = jnp.float32
BF16 = jnp.bfloat16


def _head_rms_lanes(acc, gain):
    left = lax.broadcasted_iota(jnp.int32, (acc.shape[0], LANES), 1) < HEAD_DIM
    groups = []
    for j in range(acc.shape[1] // LANES):
        a = acc[:, j * LANES:(j + 1) * LANES]
        sq = a * a
        ss_a = jnp.sum(jnp.where(left, sq, 0.0), axis=1, keepdims=True)
        ss_b = jnp.sum(jnp.where(left, 0.0, sq), axis=1, keepdims=True)
        inv = jnp.where(left, lax.rsqrt(ss_a * (1.0 / HEAD_DIM) + EPS),
                        lax.rsqrt(ss_b * (1.0 / HEAD_DIM) + EPS))
        group_gain = gain if gain.shape[1] == LANES else gain[:, j * LANES:(j + 1) * LANES]
        groups.append(a * inv * group_gain)
    return jnp.concatenate(groups, axis=1)


def _silu(x):
    half = 0.5 * x
    return half + half * jnp.tanh(half)


def _store_both_layouts(val, name, seq_ref, cls_ref, work_ref):
    slab_ref, mid_ref = work_ref.at[0], work_ref.at[1]
    rows = val.shape[0]
    per_mid, per_class = rows // MID, rows // CLASSES
    for s in range(val.shape[1] // LANES):
        group = val[:, s * LANES:(s + 1) * LANES]
        seq_ref[:, SEQ_COLS[name].start + s * LANES:SEQ_COLS[name].start + (s + 1) * LANES] = group.astype(BF16)
        slab_ref[s] = group
        out_lanes = slice(CLS_COLS[name].start + s * LANES, CLS_COLS[name].start + (s + 1) * LANES)
        for r4 in range(MID):
            mid_ref[s, r4 * per_mid:(r4 + 1) * per_mid] = slab_ref[s, pl.ds(r4, per_mid, stride=MID), :]
        for r4 in range(MID):
            for g in range(SLABS_PER_MID):
                cls_ref[r4 + MID * g, :, out_lanes] = (
                    mid_ref[s, pl.ds(r4 * per_mid + g, per_class, stride=SLABS_PER_MID), :].astype(BF16))


def _proj_kernel(x_ref, xgain_ref, gvgain_ref, hgains_ref, w32_ref, seq_ref, cls_ref, w_ref, h_ref, work_ref):
    @pl.when(pl.program_id(0) == 0)
    def _():
        w_ref[...] = w32_ref[...].astype(BF16)

    def gain(name):
        if name == "g_v":
            return gvgain_ref[...]
        return hgains_ref[HEAD_GAINS.index(name):HEAD_GAINS.index(name) + 1, :]

    x = x_ref[...]
    ms = jnp.mean(x * x, axis=-1, keepdims=True)
    h_ref[...] = (x * lax.rsqrt(ms + EPS) * xgain_ref[...]).astype(BF16)

    def project(name):
        return jnp.dot(h_ref[...], w_ref[:, IN_COLS[name]], preferred_element_type=F32)

    def put(name, val):
        seq_ref[:, SEQ_COLS[name]] = val.astype(BF16)

    a_q = project("a_q")
    a_k = project("a_k")
    _store_both_layouts(_head_rms_lanes(a_q, gain("a_q")) * LOGIT_SCALE, "q", seq_ref, cls_ref, work_ref)
    a_v = project("a_v")
    _store_both_layouts(_head_rms_lanes(a_k, gain("a_k")), "k", seq_ref, cls_ref, work_ref)
    a_gate = project("a_gate")
    _store_both_layouts(a_v, "v", seq_ref, cls_ref, work_ref)
    m_q = project("m_q")
    put("a_gate", _silu(a_gate))
    g_u = project("g_u")
    put("m_q", _head_rms_lanes(m_q, gain("m_q")) * LOGIT_SCALE)
    g_v = project("g_v")
    g_gate = project("g_gate")
    put("g_vn", _head_rms_lanes(g_v, gain("g_v")))
    m_gate = project("m_gate")
    put("g_ug", g_u * _silu(g_gate))
    put("m_gate", _silu(m_gate))


def _pair_attention(q2, k2, v2, bias):
    rows = q2.shape[0]
    left = lax.broadcasted_iota(jnp.int32, (rows, LANES), 1) < HEAD_DIM
    zeros = jnp.zeros_like(q2)
    q_stack = jnp.concatenate([jnp.where(left, q2, zeros), jnp.where(left, zeros, q2)], axis=0)
    v_aug = jnp.concatenate([v2, jnp.ones_like(v2)], axis=1)
    s = lax.dot_general(q_stack, k2, (((1,), (1,)), ((), ())), preferred_element_type=F32)
    if bias is not None:
        s = s + bias
    m = jnp.max(s, axis=1, keepdims=True)
    p = jnp.exp2(s - m).astype(BF16)
    pv = jnp.dot(p, v_aug, preferred_element_type=F32)
    pick = lambda t: jnp.where(left, t[:rows], t[rows:])
    return pick(pv[:, :LANES]), pick(pv[:, LANES:]), pick(jnp.broadcast_to(m, (2 * rows, LANES)))


def _merge_states(a, b):
    top = jnp.maximum(a[2], b[2])
    w_a, w_b = jnp.exp2(a[2] - top), jnp.exp2(b[2] - top)
    return w_a * a[0] + w_b * b[0], w_a * a[1] + w_b * b[1], top


def _normalised(state):
    acc, den, top = state
    return acc / den, top + jnp.log(den) * LOG2E


def _band_bias(rel):
    return jnp.where((rel >= 0) & (rel <= BLOCK), 0.0, MASKED).astype(F32)


def _fill_bias_tables(bias_ref, slabs):
    row = lax.broadcasted_iota(jnp.int32, (2 * BLOCK, 2 * BLOCK), 0) % BLOCK
    col = lax.broadcasted_iota(jnp.int32, (2 * BLOCK, 2 * BLOCK), 1)
    bias_ref[0] = _band_bias(row - col)
    bias_ref[1] = _band_bias(row - col + BLOCK)
    if bias_ref.shape[0] > 2:
        q_rows, k_rows = BLOCK // slabs, 2 * BLOCK // slabs
        rel = (slabs * (row % q_rows) + row // q_rows) - (slabs * (col % k_rows) + col // k_rows)
        bias_ref[2] = _band_bias(rel)
        bias_ref[3] = _band_bias(rel + BLOCK)


def _first_grid_step():
    return (pl.program_id(0) == 0) & (pl.program_id(1) == 0)


def _strided_kernel(qkv_ref, o_ref, lse_ref, state16_ref, bias_ref):
    @pl.when(_first_grid_step())
    def _():
        _fill_bias_tables(bias_ref, SLABS_PER_MID)

    for j in range(qkv_ref.shape[1]):
        _strided_class(qkv_ref.at[:, j], o_ref.at[:, j], lse_ref.at[:, j], state16_ref, bias_ref)


def _strided_class(qkv_ref, o_ref, lse_ref, state16_ref, bias_ref):
    slab_len = qkv_ref.shape[1]

    def lanes_of(name, pair):
        start = CLS_COLS[name].start + pair * LANES
        return slice(start, start + LANES)

    def coarse(g, blk):
        kstart = max(blk - 1, 0) * BLOCK
        keys = slice(kstart, kstart + 2 * BLOCK)
        own = slice(blk * BLOCK, (blk + 1) * BLOCK)
        for pair in range(PAIRS):
            lanes = slice(pair * LANES, (pair + 1) * LANES)
            state = _pair_attention(qkv_ref[g, own, lanes_of("q", pair)], qkv_ref[g, keys, lanes_of("k", pair)],
                                    qkv_ref[g, keys, lanes_of("v", pair)], bias_ref[min(blk, 1)])
            for part in range(3):
                state16_ref[part, g, own, lanes] = state[part]

    q_rows = BLOCK // SLABS_PER_MID
    k_rows = 2 * q_rows

    def gather(ref, rows, lanes):
        return jnp.concatenate([ref[g, rows, lanes] for g in range(SLABS_PER_MID)], axis=0)

    def mid(i):
        own = slice(i * q_rows, (i + 1) * q_rows)
        kstart = max(i - 1, 0) * q_rows
        keys = slice(kstart, kstart + k_rows)
        for pair in range(PAIRS):
            lanes = slice(pair * LANES, (pair + 1) * LANES)
            state4 = _pair_attention(gather(qkv_ref, own, lanes_of("q", pair)),
                                     gather(qkv_ref, keys, lanes_of("k", pair)),
                                     gather(qkv_ref, keys, lanes_of("v", pair)), bias_ref[2 + min(i, 1)])
            state16 = tuple(gather(state16_ref.at[part], own, lanes) for part in range(3))
            o, lse = _normalised(_merge_states(state4, state16))
            for g in range(SLABS_PER_MID):
                o_ref[g, own, lanes] = o[g * q_rows:(g + 1) * q_rows].astype(BF16)
                lse_ref[g, own, lanes] = lse[g * q_rows:(g + 1) * q_rows]

    bands = slab_len // BLOCK
    per_band = BLOCK // q_rows
    for band in range(bands):
        for g in range(SLABS_PER_MID):
            coarse(g, band)
            if band > 0:
                for i in range(g * per_band // SLABS_PER_MID, (g + 1) * per_band // SLABS_PER_MID):
                    mid((band - 1) * per_band + i)
    for i in range(per_band):
        mid((bands - 1) * per_band + i)


def _to_sequence_order(cls_ref, slab_ref, mid_ref):
    per_class = cls_ref.shape[1]
    per_mid = per_class * SLABS_PER_MID
    n_slabs = cls_ref.shape[2] // LANES
    for s in range(n_slabs):
        lanes = slice(s * LANES, (s + 1) * LANES)
        for r4 in range(MID):
            for g in range(SLABS_PER_MID):
                mid_ref[s, pl.ds(r4 * per_mid + g, per_class, stride=SLABS_PER_MID), :] = (
                    cls_ref[r4 + MID * g, :, lanes].astype(F32))
        for r4 in range(MID):
            slab_ref[s, pl.ds(r4, per_mid, stride=MID), :] = mid_ref[s, r4 * per_mid:(r4 + 1) * per_mid]


def _mix_kernel(x_ref, seq_ref, prev_ref, ws_ref, bs_ref, oc_ref, lc_ref,
                mem_ref, mgain_ref, hgains_ref, wkv32_ref, wout32_ref, out_ref,
                oslab_ref, lslab_ref, bias_ref, wout_ref, wkv_ref, mk_ref, mv_ref, *, steps_per_batch):
    rows = x_ref.shape[0]
    step = pl.program_id(0)

    def seq(name, row_slice=slice(None), pair=None):
        cols = SEQ_COLS[name]
        if pair is not None:
            cols = slice(cols.start + pair * LANES, cols.start + (pair + 1) * LANES)
        return seq_ref[row_slice, cols]

    @pl.when(step % steps_per_batch == 0)
    def _():
        @pl.when(step == 0)
        def _():
            wkv_ref[...] = wkv32_ref[...].astype(BF16)
        mem = mem_ref[...]
        ms = jnp.mean(mem * mem, axis=-1, keepdims=True)
        hm = (mem * lax.rsqrt(ms + EPS) * mgain_ref[...]).astype(BF16)
        mk = jnp.dot(hm, wkv_ref[:, :MEM_WIDTH], preferred_element_type=F32)
        mem_k = HEAD_GAINS.index("m_k")
        mk_ref[...] = _head_rms_lanes(mk, hgains_ref[mem_k:mem_k + 1, :]).astype(BF16)
        mv_ref[...] = jnp.dot(hm, wkv_ref[:, MEM_WIDTH:], preferred_element_type=F32).astype(BF16)

    @pl.when(step == 0)
    def _():
        wout_ref[...] = wout32_ref[...].astype(BF16)
        row = lax.broadcasted_iota(jnp.int32, (2 * BLOCK, 2 * BLOCK), 0) % BLOCK
        col = lax.broadcasted_iota(jnp.int32, (2 * BLOCK, 2 * BLOCK), 1)
        band = _band_bias(row - col + BLOCK)
        bias_ref[1] = band
        bias_ref[0] = jnp.where(col >= BLOCK, band, MASKED)
    tri = (lax.broadcasted_iota(jnp.int32, (CHUNK, CHUNK), 0)
           >= lax.broadcasted_iota(jnp.int32, (CHUNK, CHUNK), 1))
    left = lax.broadcasted_iota(jnp.int32, (CHUNK, LANES), 1) < HEAD_DIM
    w_tri = [jnp.where(tri, ws_ref[h], 0.0).astype(BF16) for h in range(GMLP_HEADS)]
    sp_chunks = []
    for c in range(rows // CHUNK):
        pairs = []
        for pair in range(GMLP_WIDTH // LANES):
            vn2 = seq("g_vn", slice(c * CHUNK, (c + 1) * CHUNK), pair)
            sp_a = jnp.dot(w_tri[2 * pair], vn2, preferred_element_type=F32)
            sp_b = jnp.dot(w_tri[2 * pair + 1], vn2, preferred_element_type=F32)
            pairs.append(jnp.where(left, sp_a, sp_b))
        sp_chunks.append(jnp.concatenate(pairs, axis=1) + bs_ref[...])
    y_g = seq("g_ug").astype(F32) * jnp.concatenate(sp_chunks, axis=0)

    def memory_pair(pair):
        lanes = slice(pair * LANES, (pair + 1) * LANES)
        acc, den, _ = _pair_attention(seq("m_q", pair=pair), mk_ref[:, lanes], mv_ref[:, lanes], None)
        return acc / den

    mem_pairs = []

    _to_sequence_order(oc_ref, oslab_ref.at[0], oslab_ref.at[1])
    _to_sequence_order(lc_ref, lslab_ref.at[0], lslab_ref.at[1])
    first_bias = jnp.where(step % steps_per_batch == 0, 0, 1)
    block_outs = []
    n_blocks = rows // BLOCK
    assert MEM_AT + MEM_WIDTH // LANES <= n_blocks
    for blk in range(n_blocks):
        if MEM_AT <= blk < MEM_AT + MEM_WIDTH // LANES:
            mem_pairs.append(memory_pair(blk - MEM_AT))
        if blk == n_blocks - 1:
            y_m = jnp.concatenate(mem_pairs, axis=1) * seq("m_gate").astype(F32)
            partial = (x_ref[...]
                       + jnp.dot(y_g.astype(BF16), wout_ref[:GMLP_WIDTH, :], preferred_element_type=F32)
                       + jnp.dot(y_m.astype(BF16), wout_ref[GMLP_WIDTH + ATTN_WIDTH:, :],
                                 preferred_element_type=F32))
        own = slice(blk * BLOCK, (blk + 1) * BLOCK)
        pair_outs = []
        for pair in range(PAIRS):
            lanes = slice(pair * LANES, (pair + 1) * LANES)
            if blk == 0:
                k_lanes = slice(SEQ_COLS["k"].start + pair * LANES, SEQ_COLS["k"].start + (pair + 1) * LANES)
                v_lanes = slice(SEQ_COLS["v"].start + pair * LANES, SEQ_COLS["v"].start + (pair + 1) * LANES)
                k2 = jnp.concatenate([prev_ref[:, k_lanes], seq("k", own, pair)], axis=0)
                v2 = jnp.concatenate([prev_ref[:, v_lanes], seq("v", own, pair)], axis=0)
                bias = bias_ref[first_bias]
            else:
                window = slice((blk - 1) * BLOCK, (blk + 1) * BLOCK)
                k2, v2, bias = seq("k", window, pair), seq("v", window, pair), bias_ref[1]
            acc1, den1, top1 = _pair_attention(seq("q", own, pair), k2, v2, bias)
            o_c, top_c = oslab_ref[0, pair, own, :], lslab_ref[0, pair, own, :]
            top = jnp.maximum(top1, top_c)
            w_1, w_c = jnp.exp2(top1 - top), jnp.exp2(top_c - top)
            pair_outs.append((w_1 * acc1 + w_c * o_c) / (w_1 * den1 + w_c))
        block_outs.append(jnp.concatenate(pair_outs, axis=1))
    y_a = jnp.concatenate(block_outs, axis=0) * seq("a_gate").astype(F32)
    out_ref[...] = partial + jnp.dot(y_a.astype(BF16), wout_ref[GMLP_WIDTH:GMLP_WIDTH + ATTN_WIDTH, :],
                                     preferred_element_type=F32)


def _cparams(sem):
    return pltpu.CompilerParams(dimension_semantics=sem, vmem_limit_bytes=VMEM_LIMIT)


def _const_spec(shape):
    return pl.BlockSpec(shape, lambda *idx: (0,) * len(shape))


def _regroup_scratch(tile):
    return pltpu.VMEM((2, ATTN_WIDTH // LANES, tile, LANES), F32)


def _class_tile_spec(tile, steps_per_batch, width):
    return pl.BlockSpec((None, CLASSES, tile // CLASSES, width),
                        lambda i: (i // steps_per_batch, 0, i % steps_per_batch, 0))


def _project(x2d, batch, seq, x_gain, gv_gain, head_gains, w_in):
    rows, d_model = x2d.shape
    return pl.pallas_call(
        _proj_kernel,
        grid=(rows // PROJ_TILE,),
        in_specs=[pl.BlockSpec((PROJ_TILE, d_model), lambda i: (i, 0)),
                  _const_spec(x_gain.shape), _const_spec(gv_gain.shape), _const_spec(head_gains.shape),
                  pl.BlockSpec(w_in.shape, lambda i: (0, 0), pipeline_mode=pl.Buffered(1))],
        out_specs=[pl.BlockSpec((PROJ_TILE, SEQ_WIDTH), lambda i: (i, 0)),
                   _class_tile_spec(PROJ_TILE, seq // PROJ_TILE, CLS_WIDTH)],
        out_shape=[jax.ShapeDtypeStruct((rows, SEQ_WIDTH), BF16),
                   jax.ShapeDtypeStruct((batch, CLASSES, seq // CLASSES, CLS_WIDTH), BF16)],
        scratch_shapes=[pltpu.VMEM(w_in.shape, BF16), pltpu.VMEM((PROJ_TILE, d_model), BF16),
                        _regroup_scratch(PROJ_TILE)],
        compiler_params=_cparams(("arbitrary",)),
    )(x2d, x_gain, gv_gain, head_gains, w_in)


def _strided_attention(qkv_cls):
    batch, _, slab_len, _ = qkv_cls.shape
    spec = lambda width: pl.BlockSpec((None, SLABS_PER_MID, CLASSES_PER_STEP, slab_len, width),
                                      lambda b, r4: (b, 0, r4, 0, 0))
    split = lambda width: (batch, SLABS_PER_MID, MID, slab_len, width)
    o, lse = pl.pallas_call(
        _strided_kernel,
        grid=(batch, MID // CLASSES_PER_STEP),
        in_specs=[spec(CLS_WIDTH)],
        out_specs=[spec(ATTN_WIDTH)] * 2,
        out_shape=[jax.ShapeDtypeStruct(split(ATTN_WIDTH), BF16), jax.ShapeDtypeStruct(split(ATTN_WIDTH), F32)],
        scratch_shapes=[pltpu.VMEM((3, SLABS_PER_MID, slab_len, ATTN_WIDTH), F32),
                        pltpu.VMEM((4, 2 * BLOCK, 2 * BLOCK), F32)],
        compiler_params=_cparams(("arbitrary", "arbitrary")),
    )(qkv_cls.reshape(split(CLS_WIDTH)))
    merged = (batch, CLASSES, slab_len, ATTN_WIDTH)
    return o.reshape(merged), lse.reshape(merged)


def _mix(x2d, acts, w_s, b_full, oc, lc, mem, mem_gain, head_gains, w_kv, w_out, seq):
    rows, d_model = x2d.shape
    mem_len = mem.shape[1]
    steps_per_batch = seq // ROW_TILE
    row_spec = lambda w: pl.BlockSpec((ROW_TILE, w), lambda i: (i, 0))
    prev_spec = pl.BlockSpec((BLOCK, SEQ_COLS["v"].stop),
                             lambda i: (jnp.maximum(i * (ROW_TILE // BLOCK) - 1, 0), 0))
    mem_spec = pl.BlockSpec((None,) + mem.shape[1:], lambda i: (i // steps_per_batch, 0, 0))
    weight_spec = lambda w: pl.BlockSpec(w.shape, lambda i: (0, 0), pipeline_mode=pl.Buffered(1))
    cls_spec = _class_tile_spec(ROW_TILE, steps_per_batch, ATTN_WIDTH)
    return pl.pallas_call(
        functools.partial(_mix_kernel, steps_per_batch=steps_per_batch),
        grid=(rows // ROW_TILE,),
        in_specs=[row_spec(d_model), row_spec(SEQ_WIDTH), prev_spec,
                  _const_spec(w_s.shape), _const_spec(b_full.shape), cls_spec, cls_spec,
                  mem_spec, _const_spec(mem_gain.shape), _const_spec(head_gains.shape),
                  weight_spec(w_kv), weight_spec(w_out)],
        out_specs=row_spec(d_model),
        out_shape=jax.ShapeDtypeStruct((rows, d_model), F32),
        scratch_shapes=[_regroup_scratch(ROW_TILE)] * 2
                       + [pltpu.VMEM((2, 2 * BLOCK, 2 * BLOCK), F32), pltpu.VMEM(w_out.shape, BF16),
                          pltpu.VMEM(w_kv.shape, BF16)] + [pltpu.VMEM((mem_len, MEM_WIDTH), BF16)] * 2,
        compiler_params=_cparams(("arbitrary",)),
    )(x2d, acts, acts, w_s, b_full, oc, lc, mem, mem_gain, head_gains, w_kv, w_out)


def kernel(x, mem, norm_gain, w_in, gmlp_v_gain, gmlp_w_s, gmlp_b, attn_q_gain, attn_k_gain,
           mem_norm_gain, w_mem_kv, mem_q_gain, mem_k_gain, w_out):
    batch, seq, d_model = x.shape
    depth = w_in.shape[0]
    assert DILATED_CONFIGS == ((BLOCK, 1), (BLOCK * MID, MID), (BLOCK * CLASSES, CLASSES))
    assert seq % PROJ_TILE == 0 and seq % ROW_TILE == 0 and seq % (BLOCK * CLASSES) == 0
    x2d = x.reshape(batch * seq, d_model)
    for l in range(depth):
        row = lambda g: g.reshape(1, -1).astype(F32)
        shared = dict(a_q=attn_q_gain[l], a_k=attn_k_gain[l], m_q=mem_q_gain[l], m_k=mem_k_gain[l])
        head_gains = jnp.tile(jnp.stack([shared[name] for name in HEAD_GAINS]).astype(F32), (1, 2))
        acts, qkv_cls = _project(x2d, batch, seq, row(norm_gain[l]), row(gmlp_v_gain[l]), head_gains,
                                 w_in[l])
        oc, lc = _strided_attention(qkv_cls)
        b_full = jnp.repeat(gmlp_b[l].T, HEAD_DIM, axis=1)
        x2d = _mix(x2d, acts, gmlp_w_s[l], b_full, oc, lc, mem, row(mem_norm_gain[l]), head_gains,
                   w_mem_kv[l], w_out[l], seq)
    return x2d.reshape(batch, seq, d_model)
```

```python
import functools
import math

import jax
import jax.numpy as jnp
from jax import lax
from jax.experimental import pallas as pl
from jax.experimental.pallas import tpu as pltpu

HEAD_DIM = 64
GMLP_HEADS = 4
ATTN_HEADS = 8
MEM_HEADS = 4
GMLP_WIDTH = GMLP_HEADS * HEAD_DIM
ATTN_WIDTH = ATTN_HEADS * HEAD_DIM
MEM_WIDTH = MEM_HEADS * HEAD_DIM
CHUNK = 128
BLOCK = 128
DILATED_CONFIGS = ((128, 1), (512, 4), (2048, 16))
EPS = 1e-6
MASKED = -1e30
LOG2E = math.log2(math.e)
LOGIT_SCALE = LOG2E / math.sqrt(HEAD_DIM)

IN_SEGMENTS = (("g_u", GMLP_WIDTH), ("g_v", GMLP_WIDTH), ("g_gate", GMLP_WIDTH),
               ("a_q", ATTN_WIDTH), ("a_k", ATTN_WIDTH), ("a_v", ATTN_WIDTH), ("a_gate", ATTN_WIDTH),
               ("m_q", MEM_WIDTH), ("m_gate", MEM_WIDTH))


def _column_slices(segments):
    slices, start = {}, 0
    for name, width in segments:
        slices[name] = slice(start, start + width)
        start += width
    return slices, start


IN_COLS, _ = _column_slices(IN_SEGMENTS)
SEQ_COLS, SEQ_WIDTH = _column_slices((("k", ATTN_WIDTH), ("v", ATTN_WIDTH), ("q", ATTN_WIDTH),
                                      ("a_gate", ATTN_WIDTH), ("g_ug", GMLP_WIDTH), ("g_vn", GMLP_WIDTH),
                                      ("m_q", MEM_WIDTH), ("m_gate", MEM_WIDTH)))
CLS_COLS, CLS_WIDTH = _column_slices((("q", ATTN_WIDTH), ("k", ATTN_WIDTH), ("v", ATTN_WIDTH)))

HEAD_GAINS = ("a_q", "a_k", "m_q", "m_k")

CLASSES = 16
MID = 4
SLABS_PER_MID = CLASSES // MID
LANES = 128
PAIRS = ATTN_WIDTH // LANES
PROJ_TILE = 1024
REGROUP_STRIDES = (2, 8)
ROW_TILE = 512
CLASSES_PER_STEP = 1
MEM_AT = 1
VMEM_LIMIT = 56 * 1024 * 1024

F32 = jnp.float32
BF16 = jnp.bfloat16


def _head_rms_lanes(acc, gain):
    left = lax.broadcasted_iota(jnp.int32, (acc.shape[0], LANES), 1) < HEAD_DIM
    groups = []
    for j in range(acc.shape[1] // LANES):
        a = acc[:, j * LANES:(j + 1) * LANES]
        sq = a * a
        ss_a = jnp.sum(jnp.where(left, sq, 0.0), axis=1, keepdims=True)
        ss_b = jnp.sum(jnp.where(left, 0.0, sq), axis=1, keepdims=True)
        inv = jnp.where(left, lax.rsqrt(ss_a * (1.0 / HEAD_DIM) + EPS),
                        lax.rsqrt(ss_b * (1.0 / HEAD_DIM) + EPS))
        group_gain = gain if gain.shape[1] == LANES else gain[:, j * LANES:(j + 1) * LANES]
        groups.append(a * inv * group_gain)
    return jnp.concatenate(groups, axis=1)


def _silu(x):
    half = 0.5 * x
    return half + half * jnp.tanh(half)


def _store_both_layouts(val, name, seq_ref, cls_ref, work_ref):
    seq_ref[:, SEQ_COLS[name]] = val.astype(BF16)
    slab_ref, mid_ref = work_ref.at[0], work_ref.at[1]
    rows = val.shape[0]
    first, second = REGROUP_STRIDES
    per_mid, per_class = rows // first, rows // CLASSES
    for s in range(val.shape[1] // LANES):
        slab_ref[s] = val[:, s * LANES:(s + 1) * LANES]
        out_lanes = slice(CLS_COLS[name].start + s * LANES, CLS_COLS[name].start + (s + 1) * LANES)
        for r in range(first):
            mid_ref[s, r * per_mid:(r + 1) * per_mid] = slab_ref[s, pl.ds(r, per_mid, stride=first), :]
        for r in range(first):
            for g in range(second):
                cls_ref[r + first * g, :, out_lanes] = (
                    mid_ref[s, pl.ds(r * per_mid + g, per_class, stride=second), :].astype(BF16))


def _proj_kernel(x_ref, xgain_ref, gvgain_ref, hgains_ref, w32_ref, seq_ref, cls_ref, w_ref, h_ref, work_ref):
    @pl.when(pl.program_id(0) == 0)
    def _():
        w_ref[...] = w32_ref[...].astype(BF16)

    def gain(name):
        if name == "g_v":
            return gvgain_ref[...]
        return hgains_ref[HEAD_GAINS.index(name):HEAD_GAINS.index(name) + 1, :]

    x = x_ref[...]
    ms = jnp.mean(x * x, axis=-1, keepdims=True)
    h_ref[...] = (x * lax.rsqrt(ms + EPS) * xgain_ref[...]).astype(BF16)

    def project(name):
        return jnp.dot(h_ref[...], w_ref[:, IN_COLS[name]], preferred_element_type=F32)

    def put(name, val):
        seq_ref[:, SEQ_COLS[name]] = val.astype(BF16)

    a_q = project("a_q")
    a_k = project("a_k")
    _store_both_layouts(_head_rms_lanes(a_q, gain("a_q")) * LOGIT_SCALE, "q", seq_ref, cls_ref, work_ref)
    a_v = project("a_v")
    _store_both_layouts(_head_rms_lanes(a_k, gain("a_k")), "k", seq_ref, cls_ref, work_ref)
    a_gate = project("a_gate")
    _store_both_layouts(a_v, "v", seq_ref, cls_ref, work_ref)
    m_q = project("m_q")
    put("a_gate", _silu(a_gate))
    g_u = project("g_u")
    put("m_q", _head_rms_lanes(m_q, gain("m_q")) * LOGIT_SCALE)
    g_v = project("g_v")
    g_gate = project("g_gate")
    put("g_vn", _head_rms_lanes(g_v, gain("g_v")))
    m_gate = project("m_gate")
    put("g_ug", g_u * _silu(g_gate))
    put("m_gate", _silu(m_gate))


def _pair_attention(q2, k2, v2, bias):
    rows = q2.shape[0]
    left = lax.broadcasted_iota(jnp.int32, (rows, LANES), 1) < HEAD_DIM
    zeros = jnp.zeros_like(q2)
    q_stack = jnp.concatenate([jnp.where(left, q2, zeros), jnp.where(left, zeros, q2)], axis=0)
    v_aug = jnp.concatenate([v2, jnp.ones_like(v2)], axis=1)
    s = lax.dot_general(q_stack, k2, (((1,), (1,)), ((), ())), preferred_element_type=F32)
    if bias is not None:
        s = s + bias
    m = jnp.max(s, axis=1, keepdims=True)
    p = jnp.exp2(s - m).astype(BF16)
    pv = jnp.dot(p, v_aug, preferred_element_type=F32)
    pick = lambda t: jnp.where(left, t[:rows], t[rows:])
    return pick(pv[:, :LANES]), pick(pv[:, LANES:]), pick(jnp.broadcast_to(m, (2 * rows, LANES)))


def _merge_states(a, b):
    top = jnp.maximum(a[2], b[2])
    w_a, w_b = jnp.exp2(a[2] - top), jnp.exp2(b[2] - top)
    return w_a * a[0] + w_b * b[0], w_a * a[1] + w_b * b[1], top


def _normalised(state):
    acc, den, top = state
    return acc / den, top + jnp.log(den) * LOG2E


def _band_bias(rel):
    return jnp.where((rel >= 0) & (rel <= BLOCK), 0.0, MASKED).astype(F32)


def _fill_bias_tables(bias_ref, slabs):
    row = lax.broadcasted_iota(jnp.int32, (2 * BLOCK, 2 * BLOCK), 0) % BLOCK
    col = lax.broadcasted_iota(jnp.int32, (2 * BLOCK, 2 * BLOCK), 1)
    bias_ref[0] = _band_bias(row - col)
    bias_ref[1] = _band_bias(row - col + BLOCK)
    if bias_ref.shape[0] > 2:
        q_rows, k_rows = BLOCK // slabs, 2 * BLOCK // slabs
        rel = (slabs * (row % q_rows) + row // q_rows) - (slabs * (col % k_rows) + col // k_rows)
        bias_ref[2] = _band_bias(rel)
        bias_ref[3] = _band_bias(rel + BLOCK)


def _first_grid_step():
    return (pl.program_id(0) == 0) & (pl.program_id(1) == 0)


def _strided_kernel(qkv_ref, o_ref, lse_ref, state16_ref, bias_ref):
    @pl.when(_first_grid_step())
    def _():
        _fill_bias_tables(bias_ref, SLABS_PER_MID)

    for j in range(qkv_ref.shape[1]):
        _strided_class(qkv_ref.at[:, j], o_ref.at[:, j], lse_ref.at[:, j], state16_ref, bias_ref)


def _strided_class(qkv_ref, o_ref, lse_ref, state16_ref, bias_ref):
    slab_len = qkv_ref.shape[1]

    def lanes_of(name, pair):
        start = CLS_COLS[name].start + pair * LANES
        return slice(start, start + LANES)

    def coarse(g, blk):
        kstart = max(blk - 1, 0) * BLOCK
        keys = slice(kstart, kstart + 2 * BLOCK)
        own = slice(blk * BLOCK, (blk + 1) * BLOCK)
        for pair in range(PAIRS):
            lanes = slice(pair * LANES, (pair + 1) * LANES)
            state = _pair_attention(qkv_ref[g, own, lanes_of("q", pair)], qkv_ref[g, keys, lanes_of("k", pair)],
                                    qkv_ref[g, keys, lanes_of("v", pair)], bias_ref[min(blk, 1)])
            for part in range(3):
                state16_ref[part, g, own, lanes] = state[part]

    q_rows = BLOCK // SLABS_PER_MID
    k_rows = 2 * q_rows

    def gather(ref, rows, lanes):
        return jnp.concatenate([ref[g, rows, lanes] for g in range(SLABS_PER_MID)], axis=0)

    def mid(i):
        own = slice(i * q_rows, (i + 1) * q_rows)
        kstart = max(i - 1, 0) * q_rows
        keys = slice(kstart, kstart + k_rows)
        for pair in range(PAIRS):
            lanes = slice(pair * LANES, (pair + 1) * LANES)
            state4 = _pair_attention(gather(qkv_ref, own, lanes_of("q", pair)),
                                     gather(qkv_ref, keys, lanes_of("k", pair)),
                                     gather(qkv_ref, keys, lanes_of("v", pair)), bias_ref[2 + min(i, 1)])
            state16 = tuple(gather(state16_ref.at[part], own, lanes) for part in range(3))
            o, lse = _normalised(_merge_states(state4, state16))
            for g in range(SLABS_PER_MID):
                o_ref[g, own, lanes] = o[g * q_rows:(g + 1) * q_rows].astype(BF16)
                lse_ref[g, own, lanes] = lse[g * q_rows:(g + 1) * q_rows]

    bands = slab_len // BLOCK
    per_band = BLOCK // q_rows
    for band in range(bands):
        for g in range(SLABS_PER_MID):
            coarse(g, band)
            if band > 0:
                for i in range(g * per_band // SLABS_PER_MID, (g + 1) * per_band // SLABS_PER_MID):
                    mid((band - 1) * per_band + i)
    for i in range(per_band):
        mid((bands - 1) * per_band + i)


def _to_sequence_order(cls_ref, slab_ref, mid_ref):
    per_class = cls_ref.shape[1]
    per_mid = per_class * SLABS_PER_MID
    n_slabs = cls_ref.shape[2] // LANES
    for s in range(n_slabs):
        lanes = slice(s * LANES, (s + 1) * LANES)
        for r4 in range(MID):
            for g in range(SLABS_PER_MID):
                mid_ref[s, pl.ds(r4 * per_mid + g, per_class, stride=SLABS_PER_MID), :] = (
                    cls_ref[r4 + MID * g, :, lanes].astype(F32))
        for r4 in range(MID):
            slab_ref[s, pl.ds(r4, per_mid, stride=MID), :] = mid_ref[s, r4 * per_mid:(r4 + 1) * per_mid]


def _mix_kernel(x_ref, seq_ref, prev_ref, ws_ref, bs_ref, oc_ref, lc_ref,
                mem_ref, mgain_ref, hgains_ref, wkv32_ref, wout32_ref, out_ref,
                oslab_ref, lslab_ref, bias_ref, wout_ref, wkv_ref, mk_ref, mv_ref, *, steps_per_batch):
    rows = x_ref.shape[0]
    step = pl.program_id(0)

    def seq(name, row_slice=slice(None), pair=None):
        cols = SEQ_COLS[name]
        if pair is not None:
            cols = slice(cols.start + pair * LANES, cols.start + (pair + 1) * LANES)
        return seq_ref[row_slice, cols]

    @pl.when(step % steps_per_batch == 0)
    def _():
        @pl.when(step == 0)
        def _():
            wkv_ref[...] = wkv32_ref[...].astype(BF16)
        mem = mem_ref[...]
        ms = jnp.mean(mem * mem, axis=-1, keepdims=True)
        hm = (mem * lax.rsqrt(ms + EPS) * mgain_ref[...]).astype(BF16)
        mk = jnp.dot(hm, wkv_ref[:, :MEM_WIDTH], preferred_element_type=F32)
        mem_k = HEAD_GAINS.index("m_k")
        mk_ref[...] = _head_rms_lanes(mk, hgains_ref[mem_k:mem_k + 1, :]).astype(BF16)
        mv_ref[...] = jnp.dot(hm, wkv_ref[:, MEM_WIDTH:], preferred_element_type=F32).astype(BF16)

    @pl.when(step == 0)
    def _():
        wout_ref[...] = wout32_ref[...].astype(BF16)
        row = lax.broadcasted_iota(jnp.int32, (2 * BLOCK, 2 * BLOCK), 0) % BLOCK
        col = lax.broadcasted_iota(jnp.int32, (2 * BLOCK, 2 * BLOCK), 1)
        band = _band_bias(row - col + BLOCK)
        bias_ref[1] = band
        bias_ref[0] = jnp.where(col >= BLOCK, band, MASKED)
    tri = (lax.broadcasted_iota(jnp.int32, (CHUNK, CHUNK), 0)
           >= lax.broadcasted_iota(jnp.int32, (CHUNK, CHUNK), 1))
    left = lax.broadcasted_iota(jnp.int32, (CHUNK, LANES), 1) < HEAD_DIM
    w_tri = [jnp.where(tri, ws_ref[h], 0.0).astype(BF16) for h in range(GMLP_HEADS)]
    sp_chunks = []
    for c in range(rows // CHUNK):
        pairs = []
        for pair in range(GMLP_WIDTH // LANES):
            vn2 = seq("g_vn", slice(c * CHUNK, (c + 1) * CHUNK), pair)
            sp_a = jnp.dot(w_tri[2 * pair], vn2, preferred_element_type=F32)
            sp_b = jnp.dot(w_tri[2 * pair + 1], vn2, preferred_element_type=F32)
            pairs.append(jnp.where(left, sp_a, sp_b))
        sp_chunks.append(jnp.concatenate(pairs, axis=1) + bs_ref[...])
    y_g = seq("g_ug").astype(F32) * jnp.concatenate(sp_chunks, axis=0)

    def memory_pair(pair):
        lanes = slice(pair * LANES, (pair + 1) * LANES)
        acc, den, _ = _pair_attention(seq("m_q", pair=pair), mk_ref[:, lanes], mv_ref[:, lanes], None)
        return acc / den

    mem_pairs = []

    _to_sequence_order(oc_ref, oslab_ref.at[0], oslab_ref.at[1])
    _to_sequence_order(lc_ref, lslab_ref.at[0], lslab_ref.at[1])
    first_bias = jnp.where(step % steps_per_batch == 0, 0, 1)
    block_outs = []
    n_blocks = rows // BLOCK
    assert MEM_AT + MEM_WIDTH // LANES <= n_blocks
    for blk in range(n_blocks):
        if MEM_AT <= blk < MEM_AT + MEM_WIDTH // LANES:
            mem_pairs.append(memory_pair(blk - MEM_AT))
        if blk == n_blocks - 1:
            y_m = jnp.concatenate(mem_pairs, axis=1) * seq("m_gate").astype(F32)
            partial = (x_ref[...]
                       + jnp.dot(y_g.astype(BF16), wout_ref[:GMLP_WIDTH, :], preferred_element_type=F32)
                       + jnp.dot(y_m.astype(BF16), wout_ref[GMLP_WIDTH + ATTN_WIDTH:, :],
                                 preferred_element_type=F32))
        own = slice(blk * BLOCK, (blk + 1) * BLOCK)
        pair_outs = []
        for pair in range(PAIRS):
            lanes = slice(pair * LANES, (pair + 1) * LANES)
            if blk == 0:
                k_lanes = slice(SEQ_COLS["k"].start + pair * LANES, SEQ_COLS["k"].start + (pair + 1) * LANES)
                v_lanes = slice(SEQ_COLS["v"].start + pair * LANES, SEQ_COLS["v"].start + (pair + 1) * LANES)
                k2 = jnp.concatenate([prev_ref[:, k_lanes], seq("k", own, pair)], axis=0)
                v2 = jnp.concatenate([prev_ref[:, v_lanes], seq("v", own, pair)], axis=0)
                bias = bias_ref[first_bias]
            else:
                window = slice((blk - 1) * BLOCK, (blk + 1) * BLOCK)
                k2, v2, bias = seq("k", window, pair), seq("v", window, pair), bias_ref[1]
            acc1, den1, top1 = _pair_attention(seq("q", own, pair), k2, v2, bias)
            o_c, top_c = oslab_ref[0, pair, own, :], lslab_ref[0, pair, own, :]
            top = jnp.maximum(top1, top_c)
            w_1, w_c = jnp.exp2(top1 - top), jnp.exp2(top_c - top)
            pair_outs.append((w_1 * acc1 + w_c * o_c) / (w_1 * den1 + w_c))
        block_outs.append(jnp.concatenate(pair_outs, axis=1))
    y_a = jnp.concatenate(block_outs, axis=0) * seq("a_gate").astype(F32)
    out_ref[...] = partial + jnp.dot(y_a.astype(BF16), wout_ref[GMLP_WIDTH:GMLP_WIDTH + ATTN_WIDTH, :],
                                     preferred_element_type=F32)


def _cparams(sem):
    return pltpu.CompilerParams(dimension_semantics=sem, vmem_limit_bytes=VMEM_LIMIT)


def _const_spec(shape):
    return pl.BlockSpec(shape, lambda *idx: (0,) * len(shape))


def _regroup_scratch(tile):
    return pltpu.VMEM((2, ATTN_WIDTH // LANES, tile, LANES), F32)


def _class_tile_spec(tile, steps_per_batch, width):
    return pl.BlockSpec((None, CLASSES, tile // CLASSES, width),
                        lambda i: (i // steps_per_batch, 0, i % steps_per_batch, 0))


def _project(x2d, batch, seq, x_gain, gv_gain, head_gains, w_in):
    rows, d_model = x2d.shape
    return pl.pallas_call(
        _proj_kernel,
        grid=(rows // PROJ_TILE,),
        in_specs=[pl.BlockSpec((PROJ_TILE, d_model), lambda i: (i, 0)),
                  _const_spec(x_gain.shape), _const_spec(gv_gain.shape), _const_spec(head_gains.shape),
                  pl.BlockSpec(w_in.shape, lambda i: (0, 0), pipeline_mode=pl.Buffered(1))],
        out_specs=[pl.BlockSpec((PROJ_TILE, SEQ_WIDTH), lambda i: (i, 0)),
                   _class_tile_spec(PROJ_TILE, seq // PROJ_TILE, CLS_WIDTH)],
        out_shape=[jax.ShapeDtypeStruct((rows, SEQ_WIDTH), BF16),
                   jax.ShapeDtypeStruct((batch, CLASSES, seq // CLASSES, CLS_WIDTH), BF16)],
        scratch_shapes=[pltpu.VMEM(w_in.shape, BF16), pltpu.VMEM((PROJ_TILE, d_model), BF16),
                        _regroup_scratch(PROJ_TILE)],
        compiler_params=_cparams(("arbitrary",)),
    )(x2d, x_gain, gv_gain, head_gains, w_in)


def _strided_attention(qkv_cls):
    batch, _, slab_len, _ = qkv_cls.shape
    spec = lambda width: pl.BlockSpec((None, SLABS_PER_MID, CLASSES_PER_STEP, slab_len, width),
                                      lambda b, r4: (b, 0, r4, 0, 0))
    split = lambda width: (batch, SLABS_PER_MID, MID, slab_len, width)
    o, lse = pl.pallas_call(
        _strided_kernel,
        grid=(batch, MID // CLASSES_PER_STEP),
        in_specs=[spec(CLS_WIDTH)],
        out_specs=[spec(ATTN_WIDTH)] * 2,
        out_shape=[jax.ShapeDtypeStruct(split(ATTN_WIDTH), BF16), jax.ShapeDtypeStruct(split(ATTN_WIDTH), F32)],
        scratch_shapes=[pltpu.VMEM((3, SLABS_PER_MID, slab_len, ATTN_WIDTH), F32),
                        pltpu.VMEM((4, 2 * BLOCK, 2 * BLOCK), F32)],
        compiler_params=_cparams(("arbitrary", "arbitrary")),
    )(qkv_cls.reshape(split(CLS_WIDTH)))
    merged = (batch, CLASSES, slab_len, ATTN_WIDTH)
    return o.reshape(merged), lse.reshape(merged)


def _mix(x2d, acts, w_s, b_full, oc, lc, mem, mem_gain, head_gains, w_kv, w_out, seq):
    rows, d_model = x2d.shape
    mem_len = mem.shape[1]
    steps_per_batch = seq // ROW_TILE
    row_spec = lambda w: pl.BlockSpec((ROW_TILE, w), lambda i: (i, 0))
    prev_spec = pl.BlockSpec((BLOCK, SEQ_COLS["v"].stop),
                             lambda i: (jnp.maximum(i * (ROW_TILE // BLOCK) - 1, 0), 0))
    mem_spec = pl.BlockSpec((None,) + mem.shape[1:], lambda i: (i // steps_per_batch, 0, 0))
    weight_spec = lambda w: pl.BlockSpec(w.shape, lambda i: (0, 0), pipeline_mode=pl.Buffered(1))
    cls_spec = _class_tile_spec(ROW_TILE, steps_per_batch, ATTN_WIDTH)
    return pl.pallas_call(
        functools.partial(_mix_kernel, steps_per_batch=steps_per_batch),
        grid=(rows // ROW_TILE,),
        in_specs=[row_spec(d_model), row_spec(SEQ_WIDTH), prev_spec,
                  _const_spec(w_s.shape), _const_spec(b_full.shape), cls_spec, cls_spec,
                  mem_spec, _const_spec(mem_gain.shape), _const_spec(head_gains.shape),
                  weight_spec(w_kv), weight_spec(w_out)],
        out_specs=row_spec(d_model),
        out_shape=jax.ShapeDtypeStruct((rows, d_model), F32),
        scratch_shapes=[_regroup_scratch(ROW_TILE)] * 2
                       + [pltpu.VMEM((2, 2 * BLOCK, 2 * BLOCK), F32), pltpu.VMEM(w_out.shape, BF16),
                          pltpu.VMEM(w_kv.shape, BF16)] + [pltpu.VMEM((mem_len, MEM_WIDTH), BF16)] * 2,
        compiler_params=_cparams(("arbitrary",)),
    )(x2d, acts, acts, w_s, b_full, oc, lc, mem, mem_gain, head_gains, w_kv, w_out)


def kernel(x, mem, norm_gain, w_in, gmlp_v_gain, gmlp_w_s, gmlp_b, attn_q_gain, attn_k_gain,
           mem_norm_gain, w_mem_kv, mem_q_gain, mem_k_gain, w_out):
    batch, seq, d_model = x.shape
    depth = w_in.shape[0]
    assert DILATED_CONFIGS == ((BLOCK, 1), (BLOCK * MID, MID), (BLOCK * CLASSES, CLASSES))
    assert seq % PROJ_TILE == 0 and seq % ROW_TILE == 0 and seq % (BLOCK * CLASSES) == 0
    x2d = x.reshape(batch * seq, d_model)
    for l in range(depth):
        row = lambda g: g.reshape(1, -1).astype(F32)
        shared = dict(a_q=attn_q_gain[l], a_k=attn_k_gain[l], m_q=mem_q_gain[l], m_k=mem_k_gain[l])
        head_gains = jnp.tile(jnp.stack([shared[name] for name in HEAD_GAINS]).astype(F32), (1, 2))
        acts, qkv_cls = _project(x2d, batch, seq, row(norm_gain[l]), row(gmlp_v_gain[l]), head_gains,
                                 w_in[l])
        oc, lc = _strided_attention(qkv_cls)
        b_full = jnp.repeat(gmlp_b[l].T, HEAD_DIM, axis=1)
        x2d = _mix(x2d, acts, gmlp_w_s[l], b_full, oc, lc, mem, row(mem_norm_gain[l]), head_gains,
                   w_mem_kv[l], w_out[l], seq)
    return x2d.reshape(batch, seq, d_model)
```
